```python
import math
import jax
import jax.numpy as jnp
from jax import lax
import numpy as np

D_MODEL = 2048
BATCH = 4
SEQ = 4096
DEPTH = 2

CHUNK = 64
N_EVEN = (DEPTH + 1) // 2
N_ODD = DEPTH // 2
NORM_EPS = 1e-6

RWKV_WIDTH = D_MODEL // 2
RWKV_HEAD = 64
RWKV_HEADS = RWKV_WIDTH // RWKV_HEAD
DECAY_LORA = max(32, int(round(1.8 * D_MODEL ** 0.5 / 32)) * 32)
AAA_LORA = max(32, int(round(1.8 * D_MODEL ** 0.5 / 32)) * 32)
GATE_LORA = max(32, int(round(0.6 * D_MODEL ** 0.8 / 32)) * 32)
RWKV_LN_EPS = 64e-5
RWKV_PROJ = 3 * RWKV_WIDTH + DECAY_LORA + AAA_LORA + GATE_LORA

SSM_WIDTH = D_MODEL // 2
SSM_HEAD = 64
SSM_HEADS = SSM_WIDTH // SSM_HEAD
SSM_STATE = 128
SSM_GROUPS = 2
SSM_CONV = 4
SSM_CONV_CH = SSM_WIDTH + 2 * SSM_GROUPS * SSM_STATE
SSM_PROJ = SSM_WIDTH + SSM_CONV_CH + SSM_HEADS
IN_AB = RWKV_PROJ + SSM_PROJ
MIX_AB = RWKV_WIDTH + SSM_WIDTH

ATT_HEADS = 16
ATT_HEAD = D_MODEL // ATT_HEADS
LEFT_CHUNKS = 8
BAND = (LEFT_CHUNKS + 1) * CHUNK
REL_PAST_CLIP = 256
REL_FUTURE = CHUNK - 1
REL_BUCKETS = REL_PAST_CLIP + REL_FUTURE + 1

D_FF = -(-8 * D_MODEL // (3 * 256)) * 256

kernel_name = 'hybrid_rwkv7_ssd_chunkattn_encoder'


def rms_norm(x, g):
    xf = x.astype(jnp.float32)
    y = xf * lax.rsqrt(jnp.mean(xf * xf, axis=-1, keepdims=True) + NORM_EPS)
    return (y * g.astype(jnp.float32)).astype(x.dtype)


def token_shift(x):
    return jnp.pad(x, ((0, 0), (1, 0), (0, 0)))[:, :-1]


def causal_dwconv(x, w, b):
    K, C = w.shape
    y = lax.conv_general_dilated(x, w[:, None, :], window_strides=(1,), padding=[(K - 1, 0)],
                                 dimension_numbers=('NWC', 'WIO', 'NWC'), feature_group_count=C)
    return y + b


def rwkv7_time_mix(p, mu, w0, w2, a0, a2, g2, k_k, k_a, r_k, ln_w, ln_b):
    f32 = jnp.float32
    Bsz, T, _ = p.shape
    p = p.astype(f32)
    p = p + (token_shift(p) - p) * mu
    s1 = RWKV_WIDTH
    s4 = 3 * s1 + DECAY_LORA
    s5 = s4 + AAA_LORA
    r, k, v, wc, ac, gc = jnp.split(p, [s1, 2 * s1, 3 * s1, s4, s5], axis=-1)
    w = -jax.nn.softplus(-(w0 + jnp.tanh(wc) @ w2)) - 0.5
    decay = jnp.exp(-jnp.exp(w))
    a = jax.nn.sigmoid(a0 + ac @ a2)
    g = jax.nn.sigmoid(gc) @ g2

    def heads(t):
        return t.reshape(Bsz, T, RWKV_HEADS, RWKV_HEAD)

    kk = heads(k * k_k)
    kk = kk * lax.rsqrt(jnp.maximum(jnp.sum(kk * kk, axis=-1, keepdims=True), 1e-24))
    k = k * (1.0 + (a - 1.0) * k_a)
    a_vec = -kk
    b_vec = kk * heads(a)

    def step(S, inp):
        r_t, w_t, k_t, v_t, a_t, b_t = inp
        sa = jnp.einsum('bhvk,bhk->bhv', S, a_t)
        S = S * w_t[:, :, None, :] + sa[..., None] * b_t[:, :, None, :] + v_t[..., None] * k_t[:, :, None, :]
        return S, jnp.einsum('bhvk,bhk->bhv', S, r_t)

    xs = tuple(jnp.swapaxes(t, 0, 1) for t in (heads(r), heads(decay), heads(k), heads(v), a_vec, b_vec))
    S0 = jnp.zeros((Bsz, RWKV_HEADS, RWKV_HEAD, RWKV_HEAD), f32)
    _, y = lax.scan(step, S0, xs)
    y = jnp.swapaxes(y, 0, 1)
    mean = jnp.mean(y, axis=-1, keepdims=True)
    var = jnp.mean(jnp.square(y - mean), axis=-1, keepdims=True)
    y = ((y - mean) * lax.rsqrt(var + RWKV_LN_EPS)).reshape(Bsz, T, RWKV_WIDTH) * ln_w + ln_b
    bonus = jnp.sum(heads(r) * heads(k) * r_k, axis=-1, keepdims=True) * heads(v)
    return (y + bonus.reshape(Bsz, T, RWKV_WIDTH)) * g


def ssd_chunked(x, dt, A, Bm, Cm):
    Bsz, T, H, P = x.shape
    nc = T // CHUNK
    hpg = H // SSM_GROUPS
    Xc = (x * dt[..., None]).reshape(Bsz, nc, CHUNK, H, P)
    Ac = jnp.transpose((A * dt).reshape(Bsz, nc, CHUNK, H), (0, 3, 1, 2))
    Bh = jnp.repeat(Bm, hpg, axis=2).reshape(Bsz, nc, CHUNK, H, SSM_STATE)
    Ch = jnp.repeat(Cm, hpg, axis=2).reshape(Bsz, nc, CHUNK, H, SSM_STATE)
    A_cs = jnp.cumsum(Ac, axis=-1)
    causal = jnp.asarray(np.tril(np.ones((CHUNK, CHUNK), dtype=bool)))
    seg = A_cs[..., :, None] - A_cs[..., None, :]
    L = jnp.where(causal, jnp.exp(jnp.where(causal, seg, 0.0)), 0.0)
    scores = jnp.einsum('bclhn,bcshn->bhcls', Ch, Bh) * L
    y_diag = jnp.einsum('bhcls,bcshp->bclhp', scores, Xc)
    decay_states = jnp.exp(A_cs[..., -1:] - A_cs)
    states = jnp.einsum('bclhn,bhcl,bclhp->bchpn', Bh, decay_states, Xc)
    chunk_decay = jnp.exp(A_cs[..., -1])

    def step(hstate, inp):
        s_c, d_c = inp
        return hstate * d_c[..., None, None] + s_c, hstate

    h0 = jnp.zeros((Bsz, H, P, SSM_STATE), x.dtype)
    _, prev = lax.scan(step, h0, (jnp.swapaxes(states, 0, 1), jnp.transpose(chunk_decay, (2, 0, 1))))
    prev = jnp.swapaxes(prev, 0, 1)
    y_off = jnp.einsum('bclhn,bchpn,bhcl->bclhp', Ch, prev, jnp.exp(A_cs))
    return (y_diag + y_off).reshape(Bsz, T, H, P)


def mamba2_mix(p, conv_w, conv_b, dt_bias, A_log, D_skip, norm_w):
    f32 = jnp.float32
    Bsz, T, _ = p.shape
    p = p.astype(f32)
    z, xBC, dt = jnp.split(p, [SSM_WIDTH, SSM_WIDTH + SSM_CONV_CH], axis=-1)
    xBC = jax.nn.silu(causal_dwconv(xBC, conv_w.astype(f32), conv_b.astype(f32)))
    xs, Bm, Cm = jnp.split(xBC, [SSM_WIDTH, SSM_WIDTH + SSM_GROUPS * SSM_STATE], axis=-1)
    dt = jax.nn.softplus(dt + dt_bias)
    A = -jnp.exp(A_log.astype(f32))
    xh = xs.reshape(Bsz, T, SSM_HEADS, SSM_HEAD)
    y = ssd_chunked(xh, dt, A, Bm.reshape(Bsz, T, SSM_GROUPS, SSM_STATE),
                    Cm.reshape(Bsz, T, SSM_GROUPS, SSM_STATE))
    y = (y + D_skip[:, None] * xh).reshape(Bsz, T, SSM_WIDTH) * jax.nn.silu(z)
    yg = y.reshape(Bsz, T, SSM_GROUPS, SSM_WIDTH // SSM_GROUPS)
    yg = yg * lax.rsqrt(jnp.mean(yg * yg, axis=-1, keepdims=True) + NORM_EPS)
    return yg.reshape(Bsz, T, SSM_WIDTH) * norm_w


def parallel_ab_mixer(h, w_in, mu, w0, w2, a0, a2, g2, k_k, k_a, r_k, ln_w, ln_b,
                      conv_w, conv_b, dt_bias, A_log, D_skip, norm_w, w_out):
    p = h @ w_in
    p_rwkv, p_ssm = jnp.split(p, [RWKV_PROJ], axis=-1)
    y_a = rwkv7_time_mix(p_rwkv, mu, w0, w2, a0, a2, g2, k_k, k_a, r_k, ln_w, ln_b)
    y_b = mamba2_mix(p_ssm, conv_w, conv_b, dt_bias, A_log, D_skip, norm_w)
    y = jnp.concatenate([y_a, y_b], axis=-1).astype(h.dtype)
    return y @ w_out


def chunk_band_attention(h, w_qkv, rel_bias, w_out):
    Bsz, T, _ = h.shape
    nc = T // CHUNK
    pad = LEFT_CHUNKS * CHUNK
    q, k, v = jnp.split(h @ w_qkv, 3, axis=-1)
    q = q.reshape(Bsz, T, ATT_HEADS, ATT_HEAD)
    k = jnp.pad(k.reshape(Bsz, T, ATT_HEADS, ATT_HEAD), ((0, 0), (pad, 0), (0, 0), (0, 0)))
    v = jnp.pad(v.reshape(Bsz, T, ATT_HEADS, ATT_HEAD), ((0, 0), (pad, 0), (0, 0), (0, 0)))
    rel = (jnp.arange(CHUNK)[:, None] + pad) - jnp.arange(BAND)[None, :]
    rel = jnp.clip(rel, -REL_FUTURE, REL_PAST_CLIP) + REL_FUTURE
    bias = rel_bias[:, rel].astype(jnp.float32)
    scale = ATT_HEAD ** -0.5
    q_chunks = jnp.swapaxes(q.reshape(Bsz, nc, CHUNK, ATT_HEADS, ATT_HEAD), 0, 1)

    def one_chunk(args):
        q_c, c = args
        start = c * CHUNK
        k_b = lax.dynamic_slice_in_dim(k, start, BAND, axis=1)
        v_b = lax.dynamic_slice_in_dim(v, start, BAND, axis=1)
        s = jnp.einsum('bqhd,bkhd->bhqk', q_c, k_b, preferred_element_type=jnp.float32) * scale + bias
        valid = (start - pad + jnp.arange(BAND)) >= 0
        s = jnp.where(valid, s, -1e30)
        pr = jax.nn.softmax(s, axis=-1)
        return jnp.einsum('bhqk,bkhd->bqhd', pr.astype(v_b.dtype), v_b)

    out = lax.map(one_chunk, (q_chunks, jnp.arange(nc)))
    out = jnp.swapaxes(out, 0, 1).reshape(Bsz, T, D_MODEL)
    return out @ w_out


def swiglu(h, w_gate, w_up, w_down):
    return (jax.nn.silu(h @ w_gate) * (h @ w_up)) @ w_down


def setup_inputs(seed: int = 0) -> dict:
    key = jax.random.key(seed)
    ks = jax.random.split(key, 32)
    f32 = jnp.float32

    def nrm(i, shape, s):
        return jax.random.normal(ks[i], shape, f32) * s

    def unif(i, shape, lo, hi):
        return jax.random.uniform(ks[i], shape, f32, lo, hi)

    E, O = N_EVEN, N_ODD
    dt0 = jnp.exp(unif(15, (E, SSM_HEADS), math.log(1e-3), math.log(1e-1)))
    return {
        'x': nrm(0, (BATCH, SEQ, D_MODEL), 1.0),
        'norm_g': 1.0 + nrm(1, (DEPTH, 4, D_MODEL), 0.02),
        'w_in_ab': nrm(2, (E, D_MODEL, IN_AB), D_MODEL ** -0.5),
        'rwkv_mu': unif(3, (E, RWKV_PROJ), 0.0, 1.0),
        'rwkv_w0': unif(4, (E, RWKV_WIDTH), -6.0, -1.0),
        'rwkv_w2': nrm(5, (E, DECAY_LORA, RWKV_WIDTH), DECAY_LORA ** -0.5),
        'rwkv_a0': nrm(6, (E, RWKV_WIDTH), 0.1),
        'rwkv_a2': nrm(7, (E, AAA_LORA, RWKV_WIDTH), AAA_LORA ** -0.5),
        'rwkv_g2': nrm(8, (E, GATE_LORA, RWKV_WIDTH), GATE_LORA ** -0.5),
        'rwkv_k_k': 0.85 + nrm(9, (E, RWKV_WIDTH), 0.02),
        'rwkv_k_a': 1.0 + nrm(10, (E, RWKV_WIDTH), 0.02),
        'rwkv_r_k': nrm(11, (E, RWKV_HEADS, RWKV_HEAD), 0.1),
        'rwkv_ln_w': 1.0 + nrm(12, (E, RWKV_WIDTH), 0.02),
        'rwkv_ln_b': nrm(13, (E, RWKV_WIDTH), 0.02),
        'ssm_conv_w': nrm(14, (E, SSM_CONV, SSM_CONV_CH), SSM_CONV ** -0.5),
        'ssm_conv_b': nrm(16, (E, SSM_CONV_CH), 0.02),
        'ssm_dt_bias': dt0 + jnp.log(-jnp.expm1(-dt0)),
        'ssm_A_log': jnp.log(unif(17, (E, SSM_HEADS), 1.0, 16.0)),
        'ssm_D': 1.0 + nrm(18, (E, SSM_HEADS), 0.1),
        'ssm_norm_w': 1.0 + nrm(19, (E, SSM_WIDTH), 0.02),
        'w_out_ab': nrm(20, (E, MIX_AB, D_MODEL), MIX_AB ** -0.5),
        'w_qkv': nrm(21, (O, D_MODEL, 3 * D_MODEL), D_MODEL ** -0.5),
        'attn_rel_bias': nrm(22, (O, ATT_HEADS, REL_BUCKETS), 0.2),
        'w_out_c': nrm(23, (O, D_MODEL, D_MODEL), D_MODEL ** -0.5),
        'ffn_w_gate': nrm(24, (DEPTH, D_MODEL, D_FF), D_MODEL ** -0.5),
        'ffn_w_up': nrm(25, (DEPTH, D_MODEL, D_FF), D_MODEL ** -0.5),
        'ffn_w_down': nrm(26, (DEPTH, D_FF, D_MODEL), D_FF ** -0.5),
    }


def reference(x, norm_g, w_in_ab, rwkv_mu, rwkv_w0, rwkv_w2, rwkv_a0, rwkv_a2, rwkv_g2,
              rwkv_k_k, rwkv_k_a, rwkv_r_k, rwkv_ln_w, rwkv_ln_b, ssm_conv_w, ssm_conv_b,
              ssm_dt_bias, ssm_A_log, ssm_D, ssm_norm_w, w_out_ab, w_qkv, attn_rel_bias,
              w_out_c, ffn_w_gate, ffn_w_up, ffn_w_down):
    for l in range(DEPTH):
        i = l // 2
        h = rms_norm(x, norm_g[l, 0])
        if l % 2 == 0:
            m = parallel_ab_mixer(h, w_in_ab[i], rwkv_mu[i], rwkv_w0[i], rwkv_w2[i], rwkv_a0[i],
                                  rwkv_a2[i], rwkv_g2[i], rwkv_k_k[i], rwkv_k_a[i], rwkv_r_k[i],
                                  rwkv_ln_w[i], rwkv_ln_b[i], ssm_conv_w[i], ssm_conv_b[i],
                                  ssm_dt_bias[i], ssm_A_log[i], ssm_D[i], ssm_norm_w[i], w_out_ab[i])
        else:
            m = chunk_band_attention(h, w_qkv[i], attn_rel_bias[i], w_out_c[i])
        x = x + rms_norm(m, norm_g[l, 1])
        f = swiglu(rms_norm(x, norm_g[l, 2]), ffn_w_gate[l], ffn_w_up[l], ffn_w_down[l])
        x = x + rms_norm(f, norm_g[l, 3])
    return x
```

```python
import functools
import math

import jax
import jax.numpy as jnp
from jax import lax
from jax.experimental import pallas as pl
from jax.experimental.pallas import tpu as pltpu

F32 = jnp.float32
BF16 = jnp.bfloat16

NORM_EPS = 1e-6
RWKV_LN_EPS = 64e-5
CHUNK = 64
HEAD = 64
LANE = 128
LEFT_CHUNKS = 8
REL_PAST_CLIP = 256
VMEM_LIMIT = 56 * 1024 * 1024


def _cparams(n_axes):
    return pltpu.CompilerParams(dimension_semantics=("arbitrary",) * n_axes, vmem_limit_bytes=VMEM_LIMIT)


def _dot(a, b):
    return jnp.dot(a, b, preferred_element_type=F32)


def _dot_nt(a, b):
    return lax.dot_general(a, b, (((1,), (1,)), ((), ())), preferred_element_type=F32)


def _dot_tn(a, b):
    return lax.dot_general(a, b, (((0,), (0,)), ((), ())), preferred_element_type=F32)


def _split3(t):
    hi = t.astype(BF16)
    r1 = t - hi.astype(F32)
    mid = r1.astype(BF16)
    lo = (r1 - mid.astype(F32)).astype(BF16)
    return hi, mid, lo


def _dot_sel_r(t, sel):
    hi, mid, lo = _split3(t)
    return _dot(hi, sel) + _dot(mid, sel) + _dot(lo, sel)


def _dot_sel_l(sel, t):
    hi, mid, lo = _split3(t)
    return _dot(sel, hi) + _dot(sel, mid) + _dot(sel, lo)


def _rms(x, g):
    ms = jnp.mean(x * x, axis=-1, keepdims=True)
    return x * lax.rsqrt(ms + NORM_EPS) * g


def _sigmoid(x):
    return 1.0 / (1.0 + jnp.exp(-x))


def _silu(x):
    return x * _sigmoid(x)


def _softplus(x):
    return jnp.maximum(x, 0.0) + jnp.log1p(jnp.exp(-jnp.abs(x)))


def _iota(shape, axis):
    return lax.broadcasted_iota(jnp.int32, shape, axis)


def _norm_matmul_body(x_ref, g_ref, w_ref, o_ref, h_ref):
    @pl.when(pl.program_id(1) == 0)
    def _():
        h_ref[...] = _rms(x_ref[...], g_ref[...]).astype(BF16)

    o_ref[...] = _dot(h_ref[...], w_ref[...]).astype(o_ref.dtype)


def _norm_matmul(x, g, w, out_dtype, tm, tn):
    n, d = x.shape
    m = w.shape[1]
    return pl.pallas_call(
        _norm_matmul_body,
        grid=(n // tm, m // tn),
        in_specs=[
            pl.BlockSpec((tm, d), lambda i, j: (i, 0)),
            pl.BlockSpec((1, d), lambda i, j: (0, 0)),
            pl.BlockSpec((d, tn), lambda i, j: (0, j)),
        ],
        out_specs=pl.BlockSpec((tm, tn), lambda i, j: (i, j)),
        out_shape=jax.ShapeDtypeStruct((n, m), out_dtype),
        scratch_shapes=[pltpu.VMEM((tm, d), BF16)],
        compiler_params=_cparams(2),
        name="norm_matmul",
    )(x, g, w)


def _proj_res_body(*refs, n_in):
    y_refs, w_refs = refs[:n_in], refs[n_in:2 * n_in]
    x_ref, g_ref, o_ref = refs[2 * n_in:]
    m = _dot(y_refs[0][...], w_refs[0][...])
    for y_ref, w_ref in zip(y_refs[1:], w_refs[1:]):
        m = m + _dot(y_ref[...], w_ref[...])
    o_ref[...] = x_ref[...] + _rms(m, g_ref[...])


def _proj_res(ys, ws, x, g, tm):
    n, d = x.shape
    n_in = len(ys)
    in_specs = [pl.BlockSpec((tm, y.shape[1]), lambda i: (i, 0)) for y in ys]
    in_specs += [pl.BlockSpec(w.shape, lambda i: (0, 0)) for w in ws]
    in_specs += [pl.BlockSpec((tm, d), lambda i: (i, 0)), pl.BlockSpec((1, d), lambda i: (0, 0))]
    return pl.pallas_call(
        functools.partial(_proj_res_body, n_in=n_in),
        grid=(n // tm,),
        in_specs=in_specs,
        out_specs=pl.BlockSpec((tm, d), lambda i: (i, 0)),
        out_shape=jax.ShapeDtypeStruct((n, d), F32),
        compiler_params=_cparams(1),
        name="proj_res",
    )(*ys, *ws, x, g)


def _ffn_body(x_ref, gi_ref, go_ref, wg_ref, wu_ref, wd_ref, o_ref, h_ref, acc_ref):
    j = pl.program_id(1)

    @pl.when(j == 0)
    def _():
        h_ref[...] = _rms(x_ref[...], gi_ref[...]).astype(BF16)

    h = h_ref[...]
    a = (_silu(_dot(h, wg_ref[...])) * _dot(h, wu_ref[...])).astype(BF16)
    part = _dot(a, wd_ref[...])

    @pl.when(j == 0)
    def _():
        acc_ref[...] = part

    @pl.when(j > 0)
    def _():
        acc_ref[...] += part

    @pl.when(j == pl.num_programs(1) - 1)
    def _():
        o_ref[...] = x_ref[...] + _rms(acc_ref[...], go_ref[...])


def _ffn(x, g_in, g_out, wg, wu, wd, tm, tf):
    n, d = x.shape
    f = wg.shape[1]
    return pl.pallas_call(
        _ffn_body,
        grid=(n // tm, f // tf),
        in_specs=[
            pl.BlockSpec((tm, d), lambda i, j: (i, 0)),
            pl.BlockSpec((1, d), lambda i, j: (0, 0)),
            pl.BlockSpec((1, d), lambda i, j: (0, 0)),
            pl.BlockSpec((d, tf), lambda i, j: (0, j)),
            pl.BlockSpec((d, tf), lambda i, j: (0, j)),
            pl.BlockSpec((tf, d), lambda i, j: (j, 0)),
        ],
        out_specs=pl.BlockSpec((tm, d), lambda i, j: (i, 0)),
        out_shape=jax.ShapeDtypeStruct((n, d), F32),
        scratch_shapes=[pltpu.VMEM((tm, d), BF16), pltpu.VMEM((tm, d), F32)],
        compiler_params=_cparams(2),
        name="ffn",
    )(x, g_in, g_out, wg, wu, wd)


def _seg_matrices(width, n_seg_pad):
    e = (_iota((width, n_seg_pad), 0) // HEAD == _iota((width, n_seg_pad), 1)).astype(BF16)
    et = (_iota((n_seg_pad, width), 1) // HEAD == _iota((n_seg_pad, width), 0)).astype(BF16)
    return e, et


def _blockdiag_tril(tc):
    r, c = _iota((tc, tc), 0), _iota((tc, tc), 1)
    return ((r // CHUNK == c // CHUNK) & (c <= r)).astype(BF16)


def _rwkv_body(p_ref, pp_ref, mu_ref, w0_ref, w2_ref, a0_ref, a2_ref, g2_ref, kk_ref, ka_ref, rk_ref,
               lnw_ref, lnb_ref, o_ref, s_ref, at_ref, rt_ref, bt_ref, kt_ref, v_ref, gl_ref, y_ref, *, tc, width):
    i = pl.program_id(1)
    n_heads = width // HEAD
    n_chunks = tc // CHUNK

    @pl.when(i == 0)
    def _():
        s_ref[...] = jnp.zeros_like(s_ref)

    p = p_ref[...]
    prev = jnp.where(i == 0, 0.0, pp_ref[7:8, :])
    psh = jnp.where(_iota((tc, 1), 0) == 0, prev, pltpu.roll(p, 1, 0))
    xr = p + (psh - p) * mu_ref[...]

    r = xr[:, 0:width]
    k = xr[:, width:2 * width]
    v = xr[:, 2 * width:3 * width]
    o1 = 3 * width
    wc = xr[:, o1:o1 + LANE]
    ac = xr[:, o1 + LANE:o1 + 2 * LANE]
    gc = xr[:, o1 + 2 * LANE:o1 + 4 * LANE]

    z = w0_ref[...] + _dot(jnp.tanh(wc).astype(BF16), w2_ref[...])
    lw = -jnp.exp(-_softplus(-z) - 0.5)
    a = _sigmoid(a0_ref[...] + _dot(ac.astype(BF16), a2_ref[...]))
    g = _dot(_sigmoid(gc).astype(BF16), g2_ref[...])

    e, et = _seg_matrices(width, LANE)

    def seg_sum(t):
        return _dot_sel_r(_dot_sel_r(t, e), et)

    kk = k * kk_ref[...]
    kk = kk * lax.rsqrt(jnp.maximum(seg_sum(kk * kk), 1e-24))
    k2 = k * (1.0 + (a - 1.0) * ka_ref[...])
    bonus = seg_sum(r * k2 * rk_ref[...]) * v

    cs = _dot_sel_l(_blockdiag_tril(tc), lw)
    ginv = jnp.exp(-cs)
    at_ref[...] = (-kk * jnp.exp(cs - lw)).astype(BF16)
    rt_ref[...] = (r * jnp.exp(cs)).astype(BF16)
    bt_ref[...] = (kk * a * ginv).astype(BF16)
    kt_ref[...] = (k2 * ginv).astype(BF16)
    v_ref[...] = v.astype(BF16)
    for c in range(n_chunks):
        last = c * CHUNK + CHUNK - 1
        gl_ref[c:c + 1, :] = jnp.exp(cs[last:last + 1, :])

    rr, cc = _iota((CHUNK, CHUNK), 0), _iota((CHUNK, CHUNK), 1)
    strict = (cc < rr).astype(F32)
    rr2, cc2 = _iota((CHUNK, 2 * CHUNK), 0), _iota((CHUNK, 2 * CHUNK), 1) % CHUNK
    strict2 = (cc2 < rr2).astype(F32)
    incl2 = (cc2 <= rr2).astype(F32)
    eye = (cc == rr).astype(F32)
    n_lvl = int(math.log2(CHUNK))
    lvl_masks = []
    for lv in range(n_lvl):
        same = (rr >> (lv + 1)) == (cc >> (lv + 1))
        lvl_masks.append((same & (((rr >> lv) & 1) == 1) & (((cc >> lv) & 1) == 0)).astype(F32))

    def chunk_step(c, carry):
        t0 = pl.multiple_of(c * CHUNK, CHUNK)
        rows = pl.ds(t0, CHUNK)
        gl = gl_ref[pl.ds(c, 1), :]
        for h in range(n_heads):
            hs = slice(h * HEAD, (h + 1) * HEAD)
            at, rt = at_ref[rows, hs], rt_ref[rows, hs]
            bt, kt = bt_ref[rows, hs], kt_ref[rows, hs]
            vh = v_ref[rows, hs]
            ar = jnp.concatenate([at, rt], axis=0)
            bk = jnp.concatenate([bt, kt], axis=0)
            pm = _dot_nt(ar, bk)
            s0 = s_ref[h]
            ars = _dot_nt(ar, s0.astype(BF16))
            pa = pm[:CHUNK, :] * strict2
            aab = pa[:, :CHUNK]
            aak = pa[:, CHUNK:]
            rhs = ars[:CHUNK, :] + _dot(aak.astype(BF16), vh)
            x = eye + aab * lvl_masks[0]
            for lv in range(1, n_lvl):
                tm_ = _dot((aab * lvl_masks[lv]).astype(BF16), x.astype(BF16))
                x = x + _dot(x.astype(BF16), tm_.astype(BF16))
            u = _dot(x.astype(BF16), rhs.astype(BF16))
            uv = jnp.concatenate([u.astype(BF16), vh], axis=0)
            pr = (pm[CHUNK:, :] * incl2).astype(BF16)
            y_ref[rows, hs] = ars[CHUNK:, :] + _dot(pr, uv)
            s_ref[h] = (s0 + _dot_tn(uv, bk)) * gl[:, hs]
        return carry

    lax.fori_loop(0, n_chunks, chunk_step, 0)

    y = y_ref[...]
    inv_n = 1.0 / HEAD
    mean = seg_sum(y) * inv_n
    yc = y - mean
    var = seg_sum(yc * yc) * inv_n
    yn = yc * lax.rsqrt(var + RWKV_LN_EPS) * lnw_ref[...] + lnb_ref[...]
    o_ref[...] = ((yn + bonus) * g).astype(o_ref.dtype)


def _rwkv_mix(p, mu, w0, w2, a0, a2, g2, k_k, k_a, r_k, ln_w, ln_b, batch, seq, tc):
    n, pw = p.shape
    width = w0.shape[1]
    n_heads = width // HEAD
    nt = seq // tc
    row_blk = lambda b, i: (b * nt + i, 0)
    prev_blk = lambda b, i: (jnp.maximum((b * seq + i * tc) // 8 - 1, 0), 0)
    const = lambda b, i: (0, 0)
    vec = lambda a: pl.BlockSpec(a.shape, const)
    return pl.pallas_call(
        functools.partial(_rwkv_body, tc=tc, width=width),
        grid=(batch, nt),
        in_specs=[pl.BlockSpec((tc, pw), row_blk), pl.BlockSpec((8, pw), prev_blk)]
        + [vec(a) for a in (mu, w0, w2, a0, a2, g2, k_k, k_a, r_k, ln_w, ln_b)],
        out_specs=pl.BlockSpec((tc, width), row_blk),
        out_shape=jax.ShapeDtypeStruct((n, width), BF16),
        scratch_shapes=[pltpu.VMEM((n_heads, HEAD, HEAD), F32)]
        + [pltpu.VMEM((tc, width), BF16) for _ in range(5)]
        + [pltpu.VMEM((max(tc // CHUNK, 8), width), F32), pltpu.VMEM((tc, width), F32)],
        compiler_params=_cparams(2),
        name="rwkv_mix",
    )(p, p, mu, w0, w2, a0, a2, g2, k_k, k_a, r_k, ln_w, ln_b)


def _ssd_body(p_ref, pp_ref, cw_ref, cb_ref, dtb_ref, alog_ref, dskip_ref, nw_ref, o_ref,
              st_ref, y_ref, *, tc, width, state, groups):
    i = pl.program_id(1)
    n_chunks = tc // CHUNK
    gw = width // groups
    hpg = gw // HEAD
    conv_ch = width + 2 * groups * state
    n_tap = cw_ref.shape[0]

    @pl.when(i == 0)
    def _():
        st_ref[...] = jnp.zeros_like(st_ref)

    p = p_ref[...]
    z = p[:, 0:width]
    xbc_raw = p[:, width:width + conv_ch]
    dt_raw = p[:, width + conv_ch:width + conv_ch + LANE]

    prev = jnp.where(i == 0, 0.0, pp_ref[:, width:width + conv_ch])
    xcat = jnp.concatenate([prev, xbc_raw], axis=0)
    conv = cb_ref[...]
    for j in range(n_tap):
        off = 8 - (n_tap - 1) + j
        conv = conv + cw_ref[j:j + 1, :] * xcat[off:off + tc, :]
    xbc = _silu(conv)
    xs = xbc[:, 0:width]
    bm = xbc[:, width:width + groups * state].astype(BF16)
    cm = xbc[:, width + groups * state:conv_ch].astype(BF16)

    e, et = _seg_matrices(width, LANE)
    dt = _softplus(dt_raw + dtb_ref[...])
    da = -jnp.exp(alog_ref[...]) * dt
    cs_h = _dot_sel_l(_blockdiag_tril(tc), da)
    cs = _dot_sel_r(cs_h, et)
    dtx = _dot_sel_r(dt, et)
    xc = xs * dtx
    ecs = jnp.exp(cs)

    rr, cc = _iota((CHUNK, CHUNK), 0), _iota((CHUNK, CHUNK), 1)
    causal = cc <= rr

    for c in range(n_chunks):
        r0 = c * CHUNK
        rows = slice(r0, r0 + CHUNK)
        cs_c = cs[rows, :]
        cs_last = cs_c[CHUNK - 1:CHUNK, :]
        xc_c = xc[rows, :]
        xd = (xc_c * jnp.exp(cs_last - cs_c)).astype(BF16)
        xcb = xc_c.astype(BF16)
        cs_t = cs_h[rows, :].T
        for g in range(groups):
            gl = slice(g * gw, (g + 1) * gw)
            b_g = bm[rows, g * state:(g + 1) * state]
            c_g = cm[rows, g * state:(g + 1) * state]
            cb = _dot_nt(c_g, b_g)
            st = st_ref[g]
            y_off = _dot(c_g, st.astype(BF16)) * ecs[rows, gl]
            for hh in range(hpg):
                h = g * hpg + hh
                hs = slice(h * HEAD, (h + 1) * HEAD)
                seg = cs_c[:, hs] - cs_t[h:h + 1, :]
                lmat = jnp.where(causal, jnp.exp(jnp.where(causal, seg, 0.0)), 0.0)
                y_ref[rows, hs] = _dot((cb * lmat).astype(BF16), xcb[:, hs]) + y_off[:, hh * HEAD:(hh + 1) * HEAD]
            st_ref[g] = st * ecs[r0 + CHUNK - 1:r0 + CHUNK, gl] + _dot_tn(b_g, xd[:, gl])

    y = (y_ref[...] + dskip_ref[...] * xs) * _silu(z)
    for g in range(groups):
        gl = slice(g * gw, (g + 1) * gw)
        yg = y[:, gl]
        ms = jnp.mean(yg * yg, axis=-1, keepdims=True)
        o_ref[:, gl] = (yg * lax.rsqrt(ms + NORM_EPS) * nw_ref[:, gl]).astype(o_ref.dtype)


def _ssd_mix(p, conv_w, conv_b, dt_bias, a_log, d_skip, norm_w, batch, seq, tc, width, state, groups):
    n, pw = p.shape
    nt = seq // tc
    row_blk = lambda b, i: (b * nt + i, 0)
    prev_blk = lambda b, i: (jnp.maximum((b * seq + i * tc) // 8 - 1, 0), 0)
    const = lambda b, i: (0, 0)
    vec = lambda a: pl.BlockSpec(a.shape, const)
    return pl.pallas_call(
        functools.partial(_ssd_body, tc=tc, width=width, state=state, groups=groups),
        grid=(batch, nt),
        in_specs=[pl.BlockSpec((tc, pw), row_blk), pl.BlockSpec((8, pw), prev_blk)]
        + [vec(a) for a in (conv_w, conv_b, dt_bias, a_log, d_skip, norm_w)],
        out_specs=pl.BlockSpec((tc, width), row_blk),
        out_shape=jax.ShapeDtypeStruct((n, width), BF16),
        scratch_shapes=[pltpu.VMEM((groups, state, width // groups), F32), pltpu.VMEM((tc, width), F32)],
        compiler_params=_cparams(2),
        name="ssd_mix",
    )(p, p, conv_w, conv_b, dt_bias, a_log, d_skip, norm_w)


def _bias_table_body(rb_ref, o_ref, *, band, n_bucket, rel_future):
    ext_w = o_ref.shape[-1] + LANE
    m = _iota((n_bucket, ext_w), 1)
    bucket = jnp.clip(band - 1 - m, -rel_future, REL_PAST_CLIP) + rel_future
    sel = (bucket == _iota((n_bucket, ext_w), 0)).astype(BF16)
    ext = _dot_sel_r(rb_ref[...], sel)
    for q in range(CHUNK):
        o_ref[q] = ext[:, CHUNK - 1 - q:CHUNK - 1 - q + band]


def _bias_table(rel_bias, band):
    n_heads, n_bucket = rel_bias.shape
    out = pl.pallas_call(
        functools.partial(_bias_table_body, band=band, n_bucket=n_bucket, rel_future=CHUNK - 1),
        out_shape=jax.ShapeDtypeStruct((CHUNK, n_heads, band), F32),
        name="bias_table",
    )(rel_bias)
    return jnp.transpose(out, (1, 0, 2))


def _attn_body(q_ref, kp_ref, kc_ref, vp_ref, vc_ref, b_ref, o_ref, *, tq, band, scale):
    i = pl.program_id(2)
    kcat = jnp.concatenate([kp_ref[...], kc_ref[...]], axis=0)
    vcat = jnp.concatenate([vp_ref[...], vc_ref[...]], axis=0)
    bias = b_ref[0]
    col = _iota((CHUNK, band), 1)
    for j in range(tq // CHUNK):
        k0 = tq - (band - CHUNK) + j * CHUNK
        q = q_ref[j * CHUNK:(j + 1) * CHUNK, :]
        s = _dot_nt(q, kcat[k0:k0 + band, :]) * scale + bias
        valid = (i > 0) | (col + k0 >= tq)
        s = jnp.where(valid, s, -1e30)
        mx = jnp.max(s, axis=-1, keepdims=True)
        pexp = jnp.exp(s - mx)
        den = jnp.sum(pexp, axis=-1, keepdims=True)
        o = _dot(pexp.astype(BF16), vcat[k0:k0 + band, :])
        o_ref[j * CHUNK:(j + 1) * CHUNK, :] = (o / den).astype(o_ref.dtype)


def _band_attention(qkv, bias, batch, seq, n_heads, tq):
    n, three_d = qkv.shape
    d = three_d // 3
    dh = d // n_heads
    band = (LEFT_CHUNKS + 1) * CHUNK
    nt = seq // tq
    cur = lambda off: (lambda b, h, i: (b * nt + i, off * n_heads + h))
    prv = lambda off: (lambda b, h, i: (b * nt + jnp.maximum(i - 1, 0), off * n_heads + h))
    blk = lambda f: pl.BlockSpec((tq, dh), f)
    return pl.pallas_call(
        functools.partial(_attn_body, tq=tq, band=band, scale=dh ** -0.5),
        grid=(batch, n_heads, nt),
        in_specs=[blk(cur(0)), blk(prv(1)), blk(cur(1)), blk(prv(2)), blk(cur(2)),
                  pl.BlockSpec((1, CHUNK, band), lambda b, h, i: (h, 0, 0))],
        out_specs=pl.BlockSpec((tq, dh), lambda b, h, i: (b * nt + i, h)),
        out_shape=jax.ShapeDtypeStruct((n, d), BF16),
        compiler_params=_cparams(3),
        name="band_attention",
    )(qkv, qkv, qkv, qkv, qkv, bias)


def _pad_cols(a, width):
    return jnp.pad(a, ((0, 0), (0, width - a.shape[1])))


def _pad_rows(a, rows):
    return jnp.pad(a, ((0, rows - a.shape[0]), (0, 0)))


def _row(a):
    return a.reshape(1, -1)


def kernel(x, norm_g, w_in_ab, rwkv_mu, rwkv_w0, rwkv_w2, rwkv_a0, rwkv_a2, rwkv_g2, rwkv_k_k, rwkv_k_a, rwkv_r_k, rwkv_ln_w, rwkv_ln_b, ssm_conv_w, ssm_conv_b, ssm_dt_bias, ssm_A_log, ssm_D, ssm_norm_w, w_out_ab, w_qkv, attn_rel_bias, w_out_c, ffn_w_gate, ffn_w_up, ffn_w_down):
    batch, seq, d = x.shape
    n = batch * seq
    depth = norm_g.shape[0]
    xf = x.reshape(n, d)

    rw = rwkv_w0.shape[1]
    lora_w, lora_a, lora_g = rwkv_w2.shape[1], rwkv_a2.shape[1], rwkv_g2.shape[1]
    sw = ssm_norm_w.shape[1]
    n_ssm_heads = ssm_A_log.shape[1]
    conv_ch = ssm_conv_w.shape[2]
    groups = 2
    state = (conv_ch - sw) // (2 * groups)
    n_att_heads = attn_rel_bias.shape[1]
    rwkv_proj = 3 * rw + lora_w + lora_a + lora_g
    tm = min(1024, n)
    tc = min(256, seq)

    for l in range(depth):
        i = l // 2
        g = norm_g[l]
        if l % 2 == 0:
            w_in = w_in_ab[i]
            o1, o2, o3 = 3 * rw, 3 * rw + lora_w, 3 * rw + lora_w + lora_a
            seg = lambda a: [a[:, 0:o1], _pad_cols(a[:, o1:o2], LANE), _pad_cols(a[:, o2:o3], LANE), a[:, o3:rwkv_proj]]
            w_r = jnp.concatenate(seg(w_in), axis=1).astype(BF16)
            mu = jnp.concatenate(seg(_row(rwkv_mu[i])), axis=1)
            o4 = rwkv_proj + sw + conv_ch
            w_s = jnp.concatenate([w_in[:, rwkv_proj:o4], _pad_cols(w_in[:, o4:], LANE)], axis=1).astype(BF16)

            p_r = _norm_matmul(xf, g[0:1], w_r, F32, tm, w_r.shape[1] // 4)
            p_s = _norm_matmul(xf, g[0:1], w_s, F32, tm, w_s.shape[1] // 3)

            y_a = _rwkv_mix(
                p_r, mu, _row(rwkv_w0[i]), _pad_rows(rwkv_w2[i], LANE).astype(BF16), _row(rwkv_a0[i]),
                _pad_rows(rwkv_a2[i], LANE).astype(BF16), rwkv_g2[i].astype(BF16), _row(rwkv_k_k[i]),
                _row(rwkv_k_a[i]), _row(rwkv_r_k[i]), _row(rwkv_ln_w[i]), _row(rwkv_ln_b[i]), batch, seq, tc)
            y_b = _ssd_mix(
                p_s, ssm_conv_w[i], _row(ssm_conv_b[i]), _pad_cols(_row(ssm_dt_bias[i]), LANE),
                _pad_cols(_row(ssm_A_log[i]), LANE), _row(jnp.repeat(ssm_D[i], sw // n_ssm_heads)),
                _row(ssm_norm_w[i]), batch, seq, tc, sw, state, groups)
            w_o = w_out_ab[i].astype(BF16)
            xf = _proj_res([y_a, y_b], [w_o[:rw], w_o[rw:]], xf, g[1:2], tm // 2)
        else:
            qkv = _norm_matmul(xf, g[0:1], w_qkv[i].astype(BF16), BF16, tm, 1024)
            bias = _bias_table(attn_rel_bias[i], (LEFT_CHUNKS + 1) * CHUNK)
            att = _band_attention(qkv, bias, batch, seq, n_att_heads, min(512, seq))
            xf = _proj_res([att], [w_out_c[i].astype(BF16)], xf, g[1:2], tm // 2)
        xf = _ffn(xf, g[2:3], g[3:4], ffn_w_gate[l].astype(BF16), ffn_w_up[l].astype(BF16),
                  ffn_w_down[l].astype(BF16), tm // 2, 512)
    return xf.reshape(batch, seq, d)
```

```python
import functools
import math

import jax
import jax.numpy as jnp
from jax import lax
from jax.experimental import pallas as pl
from jax.experimental.pallas import tpu as pltpu

F32 = jnp.float32
BF16 = jnp.bfloat16

NORM_EPS = 1e-6
RWKV_LN_EPS = 64e-5
CHUNK = 64
HEAD = 64
LANE = 128
LEFT_CHUNKS = 8
REL_PAST_CLIP = 256
VMEM_LIMIT = 56 * 1024 * 1024


def _cparams(n_axes):
    return pltpu.CompilerParams(dimension_semantics=("arbitrary",) * n_axes, vmem_limit_bytes=VMEM_LIMIT)


def _dot(a, b):
    return jnp.dot(a, b, preferred_element_type=F32)


def _dot_nt(a, b):
    return lax.dot_general(a, b, (((1,), (1,)), ((), ())), preferred_element_type=F32)


def _dot_tn(a, b):
    return lax.dot_general(a, b, (((0,), (0,)), ((), ())), preferred_element_type=F32)


def _split3(t):
    hi = t.astype(BF16)
    r1 = t - hi.astype(F32)
    mid = r1.astype(BF16)
    lo = (r1 - mid.astype(F32)).astype(BF16)
    return hi, mid, lo


def _dot_sel_r(t, sel):
    hi, mid, lo = _split3(t)
    return _dot(hi, sel) + _dot(mid, sel) + _dot(lo, sel)


def _dot_sel_l(sel, t):
    hi, mid, lo = _split3(t)
    return _dot(sel, hi) + _dot(sel, mid) + _dot(sel, lo)


def _rms(x, g):
    ms = jnp.mean(x * x, axis=-1, keepdims=True)
    return x * lax.rsqrt(ms + NORM_EPS) * g


def _sigmoid(x):
    return 1.0 / (1.0 + jnp.exp(-x))


def _silu(x):
    return x * _sigmoid(x)


def _softplus(x):
    return jnp.maximum(x, 0.0) + jnp.log1p(jnp.exp(-jnp.abs(x)))


def _iota(shape, axis):
    return lax.broadcasted_iota(jnp.int32, shape, axis)


def _norm_matmul_body(x_ref, g_ref, w_ref, o_ref, h_ref):
    @pl.when(pl.program_id(1) == 0)
    def _():
        h_ref[...] = _rms(x_ref[...], g_ref[...]).astype(BF16)

    o_ref[...] = _dot(h_ref[...], w_ref[...]).astype(o_ref.dtype)


def _norm_matmul(x, g, w, out_dtype, tm, tn):
    n, d = x.shape
    m = w.shape[1]
    return pl.pallas_call(
        _norm_matmul_body,
        grid=(n // tm, m // tn),
        in_specs=[
            pl.BlockSpec((tm, d), lambda i, j: (i, 0)),
            pl.BlockSpec((1, d), lambda i, j: (0, 0)),
            pl.BlockSpec((d, tn), lambda i, j: (0, j)),
        ],
        out_specs=pl.BlockSpec((tm, tn), lambda i, j: (i, j)),
        out_shape=jax.ShapeDtypeStruct((n, m), out_dtype),
        scratch_shapes=[pltpu.VMEM((tm, d), BF16)],
        compiler_params=_cparams(2),
        name="norm_matmul",
    )(x, g, w)


def _proj_res_body(*refs, n_in):
    y_refs, w_refs = refs[:n_in], refs[n_in:2 * n_in]
    x_ref, g_ref, o_ref = refs[2 * n_in:]
    m = _dot(y_refs[0][...], w_refs[0][...])
    for y_ref, w_ref in zip(y_refs[1:], w_refs[1:]):
        m = m + _dot(y_ref[...], w_ref[...])
    o_ref[...] = x_ref[...] + _rms(m, g_ref[...])


def _proj_res(ys, ws, x, g, tm):
    n, d = x.shape
    n_in = len(ys)
    in_specs = [pl.BlockSpec((tm, y.shape[1]), lambda i: (i, 0)) for y in ys]
    in_specs += [pl.BlockSpec(w.shape, lambda i: (0, 0)) for w in ws]
    in_specs += [pl.BlockSpec((tm, d), lambda i: (i, 0)), pl.BlockSpec((1, d), lambda i: (0, 0))]
    return pl.pallas_call(
        functools.partial(_proj_res_body, n_in=n_in),
        grid=(n // tm,),
        in_specs=in_specs,
        out_specs=pl.BlockSpec((tm, d), lambda i: (i, 0)),
        out_shape=jax.ShapeDtypeStruct((n, d), F32),
        compiler_params=_cparams(1),
        name="proj_res",
    )(*ys, *ws, x, g)


def _ffn_body(x_ref, gi_ref, go_ref, wg_ref, wu_ref, wd_ref, o_ref, h_ref, acc_ref):
    j = pl.program_id(1)

    @pl.when(j == 0)
    def _():
        h_ref[...] = _rms(x_ref[...], gi_ref[...]).astype(BF16)

    h = h_ref[...]
    a = (_silu(_dot(h, wg_ref[...])) * _dot(h, wu_ref[...])).astype(BF16)
    part = _dot(a, wd_ref[...])

    @pl.when(j == 0)
    def _():
        acc_ref[...] = part

    @pl.when(j > 0)
    def _():
        acc_ref[...] += part

    @pl.when(j == pl.num_programs(1) - 1)
    def _():
        o_ref[...] = x_ref[...] + _rms(acc_ref[...], go_ref[...])


def _ffn(x, g_in, g_out, wg, wu, wd, tm, tf):
    n, d = x.shape
    f = wg.shape[1]
    return pl.pallas_call(
        _ffn_body,
        grid=(n // tm, f // tf),
        in_specs=[
            pl.BlockSpec((tm, d), lambda i, j: (i, 0)),
            pl.BlockSpec((1, d), lambda i, j: (0, 0)),
            pl.BlockSpec((1, d), lambda i, j: (0, 0)),
            pl.BlockSpec((d, tf), lambda i, j: (0, j)),
            pl.BlockSpec((d, tf), lambda i, j: (0, j)),
            pl.BlockSpec((tf, d), lambda i, j: (j, 0)),
        ],
        out_specs=pl.BlockSpec((tm, d), lambda i, j: (i, 0)),
        out_shape=jax.ShapeDtypeStruct((n, d), F32),
        scratch_shapes=[pltpu.VMEM((tm, d), BF16), pltpu.VMEM((tm, d), F32)],
        compiler_params=_cparams(2),
        name="ffn",
    )(x, g_in, g_out, wg, wu, wd)


def _seg_matrices(width, n_seg_pad):
    e = (_iota((width, n_seg_pad), 0) // HEAD == _iota((width, n_seg_pad), 1)).astype(BF16)
    et = (_iota((n_seg_pad, width), 1) // HEAD == _iota((n_seg_pad, width), 0)).astype(BF16)
    return e, et


def _blockdiag_tril(tc):
    r, c = _iota((tc, tc), 0), _iota((tc, tc), 1)
    return ((r // CHUNK == c // CHUNK) & (c <= r)).astype(BF16)


def _rwkv_body(p_ref, pp_ref, mu_ref, w0_ref, w2_ref, a0_ref, a2_ref, g2_ref, kk_ref, ka_ref, rk_ref,
               lnw_ref, lnb_ref, o_ref, s_ref, at_ref, rt_ref, bt_ref, kt_ref, v_ref, gl_ref, y_ref, *, tc, width):
    i = pl.program_id(1)
    n_heads = width // HEAD
    n_chunks = tc // CHUNK

    @pl.when(i == 0)
    def _():
        s_ref[...] = jnp.zeros_like(s_ref)

    p = p_ref[...]
    prev = jnp.where(i == 0, 0.0, pp_ref[7:8, :])
    psh = jnp.where(_iota((tc, 1), 0) == 0, prev, pltpu.roll(p, 1, 0))
    xr = p + (psh - p) * mu_ref[...]

    r = xr[:, 0:width]
    k = xr[:, width:2 * width]
    v = xr[:, 2 * width:3 * width]
    o1 = 3 * width
    wc = xr[:, o1:o1 + LANE]
    ac = xr[:, o1 + LANE:o1 + 2 * LANE]
    gc = xr[:, o1 + 2 * LANE:o1 + 4 * LANE]

    z = w0_ref[...] + _dot(jnp.tanh(wc).astype(BF16), w2_ref[...])
    lw = -jnp.exp(-_softplus(-z) - 0.5)
    a = _sigmoid(a0_ref[...] + _dot(ac.astype(BF16), a2_ref[...]))
    g = _dot(_sigmoid(gc).astype(BF16), g2_ref[...])

    e, et = _seg_matrices(width, LANE)

    def seg_sum(t):
        return _dot_sel_r(_dot_sel_r(t, e), et)

    kk = k * kk_ref[...]
    kk = kk * lax.rsqrt(jnp.maximum(seg_sum(kk * kk), 1e-24))
    k2 = k * (1.0 + (a - 1.0) * ka_ref[...])
    bonus = seg_sum(r * k2 * rk_ref[...]) * v

    cs = _dot_sel_l(_blockdiag_tril(tc), lw)
    ginv = jnp.exp(-cs)
    at_ref[...] = (-kk * jnp.exp(cs - lw)).astype(BF16)
    rt_ref[...] = (r * jnp.exp(cs)).astype(BF16)
    bt_ref[...] = (kk * a * ginv).astype(BF16)
    kt_ref[...] = (k2 * ginv).astype(BF16)
    v_ref[...] = v.astype(BF16)
    for c in range(n_chunks):
        last = c * CHUNK + CHUNK - 1
        gl_ref[c:c + 1, :] = jnp.exp(cs[last:last + 1, :])

    rr, cc = _iota((CHUNK, CHUNK), 0), _iota((CHUNK, CHUNK), 1)
    strict = (cc < rr).astype(F32)
    rr2, cc2 = _iota((CHUNK, 2 * CHUNK), 0), _iota((CHUNK, 2 * CHUNK), 1) % CHUNK
    strict2 = (cc2 < rr2).astype(F32)
    incl2 = (cc2 <= rr2).astype(F32)
    eye = (cc == rr).astype(F32)
    n_lvl = int(math.log2(CHUNK))
    lvl_masks = []
    for lv in range(n_lvl):
        same = (rr >> (lv + 1)) == (cc >> (lv + 1))
        lvl_masks.append((same & (((rr >> lv) & 1) == 1) & (((cc >> lv) & 1) == 0)).astype(F32))

    def chunk_step(c, carry):
        t0 = pl.multiple_of(c * CHUNK, CHUNK)
        rows = pl.ds(t0, CHUNK)
        gl = gl_ref[pl.ds(c, 1), :]
        heads = range(n_heads)
        hsl = [slice(h * HEAD, (h + 1) * HEAD) for h in heads]
        vh = [v_ref[rows, hs] for hs in hsl]
        ar = [jnp.concatenate([at_ref[rows, hs], rt_ref[rows, hs]], axis=0) for hs in hsl]
        bk = [jnp.concatenate([bt_ref[rows, hs], kt_ref[rows, hs]], axis=0) for hs in hsl]
        pm = [_dot_nt(ar[h], bk[h]) for h in heads]
        s0 = [s_ref[h] for h in heads]
        ars = [_dot_nt(ar[h], s0[h].astype(BF16)) for h in heads]
        pa = [pm[h][:CHUNK, :] * strict2 for h in heads]
        aab = [pa[h][:, :CHUNK] for h in heads]
        rhs = [ars[h][:CHUNK, :] + _dot(pa[h][:, CHUNK:].astype(BF16), vh[h]) for h in heads]
        x = [eye + aab[h] * lvl_masks[0] for h in heads]
        for lv in range(1, n_lvl):
            xb = [x[h].astype(BF16) for h in heads]
            tm_ = [_dot((aab[h] * lvl_masks[lv]).astype(BF16), xb[h]) for h in heads]
            x = [x[h] + _dot(xb[h], tm_[h].astype(BF16)) for h in heads]
        u = [_dot(x[h].astype(BF16), rhs[h].astype(BF16)) for h in heads]
        uv = [jnp.concatenate([u[h].astype(BF16), vh[h]], axis=0) for h in heads]
        for h in heads:
            pr = (pm[h][CHUNK:, :] * incl2).astype(BF16)
            y_ref[rows, hsl[h]] = ars[h][CHUNK:, :] + _dot(pr, uv[h])
        for h in heads:
            s_ref[h] = (s0[h] + _dot_tn(uv[h], bk[h])) * gl[:, hsl[h]]
        return carry

    lax.fori_loop(0, n_chunks, chunk_step, 0)

    y = y_ref[...]
    inv_n = 1.0 / HEAD
    mean = seg_sum(y) * inv_n
    yc = y - mean
    var = seg_sum(yc * yc) * inv_n
    yn = yc * lax.rsqrt(var + RWKV_LN_EPS) * lnw_ref[...] + lnb_ref[...]
    o_ref[...] = ((yn + bonus) * g).astype(o_ref.dtype)


def _rwkv_mix(p, mu, w0, w2, a0, a2, g2, k_k, k_a, r_k, ln_w, ln_b, batch, seq, tc):
    n, pw = p.shape
    width = w0.shape[1]
    n_heads = width // HEAD
    nt = seq // tc
    row_blk = lambda b, i: (b * nt + i, 0)
    prev_blk = lambda b, i: (jnp.maximum((b * seq + i * tc) // 8 - 1, 0), 0)
    const = lambda b, i: (0, 0)
    vec = lambda a: pl.BlockSpec(a.shape, const)
    return pl.pallas_call(
        functools.partial(_rwkv_body, tc=tc, width=width),
        grid=(batch, nt),
        in_specs=[pl.BlockSpec((tc, pw), row_blk), pl.BlockSpec((8, pw), prev_blk)]
        + [vec(a) for a in (mu, w0, w2, a0, a2, g2, k_k, k_a, r_k, ln_w, ln_b)],
        out_specs=pl.BlockSpec((tc, width), row_blk),
        out_shape=jax.ShapeDtypeStruct((n, width), BF16),
        scratch_shapes=[pltpu.VMEM((n_heads, HEAD, HEAD), F32)]
        + [pltpu.VMEM((tc, width), BF16) for _ in range(5)]
        + [pltpu.VMEM((max(tc // CHUNK, 8), width), F32), pltpu.VMEM((tc, width), F32)],
        compiler_params=_cparams(2),
        name="rwkv_mix",
    )(p, p, mu, w0, w2, a0, a2, g2, k_k, k_a, r_k, ln_w, ln_b)


def _ssd_body(p_ref, pp_ref, cw_ref, cb_ref, dtb_ref, alog_ref, dskip_ref, nw_ref, o_ref,
              st_ref, y_ref, *, tc, width, state, groups):
    i = pl.program_id(1)
    n_chunks = tc // CHUNK
    gw = width // groups
    hpg = gw // HEAD
    conv_ch = width + 2 * groups * state
    n_tap = cw_ref.shape[0]

    @pl.when(i == 0)
    def _():
        st_ref[...] = jnp.zeros_like(st_ref)

    p = p_ref[...]
    z = p[:, 0:width]
    xbc_raw = p[:, width:width + conv_ch]
    dt_raw = p[:, width + conv_ch:width + conv_ch + LANE]

    prev = jnp.where(i == 0, 0.0, pp_ref[:, width:width + conv_ch])
    xcat = jnp.concatenate([prev, xbc_raw], axis=0)
    conv = cb_ref[...]
    for j in range(n_tap):
        off = 8 - (n_tap - 1) + j
        conv = conv + cw_ref[j:j + 1, :] * xcat[off:off + tc, :]
    xbc = _silu(conv)
    xs = xbc[:, 0:width]
    bm = xbc[:, width:width + groups * state].astype(BF16)
    cm = xbc[:, width + groups * state:conv_ch].astype(BF16)

    e, et = _seg_matrices(width, LANE)
    dt = _softplus(dt_raw + dtb_ref[...])
    da = -jnp.exp(alog_ref[...]) * dt
    cs_h = _dot_sel_l(_blockdiag_tril(tc), da)
    cs = _dot_sel_r(cs_h, et)
    dtx = _dot_sel_r(dt, et)
    xc = xs * dtx
    ecs = jnp.exp(cs)

    rr, cc = _iota((CHUNK, CHUNK), 0), _iota((CHUNK, CHUNK), 1)
    causal = cc <= rr

    for c in range(n_chunks):
        r0 = c * CHUNK
        rows = slice(r0, r0 + CHUNK)
        cs_c = cs[rows, :]
        cs_last = cs_c[CHUNK - 1:CHUNK, :]
        xc_c = xc[rows, :]
        xd = (xc_c * jnp.exp(cs_last - cs_c)).astype(BF16)
        xcb = xc_c.astype(BF16)
        cs_t = cs_h[rows, :].T
        for g in range(groups):
            gl = slice(g * gw, (g + 1) * gw)
            b_g = bm[rows, g * state:(g + 1) * state]
            c_g = cm[rows, g * state:(g + 1) * state]
            cb = _dot_nt(c_g, b_g)
            st = st_ref[g]
            y_off = _dot(c_g, st.astype(BF16)) * ecs[rows, gl]
            for hh in range(hpg):
                h = g * hpg + hh
                hs = slice(h * HEAD, (h + 1) * HEAD)
                seg = cs_c[:, hs] - cs_t[h:h + 1, :]
                lmat = jnp.where(causal, jnp.exp(jnp.where(causal, seg, 0.0)), 0.0)
                y_ref[rows, hs] = _dot((cb * lmat).astype(BF16), xcb[:, hs]) + y_off[:, hh * HEAD:(hh + 1) * HEAD]
            st_ref[g] = st * ecs[r0 + CHUNK - 1:r0 + CHUNK, gl] + _dot_tn(b_g, xd[:, gl])

    y = (y_ref[...] + dskip_ref[...] * xs) * _silu(z)
    for g in range(groups):
        gl = slice(g * gw, (g + 1) * gw)
        yg = y[:, gl]
        ms = jnp.mean(yg * yg, axis=-1, keepdims=True)
        o_ref[:, gl] = (yg * lax.rsqrt(ms + NORM_EPS) * nw_ref[:, gl]).astype(o_ref.dtype)


def _ssd_mix(p, conv_w, conv_b, dt_bias, a_log, d_skip, norm_w, batch, seq, tc, width, state, groups):
    n, pw = p.shape
    nt = seq // tc
    row_blk = lambda b, i: (b * nt + i, 0)
    prev_blk = lambda b, i: (jnp.maximum((b * seq + i * tc) // 8 - 1, 0), 0)
    const = lambda b, i: (0, 0)
    vec = lambda a: pl.BlockSpec(a.shape, const)
    return pl.pallas_call(
        functools.partial(_ssd_body, tc=tc, width=width, state=state, groups=groups),
        grid=(batch, nt),
        in_specs=[pl.BlockSpec((tc, pw), row_blk), pl.BlockSpec((8, pw), prev_blk)]
        + [vec(a) for a in (conv_w, conv_b, dt_bias, a_log, d_skip, norm_w)],
        out_specs=pl.BlockSpec((tc, width), row_blk),
        out_shape=jax.ShapeDtypeStruct((n, width), BF16),
        scratch_shapes=[pltpu.VMEM((groups, state, width // groups), F32), pltpu.VMEM((tc, width), F32)],
        compiler_params=_cparams(2),
        name="ssd_mix",
    )(p, p, conv_w, conv_b, dt_bias, a_log, d_skip, norm_w)


def _bias_table_body(rb_ref, o_ref, *, band, n_bucket, rel_future):
    ext_w = o_ref.shape[-1] + LANE
    m = _iota((n_bucket, ext_w), 1)
    bucket = jnp.clip(band - 1 - m, -rel_future, REL_PAST_CLIP) + rel_future
    sel = (bucket == _iota((n_bucket, ext_w), 0)).astype(BF16)
    ext = _dot_sel_r(rb_ref[...], sel)
    for q in range(CHUNK):
        o_ref[q] = ext[:, CHUNK - 1 - q:CHUNK - 1 - q + band]


def _bias_table(rel_bias, band):
    n_heads, n_bucket = rel_bias.shape
    out = pl.pallas_call(
        functools.partial(_bias_table_body, band=band, n_bucket=n_bucket, rel_future=CHUNK - 1),
        out_shape=jax.ShapeDtypeStruct((CHUNK, n_heads, band), F32),
        name="bias_table",
    )(rel_bias)
    return jnp.transpose(out, (1, 0, 2))


def _attn_body(q_ref, kp_ref, kc_ref, vp_ref, vc_ref, b_ref, o_ref, *, tq, band, scale):
    i = pl.program_id(2)
    kcat = jnp.concatenate([kp_ref[...], kc_ref[...]], axis=0)
    vcat = jnp.concatenate([vp_ref[...], vc_ref[...]], axis=0)
    bias = b_ref[0]
    col = _iota((CHUNK, band), 1)
    for j in range(tq // CHUNK):
        k0 = tq - (band - CHUNK) + j * CHUNK
        q = q_ref[j * CHUNK:(j + 1) * CHUNK, :]
        s = _dot_nt(q, kcat[k0:k0 + band, :]) * scale + bias
        valid = (i > 0) | (col + k0 >= tq)
        s = jnp.where(valid, s, -1e30)
        mx = jnp.max(s, axis=-1, keepdims=True)
        pexp = jnp.exp(s - mx)
        den = jnp.sum(pexp, axis=-1, keepdims=True)
        o = _dot(pexp.astype(BF16), vcat[k0:k0 + band, :])
        o_ref[j * CHUNK:(j + 1) * CHUNK, :] = (o / den).astype(o_ref.dtype)


def _band_attention(qkv, bias, batch, seq, n_heads, tq):
    n, three_d = qkv.shape
    d = three_d // 3
    dh = d // n_heads
    band = (LEFT_CHUNKS + 1) * CHUNK
    nt = seq // tq
    cur = lambda off: (lambda b, h, i: (b * nt + i, off * n_heads + h))
    prv = lambda off: (lambda b, h, i: (b * nt + jnp.maximum(i - 1, 0), off * n_heads + h))
    blk = lambda f: pl.BlockSpec((tq, dh), f)
    return pl.pallas_call(
        functools.partial(_attn_body, tq=tq, band=band, scale=dh ** -0.5),
        grid=(batch, n_heads, nt),
        in_specs=[blk(cur(0)), blk(prv(1)), blk(cur(1)), blk(prv(2)), blk(cur(2)),
                  pl.BlockSpec((1, CHUNK, band), lambda b, h, i: (h, 0, 0))],
        out_specs=pl.BlockSpec((tq, dh), lambda b, h, i: (b * nt + i, h)),
        out_shape=jax.ShapeDtypeStruct((n, d), BF16),
        compiler_params=_cparams(3),
        name="band_attention",
    )(qkv, qkv, qkv, qkv, qkv, bias)


def _pad_cols(a, width):
    return jnp.pad(a, ((0, 0), (0, width - a.shape[1])))


def _pad_rows(a, rows):
    return jnp.pad(a, ((0, rows - a.shape[0]), (0, 0)))


def _row(a):
    return a.reshape(1, -1)


def kernel(x, norm_g, w_in_ab, rwkv_mu, rwkv_w0, rwkv_w2, rwkv_a0, rwkv_a2, rwkv_g2, rwkv_k_k, rwkv_k_a, rwkv_r_k, rwkv_ln_w, rwkv_ln_b, ssm_conv_w, ssm_conv_b, ssm_dt_bias, ssm_A_log, ssm_D, ssm_norm_w, w_out_ab, w_qkv, attn_rel_bias, w_out_c, ffn_w_gate, ffn_w_up, ffn_w_down):
    batch, seq, d = x.shape
    n = batch * seq
    depth = norm_g.shape[0]
    xf = x.reshape(n, d)

    rw = rwkv_w0.shape[1]
    lora_w, lora_a, lora_g = rwkv_w2.shape[1], rwkv_a2.shape[1], rwkv_g2.shape[1]
    sw = ssm_norm_w.shape[1]
    n_ssm_heads = ssm_A_log.shape[1]
    conv_ch = ssm_conv_w.shape[2]
    groups = 2
    state = (conv_ch - sw) // (2 * groups)
    n_att_heads = attn_rel_bias.shape[1]
    rwkv_proj = 3 * rw + lora_w + lora_a + lora_g
    tm = min(1024, n)
    tc = min(256, seq)

    for l in range(depth):
        i = l // 2
        g = norm_g[l]
        if l % 2 == 0:
            w_in = w_in_ab[i]
            o1, o2, o3 = 3 * rw, 3 * rw + lora_w, 3 * rw + lora_w + lora_a
            seg = lambda a: [a[:, 0:o1], _pad_cols(a[:, o1:o2], LANE), _pad_cols(a[:, o2:o3], LANE), a[:, o3:rwkv_proj]]
            w_r = jnp.concatenate(seg(w_in), axis=1).astype(BF16)
            mu = jnp.concatenate(seg(_row(rwkv_mu[i])), axis=1)
            o4 = rwkv_proj + sw + conv_ch
            w_s = jnp.concatenate([w_in[:, rwkv_proj:o4], _pad_cols(w_in[:, o4:], LANE)], axis=1).astype(BF16)

            p_r = _norm_matmul(xf, g[0:1], w_r, F32, tm, w_r.shape[1] // 4)
            p_s = _norm_matmul(xf, g[0:1], w_s, F32, tm, w_s.shape[1] // 3)

            y_a = _rwkv_mix(
                p_r, mu, _row(rwkv_w0[i]), _pad_rows(rwkv_w2[i], LANE).astype(BF16), _row(rwkv_a0[i]),
                _pad_rows(rwkv_a2[i], LANE).astype(BF16), rwkv_g2[i].astype(BF16), _row(rwkv_k_k[i]),
                _row(rwkv_k_a[i]), _row(rwkv_r_k[i]), _row(rwkv_ln_w[i]), _row(rwkv_ln_b[i]), batch, seq, tc)
            y_b = _ssd_mix(
                p_s, ssm_conv_w[i], _row(ssm_conv_b[i]), _pad_cols(_row(ssm_dt_bias[i]), LANE),
                _pad_cols(_row(ssm_A_log[i]), LANE), _row(jnp.repeat(ssm_D[i], sw // n_ssm_heads)),
                _row(ssm_norm_w[i]), batch, seq, tc, sw, state, groups)
            w_o = w_out_ab[i].astype(BF16)
            xf = _proj_res([y_a, y_b], [w_o[:rw], w_o[rw:]], xf, g[1:2], tm // 2)
        else:
            qkv = _norm_matmul(xf, g[0:1], w_qkv[i].astype(BF16), BF16, tm, 1024)
            bias = _bias_table(attn_rel_bias[i], (LEFT_CHUNKS + 1) * CHUNK)
            att = _band_attention(qkv, bias, batch, seq, n_att_heads, min(512, seq))
            xf = _proj_res([att], [w_out_c[i].astype(BF16)], xf, g[1:2], tm // 2)
        xf = _ffn(xf, g[2:3], g[3:4], ffn_w_gate[l].astype(BF16), ffn_w_up[l].astype(BF16),
                  ffn_w_down[l].astype(BF16), tm // 2, 512)
    return xf.reshape(batch, seq, d)
```

```python
import functools
import math

import jax
import jax.numpy as jnp
from jax import lax
from jax.experimental import pallas as pl
from jax.experimental.pallas import tpu as pltpu

F32 = jnp.float32
BF16 = jnp.bfloat16

NORM_EPS = 1e-6
RWKV_LN_EPS = 64e-5
CHUNK = 64
HEAD = 64
LANE = 128
LEFT_CHUNKS = 8
REL_PAST_CLIP = 256
VMEM_LIMIT = 56 * 1024 * 1024


def _cparams(n_axes):
    return pltpu.CompilerParams(dimension_semantics=("arbitrary",) * n_axes, vmem_limit_bytes=VMEM_LIMIT)


def _dot(a, b):
    return jnp.dot(a, b, preferred_element_type=F32)


def _dot_nt(a, b):
    return lax.dot_general(a, b, (((1,), (1,)), ((), ())), preferred_element_type=F32)


def _dot_tn(a, b):
    return lax.dot_general(a, b, (((0,), (0,)), ((), ())), preferred_element_type=F32)


def _split3(t):
    hi = t.astype(BF16)
    r1 = t - hi.astype(F32)
    mid = r1.astype(BF16)
    lo = (r1 - mid.astype(F32)).astype(BF16)
    return hi, mid, lo


def _dot_sel_r(t, sel):
    hi, mid, lo = _split3(t)
    return _dot(hi, sel) + _dot(mid, sel) + _dot(lo, sel)


def _dot_sel_l(sel, t):
    hi, mid, lo = _split3(t)
    return _dot(sel, hi) + _dot(sel, mid) + _dot(sel, lo)


def _rms(x, g):
    ms = jnp.mean(x * x, axis=-1, keepdims=True)
    return x * lax.rsqrt(ms + NORM_EPS) * g


def _sigmoid(x):
    return 1.0 / (1.0 + jnp.exp(-x))


def _silu(x):
    return x * _sigmoid(x)


def _softplus(x):
    return jnp.maximum(x, 0.0) + jnp.log1p(jnp.exp(-jnp.abs(x)))


def _iota(shape, axis):
    return lax.broadcasted_iota(jnp.int32, shape, axis)


def _norm_matmul_body(x_ref, g_ref, w_ref, o_ref, h_ref):
    @pl.when(pl.program_id(1) == 0)
    def _():
        h_ref[...] = _rms(x_ref[...], g_ref[...]).astype(BF16)

    o_ref[...] = _dot(h_ref[...], w_ref[...]).astype(o_ref.dtype)


def _norm_matmul(x, g, w, out_dtype, tm, tn):
    n, d = x.shape
    m = w.shape[1]
    return pl.pallas_call(
        _norm_matmul_body,
        grid=(n // tm, m // tn),
        in_specs=[
            pl.BlockSpec((tm, d), lambda i, j: (i, 0)),
            pl.BlockSpec((1, d), lambda i, j: (0, 0)),
            pl.BlockSpec((d, tn), lambda i, j: (0, j)),
        ],
        out_specs=pl.BlockSpec((tm, tn), lambda i, j: (i, j)),
        out_shape=jax.ShapeDtypeStruct((n, m), out_dtype),
        scratch_shapes=[pltpu.VMEM((tm, d), BF16)],
        compiler_params=_cparams(2),
        name="norm_matmul",
    )(x, g, w)


def _proj_res_body(*refs, n_in):
    y_refs, w_refs = refs[:n_in], refs[n_in:2 * n_in]
    x_ref, g_ref, o_ref = refs[2 * n_in:]
    m = _dot(y_refs[0][...], w_refs[0][...])
    for y_ref, w_ref in zip(y_refs[1:], w_refs[1:]):
        m = m + _dot(y_ref[...], w_ref[...])
    o_ref[...] = x_ref[...] + _rms(m, g_ref[...])


def _proj_res(ys, ws, x, g, tm):
    n, d = x.shape
    n_in = len(ys)
    in_specs = [pl.BlockSpec((tm, y.shape[1]), lambda i: (i, 0)) for y in ys]
    in_specs += [pl.BlockSpec(w.shape, lambda i: (0, 0)) for w in ws]
    in_specs += [pl.BlockSpec((tm, d), lambda i: (i, 0)), pl.BlockSpec((1, d), lambda i: (0, 0))]
    return pl.pallas_call(
        functools.partial(_proj_res_body, n_in=n_in),
        grid=(n // tm,),
        in_specs=in_specs,
        out_specs=pl.BlockSpec((tm, d), lambda i: (i, 0)),
        out_shape=jax.ShapeDtypeStruct((n, d), F32),
        compiler_params=_cparams(1),
        name="proj_res",
    )(*ys, *ws, x, g)


def _ffn_body(x_ref, gi_ref, go_ref, wg_ref, wu_ref, wd_ref, o_ref, h_ref, acc_ref):
    j = pl.program_id(1)

    @pl.when(j == 0)
    def _():
        h_ref[...] = _rms(x_ref[...], gi_ref[...]).astype(BF16)
        acc_ref[...] = jnp.zeros_like(acc_ref)

    h = h_ref[...]
    a = (_silu(_dot(h, wg_ref[...])) * _dot(h, wu_ref[...])).astype(BF16)
    acc_ref[...] += _dot(a, wd_ref[...])

    @pl.when(j == pl.num_programs(1) - 1)
    def _():
        o_ref[...] = x_ref[...] + _rms(acc_ref[...], go_ref[...])


def _ffn(x, g_in, g_out, wg, wu, wd, tm, tf):
    n, d = x.shape
    f = wg.shape[1]
    return pl.pallas_call(
        _ffn_body,
        grid=(n // tm, f // tf),
        in_specs=[
            pl.BlockSpec((tm, d), lambda i, j: (i, 0), pipeline_mode=pl.Buffered(1)),
            pl.BlockSpec((1, d), lambda i, j: (0, 0)),
            pl.BlockSpec((1, d), lambda i, j: (0, 0)),
            pl.BlockSpec((d, tf), lambda i, j: (0, j)),
            pl.BlockSpec((d, tf), lambda i, j: (0, j)),
            pl.BlockSpec((tf, d), lambda i, j: (j, 0)),
        ],
        out_specs=pl.BlockSpec((tm, d), lambda i, j: (i, 0), pipeline_mode=pl.Buffered(1)),
        out_shape=jax.ShapeDtypeStruct((n, d), F32),
        scratch_shapes=[pltpu.VMEM((tm, d), BF16), pltpu.VMEM((tm, d), F32)],
        compiler_params=_cparams(2),
        name="ffn",
    )(x, g_in, g_out, wg, wu, wd)


def _seg_matrices(width, n_seg_pad):
    e = (_iota((width, n_seg_pad), 0) // HEAD == _iota((width, n_seg_pad), 1)).astype(BF16)
    et = (_iota((n_seg_pad, width), 1) // HEAD == _iota((n_seg_pad, width), 0)).astype(BF16)
    return e, et


def _blockdiag_tril(tc):
    r, c = _iota((tc, tc), 0), _iota((tc, tc), 1)
    return ((r // CHUNK == c // CHUNK) & (c <= r)).astype(BF16)


def _rwkv_body(p_ref, pp_ref, mu_ref, w0_ref, w2_ref, a0_ref, a2_ref, g2_ref, kk_ref, ka_ref, rk_ref,
               lnw_ref, lnb_ref, o_ref, s_ref, at_ref, rt_ref, bt_ref, kt_ref, v_ref, gl_ref, y_ref, *, tc, width):
    i = pl.program_id(1)
    n_heads = width // HEAD
    n_chunks = tc // CHUNK

    @pl.when(i == 0)
    def _():
        s_ref[...] = jnp.zeros_like(s_ref)

    p = p_ref[...]
    prev = jnp.where(i == 0, 0.0, pp_ref[7:8, :])
    psh = jnp.where(_iota((tc, 1), 0) == 0, prev, pltpu.roll(p, 1, 0))
    xr = p + (psh - p) * mu_ref[...]

    r = xr[:, 0:width]
    k = xr[:, width:2 * width]
    v = xr[:, 2 * width:3 * width]
    o1 = 3 * width
    wc = xr[:, o1:o1 + LANE]
    ac = xr[:, o1 + LANE:o1 + 2 * LANE]
    gc = xr[:, o1 + 2 * LANE:o1 + 4 * LANE]

    z = w0_ref[...] + _dot(jnp.tanh(wc).astype(BF16), w2_ref[...])
    lw = -jnp.exp(-_softplus(-z) - 0.5)
    a = _sigmoid(a0_ref[...] + _dot(ac.astype(BF16), a2_ref[...]))
    g = _dot(_sigmoid(gc).astype(BF16), g2_ref[...])

    e, et = _seg_matrices(width, LANE)

    def seg_sum(t):
        return _dot_sel_r(_dot_sel_r(t, e), et)

    kk = k * kk_ref[...]
    kk = kk * lax.rsqrt(jnp.maximum(seg_sum(kk * kk), 1e-24))
    k2 = k * (1.0 + (a - 1.0) * ka_ref[...])
    bonus = seg_sum(r * k2 * rk_ref[...]) * v

    cs = _dot_sel_l(_blockdiag_tril(tc), lw)
    ginv = jnp.exp(-cs)
    at_ref[...] = (-kk * jnp.exp(cs - lw)).astype(BF16)
    rt_ref[...] = (r * jnp.exp(cs)).astype(BF16)
    bt_ref[...] = (kk * a * ginv).astype(BF16)
    kt_ref[...] = (k2 * ginv).astype(BF16)
    v_ref[...] = v.astype(BF16)
    for c in range(n_chunks):
        last = c * CHUNK + CHUNK - 1
        gl_ref[c:c + 1, :] = jnp.exp(cs[last:last + 1, :])

    rr, cc = _iota((CHUNK, CHUNK), 0), _iota((CHUNK, CHUNK), 1)
    strict = (cc < rr).astype(F32)
    rr2, cc2 = _iota((CHUNK, 2 * CHUNK), 0), _iota((CHUNK, 2 * CHUNK), 1) % CHUNK
    strict2 = (cc2 < rr2).astype(F32)
    incl2 = (cc2 <= rr2).astype(F32)
    eye = (cc == rr).astype(F32)
    n_lvl = int(math.log2(CHUNK))
    lvl_masks = []
    for lv in range(n_lvl):
        same = (rr >> (lv + 1)) == (cc >> (lv + 1))
        lvl_masks.append((same & (((rr >> lv) & 1) == 1) & (((cc >> lv) & 1) == 0)).astype(F32))

    def chunk_step(c, carry):
        t0 = pl.multiple_of(c * CHUNK, CHUNK)
        rows = pl.ds(t0, CHUNK)
        gl = gl_ref[pl.ds(c, 1), :]
        heads = range(n_heads)
        hsl = [slice(h * HEAD, (h + 1) * HEAD) for h in heads]
        vh = [v_ref[rows, hs] for hs in hsl]
        ar = [jnp.concatenate([at_ref[rows, hs], rt_ref[rows, hs]], axis=0) for hs in hsl]
        bk = [jnp.concatenate([bt_ref[rows, hs], kt_ref[rows, hs]], axis=0) for hs in hsl]
        pm = [_dot_nt(ar[h], bk[h]) for h in heads]
        s0 = [s_ref[h] for h in heads]
        ars = [_dot_nt(ar[h], s0[h].astype(BF16)) for h in heads]
        pa = [pm[h][:CHUNK, :] * strict2 for h in heads]
        aab = [pa[h][:, :CHUNK] for h in heads]
        rhs = [ars[h][:CHUNK, :] + _dot(pa[h][:, CHUNK:].astype(BF16), vh[h]) for h in heads]
        x = [eye + aab[h] * lvl_masks[0] for h in heads]
        for lv in range(1, n_lvl):
            xb = [x[h].astype(BF16) for h in heads]
            tm_ = [_dot((aab[h] * lvl_masks[lv]).astype(BF16), xb[h]) for h in heads]
            x = [x[h] + _dot(xb[h], tm_[h].astype(BF16)) for h in heads]
        u = [_dot(x[h].astype(BF16), rhs[h].astype(BF16)) for h in heads]
        uv = [jnp.concatenate([u[h].astype(BF16), vh[h]], axis=0) for h in heads]
        for h in heads:
            pr = (pm[h][CHUNK:, :] * incl2).astype(BF16)
            y_ref[rows, hsl[h]] = ars[h][CHUNK:, :] + _dot(pr, uv[h])
        for h in heads:
            s_ref[h] = (s0[h] + _dot_tn(uv[h], bk[h])) * gl[:, hsl[h]]
        return carry

    lax.fori_loop(0, n_chunks, chunk_step, 0)

    y = y_ref[...]
    inv_n = 1.0 / HEAD
    mean = seg_sum(y) * inv_n
    yc = y - mean
    var = seg_sum(yc * yc) * inv_n
    yn = yc * lax.rsqrt(var + RWKV_LN_EPS) * lnw_ref[...] + lnb_ref[...]
    o_ref[...] = ((yn + bonus) * g).astype(o_ref.dtype)


def _rwkv_mix(p, mu, w0, w2, a0, a2, g2, k_k, k_a, r_k, ln_w, ln_b, batch, seq, tc):
    n, pw = p.shape
    width = w0.shape[1]
    n_heads = width // HEAD
    nt = seq // tc
    row_blk = lambda b, i: (b * nt + i, 0)
    prev_blk = lambda b, i: (jnp.maximum((b * seq + i * tc) // 8 - 1, 0), 0)
    const = lambda b, i: (0, 0)
    vec = lambda a: pl.BlockSpec(a.shape, const)
    return pl.pallas_call(
        functools.partial(_rwkv_body, tc=tc, width=width),
        grid=(batch, nt),
        in_specs=[pl.BlockSpec((tc, pw), row_blk), pl.BlockSpec((8, pw), prev_blk)]
        + [vec(a) for a in (mu, w0, w2, a0, a2, g2, k_k, k_a, r_k, ln_w, ln_b)],
        out_specs=pl.BlockSpec((tc, width), row_blk),
        out_shape=jax.ShapeDtypeStruct((n, width), BF16),
        scratch_shapes=[pltpu.VMEM((n_heads, HEAD, HEAD), F32)]
        + [pltpu.VMEM((tc, width), BF16) for _ in range(5)]
        + [pltpu.VMEM((max(tc // CHUNK, 8), width), F32), pltpu.VMEM((tc, width), F32)],
        compiler_params=_cparams(2),
        name="rwkv_mix",
    )(p, p, mu, w0, w2, a0, a2, g2, k_k, k_a, r_k, ln_w, ln_b)


def _ssd_body(p_ref, pp_ref, cw_ref, cb_ref, dtb_ref, alog_ref, dskip_ref, nw_ref, o_ref,
              st_ref, y_ref, *, tc, width, state, groups):
    i = pl.program_id(1)
    n_chunks = tc // CHUNK
    gw = width // groups
    hpg = gw // HEAD
    conv_ch = width + 2 * groups * state
    n_tap = cw_ref.shape[0]

    @pl.when(i == 0)
    def _():
        st_ref[...] = jnp.zeros_like(st_ref)

    p = p_ref[...]
    z = p[:, 0:width]
    xbc_raw = p[:, width:width + conv_ch]
    dt_raw = p[:, width + conv_ch:width + conv_ch + LANE]

    prev = jnp.where(i == 0, 0.0, pp_ref[:, width:width + conv_ch])
    xcat = jnp.concatenate([prev, xbc_raw], axis=0)
    conv = cb_ref[...]
    for j in range(n_tap):
        off = 8 - (n_tap - 1) + j
        conv = conv + cw_ref[j:j + 1, :] * xcat[off:off + tc, :]
    xbc = _silu(conv)
    xs = xbc[:, 0:width]
    bm = xbc[:, width:width + groups * state].astype(BF16)
    cm = xbc[:, width + groups * state:conv_ch].astype(BF16)

    e, et = _seg_matrices(width, LANE)
    dt = _softplus(dt_raw + dtb_ref[...])
    da = -jnp.exp(alog_ref[...]) * dt
    cs_h = _dot_sel_l(_blockdiag_tril(tc), da)
    cs = _dot_sel_r(cs_h, et)
    dtx = _dot_sel_r(dt, et)
    xc = xs * dtx
    ecs = jnp.exp(cs)

    rr, cc = _iota((CHUNK, CHUNK), 0), _iota((CHUNK, CHUNK), 1)
    causal = cc <= rr

    for c in range(n_chunks):
        r0 = c * CHUNK
        rows = slice(r0, r0 + CHUNK)
        cs_c = cs[rows, :]
        cs_last = cs_c[CHUNK - 1:CHUNK, :]
        xc_c = xc[rows, :]
        xd = (xc_c * jnp.exp(cs_last - cs_c)).astype(BF16)
        xcb = xc_c.astype(BF16)
        cs_t = cs_h[rows, :].T
        for g in range(groups):
            gl = slice(g * gw, (g + 1) * gw)
            b_g = bm[rows, g * state:(g + 1) * state]
            c_g = cm[rows, g * state:(g + 1) * state]
            cb = _dot_nt(c_g, b_g)
            st = st_ref[g]
            y_off = _dot(c_g, st.astype(BF16)) * ecs[rows, gl]
            for hh in range(hpg):
                h = g * hpg + hh
                hs = slice(h * HEAD, (h + 1) * HEAD)
                seg = cs_c[:, hs] - cs_t[h:h + 1, :]
                lmat = jnp.where(causal, jnp.exp(jnp.where(causal, seg, 0.0)), 0.0)
                y_ref[rows, hs] = _dot((cb * lmat).astype(BF16), xcb[:, hs]) + y_off[:, hh * HEAD:(hh + 1) * HEAD]
            st_ref[g] = st * ecs[r0 + CHUNK - 1:r0 + CHUNK, gl] + _dot_tn(b_g, xd[:, gl])

    y = (y_ref[...] + dskip_ref[...] * xs) * _silu(z)
    for g in range(groups):
        gl = slice(g * gw, (g + 1) * gw)
        yg = y[:, gl]
        ms = jnp.mean(yg * yg, axis=-1, keepdims=True)
        o_ref[:, gl] = (yg * lax.rsqrt(ms + NORM_EPS) * nw_ref[:, gl]).astype(o_ref.dtype)


def _ssd_mix(p, conv_w, conv_b, dt_bias, a_log, d_skip, norm_w, batch, seq, tc, width, state, groups):
    n, pw = p.shape
    nt = seq // tc
    row_blk = lambda b, i: (b * nt + i, 0)
    prev_blk = lambda b, i: (jnp.maximum((b * seq + i * tc) // 8 - 1, 0), 0)
    const = lambda b, i: (0, 0)
    vec = lambda a: pl.BlockSpec(a.shape, const)
    return pl.pallas_call(
        functools.partial(_ssd_body, tc=tc, width=width, state=state, groups=groups),
        grid=(batch, nt),
        in_specs=[pl.BlockSpec((tc, pw), row_blk), pl.BlockSpec((8, pw), prev_blk)]
        + [vec(a) for a in (conv_w, conv_b, dt_bias, a_log, d_skip, norm_w)],
        out_specs=pl.BlockSpec((tc, width), row_blk),
        out_shape=jax.ShapeDtypeStruct((n, width), BF16),
        scratch_shapes=[pltpu.VMEM((groups, state, width // groups), F32), pltpu.VMEM((tc, width), F32)],
        compiler_params=_cparams(2),
        name="ssd_mix",
    )(p, p, conv_w, conv_b, dt_bias, a_log, d_skip, norm_w)


PAIR = 2 * CHUNK
BAND = (LEFT_CHUNKS + 1) * CHUNK
PAIR_BAND = BAND + CHUNK


def _bias_table_body(rb_ref, o_ref, *, n_bucket, rel_future):
    ext_w = PAIR_BAND + LANE
    m = _iota((n_bucket, ext_w), 1)
    bucket = jnp.clip(PAIR_BAND - 1 - m, -rel_future, REL_PAST_CLIP) + rel_future
    sel = (bucket == _iota((n_bucket, ext_w), 0)).astype(BF16)
    ext = _dot_sel_r(rb_ref[...], sel)
    for r in range(PAIR):
        o_ref[r] = ext[:, PAIR - 1 - r:PAIR - 1 - r + PAIR_BAND]


def _bias_table(rel_bias):
    n_heads, n_bucket = rel_bias.shape
    out = pl.pallas_call(
        functools.partial(_bias_table_body, n_bucket=n_bucket, rel_future=CHUNK - 1),
        out_shape=jax.ShapeDtypeStruct((PAIR, n_heads, PAIR_BAND), F32),
        name="bias_table",
    )(rel_bias)
    return jnp.transpose(out, (1, 0, 2))


def _attn_body(q_ref, kp_ref, kc_ref, vp_ref, vc_ref, b_ref, o_ref, *, tq, scale):
    i = pl.program_id(2)
    bias = b_ref[0]
    row, col = _iota((PAIR, PAIR_BAND), 0), _iota((PAIR, PAIR_BAND), 1)
    lo = (row // CHUNK) * CHUNK
    in_band = (col >= lo) & (col < lo + BAND)
    pairs = range(tq // PAIR)
    k0 = [tq - LEFT_CHUNKS * CHUNK + jp * PAIR for jp in pairs]

    def attend(first_block):
        kcat = jnp.concatenate([kp_ref[...], kc_ref[...]], axis=0)
        vcat = jnp.concatenate([vp_ref[...], vc_ref[...]], axis=0)
        s = [_dot_nt(q_ref[jp * PAIR:(jp + 1) * PAIR, :], kcat[k0[jp]:k0[jp] + PAIR_BAND, :]) * scale + bias
             for jp in pairs]
        if first_block:
            valid = [in_band & (col + k0[jp] >= tq) for jp in pairs]
        else:
            valid = [in_band for jp in pairs]
        s = [jnp.where(valid[jp], s[jp], -1e30) for jp in pairs]
        mx = [jnp.max(s[jp], axis=-1, keepdims=True) for jp in pairs]
        pexp = [jnp.exp(s[jp] - mx[jp]) for jp in pairs]
        den = [jnp.sum(pexp[jp], axis=-1, keepdims=True) for jp in pairs]
        o = [_dot(pexp[jp].astype(BF16), vcat[k0[jp]:k0[jp] + PAIR_BAND, :]) for jp in pairs]
        for jp in pairs:
            o_ref[jp * PAIR:(jp + 1) * PAIR, :] = (o[jp] / den[jp]).astype(o_ref.dtype)

    @pl.when(i == 0)
    def _():
        attend(True)

    @pl.when(i > 0)
    def _():
        attend(False)


def _band_attention(qkv, bias, batch, seq, n_heads, tq):
    n, three_d = qkv.shape
    d = three_d // 3
    dh = d // n_heads
    nt = seq // tq
    cur = lambda off: (lambda b, h, i: (b * nt + i, off * n_heads + h))
    prv = lambda off: (lambda b, h, i: (b * nt + jnp.maximum(i - 1, 0), off * n_heads + h))
    blk = lambda f: pl.BlockSpec((tq, dh), f)
    return pl.pallas_call(
        functools.partial(_attn_body, tq=tq, scale=dh ** -0.5),
        grid=(batch, n_heads, nt),
        in_specs=[blk(cur(0)), blk(prv(1)), blk(cur(1)), blk(prv(2)), blk(cur(2)),
                  pl.BlockSpec((1, PAIR, PAIR_BAND), lambda b, h, i: (h, 0, 0))],
        out_specs=pl.BlockSpec((tq, dh), lambda b, h, i: (b * nt + i, h)),
        out_shape=jax.ShapeDtypeStruct((n, d), BF16),
        compiler_params=_cparams(3),
        name="band_attention",
    )(qkv, qkv, qkv, qkv, qkv, bias)


def _pad_cols(a, width):
    return jnp.pad(a, ((0, 0), (0, width - a.shape[1])))


def _pad_rows(a, rows):
    return jnp.pad(a, ((0, rows - a.shape[0]), (0, 0)))


def _row(a):
    return a.reshape(1, -1)


def kernel(x, norm_g, w_in_ab, rwkv_mu, rwkv_w0, rwkv_w2, rwkv_a0, rwkv_a2, rwkv_g2, rwkv_k_k, rwkv_k_a, rwkv_r_k, rwkv_ln_w, rwkv_ln_b, ssm_conv_w, ssm_conv_b, ssm_dt_bias, ssm_A_log, ssm_D, ssm_norm_w, w_out_ab, w_qkv, attn_rel_bias, w_out_c, ffn_w_gate, ffn_w_up, ffn_w_down):
    batch, seq, d = x.shape
    n = batch * seq
    depth = norm_g.shape[0]
    xf = x.reshape(n, d)

    rw = rwkv_w0.shape[1]
    lora_w, lora_a, lora_g = rwkv_w2.shape[1], rwkv_a2.shape[1], rwkv_g2.shape[1]
    sw = ssm_norm_w.shape[1]
    n_ssm_heads = ssm_A_log.shape[1]
    conv_ch = ssm_conv_w.shape[2]
    groups = 2
    state = (conv_ch - sw) // (2 * groups)
    n_att_heads = attn_rel_bias.shape[1]
    rwkv_proj = 3 * rw + lora_w + lora_a + lora_g
    tm = min(1024, n)
    tc = min(256, seq)

    for l in range(depth):
        i = l // 2
        g = norm_g[l]
        if l % 2 == 0:
            w_in = w_in_ab[i]
            o1, o2, o3 = 3 * rw, 3 * rw + lora_w, 3 * rw + lora_w + lora_a
            seg = lambda a: [a[:, 0:o1], _pad_cols(a[:, o1:o2], LANE), _pad_cols(a[:, o2:o3], LANE), a[:, o3:rwkv_proj]]
            w_r = jnp.concatenate(seg(w_in), axis=1).astype(BF16)
            mu = jnp.concatenate(seg(_row(rwkv_mu[i])), axis=1)
            o4 = rwkv_proj + sw + conv_ch
            w_s = jnp.concatenate([w_in[:, rwkv_proj:o4], _pad_cols(w_in[:, o4:], LANE)], axis=1).astype(BF16)

            p_r = _norm_matmul(xf, g[0:1], w_r, F32, tm, w_r.shape[1] // 4)
            p_s = _norm_matmul(xf, g[0:1], w_s, F32, tm, w_s.shape[1] // 3)

            y_a = _rwkv_mix(
                p_r, mu, _row(rwkv_w0[i]), _pad_rows(rwkv_w2[i], LANE).astype(BF16), _row(rwkv_a0[i]),
                _pad_rows(rwkv_a2[i], LANE).astype(BF16), rwkv_g2[i].astype(BF16), _row(rwkv_k_k[i]),
                _row(rwkv_k_a[i]), _row(rwkv_r_k[i]), _row(rwkv_ln_w[i]), _row(rwkv_ln_b[i]), batch, seq, tc)
            y_b = _ssd_mix(
                p_s, ssm_conv_w[i], _row(ssm_conv_b[i]), _pad_cols(_row(ssm_dt_bias[i]), LANE),
                _pad_cols(_row(ssm_A_log[i]), LANE), _row(jnp.repeat(ssm_D[i], sw // n_ssm_heads)),
                _row(ssm_norm_w[i]), batch, seq, tc, sw, state, groups)
            w_o = w_out_ab[i].astype(BF16)
            xf = _proj_res([y_a, y_b], [w_o[:rw], w_o[rw:]], xf, g[1:2], tm // 2)
        else:
            qkv = _norm_matmul(xf, g[0:1], w_qkv[i].astype(BF16), BF16, tm, 1024)
            bias = _bias_table(attn_rel_bias[i])
            att = _band_attention(qkv, bias, batch, seq, n_att_heads, min(512, seq))
            xf = _proj_res([att], [w_out_c[i].astype(BF16)], xf, g[1:2], tm // 2)
        xf = _ffn(xf, g[2:3], g[3:4], ffn_w_gate[l].astype(BF16), ffn_w_up[l].astype(BF16),
                  ffn_w_down[l].astype(BF16), tm, 512)
    return xf.reshape(batch, seq, d)
```

```python
import functools
import math

import jax
import jax.numpy as jnp
from jax import lax
from jax.experimental import pallas as pl
from jax.experimental.pallas import tpu as pltpu

F32 = jnp.float32
BF16 = jnp.bfloat16

NORM_EPS = 1e-6
RWKV_LN_EPS = 64e-5
CHUNK = 64
HEAD = 64
LANE = 128
LEFT_CHUNKS = 8
REL_PAST_CLIP = 256
VMEM_LIMIT = 60 * 1024 * 1024


def _cparams(n_axes):
    return pltpu.CompilerParams(dimension_semantics=("arbitrary",) * n_axes, vmem_limit_bytes=VMEM_LIMIT)


def _dot(a, b):
    return jnp.dot(a, b, preferred_element_type=F32)


def _dot_nt(a, b):
    return lax.dot_general(a, b, (((1,), (1,)), ((), ())), preferred_element_type=F32)


def _dot_tn(a, b):
    return lax.dot_general(a, b, (((0,), (0,)), ((), ())), preferred_element_type=F32)


def _split3(t):
    hi = t.astype(BF16)
    r1 = t - hi.astype(F32)
    mid = r1.astype(BF16)
    lo = (r1 - mid.astype(F32)).astype(BF16)
    return hi, mid, lo


def _dot_sel_r(t, sel):
    hi, mid, lo = _split3(t)
    return _dot(hi, sel) + _dot(mid, sel) + _dot(lo, sel)


def _dot_sel_l(sel, t):
    hi, mid, lo = _split3(t)
    return _dot(sel, hi) + _dot(sel, mid) + _dot(sel, lo)


def _rms(x, g):
    ms = jnp.mean(x * x, axis=-1, keepdims=True)
    return x * lax.rsqrt(ms + NORM_EPS) * g


def _sigmoid(x):
    return 1.0 / (1.0 + jnp.exp(-x))


def _silu(x):
    return x * _sigmoid(x)


def _softplus(x):
    return jnp.maximum(x, 0.0) + jnp.log1p(jnp.exp(-jnp.abs(x)))


def _iota(shape, axis):
    return lax.broadcasted_iota(jnp.int32, shape, axis)


def _norm_matmul_body(x_ref, g_ref, w_ref, o_ref, h_ref):
    @pl.when(pl.program_id(1) == 0)
    def _():
        h_ref[...] = _rms(x_ref[...], g_ref[...]).astype(BF16)

    o_ref[...] = _dot(h_ref[...], w_ref[...]).astype(o_ref.dtype)


def _norm_matmul(x, g, w, out_dtype, tm, tn):
    n, d = x.shape
    m = w.shape[1]
    return pl.pallas_call(
        _norm_matmul_body,
        grid=(n // tm, m // tn),
        in_specs=[
            pl.BlockSpec((tm, d), lambda i, j: (i, 0)),
            pl.BlockSpec((1, d), lambda i, j: (0, 0)),
            pl.BlockSpec((d, tn), lambda i, j: (0, j)),
        ],
        out_specs=pl.BlockSpec((tm, tn), lambda i, j: (i, j)),
        out_shape=jax.ShapeDtypeStruct((n, m), out_dtype),
        scratch_shapes=[pltpu.VMEM((tm, d), BF16)],
        compiler_params=_cparams(2),
        name="norm_matmul",
    )(x, g, w)


def _proj_res_body(*refs, n_in):
    y_refs, w_refs = refs[:n_in], refs[n_in:2 * n_in]
    x_ref, g_ref, o_ref = refs[2 * n_in:]
    m = _dot(y_refs[0][...], w_refs[0][...])
    for y_ref, w_ref in zip(y_refs[1:], w_refs[1:]):
        m = m + _dot(y_ref[...], w_ref[...])
    o_ref[...] = x_ref[...] + _rms(m, g_ref[...])


def _proj_res(ys, ws, x, g, tm):
    n, d = x.shape
    n_in = len(ys)
    in_specs = [pl.BlockSpec((tm, y.shape[1]), lambda i: (i, 0)) for y in ys]
    in_specs += [pl.BlockSpec(w.shape, lambda i: (0, 0)) for w in ws]
    in_specs += [pl.BlockSpec((tm, d), lambda i: (i, 0)), pl.BlockSpec((1, d), lambda i: (0, 0))]
    return pl.pallas_call(
        functools.partial(_proj_res_body, n_in=n_in),
        grid=(n // tm,),
        in_specs=in_specs,
        out_specs=pl.BlockSpec((tm, d), lambda i: (i, 0)),
        out_shape=jax.ShapeDtypeStruct((n, d), F32),
        compiler_params=_cparams(1),
        name="proj_res",
    )(*ys, *ws, x, g)


def _ffn_body(x_ref, gi_ref, go_ref, wg_ref, wu_ref, wd_ref, o_ref, h_ref):
    j = pl.program_id(1)

    @pl.when(j == 0)
    def _():
        h_ref[...] = _rms(x_ref[...], gi_ref[...]).astype(BF16)
        o_ref[...] = jnp.zeros_like(o_ref)

    h = h_ref[...]
    a = (_silu(_dot(h, wg_ref[...])) * _dot(h, wu_ref[...])).astype(BF16)
    o_ref[...] += _dot(a, wd_ref[...])

    @pl.when(j == pl.num_programs(1) - 1)
    def _():
        o_ref[...] = x_ref[...] + _rms(o_ref[...], go_ref[...])


def _ffn(x, g_in, g_out, wg, wu, wd, tm, tf):
    n, d = x.shape
    f = wg.shape[1]
    return pl.pallas_call(
        _ffn_body,
        grid=(n // tm, f // tf),
        in_specs=[
            pl.BlockSpec((tm, d), lambda i, j: (i, 0), pipeline_mode=pl.Buffered(1)),
            pl.BlockSpec((1, d), lambda i, j: (0, 0)),
            pl.BlockSpec((1, d), lambda i, j: (0, 0)),
            pl.BlockSpec((d, tf), lambda i, j: (0, j)),
            pl.BlockSpec((d, tf), lambda i, j: (0, j)),
            pl.BlockSpec((tf, d), lambda i, j: (j, 0)),
        ],
        out_specs=pl.BlockSpec((tm, d), lambda i, j: (i, 0)),
        out_shape=jax.ShapeDtypeStruct((n, d), F32),
        scratch_shapes=[pltpu.VMEM((tm, d), BF16)],
        compiler_params=_cparams(2),
        name="ffn",
    )(x, g_in, g_out, wg, wu, wd)


def _seg_matrices(width, n_seg_pad):
    e = (_iota((width, n_seg_pad), 0) // HEAD == _iota((width, n_seg_pad), 1)).astype(BF16)
    et = (_iota((n_seg_pad, width), 1) // HEAD == _iota((n_seg_pad, width), 0)).astype(BF16)
    return e, et


def _blockdiag_tril(tc):
    r, c = _iota((tc, tc), 0), _iota((tc, tc), 1)
    return ((r // CHUNK == c // CHUNK) & (c <= r)).astype(BF16)


def _rwkv_body(p_ref, pp_ref, mu_ref, w0_ref, w2_ref, a0_ref, a2_ref, g2_ref, kk_ref, ka_ref, rk_ref,
               lnw_ref, lnb_ref, o_ref, s_ref, at_ref, rt_ref, bt_ref, kt_ref, v_ref, gl_ref, y_ref, *, tc, width):
    i = pl.program_id(1)
    n_heads = width // HEAD
    n_chunks = tc // CHUNK

    @pl.when(i == 0)
    def _():
        s_ref[...] = jnp.zeros_like(s_ref)

    p = p_ref[...]
    prev = jnp.where(i == 0, 0.0, pp_ref[7:8, :])
    psh = jnp.where(_iota((tc, 1), 0) == 0, prev, pltpu.roll(p, 1, 0))
    xr = p + (psh - p) * mu_ref[...]

    r = xr[:, 0:width]
    k = xr[:, width:2 * width]
    v = xr[:, 2 * width:3 * width]
    o1 = 3 * width
    wc = xr[:, o1:o1 + LANE]
    ac = xr[:, o1 + LANE:o1 + 2 * LANE]
    gc = xr[:, o1 + 2 * LANE:o1 + 4 * LANE]

    z = w0_ref[...] + _dot(jnp.tanh(wc).astype(BF16), w2_ref[...])
    lw = -jnp.exp(-_softplus(-z) - 0.5)
    a = _sigmoid(a0_ref[...] + _dot(ac.astype(BF16), a2_ref[...]))
    g = _dot(_sigmoid(gc).astype(BF16), g2_ref[...])

    e, et = _seg_matrices(width, LANE)

    def seg_sum(t):
        return _dot_sel_r(_dot_sel_r(t, e), et)

    kk = k * kk_ref[...]
    kk = kk * lax.rsqrt(jnp.maximum(seg_sum(kk * kk), 1e-24))
    k2 = k * (1.0 + (a - 1.0) * ka_ref[...])
    bonus = seg_sum(r * k2 * rk_ref[...]) * v

    cs = _dot_sel_l(_blockdiag_tril(tc), lw)
    ginv = jnp.exp(-cs)
    at_ref[...] = (-kk * jnp.exp(cs - lw)).astype(BF16)
    rt_ref[...] = (r * jnp.exp(cs)).astype(BF16)
    bt_ref[...] = (kk * a * ginv).astype(BF16)
    kt_ref[...] = (k2 * ginv).astype(BF16)
    v_ref[...] = v.astype(BF16)
    for c in range(n_chunks):
        last = c * CHUNK + CHUNK - 1
        gl_ref[c:c + 1, :] = jnp.exp(cs[last:last + 1, :])

    rr, cc = _iota((CHUNK, CHUNK), 0), _iota((CHUNK, CHUNK), 1)
    strict = (cc < rr).astype(F32)
    rr2, cc2 = _iota((CHUNK, 2 * CHUNK), 0), _iota((CHUNK, 2 * CHUNK), 1) % CHUNK
    strict2 = (cc2 < rr2).astype(F32)
    incl2 = (cc2 <= rr2).astype(F32)
    eye = (cc == rr).astype(F32)
    n_lvl = int(math.log2(CHUNK))
    lvl_masks = []
    for lv in range(n_lvl):
        same = (rr >> (lv + 1)) == (cc >> (lv + 1))
        lvl_masks.append((same & (((rr >> lv) & 1) == 1) & (((cc >> lv) & 1) == 0)).astype(F32))

    def chunk_step(c, carry):
        t0 = pl.multiple_of(c * CHUNK, CHUNK)
        rows = pl.ds(t0, CHUNK)
        gl = gl_ref[pl.ds(c, 1), :]
        heads = range(n_heads)
        hsl = [slice(h * HEAD, (h + 1) * HEAD) for h in heads]
        vh = [v_ref[rows, hs] for hs in hsl]
        ar = [jnp.concatenate([at_ref[rows, hs], rt_ref[rows, hs]], axis=0) for hs in hsl]
        bk = [jnp.concatenate([bt_ref[rows, hs], kt_ref[rows, hs]], axis=0) for hs in hsl]
        pm = [_dot_nt(ar[h], bk[h]) for h in heads]
        s0 = [s_ref[h] for h in heads]
        ars = [_dot_nt(ar[h], s0[h].astype(BF16)) for h in heads]
        pa = [pm[h][:CHUNK, :] * strict2 for h in heads]
        aab = [pa[h][:, :CHUNK] for h in heads]
        rhs = [ars[h][:CHUNK, :] + _dot(pa[h][:, CHUNK:].astype(BF16), vh[h]) for h in heads]
        x = [eye + aab[h] * lvl_masks[0] for h in heads]
        for lv in range(1, n_lvl):
            xb = [x[h].astype(BF16) for h in heads]
            tm_ = [_dot((aab[h] * lvl_masks[lv]).astype(BF16), xb[h]) for h in heads]
            x = [x[h] + _dot(xb[h], tm_[h].astype(BF16)) for h in heads]
        u = [_dot(x[h].astype(BF16), rhs[h].astype(BF16)) for h in heads]
        uv = [jnp.concatenate([u[h].astype(BF16), vh[h]], axis=0) for h in heads]
        for h in heads:
            pr = (pm[h][CHUNK:, :] * incl2).astype(BF16)
            y_ref[rows, hsl[h]] = ars[h][CHUNK:, :] + _dot(pr, uv[h])
        for h in heads:
            s_ref[h] = (s0[h] + _dot_tn(uv[h], bk[h])) * gl[:, hsl[h]]
        return carry

    lax.fori_loop(0, n_chunks, chunk_step, 0)

    y = y_ref[...]
    inv_n = 1.0 / HEAD
    mean = seg_sum(y) * inv_n
    yc = y - mean
    var = seg_sum(yc * yc) * inv_n
    yn = yc * lax.rsqrt(var + RWKV_LN_EPS) * lnw_ref[...] + lnb_ref[...]
    o_ref[...] = ((yn + bonus) * g).astype(o_ref.dtype)


def _rwkv_mix(p, mu, w0, w2, a0, a2, g2, k_k, k_a, r_k, ln_w, ln_b, batch, seq, tc):
    n, pw = p.shape
    width = w0.shape[1]
    n_heads = width // HEAD
    nt = seq // tc
    row_blk = lambda b, i: (b * nt + i, 0)
    prev_blk = lambda b, i: (jnp.maximum((b * seq + i * tc) // 8 - 1, 0), 0)
    const = lambda b, i: (0, 0)
    vec = lambda a: pl.BlockSpec(a.shape, const)
    return pl.pallas_call(
        functools.partial(_rwkv_body, tc=tc, width=width),
        grid=(batch, nt),
        in_specs=[pl.BlockSpec((tc, pw), row_blk), pl.BlockSpec((8, pw), prev_blk)]
        + [vec(a) for a in (mu, w0, w2, a0, a2, g2, k_k, k_a, r_k, ln_w, ln_b)],
        out_specs=pl.BlockSpec((tc, width), row_blk),
        out_shape=jax.ShapeDtypeStruct((n, width), BF16),
        scratch_shapes=[pltpu.VMEM((n_heads, HEAD, HEAD), F32)]
        + [pltpu.VMEM((tc, width), BF16) for _ in range(5)]
        + [pltpu.VMEM((max(tc // CHUNK, 8), width), F32), pltpu.VMEM((tc, width), F32)],
        compiler_params=_cparams(2),
        name="rwkv_mix",
    )(p, p, mu, w0, w2, a0, a2, g2, k_k, k_a, r_k, ln_w, ln_b)


def _ssd_body(p_ref, pp_ref, cw_ref, cb_ref, dtb_ref, alog_ref, dskip_ref, nw_ref, o_ref,
              st_ref, y_ref, *, tc, width, state, groups):
    i = pl.program_id(1)
    n_chunks = tc // CHUNK
    gw = width // groups
    hpg = gw // HEAD
    conv_ch = width + 2 * groups * state
    n_tap = cw_ref.shape[0]

    @pl.when(i == 0)
    def _():
        st_ref[...] = jnp.zeros_like(st_ref)

    p = p_ref[...]
    z = p[:, 0:width]
    xbc_raw = p[:, width:width + conv_ch]
    dt_raw = p[:, width + conv_ch:width + conv_ch + LANE]

    prev = jnp.where(i == 0, 0.0, pp_ref[:, width:width + conv_ch])
    xcat = jnp.concatenate([prev, xbc_raw], axis=0)
    conv = cb_ref[...]
    for j in range(n_tap):
        off = 8 - (n_tap - 1) + j
        conv = conv + cw_ref[j:j + 1, :] * xcat[off:off + tc, :]
    xbc = _silu(conv)
    xs = xbc[:, 0:width]
    bm = xbc[:, width:width + groups * state].astype(BF16)
    cm = xbc[:, width + groups * state:conv_ch].astype(BF16)

    e, et = _seg_matrices(width, LANE)
    dt = _softplus(dt_raw + dtb_ref[...])
    da = -jnp.exp(alog_ref[...]) * dt
    cs_h = _dot_sel_l(_blockdiag_tril(tc), da)
    cs = _dot_sel_r(cs_h, et)
    dtx = _dot_sel_r(dt, et)
    xc = xs * dtx
    ecs = jnp.exp(cs)

    rr, cc = _iota((CHUNK, CHUNK), 0), _iota((CHUNK, CHUNK), 1)
    causal = cc <= rr

    for c in range(n_chunks):
        r0 = c * CHUNK
        rows = slice(r0, r0 + CHUNK)
        cs_c = cs[rows, :]
        cs_last = cs_c[CHUNK - 1:CHUNK, :]
        xc_c = xc[rows, :]
        xd = (xc_c * jnp.exp(cs_last - cs_c)).astype(BF16)
        xcb = xc_c.astype(BF16)
        cs_t = cs_h[rows, :].T
        for g in range(groups):
            gl = slice(g * gw, (g + 1) * gw)
            b_g = bm[rows, g * state:(g + 1) * state]
            c_g = cm[rows, g * state:(g + 1) * state]
            cb = _dot_nt(c_g, b_g)
            st = st_ref[g]
            y_off = _dot(c_g, st.astype(BF16)) * ecs[rows, gl]
            for hh in range(hpg):
                h = g * hpg + hh
                hs = slice(h * HEAD, (h + 1) * HEAD)
                seg = cs_c[:, hs] - cs_t[h:h + 1, :]
                lmat = jnp.where(causal, jnp.exp(jnp.where(causal, seg, 0.0)), 0.0)
                y_ref[rows, hs] = _dot((cb * lmat).astype(BF16), xcb[:, hs]) + y_off[:, hh * HEAD:(hh + 1) * HEAD]
            st_ref[g] = st * ecs[r0 + CHUNK - 1:r0 + CHUNK, gl] + _dot_tn(b_g, xd[:, gl])

    y = (y_ref[...] + dskip_ref[...] * xs) * _silu(z)
    for g in range(groups):
        gl = slice(g * gw, (g + 1) * gw)
        yg = y[:, gl]
        ms = jnp.mean(yg * yg, axis=-1, keepdims=True)
        o_ref[:, gl] = (yg * lax.rsqrt(ms + NORM_EPS) * nw_ref[:, gl]).astype(o_ref.dtype)


def _ssd_mix(p, conv_w, conv_b, dt_bias, a_log, d_skip, norm_w, batch, seq, tc, width, state, groups):
    n, pw = p.shape
    nt = seq // tc
    row_blk = lambda b, i: (b * nt + i, 0)
    prev_blk = lambda b, i: (jnp.maximum((b * seq + i * tc) // 8 - 1, 0), 0)
    const = lambda b, i: (0, 0)
    vec = lambda a: pl.BlockSpec(a.shape, const)
    return pl.pallas_call(
        functools.partial(_ssd_body, tc=tc, width=width, state=state, groups=groups),
        grid=(batch, nt),
        in_specs=[pl.BlockSpec((tc, pw), row_blk), pl.BlockSpec((8, pw), prev_blk)]
        + [vec(a) for a in (conv_w, conv_b, dt_bias, a_log, d_skip, norm_w)],
        out_specs=pl.BlockSpec((tc, width), row_blk),
        out_shape=jax.ShapeDtypeStruct((n, width), BF16),
        scratch_shapes=[pltpu.VMEM((groups, state, width // groups), F32), pltpu.VMEM((tc, width), F32)],
        compiler_params=_cparams(2),
        name="ssd_mix",
    )(p, p, conv_w, conv_b, dt_bias, a_log, d_skip, norm_w)


PAIR = 2 * CHUNK
BAND = (LEFT_CHUNKS + 1) * CHUNK
PAIR_BAND = BAND + CHUNK


def _bias_table_body(rb_ref, o_ref, *, n_bucket, rel_future):
    ext_w = PAIR_BAND + LANE
    m = _iota((n_bucket, ext_w), 1)
    bucket = jnp.clip(PAIR_BAND - 1 - m, -rel_future, REL_PAST_CLIP) + rel_future
    sel = (bucket == _iota((n_bucket, ext_w), 0)).astype(BF16)
    ext = _dot_sel_r(rb_ref[...], sel)
    for r in range(PAIR):
        o_ref[r] = ext[:, PAIR - 1 - r:PAIR - 1 - r + PAIR_BAND]


def _bias_table(rel_bias):
    n_heads, n_bucket = rel_bias.shape
    out = pl.pallas_call(
        functools.partial(_bias_table_body, n_bucket=n_bucket, rel_future=CHUNK - 1),
        out_shape=jax.ShapeDtypeStruct((PAIR, n_heads, PAIR_BAND), F32),
        name="bias_table",
    )(rel_bias)
    return jnp.transpose(out, (1, 0, 2))


def _attn_body(q_ref, kp_ref, kc_ref, vp_ref, vc_ref, b_ref, o_ref, *, tq, dh, scale):
    i = pl.program_id(2)
    n_heads = q_ref.shape[1] // dh
    row, col = _iota((PAIR, PAIR_BAND), 0), _iota((PAIR, PAIR_BAND), 1)
    lo = (row // CHUNK) * CHUNK
    in_band = (col >= lo) & (col < lo + BAND)
    n_pairs = tq // PAIR
    k0 = [tq - LEFT_CHUNKS * CHUNK + jp * PAIR for jp in range(n_pairs)]
    work = [(h, jp) for h in range(n_heads) for jp in range(n_pairs)]

    def attend(first_block):
        hcols = [slice(h * dh, (h + 1) * dh) for h in range(n_heads)]
        kcat = [jnp.concatenate([kp_ref[:, hc], kc_ref[:, hc]], axis=0) for hc in hcols]
        vcat = [jnp.concatenate([vp_ref[:, hc], vc_ref[:, hc]], axis=0) for hc in hcols]
        s = [_dot_nt(q_ref[jp * PAIR:(jp + 1) * PAIR, hcols[h]], kcat[h][k0[jp]:k0[jp] + PAIR_BAND, :]) * scale
             + b_ref[h] for h, jp in work]
        if first_block:
            valid = [in_band & (col + k0[jp] >= tq) for jp in range(n_pairs)]
        else:
            valid = [in_band] * n_pairs
        s = [jnp.where(valid[jp], s_, -1e30) for (h, jp), s_ in zip(work, s)]
        mx = [jnp.max(s_, axis=-1, keepdims=True) for s_ in s]
        pexp = [jnp.exp(s_ - m_) for s_, m_ in zip(s, mx)]
        den = [jnp.sum(p_, axis=-1, keepdims=True) for p_ in pexp]
        o = [_dot(p_.astype(BF16), vcat[h][k0[jp]:k0[jp] + PAIR_BAND, :]) for (h, jp), p_ in zip(work, pexp)]
        for (h, jp), o_, d_ in zip(work, o, den):
            o_ref[jp * PAIR:(jp + 1) * PAIR, hcols[h]] = (o_ / d_).astype(o_ref.dtype)

    @pl.when(i == 0)
    def _():
        attend(True)

    @pl.when(i > 0)
    def _():
        attend(False)


def _band_attention(qkv, bias, batch, seq, n_heads, tq, hp):
    n, three_d = qkv.shape
    d = three_d // 3
    dh = d // n_heads
    nt = seq // tq
    ng = n_heads // hp
    cur = lambda off: (lambda b, h, i: (b * nt + i, off * ng + h))
    prv = lambda off: (lambda b, h, i: (b * nt + jnp.maximum(i - 1, 0), off * ng + h))
    blk = lambda f: pl.BlockSpec((tq, hp * dh), f)
    return pl.pallas_call(
        functools.partial(_attn_body, tq=tq, dh=dh, scale=dh ** -0.5),
        grid=(batch, ng, nt),
        in_specs=[blk(cur(0)), blk(prv(1)), blk(cur(1)), blk(prv(2)), blk(cur(2)),
                  pl.BlockSpec((hp, PAIR, PAIR_BAND), lambda b, h, i: (h, 0, 0))],
        out_specs=pl.BlockSpec((tq, hp * dh), lambda b, h, i: (b * nt + i, h)),
        out_shape=jax.ShapeDtypeStruct((n, d), BF16),
        compiler_params=_cparams(3),
        name="band_attention",
    )(qkv, qkv, qkv, qkv, qkv, bias)


def _pad_cols(a, width):
    return jnp.pad(a, ((0, 0), (0, width - a.shape[1])))


def _pad_rows(a, rows):
    return jnp.pad(a, ((0, rows - a.shape[0]), (0, 0)))


def _row(a):
    return a.reshape(1, -1)


def kernel(x, norm_g, w_in_ab, rwkv_mu, rwkv_w0, rwkv_w2, rwkv_a0, rwkv_a2, rwkv_g2, rwkv_k_k, rwkv_k_a, rwkv_r_k, rwkv_ln_w, rwkv_ln_b, ssm_conv_w, ssm_conv_b, ssm_dt_bias, ssm_A_log, ssm_D, ssm_norm_w, w_out_ab, w_qkv, attn_rel_bias, w_out_c, ffn_w_gate, ffn_w_up, ffn_w_down):
    batch, seq, d = x.shape
    n = batch * seq
    depth = norm_g.shape[0]
    xf = x.reshape(n, d)

    rw = rwkv_w0.shape[1]
    lora_w, lora_a, lora_g = rwkv_w2.shape[1], rwkv_a2.shape[1], rwkv_g2.shape[1]
    sw = ssm_norm_w.shape[1]
    n_ssm_heads = ssm_A_log.shape[1]
    conv_ch = ssm_conv_w.shape[2]
    groups = 2
    state = (conv_ch - sw) // (2 * groups)
    n_att_heads = attn_rel_bias.shape[1]
    rwkv_proj = 3 * rw + lora_w + lora_a + lora_g
    tm = min(1024, n)
    tc = min(256, seq)

    for l in range(depth):
        i = l // 2
        g = norm_g[l]
        if l % 2 == 0:
            w_in = w_in_ab[i]
            o1, o2, o3 = 3 * rw, 3 * rw + lora_w, 3 * rw + lora_w + lora_a
            seg = lambda a: [a[:, 0:o1], _pad_cols(a[:, o1:o2], LANE), _pad_cols(a[:, o2:o3], LANE), a[:, o3:rwkv_proj]]
            w_r = jnp.concatenate(seg(w_in), axis=1).astype(BF16)
            mu = jnp.concatenate(seg(_row(rwkv_mu[i])), axis=1)
            o4 = rwkv_proj + sw + conv_ch
            w_s = jnp.concatenate([w_in[:, rwkv_proj:o4], _pad_cols(w_in[:, o4:], LANE)], axis=1).astype(BF16)

            p_r = _norm_matmul(xf, g[0:1], w_r, F32, tm, w_r.shape[1] // 4)
            p_s = _norm_matmul(xf, g[0:1], w_s, F32, tm, w_s.shape[1] // 3)

            y_a = _rwkv_mix(
                p_r, mu, _row(rwkv_w0[i]), _pad_rows(rwkv_w2[i], LANE).astype(BF16), _row(rwkv_a0[i]),
                _pad_rows(rwkv_a2[i], LANE).astype(BF16), rwkv_g2[i].astype(BF16), _row(rwkv_k_k[i]),
                _row(rwkv_k_a[i]), _row(rwkv_r_k[i]), _row(rwkv_ln_w[i]), _row(rwkv_ln_b[i]), batch, seq, tc)
            y_b = _ssd_mix(
                p_s, ssm_conv_w[i], _row(ssm_conv_b[i]), _pad_cols(_row(ssm_dt_bias[i]), LANE),
                _pad_cols(_row(ssm_A_log[i]), LANE), _row(jnp.repeat(ssm_D[i], sw // n_ssm_heads)),
                _row(ssm_norm_w[i]), batch, seq, tc, sw, state, groups)
            w_o = w_out_ab[i].astype(BF16)
            xf = _proj_res([y_a, y_b], [w_o[:rw], w_o[rw:]], xf, g[1:2], tm // 2)
        else:
            qkv = _norm_matmul(xf, g[0:1], w_qkv[i].astype(BF16), BF16, tm, 1024)
            bias = _bias_table(attn_rel_bias[i])
            att = _band_attention(qkv, bias, batch, seq, n_att_heads, min(512, seq), 2)
            xf = _proj_res([att], [w_out_c[i].astype(BF16)], xf, g[1:2], tm // 2)
        xf = _ffn(xf, g[2:3], g[3:4], ffn_w_gate[l].astype(BF16), ffn_w_up[l].astype(BF16),
                  ffn_w_down[l].astype(BF16), tm, 512)
    return xf.reshape(batch, seq, d)
```

```python
import functools
import math

import jax
import jax.numpy as jnp
from jax import lax
from jax.experimental import pallas as pl
from jax.experimental.pallas import tpu as pltpu

F32 = jnp.float32
BF16 = jnp.bfloat16

NORM_EPS = 1e-6
RWKV_LN_EPS = 64e-5
CHUNK = 64
HEAD = 64
LANE = 128
LEFT_CHUNKS = 8
REL_PAST_CLIP = 256
VMEM_LIMIT = 60 * 1024 * 1024


def _cparams(n_axes):
    return pltpu.CompilerParams(dimension_semantics=("arbitrary",) * n_axes, vmem_limit_bytes=VMEM_LIMIT)


def _dot(a, b):
    return jnp.dot(a, b, preferred_element_type=F32)


def _dot_nt(a, b):
    return lax.dot_general(a, b, (((1,), (1,)), ((), ())), preferred_element_type=F32)


def _dot_tn(a, b):
    return lax.dot_general(a, b, (((0,), (0,)), ((), ())), preferred_element_type=F32)


def _split2(t):
    hi = t.astype(BF16)
    lo = (t - hi.astype(F32)).astype(BF16)
    return hi, lo


def _dot_sel_r(t, sel):
    hi, lo = _split2(t)
    return _dot(hi, sel) + _dot(lo, sel)


def _dot_sel_l(sel, t):
    hi, lo = _split2(t)
    return _dot(sel, hi) + _dot(sel, lo)


def _rms(x, g):
    ms = jnp.mean(x * x, axis=-1, keepdims=True)
    return x * lax.rsqrt(ms + NORM_EPS) * g


def _sigmoid(x):
    return 1.0 / (1.0 + jnp.exp(-x))


def _silu(x):
    return x * _sigmoid(x)


def _softplus(x):
    return jnp.maximum(x, 0.0) + jnp.log1p(jnp.exp(-jnp.abs(x)))


def _iota(shape, axis):
    return lax.broadcasted_iota(jnp.int32, shape, axis)


def _norm_matmul_body(x_ref, g_ref, w_ref, o_ref, h_ref):
    @pl.when(pl.program_id(1) == 0)
    def _():
        h_ref[...] = _rms(x_ref[...], g_ref[...]).astype(BF16)

    o_ref[...] = _dot(h_ref[...], w_ref[...]).astype(o_ref.dtype)


def _norm_matmul(x, g, w, out_dtype, tm, tn):
    n, d = x.shape
    m = w.shape[1]
    return pl.pallas_call(
        _norm_matmul_body,
        grid=(n // tm, m // tn),
        in_specs=[
            pl.BlockSpec((tm, d), lambda i, j: (i, 0)),
            pl.BlockSpec((1, d), lambda i, j: (0, 0)),
            pl.BlockSpec((d, tn), lambda i, j: (0, j)),
        ],
        out_specs=pl.BlockSpec((tm, tn), lambda i, j: (i, j)),
        out_shape=jax.ShapeDtypeStruct((n, m), out_dtype),
        scratch_shapes=[pltpu.VMEM((tm, d), BF16)],
        compiler_params=_cparams(2),
        name="norm_matmul",
    )(x, g, w)


def _proj_res_body(*refs, n_in):
    y_refs, w_refs = refs[:n_in], refs[n_in:2 * n_in]
    x_ref, g_ref, o_ref = refs[2 * n_in:]
    m = _dot(y_refs[0][...], w_refs[0][...])
    for y_ref, w_ref in zip(y_refs[1:], w_refs[1:]):
        m = m + _dot(y_ref[...], w_ref[...])
    o_ref[...] = x_ref[...] + _rms(m, g_ref[...])


def _proj_res(ys, ws, x, g, tm):
    n, d = x.shape
    n_in = len(ys)
    in_specs = [pl.BlockSpec((tm, y.shape[1]), lambda i: (i, 0)) for y in ys]
    in_specs += [pl.BlockSpec(w.shape, lambda i: (0, 0)) for w in ws]
    in_specs += [pl.BlockSpec((tm, d), lambda i: (i, 0)), pl.BlockSpec((1, d), lambda i: (0, 0))]
    return pl.pallas_call(
        functools.partial(_proj_res_body, n_in=n_in),
        grid=(n // tm,),
        in_specs=in_specs,
        out_specs=pl.BlockSpec((tm, d), lambda i: (i, 0)),
        out_shape=jax.ShapeDtypeStruct((n, d), F32),
        compiler_params=_cparams(1),
        name="proj_res",
    )(*ys, *ws, x, g)


def _ffn_body(x_ref, gi_ref, go_ref, wg_ref, wu_ref, wd_ref, o_ref, h_ref):
    j = pl.program_id(1)

    @pl.when(j == 0)
    def _():
        h_ref[...] = _rms(x_ref[...], gi_ref[...]).astype(BF16)
        o_ref[...] = jnp.zeros_like(o_ref)

    h = h_ref[...]
    a = (_silu(_dot(h, wg_ref[...])) * _dot(h, wu_ref[...])).astype(BF16)
    o_ref[...] += _dot(a, wd_ref[...])

    @pl.when(j == pl.num_programs(1) - 1)
    def _():
        o_ref[...] = x_ref[...] + _rms(o_ref[...], go_ref[...])


def _ffn(x, g_in, g_out, wg, wu, wd, tm, tf):
    n, d = x.shape
    f = wg.shape[1]
    return pl.pallas_call(
        _ffn_body,
        grid=(n // tm, f // tf),
        in_specs=[
            pl.BlockSpec((tm, d), lambda i, j: (i, 0), pipeline_mode=pl.Buffered(1)),
            pl.BlockSpec((1, d), lambda i, j: (0, 0)),
            pl.BlockSpec((1, d), lambda i, j: (0, 0)),
            pl.BlockSpec((d, tf), lambda i, j: (0, j)),
            pl.BlockSpec((d, tf), lambda i, j: (0, j)),
            pl.BlockSpec((tf, d), lambda i, j: (j, 0)),
        ],
        out_specs=pl.BlockSpec((tm, d), lambda i, j: (i, 0)),
        out_shape=jax.ShapeDtypeStruct((n, d), F32),
        scratch_shapes=[pltpu.VMEM((tm, d), BF16)],
        compiler_params=_cparams(2),
        name="ffn",
    )(x, g_in, g_out, wg, wu, wd)


def _seg_matrices(width, n_seg_pad):
    e = (_iota((width, n_seg_pad), 0) // HEAD == _iota((width, n_seg_pad), 1)).astype(BF16)
    et = (_iota((n_seg_pad, width), 1) // HEAD == _iota((n_seg_pad, width), 0)).astype(BF16)
    return e, et


def _blockdiag_tril(tc):
    r, c = _iota((tc, tc), 0), _iota((tc, tc), 1)
    return ((r // CHUNK == c // CHUNK) & (c <= r)).astype(BF16)


def _rwkv_body(x_ref, gn_ref, w_ref, mu_ref, w0_ref, w2_ref, a0_ref, a2_ref, g2_ref, kk_ref, ka_ref, rk_ref,
               lnw_ref, lnb_ref, o_ref, p_ref, prow_ref, s_ref, at_ref, rt_ref, bt_ref, kt_ref, v_ref, gl_ref, y_ref,
               *, tc, width):
    step = pl.program_id(1)

    tw = p_ref.shape[1] // N_PROJ_TILES

    def h_next_fn():
        return _rms(x_ref[...], gn_ref[...]).astype(BF16)

    def project_tile(h, k):
        p_ref[:, k * tw:(k + 1) * tw] = _dot(h, w_ref[:, k * tw:(k + 1) * tw])

    @pl.when(step == 0)
    def _():
        s_ref[...] = jnp.zeros_like(s_ref)
        prow_ref[...] = jnp.zeros_like(prow_ref)
        h = h_next_fn()
        for k in range(N_PROJ_TILES):
            project_tile(h, k)

    @pl.when(step > 0)
    def _():
        _rwkv_mix_block(h_next_fn, project_tile, mu_ref, w0_ref, w2_ref, a0_ref, a2_ref, g2_ref, kk_ref, ka_ref, rk_ref, lnw_ref,
                        lnb_ref, o_ref, p_ref, prow_ref, s_ref, at_ref, rt_ref, bt_ref, kt_ref, v_ref, gl_ref, y_ref,
                        tc=tc, width=width)


def _rwkv_mix_block(h_next_fn, project_tile, mu_ref, w0_ref, w2_ref, a0_ref, a2_ref, g2_ref, kk_ref, ka_ref, rk_ref, lnw_ref,
                    lnb_ref, o_ref, p_ref, prow_ref, s_ref, at_ref, rt_ref, bt_ref, kt_ref, v_ref, gl_ref, y_ref,
                    *, tc, width):
    n_heads = width // HEAD
    n_chunks = tc // CHUNK

    p = p_ref[...]
    first_row = _iota((tc, 1), 0) == 0

    def lerp(cols):
        pc = p[:, cols]
        psh = jnp.where(first_row, prow_ref[0:1, cols], pltpu.roll(pc, 1, 0))
        return pc + (psh - pc) * mu_ref[:, cols]

    o1 = 3 * width
    xl = lerp(slice(o1, o1 + 4 * LANE))
    wc, ac, gc = xl[:, 0:LANE], xl[:, LANE:2 * LANE], xl[:, 2 * LANE:4 * LANE]
    z = w0_ref[...] + _dot(jnp.tanh(wc).astype(BF16), w2_ref[...])
    a = _sigmoid(a0_ref[...] + _dot(ac.astype(BF16), a2_ref[...]))
    g = _dot(_sigmoid(gc).astype(BF16), g2_ref[...])
    lw = -jnp.exp(-_softplus(-z) - 0.5)

    h_next = h_next_fn()
    project_tile(h_next, 0)
    project_tile(h_next, 1)

    xr = lerp(slice(0, o1))
    last_row = p[tc - 1:tc, :]
    r = xr[:, 0:width]
    k = xr[:, width:2 * width]
    v = xr[:, 2 * width:3 * width]

    mxu_w = 2 * LANE
    same_head = (_iota((mxu_w, mxu_w), 0) // HEAD == _iota((mxu_w, mxu_w), 1) // HEAD).astype(BF16)

    def seg_sum(t):
        return jnp.concatenate(
            [_dot_sel_r(t[:, j:j + mxu_w], same_head) for j in range(0, width, mxu_w)], axis=1)

    kk = k * kk_ref[...]
    kk = kk * lax.rsqrt(jnp.maximum(seg_sum(kk * kk), 1e-24))
    k2 = k * (1.0 + (a - 1.0) * ka_ref[...])
    project_tile(h_next, 2)
    bonus = seg_sum(r * k2 * rk_ref[...]) * v
    cs = _dot_sel_l(_blockdiag_tril(tc), lw)
    project_tile(h_next, 3)
    prow_ref[0:1, :] = last_row

    ginv = jnp.exp(-cs)
    at_ref[...] = (-kk * jnp.exp(cs - lw)).astype(BF16)
    rt_ref[...] = (r * jnp.exp(cs)).astype(BF16)
    bt_ref[...] = (kk * a * ginv).astype(BF16)
    kt_ref[...] = (k2 * ginv).astype(BF16)
    v_ref[...] = v.astype(BF16)
    for c in range(n_chunks):
        last = c * CHUNK + CHUNK - 1
        gl_ref[c:c + 1, :] = jnp.exp(cs[last:last + 1, :])

    rr, cc = _iota((CHUNK, CHUNK), 0), _iota((CHUNK, CHUNK), 1)
    strict = (cc < rr).astype(F32)
    rr2, cc2 = _iota((CHUNK, 2 * CHUNK), 0), _iota((CHUNK, 2 * CHUNK), 1) % CHUNK
    strict2 = (cc2 < rr2).astype(F32)
    incl2 = (cc2 <= rr2).astype(F32)
    eye = (cc == rr).astype(F32)
    n_lvl = int(math.log2(CHUNK))
    lvl_masks = []
    for lv in range(n_lvl):
        same = (rr >> (lv + 1)) == (cc >> (lv + 1))
        lvl_masks.append((same & (((rr >> lv) & 1) == 1) & (((cc >> lv) & 1) == 0)).astype(F32))

    def chunk_step(c, carry):
        t0 = pl.multiple_of(c * CHUNK, CHUNK)
        rows = pl.ds(t0, CHUNK)
        gl = gl_ref[pl.ds(c, 1), :]
        heads = range(n_heads)
        hsl = [slice(h * HEAD, (h + 1) * HEAD) for h in heads]
        vh = [v_ref[rows, hs] for hs in hsl]
        ar = [jnp.concatenate([at_ref[rows, hs], rt_ref[rows, hs]], axis=0) for hs in hsl]
        bk = [jnp.concatenate([bt_ref[rows, hs], kt_ref[rows, hs]], axis=0) for hs in hsl]
        pm = [_dot_nt(ar[h], bk[h]) for h in heads]
        s0 = [s_ref[h] for h in heads]
        ars = [_dot_nt(ar[h], s0[h].astype(BF16)) for h in heads]
        pa = [pm[h][:CHUNK, :] * strict2 for h in heads]
        aab = [pa[h][:, :CHUNK] for h in heads]
        rhs = [ars[h][:CHUNK, :] + _dot(pa[h][:, CHUNK:].astype(BF16), vh[h]) for h in heads]
        x = [eye + aab[h] * lvl_masks[0] for h in heads]
        for lv in range(1, n_lvl):
            xb = [x[h].astype(BF16) for h in heads]
            tm_ = [_dot((aab[h] * lvl_masks[lv]).astype(BF16), xb[h]) for h in heads]
            x = [x[h] + _dot(xb[h], tm_[h].astype(BF16)) for h in heads]
        u = [_dot(x[h].astype(BF16), rhs[h].astype(BF16)) for h in heads]
        uv = [jnp.concatenate([u[h].astype(BF16), vh[h]], axis=0) for h in heads]
        for h in heads:
            pr = (pm[h][CHUNK:, :] * incl2).astype(BF16)
            y_ref[rows, hsl[h]] = ars[h][CHUNK:, :] + _dot(pr, uv[h])
        for h in heads:
            s_ref[h] = (s0[h] + _dot_tn(uv[h], bk[h])) * gl[:, hsl[h]]
        return carry

    lax.fori_loop(0, n_chunks, chunk_step, 0)

    y = y_ref[...]
    inv_n = 1.0 / HEAD
    mean = seg_sum(y) * inv_n
    yc = y - mean
    var = seg_sum(yc * yc) * inv_n
    yn = yc * lax.rsqrt(var + RWKV_LN_EPS) * lnw_ref[...] + lnb_ref[...]
    o_ref[...] = ((yn + bonus) * g).astype(o_ref.dtype)


def _mixer_specs(x, gn, w, params, batch, seq, tc, width):
    nt = seq // tc
    d = x.shape[1]
    const = lambda b, s: (0, 0)
    in_specs = [
        pl.BlockSpec((tc, d), lambda b, s: (b * nt + jnp.minimum(s, nt - 1), 0)),
        pl.BlockSpec(gn.shape, const),
        pl.BlockSpec(w.shape, const, pipeline_mode=pl.Buffered(1)),
    ] + [pl.BlockSpec(a.shape, const) for a in params]
    out_spec = pl.BlockSpec((tc, width), lambda b, s: (b * nt + jnp.maximum(s - 1, 0), 0))
    return (batch, nt + 1), in_specs, out_spec


def _rwkv_mix(x, gn, w, mu, w0, w2, a0, a2, g2, k_k, k_a, r_k, ln_w, ln_b, batch, seq, tc):
    n = x.shape[0]
    pw = w.shape[1]
    width = w0.shape[1]
    n_heads = width // HEAD
    params = (mu, w0, w2, a0, a2, g2, k_k, k_a, r_k, ln_w, ln_b)
    grid, in_specs, out_spec = _mixer_specs(x, gn, w, params, batch, seq, tc, width)
    return pl.pallas_call(
        functools.partial(_rwkv_body, tc=tc, width=width),
        grid=grid,
        in_specs=in_specs,
        out_specs=out_spec,
        out_shape=jax.ShapeDtypeStruct((n, width), BF16),
        scratch_shapes=[pltpu.VMEM((tc, pw), F32), pltpu.VMEM((8, pw), F32), pltpu.VMEM((n_heads, HEAD, HEAD), F32)]
        + [pltpu.VMEM((tc, width), BF16) for _ in range(5)]
        + [pltpu.VMEM((max(tc // CHUNK, 8), width), F32), pltpu.VMEM((tc, width), F32)],
        compiler_params=_cparams(2),
        name="rwkv_mix",
    )(x, gn, w, *params)


def _ssd_body(x_ref, gn_ref, w_ref, cw_ref, cb_ref, dtb_ref, alog_ref, dskip_ref, nw_ref, o_ref,
              p_ref, tail_ref, st_ref, y_ref, *, tc, width, state, groups):
    step = pl.program_id(1)
    n_chunks = tc // CHUNK
    n_lane_tiles = p_ref.shape[1] // LANE
    n_tiles = max(t for t in range(1, n_chunks + 1) if n_lane_tiles % t == 0)
    tw = p_ref.shape[1] // n_tiles

    def project_tile(h, k):
        p_ref[:, k * tw:(k + 1) * tw] = _dot(h, w_ref[:, k * tw:(k + 1) * tw])

    @pl.when(step == 0)
    def _():
        st_ref[...] = jnp.zeros_like(st_ref)
        tail_ref[...] = jnp.zeros_like(tail_ref)
        h = _rms(x_ref[...], gn_ref[...]).astype(BF16)
        for k in range(n_tiles):
            project_tile(h, k)

    @pl.when(step > 0)
    def _():
        _ssd_mix_block(x_ref, gn_ref, project_tile, n_tiles, cw_ref, cb_ref, dtb_ref, alog_ref, dskip_ref, nw_ref,
                       o_ref, p_ref, tail_ref, st_ref, y_ref, tc=tc, width=width, state=state, groups=groups)


def _ssd_mix_block(x_ref, gn_ref, project_tile, n_tiles, cw_ref, cb_ref, dtb_ref, alog_ref, dskip_ref, nw_ref,
                   o_ref, p_ref, tail_ref, st_ref, y_ref, *, tc, width, state, groups):
    n_chunks = tc // CHUNK
    gw = width // groups
    hpg = gw // HEAD
    conv_ch = width + 2 * groups * state
    n_tap = cw_ref.shape[0]

    p = p_ref[...]
    zs = _silu(p[:, 0:width])
    xbc_raw = p[:, width:width + conv_ch]
    dt_raw = p[:, width + conv_ch:width + conv_ch + LANE]
    h_next = _rms(x_ref[...], gn_ref[...]).astype(BF16)
    tiles_after_chunk = {n_chunks // 2 - 1 + k * (n_chunks // 2): k + 1 for k in range(n_tiles - 1)}
    project_tile(h_next, 0)

    prev = tail_ref[...]
    tail_ref[...] = xbc_raw[tc - 8:tc, :]
    xcat = jnp.concatenate([prev, xbc_raw], axis=0)
    conv = cb_ref[...]
    for j in range(n_tap):
        off = 8 - (n_tap - 1) + j
        conv = conv + cw_ref[j:j + 1, :] * xcat[off:off + tc, :]
    xbc = _silu(conv)
    xs = xbc[:, 0:width]
    bm = xbc[:, width:width + groups * state].astype(BF16)
    cm = xbc[:, width + groups * state:conv_ch].astype(BF16)

    e, et = _seg_matrices(width, LANE)
    dt = _softplus(dt_raw + dtb_ref[...])
    da = -jnp.exp(alog_ref[...]) * dt
    cs_h = _dot_sel_l(_blockdiag_tril(tc), da)
    cs = _dot_sel_r(cs_h, et)
    dtx = _dot_sel_r(dt, et)
    xc = xs * dtx
    ecs = jnp.exp(cs)

    rr, cc = _iota((CHUNK, CHUNK), 0), _iota((CHUNK, CHUNK), 1)
    causal = cc <= rr

    for c in range(n_chunks):
        r0 = c * CHUNK
        rows = slice(r0, r0 + CHUNK)
        cs_c = cs[rows, :]
        cs_last = cs_c[CHUNK - 1:CHUNK, :]
        xc_c = xc[rows, :]
        xd = (xc_c * jnp.exp(cs_last - cs_c)).astype(BF16)
        xcb = xc_c.astype(BF16)
        cs_t = cs_h[rows, :].T
        for g in range(groups):
            gl = slice(g * gw, (g + 1) * gw)
            b_g = bm[rows, g * state:(g + 1) * state]
            c_g = cm[rows, g * state:(g + 1) * state]
            cb = _dot_nt(c_g, b_g)
            st = st_ref[g]
            y_off = _dot(c_g, st.astype(BF16)) * ecs[rows, gl]
            for hh in range(hpg):
                h = g * hpg + hh
                hs = slice(h * HEAD, (h + 1) * HEAD)
                seg = cs_c[:, hs] - cs_t[h:h + 1, :]
                lmat = jnp.where(causal, jnp.exp(jnp.where(causal, seg, 0.0)), 0.0)
                y_ref[rows, hs] = _dot((cb * lmat).astype(BF16), xcb[:, hs]) + y_off[:, hh * HEAD:(hh + 1) * HEAD]
            st_ref[g] = st * ecs[r0 + CHUNK - 1:r0 + CHUNK, gl] + _dot_tn(b_g, xd[:, gl])
        if c in tiles_after_chunk:
            project_tile(h_next, tiles_after_chunk[c])
    assert sorted(tiles_after_chunk.values()) == list(range(1, n_tiles)) and max(tiles_after_chunk) < n_chunks

    y = (y_ref[...] + dskip_ref[...] * xs) * zs
    for g in range(groups):
        gl = slice(g * gw, (g + 1) * gw)
        yg = y[:, gl]
        ms = jnp.mean(yg * yg, axis=-1, keepdims=True)
        o_ref[:, gl] = (yg * lax.rsqrt(ms + NORM_EPS) * nw_ref[:, gl]).astype(o_ref.dtype)


def _ssd_mix(x, gn, w, conv_w, conv_b, dt_bias, a_log, d_skip, norm_w, batch, seq, tc, width, state, groups):
    n = x.shape[0]
    pw = w.shape[1]
    conv_ch = width + 2 * groups * state
    params = (conv_w, conv_b, dt_bias, a_log, d_skip, norm_w)
    grid, in_specs, out_spec = _mixer_specs(x, gn, w, params, batch, seq, tc, width)
    return pl.pallas_call(
        functools.partial(_ssd_body, tc=tc, width=width, state=state, groups=groups),
        grid=grid,
        in_specs=in_specs,
        out_specs=out_spec,
        out_shape=jax.ShapeDtypeStruct((n, width), BF16),
        scratch_shapes=[pltpu.VMEM((tc, pw), F32), pltpu.VMEM((8, conv_ch), F32),
                        pltpu.VMEM((groups, state, width // groups), F32), pltpu.VMEM((tc, width), F32)],
        compiler_params=_cparams(2),
        name="ssd_mix",
    )(x, gn, w, *params)


N_PROJ_TILES = 4
PAIR = 2 * CHUNK
BAND = (LEFT_CHUNKS + 1) * CHUNK
PAIR_BAND = BAND + CHUNK


def _bias_table_body(rb_ref, o_ref, *, n_bucket, rel_future):
    ext_w = PAIR_BAND + LANE
    m = _iota((n_bucket, ext_w), 1)
    bucket = jnp.clip(PAIR_BAND - 1 - m, -rel_future, REL_PAST_CLIP) + rel_future
    sel = (bucket == _iota((n_bucket, ext_w), 0)).astype(BF16)
    ext = _dot_sel_r(rb_ref[...], sel)
    for r in range(PAIR):
        o_ref[r] = ext[:, PAIR - 1 - r:PAIR - 1 - r + PAIR_BAND]


def _bias_table(rel_bias):
    n_heads, n_bucket = rel_bias.shape
    out = pl.pallas_call(
        functools.partial(_bias_table_body, n_bucket=n_bucket, rel_future=CHUNK - 1),
        out_shape=jax.ShapeDtypeStruct((PAIR, n_heads, PAIR_BAND), F32),
        name="bias_table",
    )(rel_bias)
    return jnp.transpose(out, (1, 0, 2))


def _attn_body(q_ref, kp_ref, kc_ref, vp_ref, vc_ref, b_ref, o_ref, *, tq, dh, scale):
    i = pl.program_id(2)
    n_heads = q_ref.shape[1] // dh
    row, col = _iota((PAIR, PAIR_BAND), 0), _iota((PAIR, PAIR_BAND), 1)
    lo = (row // CHUNK) * CHUNK
    in_band = (col >= lo) & (col < lo + BAND)
    n_pairs = tq // PAIR
    k0 = [tq - LEFT_CHUNKS * CHUNK + jp * PAIR for jp in range(n_pairs)]
    work = [(h, jp) for h in range(n_heads) for jp in range(n_pairs)]

    def attend(first_block):
        hcols = [slice(h * dh, (h + 1) * dh) for h in range(n_heads)]
        kcat = [jnp.concatenate([kp_ref[:, hc], kc_ref[:, hc]], axis=0) for hc in hcols]
        vcat = [jnp.concatenate([vp_ref[:, hc], vc_ref[:, hc]], axis=0) for hc in hcols]
        s = [_dot_nt(q_ref[jp * PAIR:(jp + 1) * PAIR, hcols[h]], kcat[h][k0[jp]:k0[jp] + PAIR_BAND, :]) * scale
             + b_ref[h] for h, jp in work]
        if first_block:
            valid = [in_band & (col + k0[jp] >= tq) for jp in range(n_pairs)]
        else:
            valid = [in_band] * n_pairs
        s = [jnp.where(valid[jp], s_, -1e30) for (h, jp), s_ in zip(work, s)]
        mx = [jnp.max(s_, axis=-1, keepdims=True) for s_ in s]
        pexp = [jnp.exp(s_ - m_) for s_, m_ in zip(s, mx)]
        den = [jnp.sum(p_, axis=-1, keepdims=True) for p_ in pexp]
        o = [_dot(p_.astype(BF16), vcat[h][k0[jp]:k0[jp] + PAIR_BAND, :]) for (h, jp), p_ in zip(work, pexp)]
        for (h, jp), o_, d_ in zip(work, o, den):
            o_ref[jp * PAIR:(jp + 1) * PAIR, hcols[h]] = (o_ / d_).astype(o_ref.dtype)

    @pl.when(i == 0)
    def _():
        attend(True)

    @pl.when(i > 0)
    def _():
        attend(False)


def _band_attention(qkv, bias, batch, seq, n_heads, tq, hp):
    n, three_d = qkv.shape
    d = three_d // 3
    dh = d // n_heads
    nt = seq // tq
    ng = n_heads // hp
    cur = lambda off: (lambda b, h, i: (b * nt + i, off * ng + h))
    prv = lambda off: (lambda b, h, i: (b * nt + jnp.maximum(i - 1, 0), off * ng + h))
    blk = lambda f: pl.BlockSpec((tq, hp * dh), f)
    return pl.pallas_call(
        functools.partial(_attn_body, tq=tq, dh=dh, scale=dh ** -0.5),
        grid=(batch, ng, nt),
        in_specs=[blk(cur(0)), blk(prv(1)), blk(cur(1)), blk(prv(2)), blk(cur(2)),
                  pl.BlockSpec((hp, PAIR, PAIR_BAND), lambda b, h, i: (h, 0, 0))],
        out_specs=pl.BlockSpec((tq, hp * dh), lambda b, h, i: (b * nt + i, h)),
        out_shape=jax.ShapeDtypeStruct((n, d), BF16),
        compiler_params=_cparams(3),
        name="band_attention",
    )(qkv, qkv, qkv, qkv, qkv, bias)


def _pad_cols(a, width):
    return jnp.pad(a, ((0, 0), (0, width - a.shape[1])))


def _pad_rows(a, rows):
    return jnp.pad(a, ((0, rows - a.shape[0]), (0, 0)))


def _row(a):
    return a.reshape(1, -1)


def kernel(x, norm_g, w_in_ab, rwkv_mu, rwkv_w0, rwkv_w2, rwkv_a0, rwkv_a2, rwkv_g2, rwkv_k_k, rwkv_k_a, rwkv_r_k, rwkv_ln_w, rwkv_ln_b, ssm_conv_w, ssm_conv_b, ssm_dt_bias, ssm_A_log, ssm_D, ssm_norm_w, w_out_ab, w_qkv, attn_rel_bias, w_out_c, ffn_w_gate, ffn_w_up, ffn_w_down):
    batch, seq, d = x.shape
    n = batch * seq
    depth = norm_g.shape[0]
    xf = x.reshape(n, d)

    rw = rwkv_w0.shape[1]
    lora_w, lora_a, lora_g = rwkv_w2.shape[1], rwkv_a2.shape[1], rwkv_g2.shape[1]
    sw = ssm_norm_w.shape[1]
    n_ssm_heads = ssm_A_log.shape[1]
    conv_ch = ssm_conv_w.shape[2]
    groups = 2
    state = (conv_ch - sw) // (2 * groups)
    n_att_heads = attn_rel_bias.shape[1]
    rwkv_proj = 3 * rw + lora_w + lora_a + lora_g
    tm = min(1024, n)
    tc = min(256, seq)

    for l in range(depth):
        i = l // 2
        g = norm_g[l]
        if l % 2 == 0:
            w_in = w_in_ab[i]
            o1, o2, o3 = 3 * rw, 3 * rw + lora_w, 3 * rw + lora_w + lora_a
            seg = lambda a: [a[:, 0:o1], _pad_cols(a[:, o1:o2], LANE), _pad_cols(a[:, o2:o3], LANE), a[:, o3:rwkv_proj]]
            w_r = jnp.concatenate(seg(w_in), axis=1).astype(BF16)
            mu = jnp.concatenate(seg(_row(rwkv_mu[i])), axis=1)
            o4 = rwkv_proj + sw + conv_ch
            w_s = jnp.concatenate([w_in[:, rwkv_proj:o4], _pad_cols(w_in[:, o4:], LANE)], axis=1).astype(BF16)

            y_a = _rwkv_mix(
                xf, g[0:1], w_r, mu, _row(rwkv_w0[i]), _pad_rows(rwkv_w2[i], LANE).astype(BF16), _row(rwkv_a0[i]),
                _pad_rows(rwkv_a2[i], LANE).astype(BF16), rwkv_g2[i].astype(BF16), _row(rwkv_k_k[i]),
                _row(rwkv_k_a[i]), _row(rwkv_r_k[i]), _row(rwkv_ln_w[i]), _row(rwkv_ln_b[i]), batch, seq, tc)
            y_b = _ssd_mix(
                xf, g[0:1], w_s, ssm_conv_w[i], _row(ssm_conv_b[i]), _pad_cols(_row(ssm_dt_bias[i]), LANE),
                _pad_cols(_row(ssm_A_log[i]), LANE), _row(jnp.repeat(ssm_D[i], sw // n_ssm_heads)),
                _row(ssm_norm_w[i]), batch, seq, tc, sw, state, groups)
            w_o = w_out_ab[i].astype(BF16)
            xf = _proj_res([y_a, y_b], [w_o[:rw], w_o[rw:]], xf, g[1:2], tm // 2)
        else:
            qkv = _norm_matmul(xf, g[0:1], w_qkv[i].astype(BF16), BF16, tm, 1024)
            bias = _bias_table(attn_rel_bias[i])
            att = _band_attention(qkv, bias, batch, seq, n_att_heads, min(512, seq), 2)
            xf = _proj_res([att], [w_out_c[i].astype(BF16)], xf, g[1:2], tm // 2)
        xf = _ffn(xf, g[2:3], g[3:4], ffn_w_gate[l].astype(BF16), ffn_w_up[l].astype(BF16),
                  ffn_w_down[l].astype(BF16), tm, 512)
    return xf.reshape(batch, seq, d)
```

```python
import functools
import math

import jax
import jax.numpy as jnp
from jax import lax
from jax.experimental import pallas as pl
from jax.experimental.pallas import tpu as pltpu

F32 = jnp.float32
BF16 = jnp.bfloat16

NORM_EPS = 1e-6
RWKV_LN_EPS = 64e-5
CHUNK = 64
HEAD = 64
LANE = 128
LEFT_CHUNKS = 8
REL_PAST_CLIP = 256
LOG2E = math.log2(math.e)
VMEM_LIMIT = 60 * 1024 * 1024


def _cparams(n_axes):
    return pltpu.CompilerParams(dimension_semantics=("arbitrary",) * n_axes, vmem_limit_bytes=VMEM_LIMIT)


def _dot(a, b):
    return jnp.dot(a, b, preferred_element_type=F32)


def _dot_nt(a, b):
    return lax.dot_general(a, b, (((1,), (1,)), ((), ())), preferred_element_type=F32)


def _dot_tn(a, b):
    return lax.dot_general(a, b, (((0,), (0,)), ((), ())), preferred_element_type=F32)


def _split2(t):
    hi = t.astype(BF16)
    lo = (t - hi.astype(F32)).astype(BF16)
    return hi, lo


def _dot_sel_r(t, sel):
    hi, lo = _split2(t)
    return _dot(hi, sel) + _dot(lo, sel)


def _dot_sel_l(sel, t):
    hi, lo = _split2(t)
    return _dot(sel, hi) + _dot(sel, lo)


def _rms(x, g):
    ms = jnp.mean(x * x, axis=-1, keepdims=True)
    return x * lax.rsqrt(ms + NORM_EPS) * g


def _sigmoid(x):
    return 1.0 / (1.0 + jnp.exp(-x))


def _silu(x):
    return x * _sigmoid(x)


def _softplus(x):
    return jnp.maximum(x, 0.0) + jnp.log1p(jnp.exp(-jnp.abs(x)))


def _iota(shape, axis):
    return lax.broadcasted_iota(jnp.int32, shape, axis)


def _norm_matmul_body(x_ref, g_ref, w_ref, o_ref, h_ref, *, n_scaled, scale):
    j = pl.program_id(1)

    @pl.when(j == 0)
    def _():
        h_ref[...] = _rms(x_ref[...], g_ref[...]).astype(BF16)

    mult = jnp.where(j < n_scaled, jnp.float32(scale), jnp.float32(1.0))
    o_ref[...] = (_dot(h_ref[...], w_ref[...]) * mult).astype(o_ref.dtype)


def _norm_matmul(x, g, w, out_dtype, tm, tn, n_scaled=0, scale=1.0):
    n, d = x.shape
    m = w.shape[1]
    return pl.pallas_call(
        functools.partial(_norm_matmul_body, n_scaled=n_scaled, scale=scale),
        grid=(n // tm, m // tn),
        in_specs=[
            pl.BlockSpec((tm, d), lambda i, j: (i, 0)),
            pl.BlockSpec((1, d), lambda i, j: (0, 0)),
            pl.BlockSpec((d, tn), lambda i, j: (0, j)),
        ],
        out_specs=pl.BlockSpec((tm, tn), lambda i, j: (i, j)),
        out_shape=jax.ShapeDtypeStruct((n, m), out_dtype),
        scratch_shapes=[pltpu.VMEM((tm, d), BF16)],
        compiler_params=_cparams(2),
        name="norm_matmul",
    )(x, g, w)


def _rms_cast_body(x_ref, g_ref, o_ref):
    o_ref[...] = _rms(x_ref[...], g_ref[...]).astype(o_ref.dtype)


def _rms_cast(x, g, tm):
    n, d = x.shape
    return pl.pallas_call(
        _rms_cast_body,
        grid=(n // tm,),
        in_specs=[pl.BlockSpec((tm, d), lambda i: (i, 0)), pl.BlockSpec((1, d), lambda i: (0, 0))],
        out_specs=pl.BlockSpec((tm, d), lambda i: (i, 0)),
        out_shape=jax.ShapeDtypeStruct((n, d), BF16),
        compiler_params=_cparams(1),
        name="rms_cast",
    )(x, g)


def _proj_res_body(*refs, n_in):
    y_refs, w_refs = refs[:n_in], refs[n_in:2 * n_in]
    x_ref, g_ref, o_ref = refs[2 * n_in:]
    m = _dot(y_refs[0][...], w_refs[0][...])
    for y_ref, w_ref in zip(y_refs[1:], w_refs[1:]):
        m = m + _dot(y_ref[...], w_ref[...])
    o_ref[...] = x_ref[...] + _rms(m, g_ref[...])


def _proj_res(ys, ws, x, g, tm):
    n, d = x.shape
    n_in = len(ys)
    in_specs = [pl.BlockSpec((tm, y.shape[1]), lambda i: (i, 0)) for y in ys]
    in_specs += [pl.BlockSpec(w.shape, lambda i: (0, 0)) for w in ws]
    in_specs += [pl.BlockSpec((tm, d), lambda i: (i, 0)), pl.BlockSpec((1, d), lambda i: (0, 0))]
    return pl.pallas_call(
        functools.partial(_proj_res_body, n_in=n_in),
        grid=(n // tm,),
        in_specs=in_specs,
        out_specs=pl.BlockSpec((tm, d), lambda i: (i, 0)),
        out_shape=jax.ShapeDtypeStruct((n, d), F32),
        compiler_params=_cparams(1),
        name="proj_res",
    )(*ys, *ws, x, g)


def _ffn_body(x_ref, gi_ref, go_ref, wg_ref, wu_ref, wd_ref, o_ref, h_ref):
    j = pl.program_id(1)
    last_j = pl.num_programs(1) - 1
    tm = x_ref.shape[0]

    def partial_out(h):
        a = (_silu(_dot(h, wg_ref[...])) * _dot(h, wu_ref[...])).astype(BF16)
        return _dot(a, wd_ref[...])

    row_groups = [slice(r, r + FFN_EDGE_ROWS) for r in range(0, tm, FFN_EDGE_ROWS)]

    @pl.when(j == 0)
    def _():
        for rs in row_groups:
            h = _rms(x_ref[rs, :], gi_ref[...]).astype(BF16)
            h_ref[rs, :] = h
            o_ref[rs, :] = partial_out(h)

    @pl.when((j > 0) & (j < last_j))
    def _():
        o_ref[...] += partial_out(h_ref[...])

    @pl.when(j == last_j)
    def _():
        for rs in row_groups:
            acc = o_ref[rs, :] + partial_out(h_ref[rs, :])
            o_ref[rs, :] = x_ref[rs, :] + _rms(acc, go_ref[...])


def _ffn(x, g_in, g_out, wg, wu, wd, tm, tf):
    n, d = x.shape
    f = wg.shape[1]
    assert f // tf >= 2 and tm % FFN_EDGE_ROWS == 0
    return pl.pallas_call(
        _ffn_body,
        grid=(n // tm, f // tf),
        in_specs=[
            pl.BlockSpec((tm, d), lambda i, j: (i, 0), pipeline_mode=pl.Buffered(1)),
            pl.BlockSpec((1, d), lambda i, j: (0, 0)),
            pl.BlockSpec((1, d), lambda i, j: (0, 0)),
            pl.BlockSpec((d, tf), lambda i, j: (0, j)),
            pl.BlockSpec((d, tf), lambda i, j: (0, j)),
            pl.BlockSpec((tf, d), lambda i, j: (j, 0)),
        ],
        out_specs=pl.BlockSpec((tm, d), lambda i, j: (i, 0)),
        out_shape=jax.ShapeDtypeStruct((n, d), F32),
        scratch_shapes=[pltpu.VMEM((tm, d), BF16)],
        compiler_params=_cparams(2),
        name="ffn",
    )(x, g_in, g_out, wg, wu, wd)


def _seg_matrices(width, n_seg_pad):
    e = (_iota((width, n_seg_pad), 0) // HEAD == _iota((width, n_seg_pad), 1)).astype(BF16)
    et = (_iota((n_seg_pad, width), 1) // HEAD == _iota((n_seg_pad, width), 0)).astype(BF16)
    return e, et


def _blockdiag_tril(tc):
    r, c = _iota((tc, tc), 0), _iota((tc, tc), 1)
    return ((r // CHUNK == c // CHUNK) & (c <= r)).astype(BF16)


def _rwkv_body(h_ref, w_ref, mu_ref, w0_ref, w2_ref, a0_ref, a2_ref, g2_ref, kk_ref, ka_ref, rk_ref,
               lnw_ref, lnb_ref, o_ref, p_ref, prow_ref, s_ref, at_ref, rt_ref, bt_ref, kt_ref, v_ref, gl_ref, y_ref,
               *, tc, width):
    step = pl.program_id(1)

    tw = p_ref.shape[1] // N_PROJ_TILES

    def h_next_fn():
        return h_ref[...]

    def project_tile(h, k):
        p_ref[:, k * tw:(k + 1) * tw] = _dot(h, w_ref[:, k * tw:(k + 1) * tw])

    @pl.when(step == 0)
    def _():
        s_ref[...] = jnp.zeros_like(s_ref)
        prow_ref[...] = jnp.zeros_like(prow_ref)
        h = h_next_fn()
        for k in range(N_PROJ_TILES):
            project_tile(h, k)

    @pl.when(step > 0)
    def _():
        _rwkv_mix_block(h_next_fn, project_tile, mu_ref, w0_ref, w2_ref, a0_ref, a2_ref, g2_ref, kk_ref, ka_ref, rk_ref, lnw_ref,
                        lnb_ref, o_ref, p_ref, prow_ref, s_ref, at_ref, rt_ref, bt_ref, kt_ref, v_ref, gl_ref, y_ref,
                        tc=tc, width=width)


def _rwkv_mix_block(h_next_fn, project_tile, mu_ref, w0_ref, w2_ref, a0_ref, a2_ref, g2_ref, kk_ref, ka_ref, rk_ref, lnw_ref,
                    lnb_ref, o_ref, p_ref, prow_ref, s_ref, at_ref, rt_ref, bt_ref, kt_ref, v_ref, gl_ref, y_ref,
                    *, tc, width):
    n_heads = width // HEAD
    n_chunks = tc // CHUNK

    p = p_ref[...]
    first_row = _iota((tc, 1), 0) == 0

    def lerp(cols):
        pc = p[:, cols]
        psh = jnp.where(first_row, prow_ref[0:1, cols], pltpu.roll(pc, 1, 0))
        return pc + (psh - pc) * mu_ref[:, cols]

    o1 = 3 * width
    xl = lerp(slice(o1, o1 + 4 * LANE))
    wc, ac, gc = xl[:, 0:LANE], xl[:, LANE:2 * LANE], xl[:, 2 * LANE:4 * LANE]
    z = w0_ref[...] + _dot(jnp.tanh(wc).astype(BF16), w2_ref[...])
    a = _sigmoid(a0_ref[...] + _dot(ac.astype(BF16), a2_ref[...]))
    g = _dot(_sigmoid(gc).astype(BF16), g2_ref[...])
    lw = -jnp.exp(-_softplus(-z) - 0.5)

    h_next = h_next_fn()
    project_tile(h_next, 0)
    project_tile(h_next, 1)

    xr = lerp(slice(0, o1))
    last_row = p[tc - 1:tc, :]
    r = xr[:, 0:width]
    k = xr[:, width:2 * width]
    v = xr[:, 2 * width:3 * width]

    mxu_w = 2 * LANE
    same_head = (_iota((mxu_w, mxu_w), 0) // HEAD == _iota((mxu_w, mxu_w), 1) // HEAD).astype(BF16)

    def seg_sum(t):
        return jnp.concatenate(
            [_dot_sel_r(t[:, j:j + mxu_w], same_head) for j in range(0, width, mxu_w)], axis=1)

    kk = k * kk_ref[...]
    kk = kk * lax.rsqrt(jnp.maximum(seg_sum(kk * kk), 1e-24))
    k2 = k * (1.0 + (a - 1.0) * ka_ref[...])
    project_tile(h_next, 2)
    bonus = seg_sum(r * k2 * rk_ref[...]) * v
    cs = _dot_sel_l(_blockdiag_tril(tc), lw)
    project_tile(h_next, 3)
    prow_ref[0:1, :] = last_row

    ginv = jnp.exp(-cs)
    at_ref[...] = (-kk * jnp.exp(cs - lw)).astype(BF16)
    rt_ref[...] = (r * jnp.exp(cs)).astype(BF16)
    bt_ref[...] = (kk * a * ginv).astype(BF16)
    kt_ref[...] = (k2 * ginv).astype(BF16)
    v_ref[...] = v.astype(BF16)
    for c in range(n_chunks):
        last = c * CHUNK + CHUNK - 1
        gl_ref[c:c + 1, :] = jnp.exp(cs[last:last + 1, :])

    rr, cc = _iota((CHUNK, CHUNK), 0), _iota((CHUNK, CHUNK), 1)
    strict = (cc < rr).astype(F32)
    rr2, cc2 = _iota((CHUNK, 2 * CHUNK), 0), _iota((CHUNK, 2 * CHUNK), 1) % CHUNK
    strict2 = (cc2 < rr2).astype(F32)
    incl2 = (cc2 <= rr2).astype(F32)
    eye = (cc == rr).astype(F32)
    n_lvl = int(math.log2(CHUNK))
    lvl_masks = []
    for lv in range(n_lvl):
        same = (rr >> (lv + 1)) == (cc >> (lv + 1))
        lvl_masks.append((same & (((rr >> lv) & 1) == 1) & (((cc >> lv) & 1) == 0)).astype(F32))

    def chunk_step(c, carry):
        t0 = pl.multiple_of(c * CHUNK, CHUNK)
        rows = pl.ds(t0, CHUNK)
        gl = gl_ref[pl.ds(c, 1), :]
        heads = range(n_heads)
        hsl = [slice(h * HEAD, (h + 1) * HEAD) for h in heads]
        vh = [v_ref[rows, hs] for hs in hsl]
        ar = [jnp.concatenate([at_ref[rows, hs], rt_ref[rows, hs]], axis=0) for hs in hsl]
        bk = [jnp.concatenate([bt_ref[rows, hs], kt_ref[rows, hs]], axis=0) for hs in hsl]
        pm = [_dot_nt(ar[h], bk[h]) for h in heads]
        s0 = [s_ref[h] for h in heads]
        ars = [_dot_nt(ar[h], s0[h].astype(BF16)) for h in heads]
        pa = [pm[h][:CHUNK, :] * strict2 for h in heads]
        aab = [pa[h][:, :CHUNK] for h in heads]
        rhs = [ars[h][:CHUNK, :] + _dot(pa[h][:, CHUNK:].astype(BF16), vh[h]) for h in heads]
        x = [eye + aab[h] * lvl_masks[0] for h in heads]
        for lv in range(1, n_lvl):
            xb = [x[h].astype(BF16) for h in heads]
            tm_ = [_dot((aab[h] * lvl_masks[lv]).astype(BF16), xb[h]) for h in heads]
            x = [x[h] + _dot(xb[h], tm_[h].astype(BF16)) for h in heads]
        u = [_dot(x[h].astype(BF16), rhs[h].astype(BF16)) for h in heads]
        uv = [jnp.concatenate([u[h].astype(BF16), vh[h]], axis=0) for h in heads]
        for h in heads:
            pr = (pm[h][CHUNK:, :] * incl2).astype(BF16)
            y_ref[rows, hsl[h]] = ars[h][CHUNK:, :] + _dot(pr, uv[h])
        for h in heads:
            s_ref[h] = (s0[h] + _dot_tn(uv[h], bk[h])) * gl[:, hsl[h]]
        return carry

    lax.fori_loop(0, n_chunks, chunk_step, 0)

    y = y_ref[...]
    inv_n = 1.0 / HEAD
    mean = seg_sum(y) * inv_n
    yc = y - mean
    var = seg_sum(yc * yc) * inv_n
    yn = yc * lax.rsqrt(var + RWKV_LN_EPS) * lnw_ref[...] + lnb_ref[...]
    o_ref[...] = ((yn + bonus) * g).astype(o_ref.dtype)


def _mixer_specs(h, w, params, batch, seq, tc, width):
    nt = seq // tc
    d = h.shape[1]
    const = lambda b, s: (0, 0)
    in_specs = [
        pl.BlockSpec((tc, d), lambda b, s: (b * nt + jnp.minimum(s, nt - 1), 0)),
        pl.BlockSpec(w.shape, const, pipeline_mode=pl.Buffered(1)),
    ] + [pl.BlockSpec(a.shape, const) for a in params]
    out_spec = pl.BlockSpec((tc, width), lambda b, s: (b * nt + jnp.maximum(s - 1, 0), 0))
    return (batch, nt + 1), in_specs, out_spec


def _rwkv_mix(h, w, mu, w0, w2, a0, a2, g2, k_k, k_a, r_k, ln_w, ln_b, batch, seq, tc):
    n = h.shape[0]
    pw = w.shape[1]
    width = w0.shape[1]
    n_heads = width // HEAD
    params = (mu, w0, w2, a0, a2, g2, k_k, k_a, r_k, ln_w, ln_b)
    grid, in_specs, out_spec = _mixer_specs(h, w, params, batch, seq, tc, width)
    return pl.pallas_call(
        functools.partial(_rwkv_body, tc=tc, width=width),
        grid=grid,
        in_specs=in_specs,
        out_specs=out_spec,
        out_shape=jax.ShapeDtypeStruct((n, width), BF16),
        scratch_shapes=[pltpu.VMEM((tc, pw), F32), pltpu.VMEM((8, pw), F32), pltpu.VMEM((n_heads, HEAD, HEAD), F32)]
        + [pltpu.VMEM((tc, width), BF16) for _ in range(5)]
        + [pltpu.VMEM((max(tc // CHUNK, 8), width), F32), pltpu.VMEM((tc, width), F32)],
        compiler_params=_cparams(2),
        name="rwkv_mix",
    )(h, w, *params)


def _ssd_body(h_ref, w_ref, cw_ref, cb_ref, dtb_ref, alog_ref, dskip_ref, nw_ref, o_ref,
              p_ref, tail_ref, st_ref, y_ref, *, tc, width, state, groups):
    step = pl.program_id(1)
    n_chunks = tc // CHUNK
    n_lane_tiles = p_ref.shape[1] // LANE
    n_tiles = max(t for t in range(1, n_chunks + 1) if n_lane_tiles % t == 0)
    tw = p_ref.shape[1] // n_tiles

    def project_tile(h, k):
        p_ref[:, k * tw:(k + 1) * tw] = _dot(h, w_ref[:, k * tw:(k + 1) * tw])

    @pl.when(step == 0)
    def _():
        st_ref[...] = jnp.zeros_like(st_ref)
        tail_ref[...] = jnp.zeros_like(tail_ref)
        h = h_ref[...]
        for k in range(n_tiles):
            project_tile(h, k)

    @pl.when(step > 0)
    def _():
        _ssd_mix_block(h_ref, project_tile, n_tiles, cw_ref, cb_ref, dtb_ref, alog_ref, dskip_ref, nw_ref,
                       o_ref, p_ref, tail_ref, st_ref, y_ref, tc=tc, width=width, state=state, groups=groups)


def _ssd_mix_block(h_ref, project_tile, n_tiles, cw_ref, cb_ref, dtb_ref, alog_ref, dskip_ref, nw_ref,
                   o_ref, p_ref, tail_ref, st_ref, y_ref, *, tc, width, state, groups):
    n_chunks = tc // CHUNK
    gw = width // groups
    hpg = gw // HEAD
    conv_ch = width + 2 * groups * state
    n_tap = cw_ref.shape[0]

    p = p_ref[...]
    zs = _silu(p[:, 0:width])
    xbc_raw = p[:, width:width + conv_ch]
    dt_raw = p[:, width + conv_ch:width + conv_ch + LANE]
    h_next = h_ref[...]
    tiles_after_chunk = {n_chunks // 2 - 1 + k * (n_chunks // 2): k + 1 for k in range(n_tiles - 1)}
    project_tile(h_next, 0)

    prev = tail_ref[...]
    tail_ref[...] = xbc_raw[tc - 8:tc, :]
    xcat = jnp.concatenate([prev, xbc_raw], axis=0)
    conv = cb_ref[...]
    for j in range(n_tap):
        off = 8 - (n_tap - 1) + j
        conv = conv + cw_ref[j:j + 1, :] * xcat[off:off + tc, :]
    xbc = _silu(conv)
    xs = xbc[:, 0:width]
    bm = xbc[:, width:width + groups * state].astype(BF16)
    cm = xbc[:, width + groups * state:conv_ch].astype(BF16)

    e, et = _seg_matrices(width, LANE)
    dt = _softplus(dt_raw + dtb_ref[...])
    da = -jnp.exp(alog_ref[...]) * dt
    cs_h = _dot_sel_l(_blockdiag_tril(tc), da)
    cs = _dot_sel_r(cs_h, et)
    dtx = _dot_sel_r(dt, et)
    xc = xs * dtx
    ecs = jnp.exp(cs)

    rr, cc = _iota((CHUNK, CHUNK), 0), _iota((CHUNK, CHUNK), 1)
    causal = cc <= rr

    for c in range(n_chunks):
        r0 = c * CHUNK
        rows = slice(r0, r0 + CHUNK)
        cs_c = cs[rows, :]
        cs_last = cs_c[CHUNK - 1:CHUNK, :]
        xc_c = xc[rows, :]
        xd = (xc_c * jnp.exp(cs_last - cs_c)).astype(BF16)
        xcb = xc_c.astype(BF16)
        cs_t = cs_h[rows, :].T
        for g in range(groups):
            gl = slice(g * gw, (g + 1) * gw)
            b_g = bm[rows, g * state:(g + 1) * state]
            c_g = cm[rows, g * state:(g + 1) * state]
            cb = _dot_nt(c_g, b_g)
            st = st_ref[g]
            y_off = _dot(c_g, st.astype(BF16)) * ecs[rows, gl]
            for hh in range(hpg):
                h = g * hpg + hh
                hs = slice(h * HEAD, (h + 1) * HEAD)
                seg = cs_c[:, hs] - cs_t[h:h + 1, :]
                lmat = jnp.where(causal, jnp.exp(jnp.where(causal, seg, 0.0)), 0.0)
                y_ref[rows, hs] = _dot((cb * lmat).astype(BF16), xcb[:, hs]) + y_off[:, hh * HEAD:(hh + 1) * HEAD]
            st_ref[g] = st * ecs[r0 + CHUNK - 1:r0 + CHUNK, gl] + _dot_tn(b_g, xd[:, gl])
        if c in tiles_after_chunk:
            project_tile(h_next, tiles_after_chunk[c])
    assert sorted(tiles_after_chunk.values()) == list(range(1, n_tiles)) and max(tiles_after_chunk) < n_chunks

    y = (y_ref[...] + dskip_ref[...] * xs) * zs
    for g in range(groups):
        gl = slice(g * gw, (g + 1) * gw)
        yg = y[:, gl]
        ms = jnp.mean(yg * yg, axis=-1, keepdims=True)
        o_ref[:, gl] = (yg * lax.rsqrt(ms + NORM_EPS) * nw_ref[:, gl]).astype(o_ref.dtype)


def _ssd_mix(h, w, conv_w, conv_b, dt_bias, a_log, d_skip, norm_w, batch, seq, tc, width, state, groups):
    n = h.shape[0]
    pw = w.shape[1]
    conv_ch = width + 2 * groups * state
    params = (conv_w, conv_b, dt_bias, a_log, d_skip, norm_w)
    grid, in_specs, out_spec = _mixer_specs(h, w, params, batch, seq, tc, width)
    return pl.pallas_call(
        functools.partial(_ssd_body, tc=tc, width=width, state=state, groups=groups),
        grid=grid,
        in_specs=in_specs,
        out_specs=out_spec,
        out_shape=jax.ShapeDtypeStruct((n, width), BF16),
        scratch_shapes=[pltpu.VMEM((tc, pw), F32), pltpu.VMEM((8, conv_ch), F32),
                        pltpu.VMEM((groups, state, width // groups), F32), pltpu.VMEM((tc, width), F32)],
        compiler_params=_cparams(2),
        name="ssd_mix",
    )(h, w, *params)


FFN_EDGE_ROWS = 256
N_PROJ_TILES = 4
PAIR = 2 * CHUNK
BAND = (LEFT_CHUNKS + 1) * CHUNK
PAIR_BAND = BAND + CHUNK


def _bias_table_body(rb_ref, o_ref, *, n_bucket, rel_future):
    ext_w = PAIR_BAND + LANE
    m = _iota((n_bucket, ext_w), 1)
    bucket = jnp.clip(PAIR_BAND - 1 - m, -rel_future, REL_PAST_CLIP) + rel_future
    sel = (bucket == _iota((n_bucket, ext_w), 0)).astype(BF16)
    ext = _dot_sel_r(rb_ref[...], sel) * LOG2E
    for r in range(PAIR):
        o_ref[r] = ext[:, PAIR - 1 - r:PAIR - 1 - r + PAIR_BAND]


def _bias_table(rel_bias):
    n_heads, n_bucket = rel_bias.shape
    out = pl.pallas_call(
        functools.partial(_bias_table_body, n_bucket=n_bucket, rel_future=CHUNK - 1),
        out_shape=jax.ShapeDtypeStruct((PAIR, n_heads, PAIR_BAND), F32),
        name="bias_table",
    )(rel_bias)
    return jnp.transpose(out, (1, 0, 2))


def _attn_body(q_ref, kp_ref, kc_ref, vp_ref, vc_ref, b_ref, o_ref, *, tq, dh):
    i = pl.program_id(2)
    n_heads = q_ref.shape[1] // dh
    row, col = _iota((PAIR, PAIR_BAND), 0), _iota((PAIR, PAIR_BAND), 1)
    lo = (row // CHUNK) * CHUNK
    in_band = (col >= lo) & (col < lo + BAND)
    n_pairs = tq // PAIR
    k0 = [tq - LEFT_CHUNKS * CHUNK + jp * PAIR for jp in range(n_pairs)]
    work = [(h, jp) for h in range(n_heads) for jp in range(n_pairs)]

    def attend(first_block):
        hcols = [slice(h * dh, (h + 1) * dh) for h in range(n_heads)]
        kcat = [jnp.concatenate([kp_ref[:, hc], kc_ref[:, hc]], axis=0) for hc in hcols]
        vcat = [jnp.concatenate([vp_ref[:, hc], vc_ref[:, hc]], axis=0) for hc in hcols]
        s = [_dot_nt(q_ref[jp * PAIR:(jp + 1) * PAIR, hcols[h]], kcat[h][k0[jp]:k0[jp] + PAIR_BAND, :]) + b_ref[h]
             for h, jp in work]
        if first_block:
            valid = [in_band & (col + k0[jp] >= tq) for jp in range(n_pairs)]
        else:
            valid = [in_band] * n_pairs
        s = [jnp.where(valid[jp], s_, -1e30) for (h, jp), s_ in zip(work, s)]
        mx = [jnp.max(s_, axis=-1, keepdims=True) for s_ in s]
        pexp = [jnp.exp2(s_ - m_) for s_, m_ in zip(s, mx)]
        den = [jnp.sum(p_, axis=-1, keepdims=True) for p_ in pexp]
        o = [_dot(p_.astype(BF16), vcat[h][k0[jp]:k0[jp] + PAIR_BAND, :]) for (h, jp), p_ in zip(work, pexp)]
        for (h, jp), o_, d_ in zip(work, o, den):
            o_ref[jp * PAIR:(jp + 1) * PAIR, hcols[h]] = (o_ / d_).astype(o_ref.dtype)

    @pl.when(i == 0)
    def _():
        attend(True)

    @pl.when(i > 0)
    def _():
        attend(False)


def _band_attention(qkv, bias, batch, seq, n_heads, tq, hp):
    n, three_d = qkv.shape
    d = three_d // 3
    dh = d // n_heads
    nt = seq // tq
    ng = n_heads // hp
    cur = lambda off: (lambda b, h, i: (b * nt + i, off * ng + h))
    prv = lambda off: (lambda b, h, i: (b * nt + jnp.maximum(i - 1, 0), off * ng + h))
    blk = lambda f: pl.BlockSpec((tq, hp * dh), f)
    return pl.pallas_call(
        functools.partial(_attn_body, tq=tq, dh=dh),
        grid=(batch, ng, nt),
        in_specs=[blk(cur(0)), blk(prv(1)), blk(cur(1)), blk(prv(2)), blk(cur(2)),
                  pl.BlockSpec((hp, PAIR, PAIR_BAND), lambda b, h, i: (h, 0, 0))],
        out_specs=pl.BlockSpec((tq, hp * dh), lambda b, h, i: (b * nt + i, h)),
        out_shape=jax.ShapeDtypeStruct((n, d), BF16),
        compiler_params=_cparams(3),
        name="band_attention",
    )(qkv, qkv, qkv, qkv, qkv, bias)


def _pad_cols(a, width):
    return jnp.pad(a, ((0, 0), (0, width - a.shape[1])))


def _pad_rows(a, rows):
    return jnp.pad(a, ((0, rows - a.shape[0]), (0, 0)))


def _row(a):
    return a.reshape(1, -1)


def kernel(x, norm_g, w_in_ab, rwkv_mu, rwkv_w0, rwkv_w2, rwkv_a0, rwkv_a2, rwkv_g2, rwkv_k_k, rwkv_k_a, rwkv_r_k, rwkv_ln_w, rwkv_ln_b, ssm_conv_w, ssm_conv_b, ssm_dt_bias, ssm_A_log, ssm_D, ssm_norm_w, w_out_ab, w_qkv, attn_rel_bias, w_out_c, ffn_w_gate, ffn_w_up, ffn_w_down):
    batch, seq, d = x.shape
    n = batch * seq
    depth = norm_g.shape[0]
    xf = x.reshape(n, d)

    rw = rwkv_w0.shape[1]
    lora_w, lora_a, lora_g = rwkv_w2.shape[1], rwkv_a2.shape[1], rwkv_g2.shape[1]
    sw = ssm_norm_w.shape[1]
    n_ssm_heads = ssm_A_log.shape[1]
    conv_ch = ssm_conv_w.shape[2]
    groups = 2
    state = (conv_ch - sw) // (2 * groups)
    n_att_heads = attn_rel_bias.shape[1]
    rwkv_proj = 3 * rw + lora_w + lora_a + lora_g
    tm = min(1024, n)
    tc = min(256, seq)

    for l in range(depth):
        i = l // 2
        g = norm_g[l]
        if l % 2 == 0:
            w_in = w_in_ab[i]
            o1, o2, o3 = 3 * rw, 3 * rw + lora_w, 3 * rw + lora_w + lora_a
            seg = lambda a: [a[:, 0:o1], _pad_cols(a[:, o1:o2], LANE), _pad_cols(a[:, o2:o3], LANE), a[:, o3:rwkv_proj]]
            w_r = jnp.concatenate(seg(w_in), axis=1).astype(BF16)
            mu = jnp.concatenate(seg(_row(rwkv_mu[i])), axis=1)
            o4 = rwkv_proj + sw + conv_ch
            w_s = jnp.concatenate([w_in[:, rwkv_proj:o4], _pad_cols(w_in[:, o4:], LANE)], axis=1).astype(BF16)

            h0 = _rms_cast(xf, g[0:1], tm)
            y_a = _rwkv_mix(
                h0, w_r, mu, _row(rwkv_w0[i]), _pad_rows(rwkv_w2[i], LANE).astype(BF16), _row(rwkv_a0[i]),
                _pad_rows(rwkv_a2[i], LANE).astype(BF16), rwkv_g2[i].astype(BF16), _row(rwkv_k_k[i]),
                _row(rwkv_k_a[i]), _row(rwkv_r_k[i]), _row(rwkv_ln_w[i]), _row(rwkv_ln_b[i]), batch, seq, tc)
            y_b = _ssd_mix(
                h0, w_s, ssm_conv_w[i], _row(ssm_conv_b[i]), _pad_cols(_row(ssm_dt_bias[i]), LANE),
                _pad_cols(_row(ssm_A_log[i]), LANE), _row(jnp.repeat(ssm_D[i], sw // n_ssm_heads)),
                _row(ssm_norm_w[i]), batch, seq, tc, sw, state, groups)
            w_o = w_out_ab[i].astype(BF16)
            xf = _proj_res([y_a, y_b], [w_o[:rw], w_o[rw:]], xf, g[1:2], tm // 2)
        else:
            tn = 1024
            qkv = _norm_matmul(xf, g[0:1], w_qkv[i].astype(BF16), BF16, tm, tn, n_scaled=d // tn,
                               scale=(d // n_att_heads) ** -0.5 * LOG2E)
            bias = _bias_table(attn_rel_bias[i])
            att = _band_attention(qkv, bias, batch, seq, n_att_heads, min(512, seq), 2)
            xf = _proj_res([att], [w_out_c[i].astype(BF16)], xf, g[1:2], tm // 2)
        xf = _ffn(xf, g[2:3], g[3:4], ffn_w_gate[l].astype(BF16), ffn_w_up[l].astype(BF16),
                  ffn_w_down[l].astype(BF16), tm, 512)
    return xf.reshape(batch, seq, d)
```

```python
import functools
import math
from typing import NamedTuple

import jax
import jax.numpy as jnp
from jax import lax
from jax.experimental import pallas as pl
from jax.experimental.pallas import tpu as pltpu

F32 = jnp.float32
BF16 = jnp.bfloat16

NORM_EPS = 1e-6
RWKV_LN_EPS = 64e-5
CHUNK = 64
HEAD = 64
LANE = 128
LEFT_CHUNKS = 8
REL_PAST_CLIP = 256
LOG2E = math.log2(math.e)
VMEM_LIMIT = 60 * 1024 * 1024


def _cparams(n_axes):
    return pltpu.CompilerParams(dimension_semantics=("arbitrary",) * n_axes, vmem_limit_bytes=VMEM_LIMIT)


def _dot(a, b):
    return jnp.dot(a, b, preferred_element_type=F32)


def _dot_nt(a, b):
    return lax.dot_general(a, b, (((1,), (1,)), ((), ())), preferred_element_type=F32)


def _dot_tn(a, b):
    return lax.dot_general(a, b, (((0,), (0,)), ((), ())), preferred_element_type=F32)


def _split2(t):
    hi = t.astype(BF16)
    lo = (t - hi.astype(F32)).astype(BF16)
    return hi, lo


def _dot_sel_r(t, sel):
    hi, lo = _split2(t)
    return _dot(hi, sel) + _dot(lo, sel)


def _dot_sel_l(sel, t):
    hi, lo = _split2(t)
    return _dot(sel, hi) + _dot(sel, lo)


def _rms(x, g):
    ms = jnp.mean(x * x, axis=-1, keepdims=True)
    return x * lax.rsqrt(ms + NORM_EPS) * g


def _sigmoid(x):
    return 1.0 / (1.0 + jnp.exp(-x))


def _silu(x):
    return x * _sigmoid(x)


def _softplus(x):
    return jnp.maximum(x, 0.0) + jnp.log1p(jnp.exp(-jnp.abs(x)))


def _iota(shape, axis):
    return lax.broadcasted_iota(jnp.int32, shape, axis)


def _norm_matmul_body(x_ref, g_ref, w_ref, o_ref, h_ref, *, n_scaled, scale):
    j = pl.program_id(1)

    @pl.when(j == 0)
    def _():
        h_ref[...] = _rms(x_ref[...], g_ref[...]).astype(BF16)

    mult = jnp.where(j < n_scaled, jnp.float32(scale), jnp.float32(1.0))
    o_ref[...] = (_dot(h_ref[...], w_ref[...]) * mult).astype(o_ref.dtype)


def _norm_matmul(x, g, w, out_dtype, tm, tn, n_scaled=0, scale=1.0):
    n, d = x.shape
    m = w.shape[1]
    return pl.pallas_call(
        functools.partial(_norm_matmul_body, n_scaled=n_scaled, scale=scale),
        grid=(n // tm, m // tn),
        in_specs=[
            pl.BlockSpec((tm, d), lambda i, j: (i, 0)),
            pl.BlockSpec((1, d), lambda i, j: (0, 0)),
            pl.BlockSpec((d, tn), lambda i, j: (0, j)),
        ],
        out_specs=pl.BlockSpec((tm, tn), lambda i, j: (i, j)),
        out_shape=jax.ShapeDtypeStruct((n, m), out_dtype),
        scratch_shapes=[pltpu.VMEM((tm, d), BF16)],
        compiler_params=_cparams(2),
        name="norm_matmul",
    )(x, g, w)


def _proj_res_body(*refs, n_in):
    y_refs, w_refs = refs[:n_in], refs[n_in:2 * n_in]
    x_ref, g_ref, o_ref = refs[2 * n_in:]
    m = _dot(y_refs[0][...], w_refs[0][...])
    for y_ref, w_ref in zip(y_refs[1:], w_refs[1:]):
        m = m + _dot(y_ref[...], w_ref[...])
    o_ref[...] = x_ref[...] + _rms(m, g_ref[...])


def _proj_res(ys, ws, x, g, tm):
    n, d = x.shape
    n_in = len(ys)
    in_specs = [pl.BlockSpec((tm, y.shape[1]), lambda i: (i, 0)) for y in ys]
    in_specs += [pl.BlockSpec(w.shape, lambda i: (0, 0)) for w in ws]
    in_specs += [pl.BlockSpec((tm, d), lambda i: (i, 0)), pl.BlockSpec((1, d), lambda i: (0, 0))]
    return pl.pallas_call(
        functools.partial(_proj_res_body, n_in=n_in),
        grid=(n // tm,),
        in_specs=in_specs,
        out_specs=pl.BlockSpec((tm, d), lambda i: (i, 0)),
        out_shape=jax.ShapeDtypeStruct((n, d), F32),
        compiler_params=_cparams(1),
        name="proj_res",
    )(*ys, *ws, x, g)


def _ffn_body(x_ref, gi_ref, go_ref, wg_ref, wu_ref, wd_ref, o_ref, h_ref):
    j = pl.program_id(1)
    last_j = pl.num_programs(1) - 1
    tm = x_ref.shape[0]

    def partial_out(h):
        a = (_silu(_dot(h, wg_ref[...])) * _dot(h, wu_ref[...])).astype(BF16)
        return _dot(a, wd_ref[...])

    row_groups = [slice(r, r + FFN_EDGE_ROWS) for r in range(0, tm, FFN_EDGE_ROWS)]

    @pl.when(j == 0)
    def _():
        for rs in row_groups:
            h = _rms(x_ref[rs, :], gi_ref[...]).astype(BF16)
            h_ref[rs, :] = h
            o_ref[rs, :] = partial_out(h)

    @pl.when((j > 0) & (j < last_j))
    def _():
        o_ref[...] += partial_out(h_ref[...])

    @pl.when(j == last_j)
    def _():
        for rs in row_groups:
            acc = o_ref[rs, :] + partial_out(h_ref[rs, :])
            o_ref[rs, :] = x_ref[rs, :] + _rms(acc, go_ref[...])


def _ffn(x, g_in, g_out, wg, wu, wd, tm, tf):
    n, d = x.shape
    f = wg.shape[1]
    assert f // tf >= 2 and tm % FFN_EDGE_ROWS == 0
    return pl.pallas_call(
        _ffn_body,
        grid=(n // tm, f // tf),
        in_specs=[
            pl.BlockSpec((tm, d), lambda i, j: (i, 0), pipeline_mode=pl.Buffered(1)),
            pl.BlockSpec((1, d), lambda i, j: (0, 0)),
            pl.BlockSpec((1, d), lambda i, j: (0, 0)),
            pl.BlockSpec((d, tf), lambda i, j: (0, j)),
            pl.BlockSpec((d, tf), lambda i, j: (0, j)),
            pl.BlockSpec((tf, d), lambda i, j: (j, 0)),
        ],
        out_specs=pl.BlockSpec((tm, d), lambda i, j: (i, 0)),
        out_shape=jax.ShapeDtypeStruct((n, d), F32),
        scratch_shapes=[pltpu.VMEM((tm, d), BF16)],
        compiler_params=_cparams(2),
        name="ffn",
    )(x, g_in, g_out, wg, wu, wd)


def _seg_matrices(width, n_seg_pad):
    e = (_iota((width, n_seg_pad), 0) // HEAD == _iota((width, n_seg_pad), 1)).astype(BF16)
    et = (_iota((n_seg_pad, width), 1) // HEAD == _iota((n_seg_pad, width), 0)).astype(BF16)
    return e, et


def _blockdiag_tril(tc):
    r, c = _iota((tc, tc), 0), _iota((tc, tc), 1)
    return ((r // CHUNK == c // CHUNK) & (c <= r)).astype(BF16)


def _rwkv_body(x_ref, gn_ref, w_ref, mu_ref, w0_ref, w2_ref, a0_ref, a2_ref, g2_ref, kk_ref, ka_ref, rk_ref,
               lnw_ref, lnb_ref, o_ref, p_ref, prow_ref, s_ref, at_ref, rt_ref, bt_ref, kt_ref, v_ref, gl_ref, y_ref,
               *, tc, width):
    step = pl.program_id(1)

    tw = p_ref.shape[1] // N_PROJ_TILES

    def h_next_fn():
        return _rms(x_ref[...], gn_ref[...]).astype(BF16)

    def project_tile(h, k):
        p_ref[:, k * tw:(k + 1) * tw] = _dot(h, w_ref[:, k * tw:(k + 1) * tw])

    @pl.when(step == 0)
    def _():
        s_ref[...] = jnp.zeros_like(s_ref)
        prow_ref[...] = jnp.zeros_like(prow_ref)
        h = h_next_fn()
        for k in range(N_PROJ_TILES):
            project_tile(h, k)

    @pl.when(step > 0)
    def _():
        _rwkv_mix_block(h_next_fn, project_tile, mu_ref, w0_ref, w2_ref, a0_ref, a2_ref, g2_ref, kk_ref, ka_ref, rk_ref, lnw_ref,
                        lnb_ref, o_ref, p_ref, prow_ref, s_ref, at_ref, rt_ref, bt_ref, kt_ref, v_ref, gl_ref, y_ref,
                        tc=tc, width=width)


def _rwkv_mix_block(h_next_fn, project_tile, mu_ref, w0_ref, w2_ref, a0_ref, a2_ref, g2_ref, kk_ref, ka_ref, rk_ref, lnw_ref,
                    lnb_ref, o_ref, p_ref, prow_ref, s_ref, at_ref, rt_ref, bt_ref, kt_ref, v_ref, gl_ref, y_ref,
                    *, tc, width):
    n_heads = width // HEAD
    n_chunks = tc // CHUNK

    p = p_ref[...]
    first_row = _iota((tc, 1), 0) == 0

    def lerp(cols):
        pc = p[:, cols]
        psh = jnp.where(first_row, prow_ref[0:1, cols], pltpu.roll(pc, 1, 0))
        return pc + (psh - pc) * mu_ref[:, cols]

    o1 = 3 * width
    xl = lerp(slice(o1, o1 + 4 * LANE))
    wc, ac, gc = xl[:, 0:LANE], xl[:, LANE:2 * LANE], xl[:, 2 * LANE:4 * LANE]
    z = w0_ref[...] + _dot(jnp.tanh(wc).astype(BF16), w2_ref[...])
    a = _sigmoid(a0_ref[...] + _dot(ac.astype(BF16), a2_ref[...]))
    g = _dot(_sigmoid(gc).astype(BF16), g2_ref[...])
    lw = -jnp.exp(-_softplus(-z) - 0.5)

    h_next = h_next_fn()
    project_tile(h_next, 0)
    project_tile(h_next, 1)

    xr = lerp(slice(0, o1))
    last_row = p[tc - 1:tc, :]
    r = xr[:, 0:width]
    k = xr[:, width:2 * width]
    v = xr[:, 2 * width:3 * width]

    mxu_w = 2 * LANE
    same_head = (_iota((mxu_w, mxu_w), 0) // HEAD == _iota((mxu_w, mxu_w), 1) // HEAD).astype(BF16)

    def seg_sum(t):
        return jnp.concatenate(
            [_dot_sel_r(t[:, j:j + mxu_w], same_head) for j in range(0, width, mxu_w)], axis=1)

    kk = k * kk_ref[...]
    kk = kk * lax.rsqrt(jnp.maximum(seg_sum(kk * kk), 1e-24))
    k2 = k * (1.0 + (a - 1.0) * ka_ref[...])
    project_tile(h_next, 2)
    bonus = seg_sum(r * k2 * rk_ref[...]) * v
    cs = _dot_sel_l(_blockdiag_tril(tc), lw)
    project_tile(h_next, 3)
    prow_ref[0:1, :] = last_row

    ginv = jnp.exp(-cs)
    at_ref[...] = (-kk * jnp.exp(cs - lw)).astype(BF16)
    rt_ref[...] = (r * jnp.exp(cs)).astype(BF16)
    bt_ref[...] = (kk * a * ginv).astype(BF16)
    kt_ref[...] = (k2 * ginv).astype(BF16)
    v_ref[...] = v.astype(BF16)
    for c in range(n_chunks):
        last = c * CHUNK + CHUNK - 1
        gl_ref[c:c + 1, :] = jnp.exp(cs[last:last + 1, :])

    rr, cc = _iota((CHUNK, CHUNK), 0), _iota((CHUNK, CHUNK), 1)
    rr2, cc2 = _iota((CHUNK, 2 * CHUNK), 0), _iota((CHUNK, 2 * CHUNK), 1) % CHUNK
    strict2 = (cc2 < rr2).astype(F32)
    incl2 = (cc2 <= rr2).astype(F32)
    eye = (cc == rr).astype(F32)
    n_lvl = int(math.log2(CHUNK))
    lvl_masks = []
    for lv in range(n_lvl):
        same = (rr >> (lv + 1)) == (cc >> (lv + 1))
        lvl_masks.append((same & (((rr >> lv) & 1) == 1) & (((cc >> lv) & 1) == 0)).astype(F32))

    def chunk_step(c, carry):
        t0 = pl.multiple_of(c * CHUNK, CHUNK)
        rows = pl.ds(t0, CHUNK)
        gl = gl_ref[pl.ds(c, 1), :]
        heads = range(n_heads)
        hsl = [slice(h * HEAD, (h + 1) * HEAD) for h in heads]
        vh = [v_ref[rows, hs] for hs in hsl]
        ar = [jnp.concatenate([at_ref[rows, hs], rt_ref[rows, hs]], axis=0) for hs in hsl]
        bk = [jnp.concatenate([bt_ref[rows, hs], kt_ref[rows, hs]], axis=0) for hs in hsl]
        pm = [_dot_nt(ar[h], bk[h]) for h in heads]
        s0 = [s_ref[h] for h in heads]
        ars = [_dot_nt(ar[h], s0[h].astype(BF16)) for h in heads]
        pa = [pm[h][:CHUNK, :] * strict2 for h in heads]
        aab = [pa[h][:, :CHUNK] for h in heads]
        rhs = [ars[h][:CHUNK, :] + _dot(pa[h][:, CHUNK:].astype(BF16), vh[h]) for h in heads]
        x = [eye + aab[h] * lvl_masks[0] for h in heads]
        for lv in range(1, n_lvl):
            xb = [x[h].astype(BF16) for h in heads]
            tm_ = [_dot((aab[h] * lvl_masks[lv]).astype(BF16), xb[h]) for h in heads]
            x = [x[h] + _dot(xb[h], tm_[h].astype(BF16)) for h in heads]
        u = [_dot(x[h].astype(BF16), rhs[h].astype(BF16)) for h in heads]
        uv = [jnp.concatenate([u[h].astype(BF16), vh[h]], axis=0) for h in heads]
        for h in heads:
            pr = (pm[h][CHUNK:, :] * incl2).astype(BF16)
            y_ref[rows, hsl[h]] = ars[h][CHUNK:, :] + _dot(pr, uv[h])
        for h in heads:
            s_ref[h] = (s0[h] + _dot_tn(uv[h], bk[h])) * gl[:, hsl[h]]
        return carry

    lax.fori_loop(0, n_chunks, chunk_step, 0)

    y = y_ref[...]
    inv_n = 1.0 / HEAD
    mean = seg_sum(y) * inv_n
    yc = y - mean
    var = seg_sum(yc * yc) * inv_n
    yn = yc * lax.rsqrt(var + RWKV_LN_EPS) * lnw_ref[...] + lnb_ref[...]
    o_ref[...] = ((yn + bonus) * g).astype(o_ref.dtype)


def _mixer_specs(x, gn, w, params, batch, seq, tc, width):
    nt = seq // tc
    d = x.shape[1]
    const = lambda b, s: (0, 0)
    in_specs = [
        pl.BlockSpec((tc, d), lambda b, s: (b * nt + jnp.minimum(s, nt - 1), 0)),
        pl.BlockSpec(gn.shape, const),
        pl.BlockSpec(w.shape, const, pipeline_mode=pl.Buffered(1)),
    ] + [pl.BlockSpec(a.shape, const) for a in params]
    out_spec = pl.BlockSpec((tc, width), lambda b, s: (b * nt + jnp.maximum(s - 1, 0), 0))
    return (batch, nt + 1), in_specs, out_spec


def _rwkv_mix(x, gn, w, mu, w0, w2, a0, a2, g2, k_k, k_a, r_k, ln_w, ln_b, batch, seq, tc):
    n = x.shape[0]
    pw = w.shape[1]
    width = w0.shape[1]
    n_heads = width // HEAD
    params = (mu, w0, w2, a0, a2, g2, k_k, k_a, r_k, ln_w, ln_b)
    grid, in_specs, out_spec = _mixer_specs(x, gn, w, params, batch, seq, tc, width)
    return pl.pallas_call(
        functools.partial(_rwkv_body, tc=tc, width=width),
        grid=grid,
        in_specs=in_specs,
        out_specs=out_spec,
        out_shape=jax.ShapeDtypeStruct((n, width), BF16),
        scratch_shapes=[pltpu.VMEM((tc, pw), F32), pltpu.VMEM((8, pw), F32), pltpu.VMEM((n_heads, HEAD, HEAD), F32)]
        + [pltpu.VMEM((tc, width), BF16) for _ in range(5)]
        + [pltpu.VMEM((max(tc // CHUNK, 8), width), F32), pltpu.VMEM((tc, width), F32)],
        compiler_params=_cparams(2),
        name="rwkv_mix",
    )(x, gn, w, *params)


def _ssd_body(x_ref, gn_ref, w_ref, cw_ref, cb_ref, dtb_ref, alog_ref, dskip_ref, nw_ref, o_ref,
              p_ref, tail_ref, st_ref, y_ref, *, tc, width, state, groups):
    step = pl.program_id(1)
    n_chunks = tc // CHUNK
    n_lane_tiles = p_ref.shape[1] // LANE
    n_tiles = max(t for t in range(1, n_chunks + 1) if n_lane_tiles % t == 0)
    tw = p_ref.shape[1] // n_tiles

    def project_tile(h, k):
        p_ref[:, k * tw:(k + 1) * tw] = _dot(h, w_ref[:, k * tw:(k + 1) * tw])

    @pl.when(step == 0)
    def _():
        st_ref[...] = jnp.zeros_like(st_ref)
        tail_ref[...] = jnp.zeros_like(tail_ref)
        h = _rms(x_ref[...], gn_ref[...]).astype(BF16)
        for k in range(n_tiles):
            project_tile(h, k)

    @pl.when(step > 0)
    def _():
        _ssd_mix_block(x_ref, gn_ref, project_tile, n_tiles, cw_ref, cb_ref, dtb_ref, alog_ref, dskip_ref, nw_ref,
                       o_ref, p_ref, tail_ref, st_ref, y_ref, tc=tc, width=width, state=state, groups=groups)


def _ssd_mix_block(x_ref, gn_ref, project_tile, n_tiles, cw_ref, cb_ref, dtb_ref, alog_ref, dskip_ref, nw_ref,
                   o_ref, p_ref, tail_ref, st_ref, y_ref, *, tc, width, state, groups):
    n_chunks = tc // CHUNK
    gw = width // groups
    hpg = gw // HEAD
    conv_ch = width + 2 * groups * state
    n_tap = cw_ref.shape[0]

    p = p_ref[...]
    zs = _silu(p[:, 0:width])
    xbc_raw = p[:, width:width + conv_ch]
    dt_raw = p[:, width + conv_ch:width + conv_ch + LANE]
    h_next = _rms(x_ref[...], gn_ref[...]).astype(BF16)
    tiles_after_chunk = {n_chunks // 2 - 1 + k * (n_chunks // 2): k + 1 for k in range(n_tiles - 1)}

    prev = tail_ref[...]
    tail_ref[...] = xbc_raw[tc - 8:tc, :]
    xcat = jnp.concatenate([prev, xbc_raw], axis=0)
    conv = cb_ref[...]
    for j in range(n_tap):
        off = 8 - (n_tap - 1) + j
        conv = conv + cw_ref[j:j + 1, :] * xcat[off:off + tc, :]
    xbc = _silu(conv)
    xs = xbc[:, 0:width]
    bm = xbc[:, width:width + groups * state].astype(BF16)
    cm = xbc[:, width + groups * state:conv_ch].astype(BF16)
    project_tile(h_next, 0)

    _, et = _seg_matrices(width, LANE)
    dt = _softplus(dt_raw + dtb_ref[...])
    da = -jnp.exp(alog_ref[...]) * dt
    cs_h = _dot_sel_l(_blockdiag_tril(tc), da)
    cs = _dot_sel_r(cs_h, et)
    dtx = _dot_sel_r(dt, et)
    xc = xs * dtx
    ecs = jnp.exp(cs)

    rr, cc = _iota((CHUNK, CHUNK), 0), _iota((CHUNK, CHUNK), 1)
    causal = cc <= rr

    for c in range(n_chunks):
        r0 = c * CHUNK
        rows = slice(r0, r0 + CHUNK)
        cs_c = cs[rows, :]
        cs_last = cs_c[CHUNK - 1:CHUNK, :]
        xc_c = xc[rows, :]
        xd = (xc_c * jnp.exp(cs_last - cs_c)).astype(BF16)
        xcb = xc_c.astype(BF16)
        cs_t = cs_h[rows, :].T
        for g in range(groups):
            gl = slice(g * gw, (g + 1) * gw)
            b_g = bm[rows, g * state:(g + 1) * state]
            c_g = cm[rows, g * state:(g + 1) * state]
            cb = _dot_nt(c_g, b_g)
            st = st_ref[g]
            y_off = _dot(c_g, st.astype(BF16)) * ecs[rows, gl]
            for hh in range(hpg):
                h = g * hpg + hh
                hs = slice(h * HEAD, (h + 1) * HEAD)
                seg = cs_c[:, hs] - cs_t[h:h + 1, :]
                lmat = jnp.where(causal, jnp.exp(jnp.where(causal, seg, 0.0)), 0.0)
                y_ref[rows, hs] = _dot((cb * lmat).astype(BF16), xcb[:, hs]) + y_off[:, hh * HEAD:(hh + 1) * HEAD]
            st_ref[g] = st * ecs[r0 + CHUNK - 1:r0 + CHUNK, gl] + _dot_tn(b_g, xd[:, gl])
        if c in tiles_after_chunk:
            project_tile(h_next, tiles_after_chunk[c])
    assert sorted(tiles_after_chunk.values()) == list(range(1, n_tiles)) and max(tiles_after_chunk) < n_chunks

    y = (y_ref[...] + dskip_ref[...] * xs) * zs
    for g in range(groups):
        gl = slice(g * gw, (g + 1) * gw)
        yg = y[:, gl]
        ms = jnp.mean(yg * yg, axis=-1, keepdims=True)
        o_ref[:, gl] = (yg * lax.rsqrt(ms + NORM_EPS) * nw_ref[:, gl]).astype(o_ref.dtype)


def _ssd_mix(x, gn, w, conv_w, conv_b, dt_bias, a_log, d_skip, norm_w, batch, seq, tc, width, state, groups):
    n = x.shape[0]
    pw = w.shape[1]
    conv_ch = width + 2 * groups * state
    params = (conv_w, conv_b, dt_bias, a_log, d_skip, norm_w)
    grid, in_specs, out_spec = _mixer_specs(x, gn, w, params, batch, seq, tc, width)
    return pl.pallas_call(
        functools.partial(_ssd_body, tc=tc, width=width, state=state, groups=groups),
        grid=grid,
        in_specs=in_specs,
        out_specs=out_spec,
        out_shape=jax.ShapeDtypeStruct((n, width), BF16),
        scratch_shapes=[pltpu.VMEM((tc, pw), F32), pltpu.VMEM((8, conv_ch), F32),
                        pltpu.VMEM((groups, state, width // groups), F32), pltpu.VMEM((tc, width), F32)],
        compiler_params=_cparams(2),
        name="ssd_mix",
    )(x, gn, w, *params)


FFN_EDGE_ROWS = 256
N_PROJ_TILES = 4
PAIR = 2 * CHUNK
BAND = (LEFT_CHUNKS + 1) * CHUNK
PAIR_BAND = BAND + CHUNK


def _bias_table_body(rb_ref, o_ref, *, n_bucket, rel_future):
    ext_w = PAIR_BAND + LANE
    m = _iota((n_bucket, ext_w), 1)
    bucket = jnp.clip(PAIR_BAND - 1 - m, -rel_future, REL_PAST_CLIP) + rel_future
    sel = (bucket == _iota((n_bucket, ext_w), 0)).astype(BF16)
    ext = _dot_sel_r(rb_ref[...], sel) * LOG2E
    for r in range(PAIR):
        o_ref[r] = ext[:, PAIR - 1 - r:PAIR - 1 - r + PAIR_BAND]


def _bias_table(rel_bias):
    n_heads, n_bucket = rel_bias.shape
    out = pl.pallas_call(
        functools.partial(_bias_table_body, n_bucket=n_bucket, rel_future=CHUNK - 1),
        out_shape=jax.ShapeDtypeStruct((PAIR, n_heads, PAIR_BAND), F32),
        name="bias_table",
    )(rel_bias)
    return jnp.transpose(out, (1, 0, 2))


def _attn_body(q_ref, kp_ref, kc_ref, vp_ref, vc_ref, b_ref, o_ref, *, tq, dh):
    i = pl.program_id(2)
    n_heads = q_ref.shape[1] // dh
    row, col = _iota((PAIR, PAIR_BAND), 0), _iota((PAIR, PAIR_BAND), 1)
    lo = (row // CHUNK) * CHUNK
    in_band = (col >= lo) & (col < lo + BAND)
    n_pairs = tq // PAIR
    k0 = [tq - LEFT_CHUNKS * CHUNK + jp * PAIR for jp in range(n_pairs)]
    work = [(h, jp) for h in range(n_heads) for jp in range(n_pairs)]

    def attend(first_block):
        hcols = [slice(h * dh, (h + 1) * dh) for h in range(n_heads)]
        kcat = [jnp.concatenate([kp_ref[:, hc], kc_ref[:, hc]], axis=0) for hc in hcols]
        vcat = [jnp.concatenate([vp_ref[:, hc], vc_ref[:, hc]], axis=0) for hc in hcols]
        s = [_dot_nt(q_ref[jp * PAIR:(jp + 1) * PAIR, hcols[h]], kcat[h][k0[jp]:k0[jp] + PAIR_BAND, :]) + b_ref[h]
             for h, jp in work]
        if first_block:
            valid = [in_band & (col + k0[jp] >= tq) for jp in range(n_pairs)]
        else:
            valid = [in_band] * n_pairs
        s = [jnp.where(valid[jp], s_, -1e30) for (h, jp), s_ in zip(work, s)]
        mx = [jnp.max(s_, axis=-1, keepdims=True) for s_ in s]
        pexp = [jnp.exp2(s_ - m_) for s_, m_ in zip(s, mx)]
        den = [jnp.sum(p_, axis=-1, keepdims=True) for p_ in pexp]
        o = [_dot(p_.astype(BF16), vcat[h][k0[jp]:k0[jp] + PAIR_BAND, :]) for (h, jp), p_ in zip(work, pexp)]
        for (h, jp), o_, d_ in zip(work, o, den):
            o_ref[jp * PAIR:(jp + 1) * PAIR, hcols[h]] = (o_ / d_).astype(o_ref.dtype)

    @pl.when(i == 0)
    def _():
        attend(True)

    @pl.when(i > 0)
    def _():
        attend(False)


def _band_attention(qkv, bias, batch, seq, n_heads, tq, hp):
    n, three_d = qkv.shape
    d = three_d // 3
    dh = d // n_heads
    nt = seq // tq
    ng = n_heads // hp
    cur = lambda off: (lambda b, h, i: (b * nt + i, off * ng + h))
    prv = lambda off: (lambda b, h, i: (b * nt + jnp.maximum(i - 1, 0), off * ng + h))
    blk = lambda f: pl.BlockSpec((tq, hp * dh), f)
    return pl.pallas_call(
        functools.partial(_attn_body, tq=tq, dh=dh),
        grid=(batch, ng, nt),
        in_specs=[blk(cur(0)), blk(prv(1)), blk(cur(1)), blk(prv(2)), blk(cur(2)),
                  pl.BlockSpec((hp, PAIR, PAIR_BAND), lambda b, h, i: (h, 0, 0))],
        out_specs=pl.BlockSpec((tq, hp * dh), lambda b, h, i: (b * nt + i, h)),
        out_shape=jax.ShapeDtypeStruct((n, d), BF16),
        compiler_params=_cparams(3),
        name="band_attention",
    )(qkv, qkv, qkv, qkv, qkv, bias)


def _pad_cols(a, width):
    return jnp.pad(a, ((0, 0), (0, width - a.shape[1])))


def _pad_rows(a, rows):
    return jnp.pad(a, ((0, rows - a.shape[0]), (0, 0)))


def _row(a):
    return a.reshape(1, -1)


class _Tiles(NamedTuple):
    rows: int
    proj_rows: int
    mix_rows: int
    attn_rows: int
    attn_heads: int
    qkv_cols: int
    ffn_cols: int


def _tiles(n, seq):
    rows = min(1024, n)
    return _Tiles(rows=rows, proj_rows=max(rows // 2, 8), mix_rows=min(4 * CHUNK, seq),
                  attn_rows=min(LEFT_CHUNKS * CHUNK, seq), attn_heads=4, qkv_cols=1024, ffn_cols=512)


def kernel(x, norm_g, w_in_ab, rwkv_mu, rwkv_w0, rwkv_w2, rwkv_a0, rwkv_a2, rwkv_g2, rwkv_k_k, rwkv_k_a, rwkv_r_k, rwkv_ln_w, rwkv_ln_b, ssm_conv_w, ssm_conv_b, ssm_dt_bias, ssm_A_log, ssm_D, ssm_norm_w, w_out_ab, w_qkv, attn_rel_bias, w_out_c, ffn_w_gate, ffn_w_up, ffn_w_down):
    batch, seq, d = x.shape
    n = batch * seq
    depth = norm_g.shape[0]
    xf = x.reshape(n, d)

    rw = rwkv_w0.shape[1]
    lora_w, lora_a, lora_g = rwkv_w2.shape[1], rwkv_a2.shape[1], rwkv_g2.shape[1]
    sw = ssm_norm_w.shape[1]
    n_ssm_heads = ssm_A_log.shape[1]
    conv_ch = ssm_conv_w.shape[2]
    groups = 2
    state = (conv_ch - sw) // (2 * groups)
    n_att_heads = attn_rel_bias.shape[1]
    rwkv_proj = 3 * rw + lora_w + lora_a + lora_g
    t = _tiles(n, seq)

    for l in range(depth):
        i = l // 2
        g = norm_g[l]
        if l % 2 == 0:
            w_in = w_in_ab[i]
            o1, o2, o3 = 3 * rw, 3 * rw + lora_w, 3 * rw + lora_w + lora_a
            seg = lambda a: [a[:, 0:o1], _pad_cols(a[:, o1:o2], LANE), _pad_cols(a[:, o2:o3], LANE), a[:, o3:rwkv_proj]]
            w_r = jnp.concatenate(seg(w_in), axis=1).astype(BF16)
            mu = jnp.concatenate(seg(_row(rwkv_mu[i])), axis=1)
            o4 = rwkv_proj + sw + conv_ch
            w_s = jnp.concatenate([w_in[:, rwkv_proj:o4], _pad_cols(w_in[:, o4:], LANE)], axis=1).astype(BF16)

            y_a = _rwkv_mix(
                xf, g[0:1], w_r, mu, _row(rwkv_w0[i]), _pad_rows(rwkv_w2[i], LANE).astype(BF16), _row(rwkv_a0[i]),
                _pad_rows(rwkv_a2[i], LANE).astype(BF16), rwkv_g2[i].astype(BF16), _row(rwkv_k_k[i]),
                _row(rwkv_k_a[i]), _row(rwkv_r_k[i]), _row(rwkv_ln_w[i]), _row(rwkv_ln_b[i]), batch, seq, t.mix_rows)
            y_b = _ssd_mix(
                xf, g[0:1], w_s, ssm_conv_w[i], _row(ssm_conv_b[i]), _pad_cols(_row(ssm_dt_bias[i]), LANE),
                _pad_cols(_row(ssm_A_log[i]), LANE), _row(jnp.repeat(ssm_D[i], sw // n_ssm_heads)),
                _row(ssm_norm_w[i]), batch, seq, t.mix_rows, sw, state, groups)
            w_o = w_out_ab[i].astype(BF16)
            xf = _proj_res([y_a, y_b], [w_o[:rw], w_o[rw:]], xf, g[1:2], t.proj_rows)
        else:
            qkv = _norm_matmul(xf, g[0:1], w_qkv[i].astype(BF16), BF16, t.rows, t.qkv_cols, n_scaled=d // t.qkv_cols,
                               scale=(d // n_att_heads) ** -0.5 * LOG2E)
            bias = _bias_table(attn_rel_bias[i])
            att = _band_attention(qkv, bias, batch, seq, n_att_heads, t.attn_rows, t.attn_heads)
            xf = _proj_res([att], [w_out_c[i].astype(BF16)], xf, g[1:2], t.proj_rows)
        xf = _ffn(xf, g[2:3], g[3:4], ffn_w_gate[l].astype(BF16), ffn_w_up[l].astype(BF16),
                  ffn_w_down[l].astype(BF16), t.rows, t.ffn_cols)
    return xf.reshape(batch, seq, d)
```

```python
import functools
import math
from typing import NamedTuple

import jax
import jax.numpy as jnp
from jax import lax
from jax.experimental import pallas as pl
from jax.experimental.pallas import tpu as pltpu

F32 = jnp.float32
BF16 = jnp.bfloat16

NORM_EPS = 1e-6
RWKV_LN_EPS = 64e-5
CHUNK = 64
HEAD = 64
LANE = 128
LEFT_CHUNKS = 8
REL_PAST_CLIP = 256
LOG2E = math.log2(math.e)
VMEM_LIMIT = 60 * 1024 * 1024


def _cparams(n_axes):
    return pltpu.CompilerParams(dimension_semantics=("arbitrary",) * n_axes, vmem_limit_bytes=VMEM_LIMIT)


def _dot(a, b):
    return jnp.dot(a, b, preferred_element_type=F32)


def _dot_nt(a, b):
    return lax.dot_general(a, b, (((1,), (1,)), ((), ())), preferred_element_type=F32)


def _dot_tn(a, b):
    return lax.dot_general(a, b, (((0,), (0,)), ((), ())), preferred_element_type=F32)


def _split2(t):
    hi = t.astype(BF16)
    lo = (t - hi.astype(F32)).astype(BF16)
    return hi, lo


def _dot_sel_r(t, sel):
    hi, lo = _split2(t)
    return _dot(hi, sel) + _dot(lo, sel)


def _dot_sel_l(sel, t):
    hi, lo = _split2(t)
    return _dot(sel, hi) + _dot(sel, lo)


def _rms(x, g):
    ms = jnp.mean(x * x, axis=-1, keepdims=True)
    return x * lax.rsqrt(ms + NORM_EPS) * g


def _sigmoid(x):
    return 1.0 / (1.0 + jnp.exp(-x))


def _silu(x):
    return x * _sigmoid(x)


def _softplus(x):
    return jnp.maximum(x, 0.0) + jnp.log1p(jnp.exp(-jnp.abs(x)))


def _iota(shape, axis):
    return lax.broadcasted_iota(jnp.int32, shape, axis)


def _norm_matmul_body(x_ref, g_ref, w_ref, o_ref, h_ref, *, n_scaled, scale):
    j = pl.program_id(1)

    @pl.when(j == 0)
    def _():
        h_ref[...] = _rms(x_ref[...], g_ref[...]).astype(BF16)

    mult = jnp.where(j < n_scaled, jnp.float32(scale), jnp.float32(1.0))
    o_ref[...] = (_dot(h_ref[...], w_ref[...]) * mult).astype(o_ref.dtype)


def _norm_matmul(x, g, w, out_dtype, tm, n_scaled=0, scale=1.0):
    n, d = x.shape
    n_col_tiles, _, tn = w.shape
    m = n_col_tiles * tn
    return pl.pallas_call(
        functools.partial(_norm_matmul_body, n_scaled=n_scaled, scale=scale),
        grid=(n // tm, n_col_tiles),
        in_specs=[
            pl.BlockSpec((tm, d), lambda i, j: (i, 0)),
            pl.BlockSpec((1, d), lambda i, j: (0, 0)),
            pl.BlockSpec((None, d, tn), lambda i, j: (j, 0, 0)),
        ],
        out_specs=pl.BlockSpec((tm, tn), lambda i, j: (i, j)),
        out_shape=jax.ShapeDtypeStruct((n, m), out_dtype),
        scratch_shapes=[pltpu.VMEM((tm, d), BF16)],
        compiler_params=_cparams(2),
        name="norm_matmul",
    )(x, g, w)


def _proj_res_body(*refs, n_in):
    y_refs, w_refs = refs[:n_in], refs[n_in:2 * n_in]
    x_ref, g_ref, o_ref = refs[2 * n_in:]
    m = _dot(y_refs[0][...], w_refs[0][...])
    for y_ref, w_ref in zip(y_refs[1:], w_refs[1:]):
        m = m + _dot(y_ref[...], w_ref[...])
    o_ref[...] = x_ref[...] + _rms(m, g_ref[...])


def _proj_res(ys, ws, x, g, tm):
    n, d = x.shape
    n_in = len(ys)
    in_specs = [pl.BlockSpec((tm, y.shape[1]), lambda i: (i, 0)) for y in ys]
    in_specs += [pl.BlockSpec(w.shape, lambda i: (0, 0)) for w in ws]
    in_specs += [pl.BlockSpec((tm, d), lambda i: (i, 0)), pl.BlockSpec((1, d), lambda i: (0, 0))]
    return pl.pallas_call(
        functools.partial(_proj_res_body, n_in=n_in),
        grid=(n // tm,),
        in_specs=in_specs,
        out_specs=pl.BlockSpec((tm, d), lambda i: (i, 0)),
        out_shape=jax.ShapeDtypeStruct((n, d), F32),
        compiler_params=_cparams(1),
        name="proj_res",
    )(*ys, *ws, x, g)


def _ffn_body(x_ref, gi_ref, go_ref, wg_ref, wu_ref, wd_ref, o_ref, h_ref):
    j = pl.program_id(1)
    last_j = pl.num_programs(1) - 1
    tm = x_ref.shape[0]

    def partial_out(h):
        a = (_silu(_dot(h, wg_ref[...])) * _dot(h, wu_ref[...])).astype(BF16)
        return _dot(a, wd_ref[...])

    row_groups = [slice(r, r + FFN_EDGE_ROWS) for r in range(0, tm, FFN_EDGE_ROWS)]

    @pl.when(j == 0)
    def _():
        for rs in row_groups:
            h = _rms(x_ref[rs, :], gi_ref[...]).astype(BF16)
            h_ref[rs, :] = h
            o_ref[rs, :] = partial_out(h)

    @pl.when((j > 0) & (j < last_j))
    def _():
        o_ref[...] += partial_out(h_ref[...])

    @pl.when(j == last_j)
    def _():
        for rs in row_groups:
            acc = o_ref[rs, :] + partial_out(h_ref[rs, :])
            o_ref[rs, :] = x_ref[rs, :] + _rms(acc, go_ref[...])


def _ffn(x, g_in, g_out, wg, wu, wd, tm):
    n, d = x.shape
    n_col_tiles, _, tf = wg.shape
    assert n_col_tiles >= 2 and tm % FFN_EDGE_ROWS == 0
    return pl.pallas_call(
        _ffn_body,
        grid=(n // tm, n_col_tiles),
        in_specs=[
            pl.BlockSpec((tm, d), lambda i, j: (i, 0), pipeline_mode=pl.Buffered(1)),
            pl.BlockSpec((1, d), lambda i, j: (0, 0)),
            pl.BlockSpec((1, d), lambda i, j: (0, 0)),
            pl.BlockSpec((None, d, tf), lambda i, j: (j, 0, 0)),
            pl.BlockSpec((None, d, tf), lambda i, j: (j, 0, 0)),
            pl.BlockSpec((tf, d), lambda i, j: (j, 0)),
        ],
        out_specs=pl.BlockSpec((tm, d), lambda i, j: (i, 0)),
        out_shape=jax.ShapeDtypeStruct((n, d), F32),
        scratch_shapes=[pltpu.VMEM((tm, d), BF16)],
        compiler_params=_cparams(2),
        name="ffn",
    )(x, g_in, g_out, wg, wu, wd)


def _seg_matrices(width, n_seg_pad):
    e = (_iota((width, n_seg_pad), 0) // HEAD == _iota((width, n_seg_pad), 1)).astype(BF16)
    et = (_iota((n_seg_pad, width), 1) // HEAD == _iota((n_seg_pad, width), 0)).astype(BF16)
    return e, et


def _blockdiag_tril(tc):
    r, c = _iota((tc, tc), 0), _iota((tc, tc), 1)
    return ((r // CHUNK == c // CHUNK) & (c <= r)).astype(BF16)


def _rwkv_body(x_ref, gn_ref, w_ref, mu_ref, w0_ref, w2_ref, a0_ref, a2_ref, g2_ref, kk_ref, ka_ref, rk_ref,
               lnw_ref, lnb_ref, o_ref, p_ref, prow_ref, s_ref, at_ref, rt_ref, bt_ref, kt_ref, v_ref, gl_ref, y_ref,
               *, tc, width):
    step = pl.program_id(1)

    assert w_ref.shape[0] == N_PROJ_TILES
    tw = w_ref.shape[2]

    def h_next_fn():
        return _rms(x_ref[...], gn_ref[...]).astype(BF16)

    def project_tile(h, k):
        p_ref[:, k * tw:(k + 1) * tw] = _dot(h, w_ref[k])

    @pl.when(step == 0)
    def _():
        s_ref[...] = jnp.zeros_like(s_ref)
        prow_ref[...] = jnp.zeros_like(prow_ref)
        h = h_next_fn()
        for k in range(N_PROJ_TILES):
            project_tile(h, k)

    @pl.when(step > 0)
    def _():
        _rwkv_mix_block(h_next_fn, project_tile, mu_ref, w0_ref, w2_ref, a0_ref, a2_ref, g2_ref, kk_ref, ka_ref, rk_ref, lnw_ref,
                        lnb_ref, o_ref, p_ref, prow_ref, s_ref, at_ref, rt_ref, bt_ref, kt_ref, v_ref, gl_ref, y_ref,
                        tc=tc, width=width)


def _rwkv_mix_block(h_next_fn, project_tile, mu_ref, w0_ref, w2_ref, a0_ref, a2_ref, g2_ref, kk_ref, ka_ref, rk_ref, lnw_ref,
                    lnb_ref, o_ref, p_ref, prow_ref, s_ref, at_ref, rt_ref, bt_ref, kt_ref, v_ref, gl_ref, y_ref,
                    *, tc, width):
    n_heads = width // HEAD
    n_chunks = tc // CHUNK

    p = p_ref[...]
    first_row = _iota((tc, 1), 0) == 0

    def lerp(cols):
        pc = p[:, cols]
        psh = jnp.where(first_row, prow_ref[0:1, cols], pltpu.roll(pc, 1, 0))
        return pc + (psh - pc) * mu_ref[:, cols]

    o1 = 3 * width
    xl = lerp(slice(o1, o1 + 4 * LANE))
    wc, ac, gc = xl[:, 0:LANE], xl[:, LANE:2 * LANE], xl[:, 2 * LANE:4 * LANE]
    z = w0_ref[...] + _dot(jnp.tanh(wc).astype(BF16), w2_ref[...])
    a = _sigmoid(a0_ref[...] + _dot(ac.astype(BF16), a2_ref[...]))
    g = _dot(_sigmoid(gc).astype(BF16), g2_ref[...])
    lw = -jnp.exp(-_softplus(-z) - 0.5)

    h_next = h_next_fn()
    project_tile(h_next, 0)
    project_tile(h_next, 1)

    xr = lerp(slice(0, o1))
    last_row = p[tc - 1:tc, :]
    r = xr[:, 0:width]
    k = xr[:, width:2 * width]
    v = xr[:, 2 * width:3 * width]

    mxu_w = 2 * LANE
    same_head = (_iota((mxu_w, mxu_w), 0) // HEAD == _iota((mxu_w, mxu_w), 1) // HEAD).astype(BF16)

    def seg_sum(t):
        return jnp.concatenate(
            [_dot_sel_r(t[:, j:j + mxu_w], same_head) for j in range(0, width, mxu_w)], axis=1)

    kk = k * kk_ref[...]
    kk = kk * lax.rsqrt(jnp.maximum(seg_sum(kk * kk), 1e-24))
    k2 = k * (1.0 + (a - 1.0) * ka_ref[...])
    project_tile(h_next, 2)
    bonus = seg_sum(r * k2 * rk_ref[...]) * v
    cs = _dot_sel_l(_blockdiag_tril(tc), lw)
    project_tile(h_next, 3)
    prow_ref[0:1, :] = last_row

    ginv = jnp.exp(-cs)
    at_ref[...] = (-kk * jnp.exp(cs - lw)).astype(BF16)
    rt_ref[...] = (r * jnp.exp(cs)).astype(BF16)
    bt_ref[...] = (kk * a * ginv).astype(BF16)
    kt_ref[...] = (k2 * ginv).astype(BF16)
    v_ref[...] = v.astype(BF16)
    for c in range(n_chunks):
        last = c * CHUNK + CHUNK - 1
        gl_ref[c:c + 1, :] = jnp.exp(cs[last:last + 1, :])

    rr, cc = _iota((CHUNK, CHUNK), 0), _iota((CHUNK, CHUNK), 1)
    rr2, cc2 = _iota((CHUNK, 2 * CHUNK), 0), _iota((CHUNK, 2 * CHUNK), 1) % CHUNK
    strict2 = (cc2 < rr2).astype(F32)
    incl2 = (cc2 <= rr2).astype(F32)
    eye = (cc == rr).astype(F32)
    n_lvl = int(math.log2(CHUNK))
    lvl_masks = []
    for lv in range(n_lvl):
        same = (rr >> (lv + 1)) == (cc >> (lv + 1))
        lvl_masks.append((same & (((rr >> lv) & 1) == 1) & (((cc >> lv) & 1) == 0)).astype(F32))

    def chunk_step(c, carry):
        t0 = pl.multiple_of(c * CHUNK, CHUNK)
        rows = pl.ds(t0, CHUNK)
        gl = gl_ref[pl.ds(c, 1), :]
        heads = range(n_heads)
        hsl = [slice(h * HEAD, (h + 1) * HEAD) for h in heads]
        vh = [v_ref[rows, hs] for hs in hsl]
        ar = [jnp.concatenate([at_ref[rows, hs], rt_ref[rows, hs]], axis=0) for hs in hsl]
        bk = [jnp.concatenate([bt_ref[rows, hs], kt_ref[rows, hs]], axis=0) for hs in hsl]
        pm = [_dot_nt(ar[h], bk[h]) for h in heads]
        s0 = [s_ref[h] for h in heads]
        ars = [_dot_nt(ar[h], s0[h].astype(BF16)) for h in heads]
        pa = [pm[h][:CHUNK, :] * strict2 for h in heads]
        aab = [pa[h][:, :CHUNK] for h in heads]
        rhs = [ars[h][:CHUNK, :] + _dot(pa[h][:, CHUNK:].astype(BF16), vh[h]) for h in heads]
        x = [eye + aab[h] * lvl_masks[0] for h in heads]
        for lv in range(1, n_lvl):
            xb = [x[h].astype(BF16) for h in heads]
            tm_ = [_dot((aab[h] * lvl_masks[lv]).astype(BF16), xb[h]) for h in heads]
            x = [x[h] + _dot(xb[h], tm_[h].astype(BF16)) for h in heads]
        u = [_dot(x[h].astype(BF16), rhs[h].astype(BF16)) for h in heads]
        uv = [jnp.concatenate([u[h].astype(BF16), vh[h]], axis=0) for h in heads]
        for h in heads:
            pr = (pm[h][CHUNK:, :] * incl2).astype(BF16)
            y_ref[rows, hsl[h]] = ars[h][CHUNK:, :] + _dot(pr, uv[h])
        for h in heads:
            s_ref[h] = (s0[h] + _dot_tn(uv[h], bk[h])) * gl[:, hsl[h]]
        return carry

    lax.fori_loop(0, n_chunks, chunk_step, 0)

    y = y_ref[...]
    inv_n = 1.0 / HEAD
    mean = seg_sum(y) * inv_n
    yc = y - mean
    var = seg_sum(yc * yc) * inv_n
    yn = yc * lax.rsqrt(var + RWKV_LN_EPS) * lnw_ref[...] + lnb_ref[...]
    o_ref[...] = ((yn + bonus) * g).astype(o_ref.dtype)


def _mixer_specs(x, gn, w, params, batch, seq, tc, width):
    nt = seq // tc
    d = x.shape[1]
    const = lambda b, s: (0, 0)
    in_specs = [
        pl.BlockSpec((tc, d), lambda b, s: (b * nt + jnp.minimum(s, nt - 1), 0)),
        pl.BlockSpec(gn.shape, const),
        pl.BlockSpec(w.shape, lambda b, s: (0, 0, 0), pipeline_mode=pl.Buffered(1)),
    ] + [pl.BlockSpec(a.shape, const) for a in params]
    out_spec = pl.BlockSpec((tc, width), lambda b, s: (b * nt + jnp.maximum(s - 1, 0), 0))
    return (batch, nt + 1), in_specs, out_spec


def _rwkv_mix(x, gn, w, mu, w0, w2, a0, a2, g2, k_k, k_a, r_k, ln_w, ln_b, batch, seq, tc):
    n = x.shape[0]
    pw = w.shape[0] * w.shape[2]
    width = w0.shape[1]
    n_heads = width // HEAD
    params = (mu, w0, w2, a0, a2, g2, k_k, k_a, r_k, ln_w, ln_b)
    grid, in_specs, out_spec = _mixer_specs(x, gn, w, params, batch, seq, tc, width)
    return pl.pallas_call(
        functools.partial(_rwkv_body, tc=tc, width=width),
        grid=grid,
        in_specs=in_specs,
        out_specs=out_spec,
        out_shape=jax.ShapeDtypeStruct((n, width), BF16),
        scratch_shapes=[pltpu.VMEM((tc, pw), F32), pltpu.VMEM((8, pw), F32), pltpu.VMEM((n_heads, HEAD, HEAD), F32)]
        + [pltpu.VMEM((tc, width), BF16) for _ in range(5)]
        + [pltpu.VMEM((max(tc // CHUNK, 8), width), F32), pltpu.VMEM((tc, width), F32)],
        compiler_params=_cparams(2),
        name="rwkv_mix",
    )(x, gn, w, *params)


def _ssd_body(x_ref, gn_ref, w_ref, cw_ref, cb_ref, dtb_ref, alog_ref, dskip_ref, nw_ref, o_ref,
              p_ref, tail_ref, st_ref, y_ref, *, tc, width, state, groups):
    step = pl.program_id(1)
    n_chunks = tc // CHUNK
    n_tiles, _, tw = w_ref.shape

    def project_tile(h, k):
        p_ref[:, k * tw:(k + 1) * tw] = _dot(h, w_ref[k])

    @pl.when(step == 0)
    def _():
        st_ref[...] = jnp.zeros_like(st_ref)
        tail_ref[...] = jnp.zeros_like(tail_ref)
        h = _rms(x_ref[...], gn_ref[...]).astype(BF16)
        for k in range(n_tiles):
            project_tile(h, k)

    @pl.when(step > 0)
    def _():
        _ssd_mix_block(x_ref, gn_ref, project_tile, n_tiles, cw_ref, cb_ref, dtb_ref, alog_ref, dskip_ref, nw_ref,
                       o_ref, p_ref, tail_ref, st_ref, y_ref, tc=tc, width=width, state=state, groups=groups)


def _ssd_mix_block(x_ref, gn_ref, project_tile, n_tiles, cw_ref, cb_ref, dtb_ref, alog_ref, dskip_ref, nw_ref,
                   o_ref, p_ref, tail_ref, st_ref, y_ref, *, tc, width, state, groups):
    n_chunks = tc // CHUNK
    gw = width // groups
    hpg = gw // HEAD
    conv_ch = width + 2 * groups * state
    n_tap = cw_ref.shape[0]

    p = p_ref[...]
    zs = _silu(p[:, 0:width])
    xbc_raw = p[:, width:width + conv_ch]
    dt_raw = p[:, width + conv_ch:width + conv_ch + LANE]
    h_next = _rms(x_ref[...], gn_ref[...]).astype(BF16)
    tiles_after_chunk = {n_chunks // 2 - 1 + k * (n_chunks // 2): k + 1 for k in range(n_tiles - 1)}

    prev = tail_ref[...]
    tail_ref[...] = xbc_raw[tc - 8:tc, :]
    xcat = jnp.concatenate([prev, xbc_raw], axis=0)
    conv = cb_ref[...]
    for j in range(n_tap):
        off = 8 - (n_tap - 1) + j
        conv = conv + cw_ref[j:j + 1, :] * xcat[off:off + tc, :]
    xbc = _silu(conv)
    xs = xbc[:, 0:width]
    bm = xbc[:, width:width + groups * state].astype(BF16)
    cm = xbc[:, width + groups * state:conv_ch].astype(BF16)
    project_tile(h_next, 0)

    _, et = _seg_matrices(width, LANE)
    dt = _softplus(dt_raw + dtb_ref[...])
    da = -jnp.exp(alog_ref[...]) * dt
    cs_h = _dot_sel_l(_blockdiag_tril(tc), da)
    cs = _dot_sel_r(cs_h, et)
    dtx = _dot_sel_r(dt, et)
    xc = xs * dtx
    ecs = jnp.exp(cs)

    rr, cc = _iota((CHUNK, CHUNK), 0), _iota((CHUNK, CHUNK), 1)
    causal = cc <= rr

    for c in range(n_chunks):
        r0 = c * CHUNK
        rows = slice(r0, r0 + CHUNK)
        cs_c = cs[rows, :]
        cs_last = cs_c[CHUNK - 1:CHUNK, :]
        xc_c = xc[rows, :]
        xd = (xc_c * jnp.exp(cs_last - cs_c)).astype(BF16)
        xcb = xc_c.astype(BF16)
        cs_t = cs_h[rows, :].T
        for g in range(groups):
            gl = slice(g * gw, (g + 1) * gw)
            b_g = bm[rows, g * state:(g + 1) * state]
            c_g = cm[rows, g * state:(g + 1) * state]
            cb = _dot_nt(c_g, b_g)
            st = st_ref[g]
            y_off = _dot(c_g, st.astype(BF16)) * ecs[rows, gl]
            for hh in range(hpg):
                h = g * hpg + hh
                hs = slice(h * HEAD, (h + 1) * HEAD)
                seg = cs_c[:, hs] - cs_t[h:h + 1, :]
                lmat = jnp.where(causal, jnp.exp(jnp.where(causal, seg, 0.0)), 0.0)
                y_ref[rows, hs] = _dot((cb * lmat).astype(BF16), xcb[:, hs]) + y_off[:, hh * HEAD:(hh + 1) * HEAD]
            st_ref[g] = st * ecs[r0 + CHUNK - 1:r0 + CHUNK, gl] + _dot_tn(b_g, xd[:, gl])
        if c in tiles_after_chunk:
            project_tile(h_next, tiles_after_chunk[c])
    assert sorted(tiles_after_chunk.values()) == list(range(1, n_tiles)) and max(tiles_after_chunk) < n_chunks

    y = (y_ref[...] + dskip_ref[...] * xs) * zs
    for g in range(groups):
        gl = slice(g * gw, (g + 1) * gw)
        yg = y[:, gl]
        ms = jnp.mean(yg * yg, axis=-1, keepdims=True)
        o_ref[:, gl] = (yg * lax.rsqrt(ms + NORM_EPS) * nw_ref[:, gl]).astype(o_ref.dtype)


def _ssd_mix(x, gn, w, conv_w, conv_b, dt_bias, a_log, d_skip, norm_w, batch, seq, tc, width, state, groups):
    n = x.shape[0]
    pw = w.shape[0] * w.shape[2]
    conv_ch = width + 2 * groups * state
    params = (conv_w, conv_b, dt_bias, a_log, d_skip, norm_w)
    grid, in_specs, out_spec = _mixer_specs(x, gn, w, params, batch, seq, tc, width)
    return pl.pallas_call(
        functools.partial(_ssd_body, tc=tc, width=width, state=state, groups=groups),
        grid=grid,
        in_specs=in_specs,
        out_specs=out_spec,
        out_shape=jax.ShapeDtypeStruct((n, width), BF16),
        scratch_shapes=[pltpu.VMEM((tc, pw), F32), pltpu.VMEM((8, conv_ch), F32),
                        pltpu.VMEM((groups, state, width // groups), F32), pltpu.VMEM((tc, width), F32)],
        compiler_params=_cparams(2),
        name="ssd_mix",
    )(x, gn, w, *params)


FFN_EDGE_ROWS = 256
N_PROJ_TILES = 4
SSD_PROJ_TILES = 3
PAIR = 2 * CHUNK
BAND = (LEFT_CHUNKS + 1) * CHUNK
PAIR_BAND = BAND + CHUNK


def _bias_table_body(rb_ref, o_ref, *, n_bucket, rel_future):
    ext_w = PAIR_BAND + LANE
    m = _iota((n_bucket, ext_w), 1)
    bucket = jnp.clip(PAIR_BAND - 1 - m, -rel_future, REL_PAST_CLIP) + rel_future
    sel = (bucket == _iota((n_bucket, ext_w), 0)).astype(BF16)
    ext = _dot_sel_r(rb_ref[...], sel) * LOG2E
    for r in range(PAIR):
        o_ref[r] = ext[:, PAIR - 1 - r:PAIR - 1 - r + PAIR_BAND]


def _bias_table(rel_bias):
    n_heads, n_bucket = rel_bias.shape
    out = pl.pallas_call(
        functools.partial(_bias_table_body, n_bucket=n_bucket, rel_future=CHUNK - 1),
        out_shape=jax.ShapeDtypeStruct((PAIR, n_heads, PAIR_BAND), F32),
        name="bias_table",
    )(rel_bias)
    return jnp.transpose(out, (1, 0, 2))


def _attn_body(q_ref, kp_ref, kc_ref, vp_ref, vc_ref, b_ref, o_ref, *, tq, dh):
    i = pl.program_id(2)
    n_heads = q_ref.shape[1] // dh
    row, col = _iota((PAIR, PAIR_BAND), 0), _iota((PAIR, PAIR_BAND), 1)
    lo = (row // CHUNK) * CHUNK
    in_band = (col >= lo) & (col < lo + BAND)
    n_pairs = tq // PAIR
    k0 = [tq - LEFT_CHUNKS * CHUNK + jp * PAIR for jp in range(n_pairs)]
    work = [(h, jp) for h in range(n_heads) for jp in range(n_pairs)]

    def attend(first_block):
        hcols = [slice(h * dh, (h + 1) * dh) for h in range(n_heads)]
        kcat = [jnp.concatenate([kp_ref[:, hc], kc_ref[:, hc]], axis=0) for hc in hcols]
        vcat = [jnp.concatenate([vp_ref[:, hc], vc_ref[:, hc]], axis=0) for hc in hcols]
        s = [_dot_nt(q_ref[jp * PAIR:(jp + 1) * PAIR, hcols[h]], kcat[h][k0[jp]:k0[jp] + PAIR_BAND, :]) + b_ref[h]
             for h, jp in work]
        if first_block:
            valid = [in_band & (col + k0[jp] >= tq) for jp in range(n_pairs)]
        else:
            valid = [in_band] * n_pairs
        s = [jnp.where(valid[jp], s_, -1e30) for (h, jp), s_ in zip(work, s)]
        mx = [jnp.max(s_, axis=-1, keepdims=True) for s_ in s]
        pexp = [jnp.exp2(s_ - m_) for s_, m_ in zip(s, mx)]
        den = [jnp.sum(p_, axis=-1, keepdims=True) for p_ in pexp]
        o = [_dot(p_.astype(BF16), vcat[h][k0[jp]:k0[jp] + PAIR_BAND, :]) for (h, jp), p_ in zip(work, pexp)]
        for (h, jp), o_, d_ in zip(work, o, den):
            o_ref[jp * PAIR:(jp + 1) * PAIR, hcols[h]] = (o_ / d_).astype(o_ref.dtype)

    @pl.when(i == 0)
    def _():
        attend(True)

    @pl.when(i > 0)
    def _():
        attend(False)


def _band_attention(qkv, bias, batch, seq, n_heads, tq, hp):
    n, three_d = qkv.shape
    d = three_d // 3
    dh = d // n_heads
    nt = seq // tq
    ng = n_heads // hp
    cur = lambda off: (lambda b, h, i: (b * nt + i, off * ng + h))
    prv = lambda off: (lambda b, h, i: (b * nt + jnp.maximum(i - 1, 0), off * ng + h))
    blk = lambda f: pl.BlockSpec((tq, hp * dh), f)
    return pl.pallas_call(
        functools.partial(_attn_body, tq=tq, dh=dh),
        grid=(batch, ng, nt),
        in_specs=[blk(cur(0)), blk(prv(1)), blk(cur(1)), blk(prv(2)), blk(cur(2)),
                  pl.BlockSpec((hp, PAIR, PAIR_BAND), lambda b, h, i: (h, 0, 0))],
        out_specs=pl.BlockSpec((tq, hp * dh), lambda b, h, i: (b * nt + i, h)),
        out_shape=jax.ShapeDtypeStruct((n, d), BF16),
        compiler_params=_cparams(3),
        name="band_attention",
    )(qkv, qkv, qkv, qkv, qkv, bias)


def _pad_cols(a, width):
    return jnp.pad(a, ((0, 0), (0, width - a.shape[1])))


def _pad_rows(a, rows):
    return jnp.pad(a, ((0, rows - a.shape[0]), (0, 0)))


def _row(a):
    return a.reshape(1, -1)


def _col_tiles(w, tn):
    k, m = w.shape
    return w.reshape(k, m // tn, tn).transpose(1, 0, 2)


class _Tiles(NamedTuple):
    rows: int
    proj_rows: int
    mix_rows: int
    attn_rows: int
    attn_heads: int
    qkv_cols: int
    ffn_cols: int


def _tiles(n, seq):
    rows = min(1024, n)
    return _Tiles(rows=rows, proj_rows=max(rows // 2, 8), mix_rows=min(4 * CHUNK, seq),
                  attn_rows=min(LEFT_CHUNKS * CHUNK, seq), attn_heads=4, qkv_cols=1024, ffn_cols=512)


def kernel(x, norm_g, w_in_ab, rwkv_mu, rwkv_w0, rwkv_w2, rwkv_a0, rwkv_a2, rwkv_g2, rwkv_k_k, rwkv_k_a, rwkv_r_k, rwkv_ln_w, rwkv_ln_b, ssm_conv_w, ssm_conv_b, ssm_dt_bias, ssm_A_log, ssm_D, ssm_norm_w, w_out_ab, w_qkv, attn_rel_bias, w_out_c, ffn_w_gate, ffn_w_up, ffn_w_down):
    batch, seq, d = x.shape
    n = batch * seq
    depth = norm_g.shape[0]
    xf = x.reshape(n, d)

    rw = rwkv_w0.shape[1]
    lora_w, lora_a, lora_g = rwkv_w2.shape[1], rwkv_a2.shape[1], rwkv_g2.shape[1]
    sw = ssm_norm_w.shape[1]
    n_ssm_heads = ssm_A_log.shape[1]
    conv_ch = ssm_conv_w.shape[2]
    groups = 2
    state = (conv_ch - sw) // (2 * groups)
    n_att_heads = attn_rel_bias.shape[1]
    rwkv_proj = 3 * rw + lora_w + lora_a + lora_g
    t = _tiles(n, seq)

    for l in range(depth):
        i = l // 2
        g = norm_g[l]
        if l % 2 == 0:
            w_in = w_in_ab[i]
            o1, o2, o3 = 3 * rw, 3 * rw + lora_w, 3 * rw + lora_w + lora_a
            seg = lambda a: [a[:, 0:o1], _pad_cols(a[:, o1:o2], LANE), _pad_cols(a[:, o2:o3], LANE), a[:, o3:rwkv_proj]]
            w_r = jnp.concatenate(seg(w_in), axis=1).astype(BF16)
            mu = jnp.concatenate(seg(_row(rwkv_mu[i])), axis=1)
            o4 = rwkv_proj + sw + conv_ch
            w_s = jnp.concatenate([w_in[:, rwkv_proj:o4], _pad_cols(w_in[:, o4:], LANE)], axis=1).astype(BF16)

            y_a = _rwkv_mix(
                xf, g[0:1], _col_tiles(w_r, w_r.shape[1] // N_PROJ_TILES), mu, _row(rwkv_w0[i]), _pad_rows(rwkv_w2[i], LANE).astype(BF16), _row(rwkv_a0[i]),
                _pad_rows(rwkv_a2[i], LANE).astype(BF16), rwkv_g2[i].astype(BF16), _row(rwkv_k_k[i]),
                _row(rwkv_k_a[i]), _row(rwkv_r_k[i]), _row(rwkv_ln_w[i]), _row(rwkv_ln_b[i]), batch, seq, t.mix_rows)
            y_b = _ssd_mix(
                xf, g[0:1], _col_tiles(w_s, w_s.shape[1] // SSD_PROJ_TILES), ssm_conv_w[i], _row(ssm_conv_b[i]), _pad_cols(_row(ssm_dt_bias[i]), LANE),
                _pad_cols(_row(ssm_A_log[i]), LANE), _row(jnp.repeat(ssm_D[i], sw // n_ssm_heads)),
                _row(ssm_norm_w[i]), batch, seq, t.mix_rows, sw, state, groups)
            w_o = w_out_ab[i].astype(BF16)
            xf = _proj_res([y_a, y_b], [w_o[:rw], w_o[rw:]], xf, g[1:2], t.proj_rows)
        else:
            qkv = _norm_matmul(xf, g[0:1], _col_tiles(w_qkv[i].astype(BF16), t.qkv_cols), BF16, t.rows,
                               n_scaled=d // t.qkv_cols, scale=(d // n_att_heads) ** -0.5 * LOG2E)
            bias = _bias_table(attn_rel_bias[i])
            att = _band_attention(qkv, bias, batch, seq, n_att_heads, t.attn_rows, t.attn_heads)
            xf = _proj_res([att], [w_out_c[i].astype(BF16)], xf, g[1:2], t.proj_rows)
        xf = _ffn(xf, g[2:3], g[3:4], _col_tiles(ffn_w_gate[l].astype(BF16), t.ffn_cols),
                  _col_tiles(ffn_w_up[l].astype(BF16), t.ffn_cols), ffn_w_down[l].astype(BF16), t.rows)
    return xf.reshape(batch, seq, d)
```

```python
import functools
import math
from typing import NamedTuple

import jax
import jax.numpy as jnp
from jax import lax
from jax.experimental import pallas as pl
from jax.experimental.pallas import tpu as pltpu

F32 = jnp.float32
BF16 = jnp.bfloat16

NORM_EPS = 1e-6
RWKV_LN_EPS = 64e-5
CHUNK = 64
HEAD = 64
LANE = 128
LEFT_CHUNKS = 8
REL_PAST_CLIP = 256
LOG2E = math.log2(math.e)
VMEM_LIMIT = 60 * 1024 * 1024


def _cparams(n_axes):
    return pltpu.CompilerParams(dimension_semantics=("arbitrary",) * n_axes, vmem_limit_bytes=VMEM_LIMIT)


def _dot(a, b):
    return jnp.dot(a, b, preferred_element_type=F32)


def _dot_nt(a, b):
    return lax.dot_general(a, b, (((1,), (1,)), ((), ())), preferred_element_type=F32)


def _dot_tn(a, b):
    return lax.dot_general(a, b, (((0,), (0,)), ((), ())), preferred_element_type=F32)


def _split2(t):
    hi = t.astype(BF16)
    lo = (t - hi.astype(F32)).astype(BF16)
    return hi, lo


def _dot_sel_r(t, sel):
    hi, lo = _split2(t)
    return _dot(hi, sel) + _dot(lo, sel)


def _dot_sel_l(sel, t):
    hi, lo = _split2(t)
    return _dot(sel, hi) + _dot(sel, lo)


def _rms(x, g):
    ms = jnp.mean(x * x, axis=-1, keepdims=True)
    return x * lax.rsqrt(ms + NORM_EPS) * g


def _sigmoid(x):
    return 1.0 / (1.0 + jnp.exp(-x))


def _silu(x):
    return x * _sigmoid(x)


def _softplus(x):
    return jnp.maximum(x, 0.0) + jnp.log1p(jnp.exp(-jnp.abs(x)))


def _iota(shape, axis):
    return lax.broadcasted_iota(jnp.int32, shape, axis)


def _norm_matmul_body(x_ref, g_ref, w_ref, o_ref, h_ref, *, n_scaled, scale):
    j = pl.program_id(1)
    mult = jnp.where(j < n_scaled, jnp.float32(scale), jnp.float32(1.0))

    @pl.when(j == 0)
    def _():
        for r in range(0, x_ref.shape[0], EDGE_ROWS):
            rs = slice(r, r + EDGE_ROWS)
            h = _rms(x_ref[rs, :], g_ref[...]).astype(BF16)
            h_ref[rs, :] = h
            o_ref[rs, :] = (_dot(h, w_ref[...]) * mult).astype(o_ref.dtype)

    @pl.when(j > 0)
    def _():
        o_ref[...] = (_dot(h_ref[...], w_ref[...]) * mult).astype(o_ref.dtype)


def _norm_matmul(x, g, w, out_dtype, tm, tn, n_scaled=0, scale=1.0):
    n, d = x.shape
    m = w.shape[1]
    return pl.pallas_call(
        functools.partial(_norm_matmul_body, n_scaled=n_scaled, scale=scale),
        grid=(n // tm, m // tn),
        in_specs=[
            pl.BlockSpec((tm, d), lambda i, j: (i, 0)),
            pl.BlockSpec((1, d), lambda i, j: (0, 0)),
            pl.BlockSpec((d, tn), lambda i, j: (0, j)),
        ],
        out_specs=pl.BlockSpec((tm, tn), lambda i, j: (i, j)),
        out_shape=jax.ShapeDtypeStruct((n, m), out_dtype),
        scratch_shapes=[pltpu.VMEM((tm, d), BF16)],
        compiler_params=_cparams(2),
        name="norm_matmul",
    )(x, g, w)


def _proj_res_body(*refs, n_in):
    y_refs, w_refs = refs[:n_in], refs[n_in:2 * n_in]
    x_ref, g_ref, o_ref = refs[2 * n_in:]
    m = _dot(y_refs[0][...], w_refs[0][...])
    for y_ref, w_ref in zip(y_refs[1:], w_refs[1:]):
        m = m + _dot(y_ref[...], w_ref[...])
    o_ref[...] = x_ref[...] + _rms(m, g_ref[...])


def _proj_res(ys, ws, x, g, tm):
    n, d = x.shape
    n_in = len(ys)
    in_specs = [pl.BlockSpec((tm, y.shape[1]), lambda i: (i, 0)) for y in ys]
    in_specs += [pl.BlockSpec(w.shape, lambda i: (0, 0)) for w in ws]
    in_specs += [pl.BlockSpec((tm, d), lambda i: (i, 0)), pl.BlockSpec((1, d), lambda i: (0, 0))]
    return pl.pallas_call(
        functools.partial(_proj_res_body, n_in=n_in),
        grid=(n // tm,),
        in_specs=in_specs,
        out_specs=pl.BlockSpec((tm, d), lambda i: (i, 0)),
        out_shape=jax.ShapeDtypeStruct((n, d), F32),
        compiler_params=_cparams(1),
        name="proj_res",
    )(*ys, *ws, x, g)


def _ffn_body(x_ref, gi_ref, go_ref, wg_ref, wu_ref, wd_ref, o_ref, h_ref):
    j = pl.program_id(1)
    last_j = pl.num_programs(1) - 1
    tm = x_ref.shape[0]

    def partial_out(h):
        a = (_silu(_dot(h, wg_ref[...])) * _dot(h, wu_ref[...])).astype(BF16)
        return _dot(a, wd_ref[...])

    row_groups = [slice(r, r + EDGE_ROWS) for r in range(0, tm, EDGE_ROWS)]

    @pl.when(j == 0)
    def _():
        for rs in row_groups:
            h = _rms(x_ref[rs, :], gi_ref[...]).astype(BF16)
            h_ref[rs, :] = h
            o_ref[rs, :] = partial_out(h)

    @pl.when((j > 0) & (j < last_j))
    def _():
        o_ref[...] += partial_out(h_ref[...])

    @pl.when(j == last_j)
    def _():
        for rs in row_groups:
            acc = o_ref[rs, :] + partial_out(h_ref[rs, :])
            o_ref[rs, :] = x_ref[rs, :] + _rms(acc, go_ref[...])


def _ffn(x, g_in, g_out, wg, wu, wd, tm, tf):
    n, d = x.shape
    f = wg.shape[1]
    assert f // tf >= 2 and tm % EDGE_ROWS == 0
    return pl.pallas_call(
        _ffn_body,
        grid=(n // tm, f // tf),
        in_specs=[
            pl.BlockSpec((tm, d), lambda i, j: (i, 0), pipeline_mode=pl.Buffered(1)),
            pl.BlockSpec((1, d), lambda i, j: (0, 0)),
            pl.BlockSpec((1, d), lambda i, j: (0, 0)),
            pl.BlockSpec((d, tf), lambda i, j: (0, j)),
            pl.BlockSpec((d, tf), lambda i, j: (0, j)),
            pl.BlockSpec((tf, d), lambda i, j: (j, 0)),
        ],
        out_specs=pl.BlockSpec((tm, d), lambda i, j: (i, 0)),
        out_shape=jax.ShapeDtypeStruct((n, d), F32),
        scratch_shapes=[pltpu.VMEM((tm, d), BF16)],
        compiler_params=_cparams(2),
        name="ffn",
    )(x, g_in, g_out, wg, wu, wd)


def _seg_matrices(width, n_seg_pad):
    e = (_iota((width, n_seg_pad), 0) // HEAD == _iota((width, n_seg_pad), 1)).astype(BF16)
    et = (_iota((n_seg_pad, width), 1) // HEAD == _iota((n_seg_pad, width), 0)).astype(BF16)
    return e, et


def _blockdiag_tril(tc):
    r, c = _iota((tc, tc), 0), _iota((tc, tc), 1)
    return ((r // CHUNK == c // CHUNK) & (c <= r)).astype(BF16)


def _rwkv_body(x_ref, gn_ref, w_ref, mu_ref, w0_ref, w2_ref, a0_ref, a2_ref, g2_ref, kk_ref, ka_ref, rk_ref,
               lnw_ref, lnb_ref, o_ref, p_ref, prow_ref, s_ref, at_ref, rt_ref, bt_ref, kt_ref, v_ref, gl_ref, y_ref,
               *, tc, width):
    step = pl.program_id(1)

    tw = p_ref.shape[1] // N_PROJ_TILES

    def h_next_fn():
        return _rms(x_ref[...], gn_ref[...]).astype(BF16)

    def project_tile(h, k):
        p_ref[:, k * tw:(k + 1) * tw] = _dot(h, w_ref[:, k * tw:(k + 1) * tw])

    @pl.when(step == 0)
    def _():
        s_ref[...] = jnp.zeros_like(s_ref)
        prow_ref[...] = jnp.zeros_like(prow_ref)
        h = h_next_fn()
        for k in range(N_PROJ_TILES):
            project_tile(h, k)

    @pl.when(step > 0)
    def _():
        _rwkv_mix_block(h_next_fn, project_tile, mu_ref, w0_ref, w2_ref, a0_ref, a2_ref, g2_ref, kk_ref, ka_ref, rk_ref, lnw_ref,
                        lnb_ref, o_ref, p_ref, prow_ref, s_ref, at_ref, rt_ref, bt_ref, kt_ref, v_ref, gl_ref, y_ref,
                        tc=tc, width=width)


def _rwkv_mix_block(h_next_fn, project_tile, mu_ref, w0_ref, w2_ref, a0_ref, a2_ref, g2_ref, kk_ref, ka_ref, rk_ref, lnw_ref,
                    lnb_ref, o_ref, p_ref, prow_ref, s_ref, at_ref, rt_ref, bt_ref, kt_ref, v_ref, gl_ref, y_ref,
                    *, tc, width):
    n_heads = width // HEAD
    n_chunks = tc // CHUNK

    p = p_ref[...]
    first_row = _iota((tc, 1), 0) == 0

    def lerp(cols):
        pc = p[:, cols]
        psh = jnp.where(first_row, prow_ref[0:1, cols], pltpu.roll(pc, 1, 0))
        return pc + (psh - pc) * mu_ref[:, cols]

    o1 = 3 * width
    xl = lerp(slice(o1, o1 + 4 * LANE))
    wc, ac, gc = xl[:, 0:LANE], xl[:, LANE:2 * LANE], xl[:, 2 * LANE:4 * LANE]
    z = w0_ref[...] + _dot(jnp.tanh(wc).astype(BF16), w2_ref[...])
    a = _sigmoid(a0_ref[...] + _dot(ac.astype(BF16), a2_ref[...]))
    g = _dot(_sigmoid(gc).astype(BF16), g2_ref[...])
    lw = -jnp.exp(-_softplus(-z) - 0.5)

    h_next = h_next_fn()
    project_tile(h_next, 0)
    project_tile(h_next, 1)

    xr = lerp(slice(0, o1))
    last_row = p[tc - 1:tc, :]
    r = xr[:, 0:width]
    k = xr[:, width:2 * width]
    v = xr[:, 2 * width:3 * width]

    mxu_w = 2 * LANE
    same_head = (_iota((mxu_w, mxu_w), 0) // HEAD == _iota((mxu_w, mxu_w), 1) // HEAD).astype(BF16)

    def seg_sum(t):
        return jnp.concatenate(
            [_dot_sel_r(t[:, j:j + mxu_w], same_head) for j in range(0, width, mxu_w)], axis=1)

    kk = k * kk_ref[...]
    kk = kk * lax.rsqrt(jnp.maximum(seg_sum(kk * kk), 1e-24))
    k2 = k * (1.0 + (a - 1.0) * ka_ref[...])
    project_tile(h_next, 2)
    bonus = seg_sum(r * k2 * rk_ref[...]) * v
    cs = _dot_sel_l(_blockdiag_tril(tc), lw)
    project_tile(h_next, 3)
    prow_ref[0:1, :] = last_row

    ginv = jnp.exp(-cs)
    at_ref[...] = (-kk * jnp.exp(cs - lw)).astype(BF16)
    rt_ref[...] = (r * jnp.exp(cs)).astype(BF16)
    bt_ref[...] = (kk * a * ginv).astype(BF16)
    kt_ref[...] = (k2 * ginv).astype(BF16)
    v_ref[...] = v.astype(BF16)
    for c in range(n_chunks):
        last = c * CHUNK + CHUNK - 1
        gl_ref[c:c + 1, :] = jnp.exp(cs[last:last + 1, :])

    rr, cc = _iota((CHUNK, CHUNK), 0), _iota((CHUNK, CHUNK), 1)
    rr2, cc2 = _iota((CHUNK, 2 * CHUNK), 0), _iota((CHUNK, 2 * CHUNK), 1) % CHUNK
    strict2 = (cc2 < rr2).astype(F32)
    incl2 = (cc2 <= rr2).astype(F32)
    eye = (cc == rr).astype(F32)
    n_lvl = int(math.log2(CHUNK))
    lvl_masks = []
    for lv in range(n_lvl):
        same = (rr >> (lv + 1)) == (cc >> (lv + 1))
        lvl_masks.append((same & (((rr >> lv) & 1) == 1) & (((cc >> lv) & 1) == 0)).astype(F32))

    def chunk_step(c, carry):
        t0 = pl.multiple_of(c * CHUNK, CHUNK)
        rows = pl.ds(t0, CHUNK)
        gl = gl_ref[pl.ds(c, 1), :]
        heads = range(n_heads)
        hsl = [slice(h * HEAD, (h + 1) * HEAD) for h in heads]
        vh = [v_ref[rows, hs] for hs in hsl]
        ar = [jnp.concatenate([at_ref[rows, hs], rt_ref[rows, hs]], axis=0) for hs in hsl]
        bk = [jnp.concatenate([bt_ref[rows, hs], kt_ref[rows, hs]], axis=0) for hs in hsl]
        pm = [_dot_nt(ar[h], bk[h]) for h in heads]
        s0 = [s_ref[h] for h in heads]
        ars = [_dot_nt(ar[h], s0[h].astype(BF16)) for h in heads]
        pa = [pm[h][:CHUNK, :] * strict2 for h in heads]
        aab = [pa[h][:, :CHUNK] for h in heads]
        rhs = [ars[h][:CHUNK, :] + _dot(pa[h][:, CHUNK:].astype(BF16), vh[h]) for h in heads]
        x = [eye + aab[h] * lvl_masks[0] for h in heads]
        for lv in range(1, n_lvl):
            xb = [x[h].astype(BF16) for h in heads]
            tm_ = [_dot((aab[h] * lvl_masks[lv]).astype(BF16), xb[h]) for h in heads]
            x = [x[h] + _dot(xb[h], tm_[h].astype(BF16)) for h in heads]
        u = [_dot(x[h].astype(BF16), rhs[h].astype(BF16)) for h in heads]
        uv = [jnp.concatenate([u[h].astype(BF16), vh[h]], axis=0) for h in heads]
        for h in heads:
            pr = (pm[h][CHUNK:, :] * incl2).astype(BF16)
            y_ref[rows, hsl[h]] = ars[h][CHUNK:, :] + _dot(pr, uv[h])
        for h in heads:
            s_ref[h] = (s0[h] + _dot_tn(uv[h], bk[h])) * gl[:, hsl[h]]
        return carry

    lax.fori_loop(0, n_chunks, chunk_step, 0)

    y = y_ref[...]
    inv_n = 1.0 / HEAD
    mean = seg_sum(y) * inv_n
    yc = y - mean
    var = seg_sum(yc * yc) * inv_n
    yn = yc * lax.rsqrt(var + RWKV_LN_EPS) * lnw_ref[...] + lnb_ref[...]
    o_ref[...] = ((yn + bonus) * g).astype(o_ref.dtype)


def _mixer_specs(x, gn, w, params, batch, seq, tc, width):
    nt = seq // tc
    d = x.shape[1]
    const = lambda b, s: (0, 0)
    in_specs = [
        pl.BlockSpec((tc, d), lambda b, s: (b * nt + jnp.minimum(s, nt - 1), 0)),
        pl.BlockSpec(gn.shape, const),
        pl.BlockSpec(w.shape, const, pipeline_mode=pl.Buffered(1)),
    ] + [pl.BlockSpec(a.shape, const) for a in params]
    out_spec = pl.BlockSpec((tc, width), lambda b, s: (b * nt + jnp.maximum(s - 1, 0), 0))
    return (batch, nt + 1), in_specs, out_spec


def _rwkv_mix(x, gn, w, mu, w0, w2, a0, a2, g2, k_k, k_a, r_k, ln_w, ln_b, batch, seq, tc):
    n = x.shape[0]
    pw = w.shape[1]
    width = w0.shape[1]
    n_heads = width // HEAD
    params = (mu, w0, w2, a0, a2, g2, k_k, k_a, r_k, ln_w, ln_b)
    grid, in_specs, out_spec = _mixer_specs(x, gn, w, params, batch, seq, tc, width)
    return pl.pallas_call(
        functools.partial(_rwkv_body, tc=tc, width=width),
        grid=grid,
        in_specs=in_specs,
        out_specs=out_spec,
        out_shape=jax.ShapeDtypeStruct((n, width), BF16),
        scratch_shapes=[pltpu.VMEM((tc, pw), F32), pltpu.VMEM((8, pw), F32), pltpu.VMEM((n_heads, HEAD, HEAD), F32)]
        + [pltpu.VMEM((tc, width), BF16) for _ in range(5)]
        + [pltpu.VMEM((max(tc // CHUNK, 8), width), F32), pltpu.VMEM((tc, width), F32)],
        compiler_params=_cparams(2),
        name="rwkv_mix",
    )(x, gn, w, *params)


def _ssd_body(x_ref, gn_ref, w_ref, cw_ref, cb_ref, dtb_ref, alog_ref, dskip_ref, nw_ref, o_ref,
              p_ref, tail_ref, st_ref, y_ref, *, tc, width, state, groups):
    step = pl.program_id(1)
    n_chunks = tc // CHUNK
    n_lane_tiles = p_ref.shape[1] // LANE
    n_tiles = max(t for t in range(1, n_chunks + 1) if n_lane_tiles % t == 0)
    tw = p_ref.shape[1] // n_tiles

    def project_tile(h, k):
        p_ref[:, k * tw:(k + 1) * tw] = _dot(h, w_ref[:, k * tw:(k + 1) * tw])

    @pl.when(step == 0)
    def _():
        st_ref[...] = jnp.zeros_like(st_ref)
        tail_ref[...] = jnp.zeros_like(tail_ref)
        h = _rms(x_ref[...], gn_ref[...]).astype(BF16)
        for k in range(n_tiles):
            project_tile(h, k)

    @pl.when(step > 0)
    def _():
        _ssd_mix_block(x_ref, gn_ref, project_tile, n_tiles, cw_ref, cb_ref, dtb_ref, alog_ref, dskip_ref, nw_ref,
                       o_ref, p_ref, tail_ref, st_ref, y_ref, tc=tc, width=width, state=state, groups=groups)


def _ssd_mix_block(x_ref, gn_ref, project_tile, n_tiles, cw_ref, cb_ref, dtb_ref, alog_ref, dskip_ref, nw_ref,
                   o_ref, p_ref, tail_ref, st_ref, y_ref, *, tc, width, state, groups):
    n_chunks = tc // CHUNK
    gw = width // groups
    hpg = gw // HEAD
    conv_ch = width + 2 * groups * state
    n_tap = cw_ref.shape[0]

    p = p_ref[...]
    zs = _silu(p[:, 0:width])
    xbc_raw = p[:, width:width + conv_ch]
    dt_raw = p[:, width + conv_ch:width + conv_ch + LANE]
    h_next = _rms(x_ref[...], gn_ref[...]).astype(BF16)
    tiles_after_chunk = {n_chunks // 2 - 1 + k * (n_chunks // 2): k + 1 for k in range(n_tiles - 1)}

    prev = tail_ref[...]
    tail_ref[...] = xbc_raw[tc - 8:tc, :]
    xcat = jnp.concatenate([prev, xbc_raw], axis=0)
    conv = cb_ref[...]
    for j in range(n_tap):
        off = 8 - (n_tap - 1) + j
        conv = conv + cw_ref[j:j + 1, :] * xcat[off:off + tc, :]
    xbc = _silu(conv)
    xs = xbc[:, 0:width]
    bm = xbc[:, width:width + groups * state].astype(BF16)
    cm = xbc[:, width + groups * state:conv_ch].astype(BF16)
    project_tile(h_next, 0)

    _, et = _seg_matrices(width, LANE)
    dt = _softplus(dt_raw + dtb_ref[...])
    da = -jnp.exp(alog_ref[...]) * dt
    cs_h = _dot_sel_l(_blockdiag_tril(tc), da)
    cs = _dot_sel_r(cs_h, et)
    dtx = _dot_sel_r(dt, et)
    xc = xs * dtx
    ecs = jnp.exp(cs)

    rr, cc = _iota((CHUNK, CHUNK), 0), _iota((CHUNK, CHUNK), 1)
    causal = cc <= rr

    for c in range(n_chunks):
        r0 = c * CHUNK
        rows = slice(r0, r0 + CHUNK)
        cs_c = cs[rows, :]
        cs_last = cs_c[CHUNK - 1:CHUNK, :]
        xc_c = xc[rows, :]
        xd = (xc_c * jnp.exp(cs_last - cs_c)).astype(BF16)
        xcb = xc_c.astype(BF16)
        cs_t = cs_h[rows, :].T
        for g in range(groups):
            gl = slice(g * gw, (g + 1) * gw)
            b_g = bm[rows, g * state:(g + 1) * state]
            c_g = cm[rows, g * state:(g + 1) * state]
            cb = _dot_nt(c_g, b_g)
            st = st_ref[g]
            y_off = _dot(c_g, st.astype(BF16)) * ecs[rows, gl]
            for hh in range(hpg):
                h = g * hpg + hh
                hs = slice(h * HEAD, (h + 1) * HEAD)
                seg = cs_c[:, hs] - cs_t[h:h + 1, :]
                lmat = jnp.where(causal, jnp.exp(jnp.where(causal, seg, 0.0)), 0.0)
                y_ref[rows, hs] = _dot((cb * lmat).astype(BF16), xcb[:, hs]) + y_off[:, hh * HEAD:(hh + 1) * HEAD]
            st_ref[g] = st * ecs[r0 + CHUNK - 1:r0 + CHUNK, gl] + _dot_tn(b_g, xd[:, gl])
        if c in tiles_after_chunk:
            project_tile(h_next, tiles_after_chunk[c])
    assert sorted(tiles_after_chunk.values()) == list(range(1, n_tiles)) and max(tiles_after_chunk) < n_chunks

    y = (y_ref[...] + dskip_ref[...] * xs) * zs
    for g in range(groups):
        gl = slice(g * gw, (g + 1) * gw)
        yg = y[:, gl]
        ms = jnp.mean(yg * yg, axis=-1, keepdims=True)
        o_ref[:, gl] = (yg * lax.rsqrt(ms + NORM_EPS) * nw_ref[:, gl]).astype(o_ref.dtype)


def _ssd_mix(x, gn, w, conv_w, conv_b, dt_bias, a_log, d_skip, norm_w, batch, seq, tc, width, state, groups):
    n = x.shape[0]
    pw = w.shape[1]
    conv_ch = width + 2 * groups * state
    params = (conv_w, conv_b, dt_bias, a_log, d_skip, norm_w)
    grid, in_specs, out_spec = _mixer_specs(x, gn, w, params, batch, seq, tc, width)
    return pl.pallas_call(
        functools.partial(_ssd_body, tc=tc, width=width, state=state, groups=groups),
        grid=grid,
        in_specs=in_specs,
        out_specs=out_spec,
        out_shape=jax.ShapeDtypeStruct((n, width), BF16),
        scratch_shapes=[pltpu.VMEM((tc, pw), F32), pltpu.VMEM((8, conv_ch), F32),
                        pltpu.VMEM((groups, state, width // groups), F32), pltpu.VMEM((tc, width), F32)],
        compiler_params=_cparams(2),
        name="ssd_mix",
    )(x, gn, w, *params)


EDGE_ROWS = 256
N_PROJ_TILES = 4
PAIR = 2 * CHUNK
BAND = (LEFT_CHUNKS + 1) * CHUNK
PAIR_BAND = BAND + CHUNK


def _bias_table_body(rb_ref, o_ref, *, n_bucket, rel_future):
    ext_w = PAIR_BAND + LANE
    m = _iota((n_bucket, ext_w), 1)
    bucket = jnp.clip(PAIR_BAND - 1 - m, -rel_future, REL_PAST_CLIP) + rel_future
    sel = (bucket == _iota((n_bucket, ext_w), 0)).astype(BF16)
    ext = _dot_sel_r(rb_ref[...], sel) * LOG2E
    for r in range(PAIR):
        o_ref[r] = ext[:, PAIR - 1 - r:PAIR - 1 - r + PAIR_BAND]


def _bias_table(rel_bias):
    n_heads, n_bucket = rel_bias.shape
    out = pl.pallas_call(
        functools.partial(_bias_table_body, n_bucket=n_bucket, rel_future=CHUNK - 1),
        out_shape=jax.ShapeDtypeStruct((PAIR, n_heads, PAIR_BAND), F32),
        name="bias_table",
    )(rel_bias)
    return jnp.transpose(out, (1, 0, 2))


def _attn_body(q_ref, kp_ref, kc_ref, vp_ref, vc_ref, b_ref, o_ref, *, tq, dh):
    i = pl.program_id(2)
    n_heads = q_ref.shape[1] // dh
    row, col = _iota((PAIR, PAIR_BAND), 0), _iota((PAIR, PAIR_BAND), 1)
    lo = (row // CHUNK) * CHUNK
    in_band = (col >= lo) & (col < lo + BAND)
    n_pairs = tq // PAIR
    k0 = [tq - LEFT_CHUNKS * CHUNK + jp * PAIR for jp in range(n_pairs)]
    work = [(h, jp) for h in range(n_heads) for jp in range(n_pairs)]

    def attend(first_block):
        hcols = [slice(h * dh, (h + 1) * dh) for h in range(n_heads)]
        kcat = [jnp.concatenate([kp_ref[:, hc], kc_ref[:, hc]], axis=0) for hc in hcols]
        vcat = [jnp.concatenate([vp_ref[:, hc], vc_ref[:, hc]], axis=0) for hc in hcols]
        s = [_dot_nt(q_ref[jp * PAIR:(jp + 1) * PAIR, hcols[h]], kcat[h][k0[jp]:k0[jp] + PAIR_BAND, :]) + b_ref[h]
             for h, jp in work]
        if first_block:
            valid = [in_band & (col + k0[jp] >= tq) for jp in range(n_pairs)]
            s = [jnp.where(valid[jp], s_, -1e30) for (h, jp), s_ in zip(work, s)]
        else:
            edge = PAIR_BAND - LANE
            s = [jnp.concatenate([jnp.where(in_band[:, :LANE], s_[:, :LANE], -1e30), s_[:, LANE:edge],
                                  jnp.where(in_band[:, edge:], s_[:, edge:], -1e30)], axis=1) for s_ in s]
        mx = [jnp.max(s_, axis=-1, keepdims=True) for s_ in s]
        pexp = [jnp.exp2(s_ - m_) for s_, m_ in zip(s, mx)]
        den = [jnp.sum(p_, axis=-1, keepdims=True) for p_ in pexp]
        o = [_dot(p_.astype(BF16), vcat[h][k0[jp]:k0[jp] + PAIR_BAND, :]) for (h, jp), p_ in zip(work, pexp)]
        for (h, jp), o_, d_ in zip(work, o, den):
            o_ref[jp * PAIR:(jp + 1) * PAIR, hcols[h]] = (o_ / d_).astype(o_ref.dtype)

    @pl.when(i == 0)
    def _():
        attend(True)

    @pl.when(i > 0)
    def _():
        attend(False)


def _band_attention(qkv, bias, batch, seq, n_heads, tq, hp):
    n, three_d = qkv.shape
    d = three_d // 3
    dh = d // n_heads
    nt = seq // tq
    ng = n_heads // hp
    cur = lambda off: (lambda b, h, i: (b * nt + i, off * ng + h))
    prv = lambda off: (lambda b, h, i: (b * nt + jnp.maximum(i - 1, 0), off * ng + h))
    blk = lambda f: pl.BlockSpec((tq, hp * dh), f)
    return pl.pallas_call(
        functools.partial(_attn_body, tq=tq, dh=dh),
        grid=(batch, ng, nt),
        in_specs=[blk(cur(0)), blk(prv(1)), blk(cur(1)), blk(prv(2)), blk(cur(2)),
                  pl.BlockSpec((hp, PAIR, PAIR_BAND), lambda b, h, i: (h, 0, 0))],
        out_specs=pl.BlockSpec((tq, hp * dh), lambda b, h, i: (b * nt + i, h)),
        out_shape=jax.ShapeDtypeStruct((n, d), BF16),
        compiler_params=_cparams(3),
        name="band_attention",
    )(qkv, qkv, qkv, qkv, qkv, bias)


def _pad_cols(a, width):
    return jnp.pad(a, ((0, 0), (0, width - a.shape[1])))


def _pad_rows(a, rows):
    return jnp.pad(a, ((0, rows - a.shape[0]), (0, 0)))


def _row(a):
    return a.reshape(1, -1)


class _Tiles(NamedTuple):
    rows: int
    proj_rows: int
    mix_rows: int
    attn_rows: int
    attn_heads: int
    qkv_cols: int
    ffn_cols: int


def _tiles(n, seq):
    rows = min(1024, n)
    return _Tiles(rows=rows, proj_rows=max(rows // 2, 8), mix_rows=min(4 * CHUNK, seq),
                  attn_rows=min(LEFT_CHUNKS * CHUNK, seq), attn_heads=4, qkv_cols=1024, ffn_cols=512)


def kernel(x, norm_g, w_in_ab, rwkv_mu, rwkv_w0, rwkv_w2, rwkv_a0, rwkv_a2, rwkv_g2, rwkv_k_k, rwkv_k_a, rwkv_r_k, rwkv_ln_w, rwkv_ln_b, ssm_conv_w, ssm_conv_b, ssm_dt_bias, ssm_A_log, ssm_D, ssm_norm_w, w_out_ab, w_qkv, attn_rel_bias, w_out_c, ffn_w_gate, ffn_w_up, ffn_w_down):
    batch, seq, d = x.shape
    n = batch * seq
    depth = norm_g.shape[0]
    xf = x.reshape(n, d)

    rw = rwkv_w0.shape[1]
    lora_w, lora_a, lora_g = rwkv_w2.shape[1], rwkv_a2.shape[1], rwkv_g2.shape[1]
    sw = ssm_norm_w.shape[1]
    n_ssm_heads = ssm_A_log.shape[1]
    conv_ch = ssm_conv_w.shape[2]
    groups = 2
    state = (conv_ch - sw) // (2 * groups)
    n_att_heads = attn_rel_bias.shape[1]
    rwkv_proj = 3 * rw + lora_w + lora_a + lora_g
    t = _tiles(n, seq)

    for l in range(depth):
        i = l // 2
        g = norm_g[l]
        if l % 2 == 0:
            w_in = w_in_ab[i]
            o1, o2, o3 = 3 * rw, 3 * rw + lora_w, 3 * rw + lora_w + lora_a
            seg = lambda a: [a[:, 0:o1], _pad_cols(a[:, o1:o2], LANE), _pad_cols(a[:, o2:o3], LANE), a[:, o3:rwkv_proj]]
            w_r = jnp.concatenate(seg(w_in), axis=1).astype(BF16)
            mu = jnp.concatenate(seg(_row(rwkv_mu[i])), axis=1)
            o4 = rwkv_proj + sw + conv_ch
            w_s = jnp.concatenate([w_in[:, rwkv_proj:o4], _pad_cols(w_in[:, o4:], LANE)], axis=1).astype(BF16)

            y_a = _rwkv_mix(
                xf, g[0:1], w_r, mu, _row(rwkv_w0[i]), _pad_rows(rwkv_w2[i], LANE).astype(BF16), _row(rwkv_a0[i]),
                _pad_rows(rwkv_a2[i], LANE).astype(BF16), rwkv_g2[i].astype(BF16), _row(rwkv_k_k[i]),
                _row(rwkv_k_a[i]), _row(rwkv_r_k[i]), _row(rwkv_ln_w[i]), _row(rwkv_ln_b[i]), batch, seq, t.mix_rows)
            y_b = _ssd_mix(
                xf, g[0:1], w_s, ssm_conv_w[i], _row(ssm_conv_b[i]), _pad_cols(_row(ssm_dt_bias[i]), LANE),
                _pad_cols(_row(ssm_A_log[i]), LANE), _row(jnp.repeat(ssm_D[i], sw // n_ssm_heads)),
                _row(ssm_norm_w[i]), batch, seq, t.mix_rows, sw, state, groups)
            w_o = w_out_ab[i].astype(BF16)
            xf = _proj_res([y_a, y_b], [w_o[:rw], w_o[rw:]], xf, g[1:2], t.proj_rows)
        else:
            qkv = _norm_matmul(xf, g[0:1], w_qkv[i].astype(BF16), BF16, t.rows, t.qkv_cols, n_scaled=d // t.qkv_cols,
                               scale=(d // n_att_heads) ** -0.5 * LOG2E)
            bias = _bias_table(attn_rel_bias[i])
            att = _band_attention(qkv, bias, batch, seq, n_att_heads, t.attn_rows, t.attn_heads)
            xf = _proj_res([att], [w_out_c[i].astype(BF16)], xf, g[1:2], t.proj_rows)
        xf = _ffn(xf, g[2:3], g[3:4], ffn_w_gate[l].astype(BF16), ffn_w_up[l].astype(BF16),
                  ffn_w_down[l].astype(BF16), t.rows, t.ffn_cols)
    return xf.reshape(batch, seq, d)
```

```python
import functools
import math
from typing import NamedTuple

import jax
import jax.numpy as jnp
from jax import lax
from jax.experimental import pallas as pl
from jax.experimental.pallas import tpu as pltpu

F32 = jnp.float32
BF16 = jnp.bfloat16

NORM_EPS = 1e-6
RWKV_LN_EPS = 64e-5
CHUNK = 64
HEAD = 64
LANE = 128
LEFT_CHUNKS = 8
REL_PAST_CLIP = 256
LOG2E = math.log2(math.e)
VMEM_LIMIT = 60 * 1024 * 1024


def _cparams(n_axes):
    return pltpu.CompilerParams(dimension_semantics=("arbitrary",) * n_axes, vmem_limit_bytes=VMEM_LIMIT)


def _dot(a, b):
    return jnp.dot(a, b, preferred_element_type=F32)


def _dot_nt(a, b):
    return lax.dot_general(a, b, (((1,), (1,)), ((), ())), preferred_element_type=F32)


def _dot_tn(a, b):
    return lax.dot_general(a, b, (((0,), (0,)), ((), ())), preferred_element_type=F32)


def _split2(t):
    hi = t.astype(BF16)
    lo = (t - hi.astype(F32)).astype(BF16)
    return hi, lo


def _dot_sel_r(t, sel):
    hi, lo = _split2(t)
    return _dot(hi, sel) + _dot(lo, sel)


def _dot_sel_l(sel, t):
    hi, lo = _split2(t)
    return _dot(sel, hi) + _dot(sel, lo)


def _rms(x, g):
    ms = jnp.mean(x * x, axis=-1, keepdims=True)
    return x * lax.rsqrt(ms + NORM_EPS) * g


def _sigmoid(x):
    return 1.0 / (1.0 + jnp.exp(-x))


def _silu(x):
    return x * _sigmoid(x)


def _softplus(x):
    return jnp.maximum(x, 0.0) + jnp.log1p(jnp.exp(-jnp.abs(x)))


def _iota(shape, axis):
    return lax.broadcasted_iota(jnp.int32, shape, axis)


def _norm_matmul_body(x_ref, g_ref, w_ref, o_ref, h_ref, *, n_scaled, scale):
    j = pl.program_id(1)
    mult = jnp.where(j < n_scaled, jnp.float32(scale), jnp.float32(1.0))

    @pl.when(j == 0)
    def _():
        for r in range(0, x_ref.shape[0], EDGE_ROWS):
            rs = slice(r, r + EDGE_ROWS)
            h = _rms(x_ref[rs, :], g_ref[...]).astype(BF16)
            h_ref[rs, :] = h
            o_ref[rs, :] = (_dot(h, w_ref[...]) * mult).astype(o_ref.dtype)

    @pl.when(j > 0)
    def _():
        o_ref[...] = (_dot(h_ref[...], w_ref[...]) * mult).astype(o_ref.dtype)


def _norm_matmul(x, g, w, out_dtype, tm, tn, n_scaled=0, scale=1.0):
    n, d = x.shape
    m = w.shape[1]
    return pl.pallas_call(
        functools.partial(_norm_matmul_body, n_scaled=n_scaled, scale=scale),
        grid=(n // tm, m // tn),
        in_specs=[
            pl.BlockSpec((tm, d), lambda i, j: (i, 0)),
            pl.BlockSpec((1, d), lambda i, j: (0, 0)),
            pl.BlockSpec((d, tn), lambda i, j: (0, j)),
        ],
        out_specs=pl.BlockSpec((tm, tn), lambda i, j: (i, j)),
        out_shape=jax.ShapeDtypeStruct((n, m), out_dtype),
        scratch_shapes=[pltpu.VMEM((tm, d), BF16)],
        compiler_params=_cparams(2),
        name="norm_matmul",
    )(x, g, w)


def _proj_res_body(*refs, n_in):
    y_refs, w_refs = refs[:n_in], refs[n_in:2 * n_in]
    x_ref, g_ref, o_ref = refs[2 * n_in:]
    m = _dot(y_refs[0][...], w_refs[0][...])
    for y_ref, w_ref in zip(y_refs[1:], w_refs[1:]):
        m = m + _dot(y_ref[...], w_ref[...])
    o_ref[...] = x_ref[...] + _rms(m, g_ref[...])


def _proj_res(ys, ws, x, g, tm):
    n, d = x.shape
    n_in = len(ys)
    in_specs = [pl.BlockSpec((tm, y.shape[1]), lambda i: (i, 0)) for y in ys]
    in_specs += [pl.BlockSpec(w.shape, lambda i: (0, 0)) for w in ws]
    in_specs += [pl.BlockSpec((tm, d), lambda i: (i, 0)), pl.BlockSpec((1, d), lambda i: (0, 0))]
    return pl.pallas_call(
        functools.partial(_proj_res_body, n_in=n_in),
        grid=(n // tm,),
        in_specs=in_specs,
        out_specs=pl.BlockSpec((tm, d), lambda i: (i, 0)),
        out_shape=jax.ShapeDtypeStruct((n, d), F32),
        compiler_params=_cparams(1),
        name="proj_res",
    )(*ys, *ws, x, g)


def _ffn_body(x_ref, gi_ref, go_ref, wg_ref, wu_ref, wd_ref, o_ref, h_ref):
    j = pl.program_id(1)
    last_j = pl.num_programs(1) - 1
    tm = x_ref.shape[0]

    def partial_out(h):
        a = (_silu(_dot(h, wg_ref[...])) * _dot(h, wu_ref[...])).astype(BF16)
        return _dot(a, wd_ref[...])

    row_groups = [slice(r, r + EDGE_ROWS) for r in range(0, tm, EDGE_ROWS)]

    @pl.when(j == 0)
    def _():
        for rs in row_groups:
            h = _rms(x_ref[rs, :], gi_ref[...]).astype(BF16)
            h_ref[rs, :] = h
            o_ref[rs, :] = partial_out(h)

    @pl.when((j > 0) & (j < last_j))
    def _():
        o_ref[...] += partial_out(h_ref[...])

    @pl.when(j == last_j)
    def _():
        for rs in row_groups:
            acc = o_ref[rs, :] + partial_out(h_ref[rs, :])
            o_ref[rs, :] = x_ref[rs, :] + _rms(acc, go_ref[...])


def _ffn(x, g_in, g_out, wg, wu, wd, tm, tf):
    n, d = x.shape
    f = wg.shape[1]
    assert f // tf >= 2 and tm % EDGE_ROWS == 0
    return pl.pallas_call(
        _ffn_body,
        grid=(n // tm, f // tf),
        in_specs=[
            pl.BlockSpec((tm, d), lambda i, j: (i, 0), pipeline_mode=pl.Buffered(1)),
            pl.BlockSpec((1, d), lambda i, j: (0, 0)),
            pl.BlockSpec((1, d), lambda i, j: (0, 0)),
            pl.BlockSpec((d, tf), lambda i, j: (0, j)),
            pl.BlockSpec((d, tf), lambda i, j: (0, j)),
            pl.BlockSpec((tf, d), lambda i, j: (j, 0)),
        ],
        out_specs=pl.BlockSpec((tm, d), lambda i, j: (i, 0)),
        out_shape=jax.ShapeDtypeStruct((n, d), F32),
        scratch_shapes=[pltpu.VMEM((tm, d), BF16)],
        compiler_params=_cparams(2),
        name="ffn",
    )(x, g_in, g_out, wg, wu, wd)


def _seg_matrices(width, n_seg_pad):
    e = (_iota((width, n_seg_pad), 0) // HEAD == _iota((width, n_seg_pad), 1)).astype(BF16)
    et = (_iota((n_seg_pad, width), 1) // HEAD == _iota((n_seg_pad, width), 0)).astype(BF16)
    return e, et


def _blockdiag_tril(tc):
    r, c = _iota((tc, tc), 0), _iota((tc, tc), 1)
    return ((r // CHUNK == c // CHUNK) & (c <= r)).astype(BF16)


def _rwkv_body(x_ref, gn_ref, w_ref, mu_ref, w0_ref, w2_ref, a0_ref, a2_ref, g2_ref, kk_ref, ka_ref, rk_ref,
               lnw_ref, lnb_ref, o_ref, p_ref, prow_ref, s_ref, at_ref, rt_ref, bt_ref, kt_ref, v_ref, gl_ref, y_ref,
               *, tc, width):
    step = pl.program_id(1)

    tw = p_ref.shape[1] // N_PROJ_TILES
    assert tw % MXU_COLS == 0

    def h_next_fn():
        return _rms(x_ref[...], gn_ref[...]).astype(BF16)

    def project_tile(h, k):
        p_ref[:, k * tw:(k + 1) * tw] = _dot(h, w_ref[:, k * tw:(k + 1) * tw])

    @pl.when(step == 0)
    def _():
        s_ref[...] = jnp.zeros_like(s_ref)
        prow_ref[...] = jnp.zeros_like(prow_ref)
        h = h_next_fn()
        for k in range(N_PROJ_TILES):
            project_tile(h, k)

    @pl.when(step > 0)
    def _():
        _rwkv_mix_block(h_next_fn, project_tile, mu_ref, w0_ref, w2_ref, a0_ref, a2_ref, g2_ref, kk_ref, ka_ref, rk_ref, lnw_ref,
                        lnb_ref, o_ref, p_ref, prow_ref, s_ref, at_ref, rt_ref, bt_ref, kt_ref, v_ref, gl_ref, y_ref,
                        tc=tc, width=width)


def _rwkv_mix_block(h_next_fn, project_tile, mu_ref, w0_ref, w2_ref, a0_ref, a2_ref, g2_ref, kk_ref, ka_ref, rk_ref, lnw_ref,
                    lnb_ref, o_ref, p_ref, prow_ref, s_ref, at_ref, rt_ref, bt_ref, kt_ref, v_ref, gl_ref, y_ref,
                    *, tc, width):
    n_heads = width // HEAD
    n_chunks = tc // CHUNK

    p = p_ref[...]
    first_row = _iota((tc, 1), 0) == 0

    def lerp(cols):
        pc = p[:, cols]
        psh = jnp.where(first_row, prow_ref[0:1, cols], pltpu.roll(pc, 1, 0))
        return pc + (psh - pc) * mu_ref[:, cols]

    o1 = 3 * width
    xl = lerp(slice(o1, o1 + 4 * LANE))
    wc, ac, gc = xl[:, 0:LANE], xl[:, LANE:2 * LANE], xl[:, 2 * LANE:4 * LANE]
    z = w0_ref[...] + _dot(jnp.tanh(wc).astype(BF16), w2_ref[...])
    a = _sigmoid(a0_ref[...] + _dot(ac.astype(BF16), a2_ref[...]))
    g = _dot(_sigmoid(gc).astype(BF16), g2_ref[...])
    lw = -jnp.exp(-_softplus(-z) - 0.5)

    h_next = h_next_fn()
    project_tile(h_next, 0)
    project_tile(h_next, 1)
    project_tile(h_next, 2)

    xr = lerp(slice(0, o1))
    last_row = p[tc - 1:tc, :]
    r = xr[:, 0:width]
    k = xr[:, width:2 * width]
    v = xr[:, 2 * width:3 * width]

    mxu_w = 2 * LANE
    same_head = (_iota((mxu_w, mxu_w), 0) // HEAD == _iota((mxu_w, mxu_w), 1) // HEAD).astype(BF16)

    def seg_sum(t):
        return jnp.concatenate(
            [_dot_sel_r(t[:, j:j + mxu_w], same_head) for j in range(0, width, mxu_w)], axis=1)

    kk = k * kk_ref[...]
    kk = kk * lax.rsqrt(jnp.maximum(seg_sum(kk * kk), 1e-24))
    k2 = k * (1.0 + (a - 1.0) * ka_ref[...])
    project_tile(h_next, 3)
    project_tile(h_next, 4)
    bonus = seg_sum(r * k2 * rk_ref[...]) * v
    cs = _dot_sel_l(_blockdiag_tril(tc), lw)
    project_tile(h_next, 5)
    project_tile(h_next, 6)
    prow_ref[0:1, :] = last_row

    ginv = jnp.exp(-cs)
    at_ref[...] = (-kk * jnp.exp(cs - lw)).astype(BF16)
    rt_ref[...] = (r * jnp.exp(cs)).astype(BF16)
    bt_ref[...] = (kk * a * ginv).astype(BF16)
    kt_ref[...] = (k2 * ginv).astype(BF16)
    v_ref[...] = v.astype(BF16)
    for c in range(n_chunks):
        last = c * CHUNK + CHUNK - 1
        gl_ref[c:c + 1, :] = jnp.exp(cs[last:last + 1, :])

    rr, cc = _iota((CHUNK, CHUNK), 0), _iota((CHUNK, CHUNK), 1)
    rr2, cc2 = _iota((CHUNK, 2 * CHUNK), 0), _iota((CHUNK, 2 * CHUNK), 1) % CHUNK
    strict2 = (cc2 < rr2).astype(F32)
    incl2 = (cc2 <= rr2).astype(F32)
    eye = (cc == rr).astype(F32)
    n_lvl = int(math.log2(CHUNK))
    lvl_masks = []
    for lv in range(n_lvl):
        same = (rr >> (lv + 1)) == (cc >> (lv + 1))
        lvl_masks.append((same & (((rr >> lv) & 1) == 1) & (((cc >> lv) & 1) == 0)).astype(F32))

    def chunk_step(c, carry):
        t0 = pl.multiple_of(c * CHUNK, CHUNK)
        rows = pl.ds(t0, CHUNK)
        gl = gl_ref[pl.ds(c, 1), :]
        heads = range(n_heads)
        hsl = [slice(h * HEAD, (h + 1) * HEAD) for h in heads]
        vh = [v_ref[rows, hs] for hs in hsl]
        ar = [jnp.concatenate([at_ref[rows, hs], rt_ref[rows, hs]], axis=0) for hs in hsl]
        bk = [jnp.concatenate([bt_ref[rows, hs], kt_ref[rows, hs]], axis=0) for hs in hsl]
        pm = [_dot_nt(ar[h], bk[h]) for h in heads]
        s0 = [s_ref[h] for h in heads]
        ars = [_dot_nt(ar[h], s0[h].astype(BF16)) for h in heads]
        pa = [pm[h][:CHUNK, :] * strict2 for h in heads]
        aab = [pa[h][:, :CHUNK] for h in heads]
        rhs = [ars[h][:CHUNK, :] + _dot(pa[h][:, CHUNK:].astype(BF16), vh[h]) for h in heads]
        x = [eye + aab[h] * lvl_masks[0] for h in heads]
        for lv in range(1, n_lvl):
            xb = [x[h].astype(BF16) for h in heads]
            tm_ = [_dot((aab[h] * lvl_masks[lv]).astype(BF16), xb[h]) for h in heads]
            x = [x[h] + _dot(xb[h], tm_[h].astype(BF16)) for h in heads]
        u = [_dot(x[h].astype(BF16), rhs[h].astype(BF16)) for h in heads]
        uv = [jnp.concatenate([u[h].astype(BF16), vh[h]], axis=0) for h in heads]
        for h in heads:
            pr = (pm[h][CHUNK:, :] * incl2).astype(BF16)
            y_ref[rows, hsl[h]] = ars[h][CHUNK:, :] + _dot(pr, uv[h])
        for h in heads:
            s_ref[h] = (s0[h] + _dot_tn(uv[h], bk[h])) * gl[:, hsl[h]]
        return carry

    lax.fori_loop(0, n_chunks, chunk_step, 0)

    y = y_ref[...]
    inv_n = 1.0 / HEAD
    mean = seg_sum(y) * inv_n
    yc = y - mean
    var = seg_sum(yc * yc) * inv_n
    yn = yc * lax.rsqrt(var + RWKV_LN_EPS) * lnw_ref[...] + lnb_ref[...]
    o_ref[...] = ((yn + bonus) * g).astype(o_ref.dtype)


def _mixer_specs(x, gn, w, params, batch, seq, tc, width):
    nt = seq // tc
    d = x.shape[1]
    const = lambda b, s: (0, 0)
    in_specs = [
        pl.BlockSpec((tc, d), lambda b, s: (b * nt + jnp.minimum(s, nt - 1), 0)),
        pl.BlockSpec(gn.shape, const),
        pl.BlockSpec(w.shape, const, pipeline_mode=pl.Buffered(1)),
    ] + [pl.BlockSpec(a.shape, const) for a in params]
    out_spec = pl.BlockSpec((tc, width), lambda b, s: (b * nt + jnp.maximum(s - 1, 0), 0))
    return (batch, nt + 1), in_specs, out_spec


def _rwkv_mix(x, gn, w, mu, w0, w2, a0, a2, g2, k_k, k_a, r_k, ln_w, ln_b, batch, seq, tc):
    n = x.shape[0]
    pw = w.shape[1]
    width = w0.shape[1]
    n_heads = width // HEAD
    params = (mu, w0, w2, a0, a2, g2, k_k, k_a, r_k, ln_w, ln_b)
    grid, in_specs, out_spec = _mixer_specs(x, gn, w, params, batch, seq, tc, width)
    return pl.pallas_call(
        functools.partial(_rwkv_body, tc=tc, width=width),
        grid=grid,
        in_specs=in_specs,
        out_specs=out_spec,
        out_shape=jax.ShapeDtypeStruct((n, width), BF16),
        scratch_shapes=[pltpu.VMEM((tc, pw), F32), pltpu.VMEM((8, pw), F32), pltpu.VMEM((n_heads, HEAD, HEAD), F32)]
        + [pltpu.VMEM((tc, width), BF16) for _ in range(5)]
        + [pltpu.VMEM((max(tc // CHUNK, 8), width), F32), pltpu.VMEM((tc, width), F32)],
        compiler_params=_cparams(2),
        name="rwkv_mix",
    )(x, gn, w, *params)


def _ssd_body(x_ref, gn_ref, w_ref, cw_ref, cb_ref, dtb_ref, alog_ref, dskip_ref, nw_ref, o_ref,
              p_ref, tail_ref, st_ref, y_ref, *, tc, width, state, groups):
    step = pl.program_id(1)
    n_chunks = tc // CHUNK
    pw = p_ref.shape[1]
    tw = 2 * MXU_COLS
    tiles = [slice(c0, min(c0 + tw, pw)) for c0 in range(0, pw, tw)]
    n_tiles = len(tiles)

    def project_tile(h, k):
        p_ref[:, tiles[k]] = _dot(h, w_ref[:, tiles[k]])

    @pl.when(step == 0)
    def _():
        st_ref[...] = jnp.zeros_like(st_ref)
        tail_ref[...] = jnp.zeros_like(tail_ref)
        h = _rms(x_ref[...], gn_ref[...]).astype(BF16)
        for k in range(n_tiles):
            project_tile(h, k)

    @pl.when(step > 0)
    def _():
        _ssd_mix_block(x_ref, gn_ref, project_tile, n_tiles, cw_ref, cb_ref, dtb_ref, alog_ref, dskip_ref, nw_ref,
                       o_ref, p_ref, tail_ref, st_ref, y_ref, tc=tc, width=width, state=state, groups=groups)


def _ssd_mix_block(x_ref, gn_ref, project_tile, n_tiles, cw_ref, cb_ref, dtb_ref, alog_ref, dskip_ref, nw_ref,
                   o_ref, p_ref, tail_ref, st_ref, y_ref, *, tc, width, state, groups):
    n_chunks = tc // CHUNK
    gw = width // groups
    hpg = gw // HEAD
    conv_ch = width + 2 * groups * state
    n_tap = cw_ref.shape[0]

    p = p_ref[...]
    zs = _silu(p[:, 0:width])
    xbc_raw = p[:, width:width + conv_ch]
    dt_raw = p[:, width + conv_ch:width + conv_ch + LANE]
    h_next = _rms(x_ref[...], gn_ref[...]).astype(BF16)
    tiles_after_chunk = {c: [k for k in range(1, n_tiles) if (k - 1) % n_chunks == c] for c in range(n_chunks)}

    prev = tail_ref[...]
    tail_ref[...] = xbc_raw[tc - 8:tc, :]
    xcat = jnp.concatenate([prev, xbc_raw], axis=0)
    conv = cb_ref[...]
    for j in range(n_tap):
        off = 8 - (n_tap - 1) + j
        conv = conv + cw_ref[j:j + 1, :] * xcat[off:off + tc, :]
    xbc = _silu(conv)
    xs = xbc[:, 0:width]
    bm = xbc[:, width:width + groups * state].astype(BF16)
    cm = xbc[:, width + groups * state:conv_ch].astype(BF16)
    project_tile(h_next, 0)

    _, et = _seg_matrices(width, LANE)
    dt = _softplus(dt_raw + dtb_ref[...])
    da = -jnp.exp(alog_ref[...]) * dt
    cs_h = _dot_sel_l(_blockdiag_tril(tc), da)
    cs = _dot_sel_r(cs_h, et)
    dtx = _dot_sel_r(dt, et)
    xc = xs * dtx
    ecs = jnp.exp(cs)

    rr, cc = _iota((CHUNK, CHUNK), 0), _iota((CHUNK, CHUNK), 1)
    causal = cc <= rr

    for c in range(n_chunks):
        r0 = c * CHUNK
        rows = slice(r0, r0 + CHUNK)
        cs_c = cs[rows, :]
        cs_last = cs_c[CHUNK - 1:CHUNK, :]
        xc_c = xc[rows, :]
        xd = (xc_c * jnp.exp(cs_last - cs_c)).astype(BF16)
        xcb = xc_c.astype(BF16)
        cs_t = cs_h[rows, :].T
        for g in range(groups):
            gl = slice(g * gw, (g + 1) * gw)
            b_g = bm[rows, g * state:(g + 1) * state]
            c_g = cm[rows, g * state:(g + 1) * state]
            cb = _dot_nt(c_g, b_g)
            st = st_ref[g]
            y_off = _dot(c_g, st.astype(BF16)) * ecs[rows, gl]
            for hh in range(hpg):
                h = g * hpg + hh
                hs = slice(h * HEAD, (h + 1) * HEAD)
                seg = cs_c[:, hs] - cs_t[h:h + 1, :]
                lmat = jnp.where(causal, jnp.exp(jnp.where(causal, seg, 0.0)), 0.0)
                y_ref[rows, hs] = _dot((cb * lmat).astype(BF16), xcb[:, hs]) + y_off[:, hh * HEAD:(hh + 1) * HEAD]
            st_ref[g] = st * ecs[r0 + CHUNK - 1:r0 + CHUNK, gl] + _dot_tn(b_g, xd[:, gl])
        for k in tiles_after_chunk[c]:
            project_tile(h_next, k)

    y = (y_ref[...] + dskip_ref[...] * xs) * zs
    for g in range(groups):
        gl = slice(g * gw, (g + 1) * gw)
        yg = y[:, gl]
        ms = jnp.mean(yg * yg, axis=-1, keepdims=True)
        o_ref[:, gl] = (yg * lax.rsqrt(ms + NORM_EPS) * nw_ref[:, gl]).astype(o_ref.dtype)


def _ssd_mix(x, gn, w, conv_w, conv_b, dt_bias, a_log, d_skip, norm_w, batch, seq, tc, width, state, groups):
    n = x.shape[0]
    pw = w.shape[1]
    conv_ch = width + 2 * groups * state
    params = (conv_w, conv_b, dt_bias, a_log, d_skip, norm_w)
    grid, in_specs, out_spec = _mixer_specs(x, gn, w, params, batch, seq, tc, width)
    return pl.pallas_call(
        functools.partial(_ssd_body, tc=tc, width=width, state=state, groups=groups),
        grid=grid,
        in_specs=in_specs,
        out_specs=out_spec,
        out_shape=jax.ShapeDtypeStruct((n, width), BF16),
        scratch_shapes=[pltpu.VMEM((tc, pw), F32), pltpu.VMEM((8, conv_ch), F32),
                        pltpu.VMEM((groups, state, width // groups), F32), pltpu.VMEM((tc, width), F32)],
        compiler_params=_cparams(2),
        name="ssd_mix",
    )(x, gn, w, *params)


EDGE_ROWS = 256
N_PROJ_TILES = 7
MXU_COLS = 256
PAIR = 2 * CHUNK
BAND = (LEFT_CHUNKS + 1) * CHUNK
PAIR_BAND = BAND + CHUNK


def _bias_table_body(rb_ref, o_ref, *, n_bucket, rel_future):
    ext_w = PAIR_BAND + LANE
    m = _iota((n_bucket, ext_w), 1)
    bucket = jnp.clip(PAIR_BAND - 1 - m, -rel_future, REL_PAST_CLIP) + rel_future
    sel = (bucket == _iota((n_bucket, ext_w), 0)).astype(BF16)
    ext = _dot_sel_r(rb_ref[...], sel) * LOG2E
    for r in range(PAIR):
        o_ref[r] = ext[:, PAIR - 1 - r:PAIR - 1 - r + PAIR_BAND]


def _bias_table(rel_bias):
    n_heads, n_bucket = rel_bias.shape
    out = pl.pallas_call(
        functools.partial(_bias_table_body, n_bucket=n_bucket, rel_future=CHUNK - 1),
        out_shape=jax.ShapeDtypeStruct((PAIR, n_heads, PAIR_BAND), F32),
        name="bias_table",
    )(rel_bias)
    return jnp.transpose(out, (1, 0, 2))


def _attn_body(q_ref, kp_ref, kc_ref, vp_ref, vc_ref, b_ref, o_ref, *, tq, dh):
    i = pl.program_id(2)
    n_heads = q_ref.shape[1] // dh
    row, col = _iota((PAIR, PAIR_BAND), 0), _iota((PAIR, PAIR_BAND), 1)
    lo = (row // CHUNK) * CHUNK
    in_band = (col >= lo) & (col < lo + BAND)
    n_pairs = tq // PAIR
    k0 = [tq - LEFT_CHUNKS * CHUNK + jp * PAIR for jp in range(n_pairs)]
    work = [(h, jp) for h in range(n_heads) for jp in range(n_pairs)]

    def attend(first_block):
        hcols = [slice(h * dh, (h + 1) * dh) for h in range(n_heads)]
        kcat = [jnp.concatenate([kp_ref[:, hc], kc_ref[:, hc]], axis=0) for hc in hcols]
        vcat = [jnp.concatenate([vp_ref[:, hc], vc_ref[:, hc]], axis=0) for hc in hcols]
        s = [_dot_nt(q_ref[jp * PAIR:(jp + 1) * PAIR, hcols[h]], kcat[h][k0[jp]:k0[jp] + PAIR_BAND, :]) + b_ref[h]
             for h, jp in work]
        if first_block:
            valid = [in_band & (col + k0[jp] >= tq) for jp in range(n_pairs)]
            s = [jnp.where(valid[jp], s_, -1e30) for (h, jp), s_ in zip(work, s)]
        else:
            edge = PAIR_BAND - LANE
            s = [jnp.concatenate([jnp.where(in_band[:, :LANE], s_[:, :LANE], -1e30), s_[:, LANE:edge],
                                  jnp.where(in_band[:, edge:], s_[:, edge:], -1e30)], axis=1) for s_ in s]
        mx = [jnp.max(s_, axis=-1, keepdims=True) for s_ in s]
        pexp = [jnp.exp2(s_ - m_) for s_, m_ in zip(s, mx)]
        den = [jnp.sum(p_, axis=-1, keepdims=True) for p_ in pexp]
        o = [_dot(p_.astype(BF16), vcat[h][k0[jp]:k0[jp] + PAIR_BAND, :]) for (h, jp), p_ in zip(work, pexp)]
        for (h, jp), o_, d_ in zip(work, o, den):
            o_ref[jp * PAIR:(jp + 1) * PAIR, hcols[h]] = (o_ / d_).astype(o_ref.dtype)

    @pl.when(i == 0)
    def _():
        attend(True)

    @pl.when(i > 0)
    def _():
        attend(False)


def _band_attention(qkv, bias, batch, seq, n_heads, tq, hp):
    n, three_d = qkv.shape
    d = three_d // 3
    dh = d // n_heads
    nt = seq // tq
    ng = n_heads // hp
    cur = lambda off: (lambda b, h, i: (b * nt + i, off * ng + h))
    prv = lambda off: (lambda b, h, i: (b * nt + jnp.maximum(i - 1, 0), off * ng + h))
    blk = lambda f: pl.BlockSpec((tq, hp * dh), f)
    return pl.pallas_call(
        functools.partial(_attn_body, tq=tq, dh=dh),
        grid=(batch, ng, nt),
        in_specs=[blk(cur(0)), blk(prv(1)), blk(cur(1)), blk(prv(2)), blk(cur(2)),
                  pl.BlockSpec((hp, PAIR, PAIR_BAND), lambda b, h, i: (h, 0, 0))],
        out_specs=pl.BlockSpec((tq, hp * dh), lambda b, h, i: (b * nt + i, h)),
        out_shape=jax.ShapeDtypeStruct((n, d), BF16),
        compiler_params=_cparams(3),
        name="band_attention",
    )(qkv, qkv, qkv, qkv, qkv, bias)


def _pad_cols(a, width):
    return jnp.pad(a, ((0, 0), (0, width - a.shape[1])))


def _pad_rows(a, rows):
    return jnp.pad(a, ((0, rows - a.shape[0]), (0, 0)))


def _row(a):
    return a.reshape(1, -1)


class _Tiles(NamedTuple):
    rows: int
    proj_rows: int
    mix_rows: int
    attn_rows: int
    attn_heads: int
    qkv_cols: int
    ffn_cols: int


def _tiles(n, seq):
    rows = min(1024, n)
    return _Tiles(rows=rows, proj_rows=max(rows // 2, 8), mix_rows=min(4 * CHUNK, seq),
                  attn_rows=min(LEFT_CHUNKS * CHUNK, seq), attn_heads=4, qkv_cols=1024, ffn_cols=512)


def kernel(x, norm_g, w_in_ab, rwkv_mu, rwkv_w0, rwkv_w2, rwkv_a0, rwkv_a2, rwkv_g2, rwkv_k_k, rwkv_k_a, rwkv_r_k, rwkv_ln_w, rwkv_ln_b, ssm_conv_w, ssm_conv_b, ssm_dt_bias, ssm_A_log, ssm_D, ssm_norm_w, w_out_ab, w_qkv, attn_rel_bias, w_out_c, ffn_w_gate, ffn_w_up, ffn_w_down):
    batch, seq, d = x.shape
    n = batch * seq
    depth = norm_g.shape[0]
    xf = x.reshape(n, d)

    rw = rwkv_w0.shape[1]
    lora_w, lora_a, lora_g = rwkv_w2.shape[1], rwkv_a2.shape[1], rwkv_g2.shape[1]
    sw = ssm_norm_w.shape[1]
    n_ssm_heads = ssm_A_log.shape[1]
    conv_ch = ssm_conv_w.shape[2]
    groups = 2
    state = (conv_ch - sw) // (2 * groups)
    n_att_heads = attn_rel_bias.shape[1]
    rwkv_proj = 3 * rw + lora_w + lora_a + lora_g
    t = _tiles(n, seq)

    for l in range(depth):
        i = l // 2
        g = norm_g[l]
        if l % 2 == 0:
            w_in = w_in_ab[i]
            o1, o2, o3 = 3 * rw, 3 * rw + lora_w, 3 * rw + lora_w + lora_a
            seg = lambda a: [a[:, 0:o1], _pad_cols(a[:, o1:o2], LANE), _pad_cols(a[:, o2:o3], LANE), a[:, o3:rwkv_proj]]
            w_r = jnp.concatenate(seg(w_in), axis=1).astype(BF16)
            mu = jnp.concatenate(seg(_row(rwkv_mu[i])), axis=1)
            o4 = rwkv_proj + sw + conv_ch
            w_s = jnp.concatenate([w_in[:, rwkv_proj:o4], _pad_cols(w_in[:, o4:], LANE)], axis=1).astype(BF16)

            y_a = _rwkv_mix(
                xf, g[0:1], w_r, mu, _row(rwkv_w0[i]), _pad_rows(rwkv_w2[i], LANE).astype(BF16), _row(rwkv_a0[i]),
                _pad_rows(rwkv_a2[i], LANE).astype(BF16), rwkv_g2[i].astype(BF16), _row(rwkv_k_k[i]),
                _row(rwkv_k_a[i]), _row(rwkv_r_k[i]), _row(rwkv_ln_w[i]), _row(rwkv_ln_b[i]), batch, seq, t.mix_rows)
            y_b = _ssd_mix(
                xf, g[0:1], w_s, ssm_conv_w[i], _row(ssm_conv_b[i]), _pad_cols(_row(ssm_dt_bias[i]), LANE),
                _pad_cols(_row(ssm_A_log[i]), LANE), _row(jnp.repeat(ssm_D[i], sw // n_ssm_heads)),
                _row(ssm_norm_w[i]), batch, seq, t.mix_rows, sw, state, groups)
            w_o = w_out_ab[i].astype(BF16)
            xf = _proj_res([y_a, y_b], [w_o[:rw], w_o[rw:]], xf, g[1:2], t.proj_rows)
        else:
            qkv = _norm_matmul(xf, g[0:1], w_qkv[i].astype(BF16), BF16, t.rows, t.qkv_cols, n_scaled=d // t.qkv_cols,
                               scale=(d // n_att_heads) ** -0.5 * LOG2E)
            bias = _bias_table(attn_rel_bias[i])
            att = _band_attention(qkv, bias, batch, seq, n_att_heads, t.attn_rows, t.attn_heads)
            xf = _proj_res([att], [w_out_c[i].astype(BF16)], xf, g[1:2], t.proj_rows)
        xf = _ffn(xf, g[2:3], g[3:4], ffn_w_gate[l].astype(BF16), ffn_w_up[l].astype(BF16),
                  ffn_w_down[l].astype(BF16), t.rows, t.ffn_cols)
    return xf.reshape(batch, seq, d)
```

```python
import functools
import math
from typing import NamedTuple

import jax
import jax.numpy as jnp
from jax import lax
from jax.experimental import pallas as pl
from jax.experimental.pallas import tpu as pltpu

F32 = jnp.float32
BF16 = jnp.bfloat16

NORM_EPS = 1e-6
RWKV_LN_EPS = 64e-5
CHUNK = 64
HEAD = 64
LANE = 128
LEFT_CHUNKS = 8
REL_PAST_CLIP = 256
LOG2E = math.log2(math.e)
VMEM_LIMIT = 60 * 1024 * 1024


def _cparams(n_axes):
    return pltpu.CompilerParams(dimension_semantics=("arbitrary",) * n_axes, vmem_limit_bytes=VMEM_LIMIT)


def _dot(a, b):
    return jnp.dot(a, b, preferred_element_type=F32)


def _dot_nt(a, b):
    return lax.dot_general(a, b, (((1,), (1,)), ((), ())), preferred_element_type=F32)


def _dot_tn(a, b):
    return lax.dot_general(a, b, (((0,), (0,)), ((), ())), preferred_element_type=F32)


def _split2(t):
    hi = t.astype(BF16)
    lo = (t - hi.astype(F32)).astype(BF16)
    return hi, lo


def _dot_sel_r(t, sel):
    hi, lo = _split2(t)
    return _dot(hi, sel) + _dot(lo, sel)


def _dot_sel_l(sel, t):
    hi, lo = _split2(t)
    return _dot(sel, hi) + _dot(sel, lo)


def _rms(x, g):
    ms = jnp.mean(x * x, axis=-1, keepdims=True)
    return x * lax.rsqrt(ms + NORM_EPS) * g


def _sigmoid(x):
    return 1.0 / (1.0 + jnp.exp(-x))


def _silu(x):
    return x * _sigmoid(x)


def _softplus(x):
    return jnp.maximum(x, 0.0) + jnp.log1p(jnp.exp(-jnp.abs(x)))


def _iota(shape, axis):
    return lax.broadcasted_iota(jnp.int32, shape, axis)


def _norm_matmul_body(x_ref, g_ref, w_ref, o_ref, h_ref, *, n_scaled, scale):
    j = pl.program_id(1)
    mult = jnp.where(j < n_scaled, jnp.float32(scale), jnp.float32(1.0))

    @pl.when(j == 0)
    def _():
        for r in range(0, x_ref.shape[0], EDGE_ROWS):
            rs = slice(r, r + EDGE_ROWS)
            h = _rms(x_ref[rs, :], g_ref[...]).astype(BF16)
            h_ref[rs, :] = h
            o_ref[rs, :] = (_dot(h, w_ref[...]) * mult).astype(o_ref.dtype)

    @pl.when(j > 0)
    def _():
        o_ref[...] = (_dot(h_ref[...], w_ref[...]) * mult).astype(o_ref.dtype)


def _norm_matmul(x, g, w, out_dtype, tm, tn, n_scaled=0, scale=1.0):
    n, d = x.shape
    m = w.shape[1]
    return pl.pallas_call(
        functools.partial(_norm_matmul_body, n_scaled=n_scaled, scale=scale),
        grid=(n // tm, m // tn),
        in_specs=[
            pl.BlockSpec((tm, d), lambda i, j: (i, 0)),
            pl.BlockSpec((1, d), lambda i, j: (0, 0)),
            pl.BlockSpec((d, tn), lambda i, j: (0, j)),
        ],
        out_specs=pl.BlockSpec((tm, tn), lambda i, j: (i, j)),
        out_shape=jax.ShapeDtypeStruct((n, m), out_dtype),
        scratch_shapes=[pltpu.VMEM((tm, d), BF16)],
        compiler_params=_cparams(2),
        name="norm_matmul",
    )(x, g, w)


def _proj_res_body(*refs, n_in):
    y_refs, w_refs = refs[:n_in], refs[n_in:2 * n_in]
    x_ref, g_ref, o_ref = refs[2 * n_in:]
    m = _dot(y_refs[0][...], w_refs[0][...])
    for y_ref, w_ref in zip(y_refs[1:], w_refs[1:]):
        m = m + _dot(y_ref[...], w_ref[...])
    o_ref[...] = x_ref[...] + _rms(m, g_ref[...])


def _proj_res(ys, ws, x, g, tm):
    n, d = x.shape
    n_in = len(ys)
    in_specs = [pl.BlockSpec((tm, y.shape[1]), lambda i: (i, 0)) for y in ys]
    in_specs += [pl.BlockSpec(w.shape, lambda i: (0, 0), pipeline_mode=pl.Buffered(1)) for w in ws]
    in_specs += [pl.BlockSpec((tm, d), lambda i: (i, 0)), pl.BlockSpec((1, d), lambda i: (0, 0))]
    return pl.pallas_call(
        functools.partial(_proj_res_body, n_in=n_in),
        grid=(n // tm,),
        in_specs=in_specs,
        out_specs=pl.BlockSpec((tm, d), lambda i: (i, 0)),
        out_shape=jax.ShapeDtypeStruct((n, d), F32),
        compiler_params=_cparams(1),
        name="proj_res",
    )(*ys, *ws, x, g)


def _ffn_body(x_ref, gi_ref, go_ref, wg_ref, wu_ref, wd_ref, o_ref, h_ref):
    j = pl.program_id(1)
    last_j = pl.num_programs(1) - 1
    tm = x_ref.shape[0]

    def partial_out(h):
        a = (_silu(_dot(h, wg_ref[...])) * _dot(h, wu_ref[...])).astype(BF16)
        return _dot(a, wd_ref[...])

    row_groups = [slice(r, r + EDGE_ROWS) for r in range(0, tm, EDGE_ROWS)]

    @pl.when(j == 0)
    def _():
        for rs in row_groups:
            h = _rms(x_ref[rs, :], gi_ref[...]).astype(BF16)
            h_ref[rs, :] = h
            o_ref[rs, :] = partial_out(h)

    @pl.when((j > 0) & (j < last_j))
    def _():
        o_ref[...] += partial_out(h_ref[...])

    @pl.when(j == last_j)
    def _():
        for rs in row_groups:
            acc = o_ref[rs, :] + partial_out(h_ref[rs, :])
            o_ref[rs, :] = x_ref[rs, :] + _rms(acc, go_ref[...])


def _ffn(x, g_in, g_out, wg, wu, wd, tm, tf):
    n, d = x.shape
    f = wg.shape[1]
    assert f // tf >= 2 and tm % EDGE_ROWS == 0
    return pl.pallas_call(
        _ffn_body,
        grid=(n // tm, f // tf),
        in_specs=[
            pl.BlockSpec((tm, d), lambda i, j: (i, 0), pipeline_mode=pl.Buffered(1)),
            pl.BlockSpec((1, d), lambda i, j: (0, 0)),
            pl.BlockSpec((1, d), lambda i, j: (0, 0)),
            pl.BlockSpec((d, tf), lambda i, j: (0, j)),
            pl.BlockSpec((d, tf), lambda i, j: (0, j)),
            pl.BlockSpec((tf, d), lambda i, j: (j, 0)),
        ],
        out_specs=pl.BlockSpec((tm, d), lambda i, j: (i, 0)),
        out_shape=jax.ShapeDtypeStruct((n, d), F32),
        scratch_shapes=[pltpu.VMEM((tm, d), BF16)],
        compiler_params=_cparams(2),
        name="ffn",
    )(x, g_in, g_out, wg, wu, wd)


def _seg_matrices(width, n_seg_pad):
    e = (_iota((width, n_seg_pad), 0) // HEAD == _iota((width, n_seg_pad), 1)).astype(BF16)
    et = (_iota((n_seg_pad, width), 1) // HEAD == _iota((n_seg_pad, width), 0)).astype(BF16)
    return e, et


def _blockdiag_tril(tc):
    r, c = _iota((tc, tc), 0), _iota((tc, tc), 1)
    return ((r // CHUNK == c // CHUNK) & (c <= r)).astype(BF16)


def _rwkv_body(x_ref, gn_ref, w_ref, mu_ref, w0_ref, w2_ref, a0_ref, a2_ref, g2_ref, kk_ref, ka_ref, rk_ref,
               lnw_ref, lnb_ref, o_ref, p_ref, prow_ref, s_ref, at_ref, rt_ref, bt_ref, kt_ref, v_ref, gl_ref, y_ref,
               *, tc, width):
    step = pl.program_id(1)

    tw = p_ref.shape[1] // N_PROJ_TILES
    assert tw % MXU_COLS == 0

    def h_next_fn():
        return _rms(x_ref[...], gn_ref[...]).astype(BF16)

    def project_tile(h, k):
        p_ref[:, k * tw:(k + 1) * tw] = _dot(h, w_ref[:, k * tw:(k + 1) * tw])

    @pl.when(step == 0)
    def _():
        s_ref[...] = jnp.zeros_like(s_ref)
        prow_ref[...] = jnp.zeros_like(prow_ref)
        h = h_next_fn()
        for k in range(N_PROJ_TILES):
            project_tile(h, k)

    @pl.when(step > 0)
    def _():
        _rwkv_mix_block(h_next_fn, project_tile, mu_ref, w0_ref, w2_ref, a0_ref, a2_ref, g2_ref, kk_ref, ka_ref, rk_ref, lnw_ref,
                        lnb_ref, o_ref, p_ref, prow_ref, s_ref, at_ref, rt_ref, bt_ref, kt_ref, v_ref, gl_ref, y_ref,
                        tc=tc, width=width)


def _rwkv_mix_block(h_next_fn, project_tile, mu_ref, w0_ref, w2_ref, a0_ref, a2_ref, g2_ref, kk_ref, ka_ref, rk_ref, lnw_ref,
                    lnb_ref, o_ref, p_ref, prow_ref, s_ref, at_ref, rt_ref, bt_ref, kt_ref, v_ref, gl_ref, y_ref,
                    *, tc, width):
    n_heads = width // HEAD
    n_chunks = tc // CHUNK

    p = p_ref[...]
    first_row = _iota((tc, 1), 0) == 0

    def lerp(cols):
        pc = p[:, cols]
        psh = jnp.where(first_row, prow_ref[0:1, cols], pltpu.roll(pc, 1, 0))
        return pc + (psh - pc) * mu_ref[:, cols]

    o1 = 3 * width
    xl = lerp(slice(o1, o1 + 4 * LANE))
    wc, ac, gc = xl[:, 0:LANE], xl[:, LANE:2 * LANE], xl[:, 2 * LANE:4 * LANE]
    z = w0_ref[...] + _dot(jnp.tanh(wc).astype(BF16), w2_ref[...])
    a = _sigmoid(a0_ref[...] + _dot(ac.astype(BF16), a2_ref[...]))
    g = _dot(_sigmoid(gc).astype(BF16), g2_ref[...])
    lw = -jnp.exp(-_softplus(-z) - 0.5)

    h_next = h_next_fn()
    project_tile(h_next, 0)
    project_tile(h_next, 1)
    project_tile(h_next, 2)

    xr = lerp(slice(0, o1))
    last_row = p[tc - 1:tc, :]
    r = xr[:, 0:width]
    k = xr[:, width:2 * width]
    v = xr[:, 2 * width:3 * width]

    mxu_w = 2 * LANE
    same_head = (_iota((mxu_w, mxu_w), 0) // HEAD == _iota((mxu_w, mxu_w), 1) // HEAD).astype(BF16)

    def seg_sum(t):
        return jnp.concatenate(
            [_dot_sel_r(t[:, j:j + mxu_w], same_head) for j in range(0, width, mxu_w)], axis=1)

    kk = k * kk_ref[...]
    kk = kk * lax.rsqrt(jnp.maximum(seg_sum(kk * kk), 1e-24))
    k2 = k * (1.0 + (a - 1.0) * ka_ref[...])
    project_tile(h_next, 3)
    project_tile(h_next, 4)
    bonus = seg_sum(r * k2 * rk_ref[...]) * v
    cs = _dot_sel_l(_blockdiag_tril(tc), lw)
    project_tile(h_next, 5)
    project_tile(h_next, 6)
    prow_ref[0:1, :] = last_row

    ginv = jnp.exp(-cs)
    at_ref[...] = (-kk * jnp.exp(cs - lw)).astype(BF16)
    rt_ref[...] = (r * jnp.exp(cs)).astype(BF16)
    bt_ref[...] = (kk * a * ginv).astype(BF16)
    kt_ref[...] = (k2 * ginv).astype(BF16)
    v_ref[...] = v.astype(BF16)
    for c in range(n_chunks):
        last = c * CHUNK + CHUNK - 1
        gl_ref[c:c + 1, :] = jnp.exp(cs[last:last + 1, :])

    rr, cc = _iota((CHUNK, CHUNK), 0), _iota((CHUNK, CHUNK), 1)
    rr2, cc2 = _iota((CHUNK, 2 * CHUNK), 0), _iota((CHUNK, 2 * CHUNK), 1) % CHUNK
    strict2 = (cc2 < rr2).astype(F32)
    incl2 = (cc2 <= rr2).astype(F32)
    eye = (cc == rr).astype(F32)
    n_lvl = int(math.log2(CHUNK))
    lvl_masks = []
    for lv in range(n_lvl):
        same = (rr >> (lv + 1)) == (cc >> (lv + 1))
        lvl_masks.append((same & (((rr >> lv) & 1) == 1) & (((cc >> lv) & 1) == 0)).astype(F32))

    def chunk_step(c, carry):
        t0 = pl.multiple_of(c * CHUNK, CHUNK)
        rows = pl.ds(t0, CHUNK)
        gl = gl_ref[pl.ds(c, 1), :]
        heads = range(n_heads)
        hsl = [slice(h * HEAD, (h + 1) * HEAD) for h in heads]
        vh = [v_ref[rows, hs] for hs in hsl]
        ar = [jnp.concatenate([at_ref[rows, hs], rt_ref[rows, hs]], axis=0) for hs in hsl]
        bk = [jnp.concatenate([bt_ref[rows, hs], kt_ref[rows, hs]], axis=0) for hs in hsl]
        pm = [_dot_nt(ar[h], bk[h]) for h in heads]
        s0 = [s_ref[h] for h in heads]
        ars = [_dot_nt(ar[h], s0[h].astype(BF16)) for h in heads]
        pa = [pm[h][:CHUNK, :] * strict2 for h in heads]
        aab = [pa[h][:, :CHUNK] for h in heads]
        rhs = [ars[h][:CHUNK, :] + _dot(pa[h][:, CHUNK:].astype(BF16), vh[h]) for h in heads]
        x = [eye + aab[h] * lvl_masks[0] for h in heads]
        for lv in range(1, n_lvl):
            xb = [x[h].astype(BF16) for h in heads]
            tm_ = [_dot((aab[h] * lvl_masks[lv]).astype(BF16), xb[h]) for h in heads]
            x = [x[h] + _dot(xb[h], tm_[h].astype(BF16)) for h in heads]
        u = [_dot(x[h].astype(BF16), rhs[h].astype(BF16)) for h in heads]
        uv = [jnp.concatenate([u[h].astype(BF16), vh[h]], axis=0) for h in heads]
        for h in heads:
            pr = (pm[h][CHUNK:, :] * incl2).astype(BF16)
            y_ref[rows, hsl[h]] = ars[h][CHUNK:, :] + _dot(pr, uv[h])
        for h in heads:
            s_ref[h] = (s0[h] + _dot_tn(uv[h], bk[h])) * gl[:, hsl[h]]
        return carry

    lax.fori_loop(0, n_chunks, chunk_step, 0)

    y = y_ref[...]
    inv_n = 1.0 / HEAD
    mean = seg_sum(y) * inv_n
    yc = y - mean
    var = seg_sum(yc * yc) * inv_n
    yn = yc * lax.rsqrt(var + RWKV_LN_EPS) * lnw_ref[...] + lnb_ref[...]
    o_ref[...] = ((yn + bonus) * g).astype(o_ref.dtype)


def _mixer_specs(x, gn, w, params, batch, seq, tc, width):
    nt = seq // tc
    d = x.shape[1]
    const = lambda b, s: (0, 0)
    in_specs = [
        pl.BlockSpec((tc, d), lambda b, s: (b * nt + jnp.minimum(s, nt - 1), 0)),
        pl.BlockSpec(gn.shape, const),
        pl.BlockSpec(w.shape, const, pipeline_mode=pl.Buffered(1)),
    ] + [pl.BlockSpec(a.shape, const) for a in params]
    out_spec = pl.BlockSpec((tc, width), lambda b, s: (b * nt + jnp.maximum(s - 1, 0), 0))
    return (batch, nt + 1), in_specs, out_spec


def _rwkv_mix(x, gn, w, mu, w0, w2, a0, a2, g2, k_k, k_a, r_k, ln_w, ln_b, batch, seq, tc):
    n = x.shape[0]
    pw = w.shape[1]
    width = w0.shape[1]
    n_heads = width // HEAD
    params = (mu, w0, w2, a0, a2, g2, k_k, k_a, r_k, ln_w, ln_b)
    grid, in_specs, out_spec = _mixer_specs(x, gn, w, params, batch, seq, tc, width)
    return pl.pallas_call(
        functools.partial(_rwkv_body, tc=tc, width=width),
        grid=grid,
        in_specs=in_specs,
        out_specs=out_spec,
        out_shape=jax.ShapeDtypeStruct((n, width), BF16),
        scratch_shapes=[pltpu.VMEM((tc, pw), F32), pltpu.VMEM((8, pw), F32), pltpu.VMEM((n_heads, HEAD, HEAD), F32)]
        + [pltpu.VMEM((tc, width), BF16) for _ in range(5)]
        + [pltpu.VMEM((max(tc // CHUNK, 8), width), F32), pltpu.VMEM((tc, width), F32)],
        compiler_params=_cparams(2),
        name="rwkv_mix",
    )(x, gn, w, *params)


def _ssd_body(x_ref, gn_ref, w_ref, cw_ref, cb_ref, dtb_ref, alog_ref, dskip_ref, nw_ref, o_ref,
              p_ref, tail_ref, st_ref, y_ref, *, tc, width, state, groups):
    step = pl.program_id(1)
    n_chunks = tc // CHUNK
    pw = p_ref.shape[1]
    tw = 2 * MXU_COLS
    tiles = [slice(c0, min(c0 + tw, pw)) for c0 in range(0, pw, tw)]
    n_tiles = len(tiles)

    def project_tile(h, k):
        p_ref[:, tiles[k]] = _dot(h, w_ref[:, tiles[k]])

    @pl.when(step == 0)
    def _():
        st_ref[...] = jnp.zeros_like(st_ref)
        tail_ref[...] = jnp.zeros_like(tail_ref)
        h = _rms(x_ref[...], gn_ref[...]).astype(BF16)
        for k in range(n_tiles):
            project_tile(h, k)

    @pl.when(step > 0)
    def _():
        _ssd_mix_block(x_ref, gn_ref, project_tile, n_tiles, cw_ref, cb_ref, dtb_ref, alog_ref, dskip_ref, nw_ref,
                       o_ref, p_ref, tail_ref, st_ref, y_ref, tc=tc, width=width, state=state, groups=groups)


def _ssd_mix_block(x_ref, gn_ref, project_tile, n_tiles, cw_ref, cb_ref, dtb_ref, alog_ref, dskip_ref, nw_ref,
                   o_ref, p_ref, tail_ref, st_ref, y_ref, *, tc, width, state, groups):
    n_chunks = tc // CHUNK
    gw = width // groups
    hpg = gw // HEAD
    conv_ch = width + 2 * groups * state
    n_tap = cw_ref.shape[0]

    p = p_ref[...]
    zs = _silu(p[:, 0:width])
    xbc_raw = p[:, width:width + conv_ch]
    dt_raw = p[:, width + conv_ch:width + conv_ch + LANE]
    h_next = _rms(x_ref[...], gn_ref[...]).astype(BF16)
    tiles_after_chunk = {c: [k for k in range(1, n_tiles) if (k - 1) % n_chunks == c] for c in range(n_chunks)}

    prev = tail_ref[...]
    tail_ref[...] = xbc_raw[tc - 8:tc, :]
    xcat = jnp.concatenate([prev, xbc_raw], axis=0)
    conv = cb_ref[...]
    for j in range(n_tap):
        off = 8 - (n_tap - 1) + j
        conv = conv + cw_ref[j:j + 1, :] * xcat[off:off + tc, :]
    xbc = _silu(conv)
    xs = xbc[:, 0:width]
    bm = xbc[:, width:width + groups * state].astype(BF16)
    cm = xbc[:, width + groups * state:conv_ch].astype(BF16)
    project_tile(h_next, 0)

    _, et = _seg_matrices(width, LANE)
    dt = _softplus(dt_raw + dtb_ref[...])
    da = -jnp.exp(alog_ref[...]) * dt
    cs_h = _dot_sel_l(_blockdiag_tril(tc), da)
    cs = _dot_sel_r(cs_h, et)
    dtx = _dot_sel_r(dt, et)
    xc = xs * dtx
    ecs = jnp.exp(cs)

    rr, cc = _iota((CHUNK, CHUNK), 0), _iota((CHUNK, CHUNK), 1)
    causal = cc <= rr

    for c in range(n_chunks):
        r0 = c * CHUNK
        rows = slice(r0, r0 + CHUNK)
        cs_c = cs[rows, :]
        cs_last = cs_c[CHUNK - 1:CHUNK, :]
        xc_c = xc[rows, :]
        xd = (xc_c * jnp.exp(cs_last - cs_c)).astype(BF16)
        xcb = xc_c.astype(BF16)
        cs_t = cs_h[rows, :].T
        for g in range(groups):
            gl = slice(g * gw, (g + 1) * gw)
            b_g = bm[rows, g * state:(g + 1) * state]
            c_g = cm[rows, g * state:(g + 1) * state]
            cb = _dot_nt(c_g, b_g)
            st = st_ref[g]
            y_off = _dot(c_g, st.astype(BF16)) * ecs[rows, gl]
            for hh in range(hpg):
                h = g * hpg + hh
                hs = slice(h * HEAD, (h + 1) * HEAD)
                seg = cs_c[:, hs] - cs_t[h:h + 1, :]
                lmat = jnp.where(causal, jnp.exp(jnp.where(causal, seg, 0.0)), 0.0)
                y_ref[rows, hs] = _dot((cb * lmat).astype(BF16), xcb[:, hs]) + y_off[:, hh * HEAD:(hh + 1) * HEAD]
            st_ref[g] = st * ecs[r0 + CHUNK - 1:r0 + CHUNK, gl] + _dot_tn(b_g, xd[:, gl])
        for k in tiles_after_chunk[c]:
            project_tile(h_next, k)

    y = (y_ref[...] + dskip_ref[...] * xs) * zs
    for g in range(groups):
        gl = slice(g * gw, (g + 1) * gw)
        yg = y[:, gl]
        ms = jnp.mean(yg * yg, axis=-1, keepdims=True)
        o_ref[:, gl] = (yg * lax.rsqrt(ms + NORM_EPS) * nw_ref[:, gl]).astype(o_ref.dtype)


def _ssd_mix(x, gn, w, conv_w, conv_b, dt_bias, a_log, d_skip, norm_w, batch, seq, tc, width, state, groups):
    n = x.shape[0]
    pw = w.shape[1]
    conv_ch = width + 2 * groups * state
    params = (conv_w, conv_b, dt_bias, a_log, d_skip, norm_w)
    grid, in_specs, out_spec = _mixer_specs(x, gn, w, params, batch, seq, tc, width)
    return pl.pallas_call(
        functools.partial(_ssd_body, tc=tc, width=width, state=state, groups=groups),
        grid=grid,
        in_specs=in_specs,
        out_specs=out_spec,
        out_shape=jax.ShapeDtypeStruct((n, width), BF16),
        scratch_shapes=[pltpu.VMEM((tc, pw), F32), pltpu.VMEM((8, conv_ch), F32),
                        pltpu.VMEM((groups, state, width // groups), F32), pltpu.VMEM((tc, width), F32)],
        compiler_params=_cparams(2),
        name="ssd_mix",
    )(x, gn, w, *params)


EDGE_ROWS = 256
N_PROJ_TILES = 7
MXU_COLS = 256
PAIR = 2 * CHUNK
BAND = (LEFT_CHUNKS + 1) * CHUNK
PAIR_BAND = BAND + CHUNK


def _bias_table_body(rb_ref, o_ref, *, n_bucket, rel_future):
    ext_w = PAIR_BAND + LANE
    m = _iota((n_bucket, ext_w), 1)
    bucket = jnp.clip(PAIR_BAND - 1 - m, -rel_future, REL_PAST_CLIP) + rel_future
    sel = (bucket == _iota((n_bucket, ext_w), 0)).astype(BF16)
    ext = _dot_sel_r(rb_ref[...], sel) * LOG2E
    for r in range(PAIR):
        o_ref[r] = ext[:, PAIR - 1 - r:PAIR - 1 - r + PAIR_BAND]


def _bias_table(rel_bias):
    n_heads, n_bucket = rel_bias.shape
    out = pl.pallas_call(
        functools.partial(_bias_table_body, n_bucket=n_bucket, rel_future=CHUNK - 1),
        out_shape=jax.ShapeDtypeStruct((PAIR, n_heads, PAIR_BAND), F32),
        name="bias_table",
    )(rel_bias)
    return jnp.transpose(out, (1, 0, 2))


def _attn_body(q_ref, kp_ref, kc_ref, vp_ref, vc_ref, b_ref, o_ref, *, tq, dh):
    i = pl.program_id(2)
    n_heads = q_ref.shape[1] // dh
    row, col = _iota((PAIR, PAIR_BAND), 0), _iota((PAIR, PAIR_BAND), 1)
    lo = (row // CHUNK) * CHUNK
    in_band = (col >= lo) & (col < lo + BAND)
    n_pairs = tq // PAIR
    k0 = [tq - LEFT_CHUNKS * CHUNK + jp * PAIR for jp in range(n_pairs)]
    work = [(h, jp) for h in range(n_heads) for jp in range(n_pairs)]

    def attend(first_block):
        hcols = [slice(h * dh, (h + 1) * dh) for h in range(n_heads)]
        kcat = [jnp.concatenate([kp_ref[:, hc], kc_ref[:, hc]], axis=0) for hc in hcols]
        vcat = [jnp.concatenate([vp_ref[:, hc], vc_ref[:, hc]], axis=0) for hc in hcols]
        s = [_dot_nt(q_ref[jp * PAIR:(jp + 1) * PAIR, hcols[h]], kcat[h][k0[jp]:k0[jp] + PAIR_BAND, :]) + b_ref[h]
             for h, jp in work]
        if first_block:
            valid = [in_band & (col + k0[jp] >= tq) for jp in range(n_pairs)]
            s = [jnp.where(valid[jp], s_, -1e30) for (h, jp), s_ in zip(work, s)]
        else:
            edge = PAIR_BAND - LANE
            s = [jnp.concatenate([jnp.where(in_band[:, :LANE], s_[:, :LANE], -1e30), s_[:, LANE:edge],
                                  jnp.where(in_band[:, edge:], s_[:, edge:], -1e30)], axis=1) for s_ in s]
        mx = [jnp.max(s_, axis=-1, keepdims=True) for s_ in s]
        pexp = [jnp.exp2(s_ - m_) for s_, m_ in zip(s, mx)]
        den = [jnp.sum(p_, axis=-1, keepdims=True) for p_ in pexp]
        o = [_dot(p_.astype(BF16), vcat[h][k0[jp]:k0[jp] + PAIR_BAND, :]) for (h, jp), p_ in zip(work, pexp)]
        for (h, jp), o_, d_ in zip(work, o, den):
            o_ref[jp * PAIR:(jp + 1) * PAIR, hcols[h]] = (o_ / d_).astype(o_ref.dtype)

    @pl.when(i == 0)
    def _():
        attend(True)

    @pl.when(i > 0)
    def _():
        attend(False)


def _band_attention(qkv, bias, batch, seq, n_heads, tq, hp):
    n, three_d = qkv.shape
    d = three_d // 3
    dh = d // n_heads
    nt = seq // tq
    ng = n_heads // hp
    cur = lambda off: (lambda b, h, i: (b * nt + i, off * ng + h))
    prv = lambda off: (lambda b, h, i: (b * nt + jnp.maximum(i - 1, 0), off * ng + h))
    blk = lambda f: pl.BlockSpec((tq, hp * dh), f)
    return pl.pallas_call(
        functools.partial(_attn_body, tq=tq, dh=dh),
        grid=(batch, ng, nt),
        in_specs=[blk(cur(0)), blk(prv(1)), blk(cur(1)), blk(prv(2)), blk(cur(2)),
                  pl.BlockSpec((hp, PAIR, PAIR_BAND), lambda b, h, i: (h, 0, 0))],
        out_specs=pl.BlockSpec((tq, hp * dh), lambda b, h, i: (b * nt + i, h)),
        out_shape=jax.ShapeDtypeStruct((n, d), BF16),
        compiler_params=_cparams(3),
        name="band_attention",
    )(qkv, qkv, qkv, qkv, qkv, bias)


def _pad_cols(a, width):
    return jnp.pad(a, ((0, 0), (0, width - a.shape[1])))


def _pad_rows(a, rows):
    return jnp.pad(a, ((0, rows - a.shape[0]), (0, 0)))


def _row(a):
    return a.reshape(1, -1)


class _Tiles(NamedTuple):
    rows: int
    proj_rows: int
    mix_rows: int
    attn_rows: int
    attn_heads: int
    qkv_cols: int
    ffn_cols: int


def _tiles(n, seq):
    rows = min(1024, n)
    return _Tiles(rows=rows, proj_rows=rows, mix_rows=min(4 * CHUNK, seq),
                  attn_rows=min(LEFT_CHUNKS * CHUNK, seq), attn_heads=8, qkv_cols=2048, ffn_cols=512)


def kernel(x, norm_g, w_in_ab, rwkv_mu, rwkv_w0, rwkv_w2, rwkv_a0, rwkv_a2, rwkv_g2, rwkv_k_k, rwkv_k_a, rwkv_r_k, rwkv_ln_w, rwkv_ln_b, ssm_conv_w, ssm_conv_b, ssm_dt_bias, ssm_A_log, ssm_D, ssm_norm_w, w_out_ab, w_qkv, attn_rel_bias, w_out_c, ffn_w_gate, ffn_w_up, ffn_w_down):
    batch, seq, d = x.shape
    n = batch * seq
    depth = norm_g.shape[0]
    xf = x.reshape(n, d)

    rw = rwkv_w0.shape[1]
    lora_w, lora_a, lora_g = rwkv_w2.shape[1], rwkv_a2.shape[1], rwkv_g2.shape[1]
    sw = ssm_norm_w.shape[1]
    n_ssm_heads = ssm_A_log.shape[1]
    conv_ch = ssm_conv_w.shape[2]
    groups = 2
    state = (conv_ch - sw) // (2 * groups)
    n_att_heads = attn_rel_bias.shape[1]
    rwkv_proj = 3 * rw + lora_w + lora_a + lora_g
    t = _tiles(n, seq)

    for l in range(depth):
        i = l // 2
        g = norm_g[l]
        if l % 2 == 0:
            w_in = w_in_ab[i]
            o1, o2, o3 = 3 * rw, 3 * rw + lora_w, 3 * rw + lora_w + lora_a
            seg = lambda a: [a[:, 0:o1], _pad_cols(a[:, o1:o2], LANE), _pad_cols(a[:, o2:o3], LANE), a[:, o3:rwkv_proj]]
            w_r = jnp.concatenate(seg(w_in), axis=1).astype(BF16)
            mu = jnp.concatenate(seg(_row(rwkv_mu[i])), axis=1)
            o4 = rwkv_proj + sw + conv_ch
            w_s = jnp.concatenate([w_in[:, rwkv_proj:o4], _pad_cols(w_in[:, o4:], LANE)], axis=1).astype(BF16)

            y_a = _rwkv_mix(
                xf, g[0:1], w_r, mu, _row(rwkv_w0[i]), _pad_rows(rwkv_w2[i], LANE).astype(BF16), _row(rwkv_a0[i]),
                _pad_rows(rwkv_a2[i], LANE).astype(BF16), rwkv_g2[i].astype(BF16), _row(rwkv_k_k[i]),
                _row(rwkv_k_a[i]), _row(rwkv_r_k[i]), _row(rwkv_ln_w[i]), _row(rwkv_ln_b[i]), batch, seq, t.mix_rows)
            y_b = _ssd_mix(
                xf, g[0:1], w_s, ssm_conv_w[i], _row(ssm_conv_b[i]), _pad_cols(_row(ssm_dt_bias[i]), LANE),
                _pad_cols(_row(ssm_A_log[i]), LANE), _row(jnp.repeat(ssm_D[i], sw // n_ssm_heads)),
                _row(ssm_norm_w[i]), batch, seq, t.mix_rows, sw, state, groups)
            w_o = w_out_ab[i].astype(BF16)
            xf = _proj_res([y_a, y_b], [w_o[:rw], w_o[rw:]], xf, g[1:2], t.proj_rows)
        else:
            qkv = _norm_matmul(xf, g[0:1], w_qkv[i].astype(BF16), BF16, t.rows, t.qkv_cols, n_scaled=d // t.qkv_cols,
                               scale=(d // n_att_heads) ** -0.5 * LOG2E)
            bias = _bias_table(attn_rel_bias[i])
            att = _band_attention(qkv, bias, batch, seq, n_att_heads, t.attn_rows, t.attn_heads)
            xf = _proj_res([att], [w_out_c[i].astype(BF16)], xf, g[1:2], t.proj_rows)
        xf = _ffn(xf, g[2:3], g[3:4], ffn_w_gate[l].astype(BF16), ffn_w_up[l].astype(BF16),
                  ffn_w_down[l].astype(BF16), t.rows, t.ffn_cols)
    return xf.reshape(batch, seq, d)
```

```python
import functools
import math
from typing import NamedTuple

import jax
import jax.numpy as jnp
from jax import lax
from jax.experimental import pallas as pl
from jax.experimental.pallas import tpu as pltpu

F32 = jnp.float32
BF16 = jnp.bfloat16

NORM_EPS = 1e-6
RWKV_LN_EPS = 64e-5
CHUNK = 64
HEAD = 64
LANE = 128
SUBLANE = 8
MXU_COLS = 256
EDGE_ROWS = 256
N_PROJ_TILES = 7
LEFT_CHUNKS = 8
REL_PAST_CLIP = 256
LOG2E = math.log2(math.e)
VMEM_LIMIT = 60 * 1024 * 1024


def _cparams(n_axes):
    return pltpu.CompilerParams(dimension_semantics=("arbitrary",) * n_axes, vmem_limit_bytes=VMEM_LIMIT)


def _dot(a, b):
    return jnp.dot(a, b, preferred_element_type=F32)


def _dot_nt(a, b):
    return lax.dot_general(a, b, (((1,), (1,)), ((), ())), preferred_element_type=F32)


def _dot_tn(a, b):
    return lax.dot_general(a, b, (((0,), (0,)), ((), ())), preferred_element_type=F32)


def _split2(t):
    hi = t.astype(BF16)
    lo = (t - hi.astype(F32)).astype(BF16)
    return hi, lo


def _dot_sel_r(t, sel):
    hi, lo = _split2(t)
    return _dot(hi, sel) + _dot(lo, sel)


def _dot_sel_l(sel, t):
    hi, lo = _split2(t)
    return _dot(sel, hi) + _dot(sel, lo)


def _rms(x, g):
    ms = jnp.mean(x * x, axis=-1, keepdims=True)
    return x * lax.rsqrt(ms + NORM_EPS) * g


def _sigmoid(x):
    return 1.0 / (1.0 + jnp.exp(-x))


def _silu(x):
    return x * _sigmoid(x)


def _softplus(x):
    return jnp.maximum(x, 0.0) + jnp.log1p(jnp.exp(-jnp.abs(x)))


def _iota(shape, axis):
    return lax.broadcasted_iota(jnp.int32, shape, axis)


def _norm_matmul_body(x_ref, g_ref, w_ref, o_ref, h_ref, *, n_scaled, scale):
    j = pl.program_id(1)
    mult = jnp.where(j < n_scaled, jnp.float32(scale), jnp.float32(1.0))

    @pl.when(j == 0)
    def _():
        for r in range(0, x_ref.shape[0], EDGE_ROWS):
            rs = slice(r, r + EDGE_ROWS)
            h = _rms(x_ref[rs, :], g_ref[...]).astype(BF16)
            h_ref[rs, :] = h
            o_ref[rs, :] = (_dot(h, w_ref[...]) * mult).astype(o_ref.dtype)

    @pl.when(j > 0)
    def _():
        o_ref[...] = (_dot(h_ref[...], w_ref[...]) * mult).astype(o_ref.dtype)


def _norm_matmul(x, g, w, out_dtype, tm, tn, n_scaled=0, scale=1.0):
    n, d = x.shape
    m = w.shape[1]
    return pl.pallas_call(
        functools.partial(_norm_matmul_body, n_scaled=n_scaled, scale=scale),
        grid=(n // tm, m // tn),
        in_specs=[
            pl.BlockSpec((tm, d), lambda i, j: (i, 0)),
            pl.BlockSpec((1, d), lambda i, j: (0, 0)),
            pl.BlockSpec((d, tn), lambda i, j: (0, j)),
        ],
        out_specs=pl.BlockSpec((tm, tn), lambda i, j: (i, j)),
        out_shape=jax.ShapeDtypeStruct((n, m), out_dtype),
        scratch_shapes=[pltpu.VMEM((tm, d), BF16)],
        compiler_params=_cparams(2),
        name="norm_matmul",
    )(x, g, w)


def _proj_res_body(*refs, n_in):
    y_refs, w_refs = refs[:n_in], refs[n_in:2 * n_in]
    x_ref, g_ref, o_ref = refs[2 * n_in:]
    m = _dot(y_refs[0][...], w_refs[0][...])
    for y_ref, w_ref in zip(y_refs[1:], w_refs[1:]):
        m = m + _dot(y_ref[...], w_ref[...])
    o_ref[...] = x_ref[...] + _rms(m, g_ref[...])


def _proj_res(ys, ws, x, g, tm):
    n, d = x.shape
    n_in = len(ys)
    in_specs = [pl.BlockSpec((tm, y.shape[1]), lambda i: (i, 0)) for y in ys]
    in_specs += [pl.BlockSpec(w.shape, lambda i: (0, 0), pipeline_mode=pl.Buffered(1)) for w in ws]
    in_specs += [pl.BlockSpec((tm, d), lambda i: (i, 0)), pl.BlockSpec((1, d), lambda i: (0, 0))]
    return pl.pallas_call(
        functools.partial(_proj_res_body, n_in=n_in),
        grid=(n // tm,),
        in_specs=in_specs,
        out_specs=pl.BlockSpec((tm, d), lambda i: (i, 0)),
        out_shape=jax.ShapeDtypeStruct((n, d), F32),
        compiler_params=_cparams(1),
        name="proj_res",
    )(*ys, *ws, x, g)


def _ffn_body(x_ref, gi_ref, go_ref, wg_ref, wu_ref, wd_ref, o_ref, h_ref):
    j = pl.program_id(1)
    last_j = pl.num_programs(1) - 1
    tm = x_ref.shape[0]

    def partial_out(h):
        a = (_silu(_dot(h, wg_ref[...])) * _dot(h, wu_ref[...])).astype(BF16)
        return _dot(a, wd_ref[...])

    row_groups = [slice(r, r + EDGE_ROWS) for r in range(0, tm, EDGE_ROWS)]

    @pl.when(j == 0)
    def _():
        for rs in row_groups:
            h = _rms(x_ref[rs, :], gi_ref[...]).astype(BF16)
            h_ref[rs, :] = h
            o_ref[rs, :] = partial_out(h)

    @pl.when((j > 0) & (j < last_j))
    def _():
        o_ref[...] += partial_out(h_ref[...])

    @pl.when(j == last_j)
    def _():
        for rs in row_groups:
            acc = o_ref[rs, :] + partial_out(h_ref[rs, :])
            o_ref[rs, :] = x_ref[rs, :] + _rms(acc, go_ref[...])


def _ffn(x, g_in, g_out, wg, wu, wd, tm, tf):
    n, d = x.shape
    f = wg.shape[1]
    assert f // tf >= 2 and tm % EDGE_ROWS == 0
    return pl.pallas_call(
        _ffn_body,
        grid=(n // tm, f // tf),
        in_specs=[
            pl.BlockSpec((tm, d), lambda i, j: (i, 0), pipeline_mode=pl.Buffered(1)),
            pl.BlockSpec((1, d), lambda i, j: (0, 0)),
            pl.BlockSpec((1, d), lambda i, j: (0, 0)),
            pl.BlockSpec((d, tf), lambda i, j: (0, j)),
            pl.BlockSpec((d, tf), lambda i, j: (0, j)),
            pl.BlockSpec((tf, d), lambda i, j: (j, 0)),
        ],
        out_specs=pl.BlockSpec((tm, d), lambda i, j: (i, 0)),
        out_shape=jax.ShapeDtypeStruct((n, d), F32),
        scratch_shapes=[pltpu.VMEM((tm, d), BF16)],
        compiler_params=_cparams(2),
        name="ffn",
    )(x, g_in, g_out, wg, wu, wd)


def _seg_matrices(width, n_seg_pad):
    e = (_iota((width, n_seg_pad), 0) // HEAD == _iota((width, n_seg_pad), 1)).astype(BF16)
    et = (_iota((n_seg_pad, width), 1) // HEAD == _iota((n_seg_pad, width), 0)).astype(BF16)
    return e, et


def _blockdiag_tril(tc):
    r, c = _iota((tc, tc), 0), _iota((tc, tc), 1)
    return ((r // CHUNK == c // CHUNK) & (c <= r)).astype(BF16)


def _rwkv_body(x_ref, gn_ref, w_ref, mu_ref, w0_ref, w2_ref, a0_ref, a2_ref, g2_ref, kk_ref, ka_ref, rk_ref,
               lnw_ref, lnb_ref, o_ref, p_ref, prow_ref, s_ref, at_ref, rt_ref, bt_ref, kt_ref, v_ref, gl_ref, y_ref,
               *, tc, width):
    step = pl.program_id(1)

    tw = p_ref.shape[1] // N_PROJ_TILES
    assert tw % MXU_COLS == 0

    def h_next_fn():
        return _rms(x_ref[...], gn_ref[...]).astype(BF16)

    def project_tile(h, k):
        p_ref[:, k * tw:(k + 1) * tw] = _dot(h, w_ref[:, k * tw:(k + 1) * tw])

    @pl.when(step == 0)
    def _():
        s_ref[...] = jnp.zeros_like(s_ref)
        prow_ref[...] = jnp.zeros_like(prow_ref)
        h = h_next_fn()
        for k in range(N_PROJ_TILES):
            project_tile(h, k)

    @pl.when(step > 0)
    def _():
        _rwkv_mix_block(h_next_fn, project_tile, mu_ref, w0_ref, w2_ref, a0_ref, a2_ref, g2_ref, kk_ref, ka_ref, rk_ref, lnw_ref,
                        lnb_ref, o_ref, p_ref, prow_ref, s_ref, at_ref, rt_ref, bt_ref, kt_ref, v_ref, gl_ref, y_ref,
                        tc=tc, width=width)


def _rwkv_mix_block(h_next_fn, project_tile, mu_ref, w0_ref, w2_ref, a0_ref, a2_ref, g2_ref, kk_ref, ka_ref, rk_ref, lnw_ref,
                    lnb_ref, o_ref, p_ref, prow_ref, s_ref, at_ref, rt_ref, bt_ref, kt_ref, v_ref, gl_ref, y_ref,
                    *, tc, width):
    n_heads = width // HEAD
    n_chunks = tc // CHUNK

    p = p_ref[...]
    first_row = _iota((tc, 1), 0) == 0

    def lerp(cols):
        pc = p[:, cols]
        psh = jnp.where(first_row, prow_ref[0:1, cols], pltpu.roll(pc, 1, 0))
        return pc + (psh - pc) * mu_ref[:, cols]

    o1 = 3 * width
    xl = lerp(slice(o1, o1 + 4 * LANE))
    wc, ac, gc = xl[:, 0:LANE], xl[:, LANE:2 * LANE], xl[:, 2 * LANE:4 * LANE]
    z = w0_ref[...] + _dot(jnp.tanh(wc).astype(BF16), w2_ref[...])
    a = _sigmoid(a0_ref[...] + _dot(ac.astype(BF16), a2_ref[...]))
    g = _dot(_sigmoid(gc).astype(BF16), g2_ref[...])
    lw = -jnp.exp(-_softplus(-z) - 0.5)

    h_next = h_next_fn()
    project_tile(h_next, 0)
    project_tile(h_next, 1)
    project_tile(h_next, 2)

    xr = lerp(slice(0, o1))
    last_row = p[tc - 1:tc, :]
    r = xr[:, 0:width]
    k = xr[:, width:2 * width]
    v = xr[:, 2 * width:3 * width]

    mxu_w = 2 * LANE
    same_head = (_iota((mxu_w, mxu_w), 0) // HEAD == _iota((mxu_w, mxu_w), 1) // HEAD).astype(BF16)

    def seg_sum(t):
        return jnp.concatenate(
            [_dot_sel_r(t[:, j:j + mxu_w], same_head) for j in range(0, width, mxu_w)], axis=1)

    kk = k * kk_ref[...]
    kk = kk * lax.rsqrt(jnp.maximum(seg_sum(kk * kk), 1e-24))
    k2 = k * (1.0 + (a - 1.0) * ka_ref[...])
    project_tile(h_next, 3)
    project_tile(h_next, 4)
    bonus = seg_sum(r * k2 * rk_ref[...]) * v
    cs = _dot_sel_l(_blockdiag_tril(tc), lw)
    project_tile(h_next, 5)
    project_tile(h_next, 6)
    prow_ref[0:1, :] = last_row

    ginv = jnp.exp(-cs)
    at_ref[...] = (-kk * jnp.exp(cs - lw)).astype(BF16)
    rt_ref[...] = (r * jnp.exp(cs)).astype(BF16)
    bt_ref[...] = (kk * a * ginv).astype(BF16)
    kt_ref[...] = (k2 * ginv).astype(BF16)
    v_ref[...] = v.astype(BF16)
    for c in range(n_chunks):
        last = c * CHUNK + CHUNK - 1
        gl_ref[c:c + 1, :] = jnp.exp(cs[last:last + 1, :])

    rr, cc = _iota((CHUNK, CHUNK), 0), _iota((CHUNK, CHUNK), 1)
    rr2, cc2 = _iota((CHUNK, 2 * CHUNK), 0), _iota((CHUNK, 2 * CHUNK), 1) % CHUNK
    strict2 = (cc2 < rr2).astype(F32)
    incl2 = (cc2 <= rr2).astype(F32)
    eye = (cc == rr).astype(F32)
    n_lvl = int(math.log2(CHUNK))
    lvl_masks = []
    for lv in range(n_lvl):
        same = (rr >> (lv + 1)) == (cc >> (lv + 1))
        lvl_masks.append((same & (((rr >> lv) & 1) == 1) & (((cc >> lv) & 1) == 0)).astype(F32))

    def chunk_step(c, carry):
        t0 = pl.multiple_of(c * CHUNK, CHUNK)
        rows = pl.ds(t0, CHUNK)
        gl = gl_ref[pl.ds(c, 1), :]
        heads = range(n_heads)
        hsl = [slice(h * HEAD, (h + 1) * HEAD) for h in heads]
        vh = [v_ref[rows, hs] for hs in hsl]
        ar = [jnp.concatenate([at_ref[rows, hs], rt_ref[rows, hs]], axis=0) for hs in hsl]
        bk = [jnp.concatenate([bt_ref[rows, hs], kt_ref[rows, hs]], axis=0) for hs in hsl]
        pm = [_dot_nt(ar[h], bk[h]) for h in heads]
        s0 = [s_ref[h] for h in heads]
        ars = [_dot_nt(ar[h], s0[h].astype(BF16)) for h in heads]
        pa = [pm[h][:CHUNK, :] * strict2 for h in heads]
        aab = [pa[h][:, :CHUNK] for h in heads]
        rhs = [ars[h][:CHUNK, :] + _dot(pa[h][:, CHUNK:].astype(BF16), vh[h]) for h in heads]
        x = [eye + aab[h] * lvl_masks[0] for h in heads]
        for lv in range(1, n_lvl):
            xb = [x[h].astype(BF16) for h in heads]
            tm_ = [_dot((aab[h] * lvl_masks[lv]).astype(BF16), xb[h]) for h in heads]
            x = [x[h] + _dot(xb[h], tm_[h].astype(BF16)) for h in heads]
        u = [_dot(x[h].astype(BF16), rhs[h].astype(BF16)) for h in heads]
        uv = [jnp.concatenate([u[h].astype(BF16), vh[h]], axis=0) for h in heads]
        for h in heads:
            pr = (pm[h][CHUNK:, :] * incl2).astype(BF16)
            y_ref[rows, hsl[h]] = ars[h][CHUNK:, :] + _dot(pr, uv[h])
        for h in heads:
            s_ref[h] = (s0[h] + _dot_tn(uv[h], bk[h])) * gl[:, hsl[h]]
        return carry

    lax.fori_loop(0, n_chunks, chunk_step, 0)

    y = y_ref[...]
    inv_n = 1.0 / HEAD
    mean = seg_sum(y) * inv_n
    yc = y - mean
    var = seg_sum(yc * yc) * inv_n
    yn = yc * lax.rsqrt(var + RWKV_LN_EPS) * lnw_ref[...] + lnb_ref[...]
    o_ref[...] = ((yn + bonus) * g).astype(o_ref.dtype)


def _mixer_specs(x, gn, w, params, batch, seq, tc, width):
    nt = seq // tc
    d = x.shape[1]
    const = lambda b, s: (0, 0)
    in_specs = [
        pl.BlockSpec((tc, d), lambda b, s: (b * nt + jnp.minimum(s, nt - 1), 0)),
        pl.BlockSpec(gn.shape, const),
        pl.BlockSpec(w.shape, const, pipeline_mode=pl.Buffered(1)),
    ] + [pl.BlockSpec(a.shape, const) for a in params]
    out_spec = pl.BlockSpec((tc, width), lambda b, s: (b * nt + jnp.maximum(s - 1, 0), 0))
    return (batch, nt + 1), in_specs, out_spec


def _rwkv_mix(x, gn, w, mu, w0, w2, a0, a2, g2, k_k, k_a, r_k, ln_w, ln_b, batch, seq, tc):
    n = x.shape[0]
    pw = w.shape[1]
    width = w0.shape[1]
    n_heads = width // HEAD
    params = (mu, w0, w2, a0, a2, g2, k_k, k_a, r_k, ln_w, ln_b)
    grid, in_specs, out_spec = _mixer_specs(x, gn, w, params, batch, seq, tc, width)
    return pl.pallas_call(
        functools.partial(_rwkv_body, tc=tc, width=width),
        grid=grid,
        in_specs=in_specs,
        out_specs=out_spec,
        out_shape=jax.ShapeDtypeStruct((n, width), BF16),
        scratch_shapes=[pltpu.VMEM((tc, pw), F32), pltpu.VMEM((SUBLANE, pw), F32),
                        pltpu.VMEM((n_heads, HEAD, HEAD), F32)]
        + [pltpu.VMEM((tc, width), BF16) for _ in range(5)]
        + [pltpu.VMEM((max(tc // CHUNK, SUBLANE), width), F32), pltpu.VMEM((tc, width), F32)],
        compiler_params=_cparams(2),
        name="rwkv_mix",
    )(x, gn, w, *params)


def _ssd_body(x_ref, gn_ref, w_ref, cw_ref, cb_ref, dtb_ref, alog_ref, dskip_ref, nw_ref, o_ref,
              p_ref, tail_ref, st_ref, y_ref, *, tc, width, state, groups):
    step = pl.program_id(1)
    n_chunks = tc // CHUNK
    pw = p_ref.shape[1]
    tw = 4 * MXU_COLS
    tiles = [slice(c0, min(c0 + tw, pw)) for c0 in range(0, pw, tw)]
    n_tiles = len(tiles)

    def project_tile(h, k):
        p_ref[:, tiles[k]] = _dot(h, w_ref[:, tiles[k]])

    @pl.when(step == 0)
    def _():
        st_ref[...] = jnp.zeros_like(st_ref)
        tail_ref[...] = jnp.zeros_like(tail_ref)
        h = _rms(x_ref[...], gn_ref[...]).astype(BF16)
        for k in range(n_tiles):
            project_tile(h, k)

    @pl.when(step > 0)
    def _():
        _ssd_mix_block(x_ref, gn_ref, project_tile, n_tiles, cw_ref, cb_ref, dtb_ref, alog_ref, dskip_ref, nw_ref,
                       o_ref, p_ref, tail_ref, st_ref, y_ref, tc=tc, width=width, state=state, groups=groups)


def _ssd_mix_block(x_ref, gn_ref, project_tile, n_tiles, cw_ref, cb_ref, dtb_ref, alog_ref, dskip_ref, nw_ref,
                   o_ref, p_ref, tail_ref, st_ref, y_ref, *, tc, width, state, groups):
    n_chunks = tc // CHUNK
    gw = width // groups
    hpg = gw // HEAD
    conv_ch = width + 2 * groups * state
    n_tap = cw_ref.shape[0]

    p = p_ref[...]
    zs = _silu(p[:, 0:width])
    xbc_raw = p[:, width:width + conv_ch]
    dt_raw = p[:, width + conv_ch:width + conv_ch + LANE]
    h_next = _rms(x_ref[...], gn_ref[...]).astype(BF16)
    tiles_after_chunk = {c: [k for k in range(1, n_tiles) if (k - 1) % n_chunks == c] for c in range(n_chunks)}

    prev = tail_ref[...]
    tail_ref[...] = xbc_raw[tc - SUBLANE:tc, :]
    xcat = jnp.concatenate([prev, xbc_raw], axis=0)
    conv = cb_ref[...]
    for j in range(n_tap):
        off = SUBLANE - (n_tap - 1) + j
        conv = conv + cw_ref[j:j + 1, :] * xcat[off:off + tc, :]
    xbc = _silu(conv)
    xs = xbc[:, 0:width]
    bm = xbc[:, width:width + groups * state].astype(BF16)
    cm = xbc[:, width + groups * state:conv_ch].astype(BF16)
    project_tile(h_next, 0)

    _, et = _seg_matrices(width, LANE)
    dt = _softplus(dt_raw + dtb_ref[...])
    da = -jnp.exp(alog_ref[...]) * dt
    cs_h = _dot_sel_l(_blockdiag_tril(tc), da)
    cs = _dot_sel_r(cs_h, et)
    dtx = _dot_sel_r(dt, et)
    xc = xs * dtx
    ecs = jnp.exp(cs)

    rr, cc = _iota((CHUNK, CHUNK), 0), _iota((CHUNK, CHUNK), 1)
    causal = cc <= rr

    for c in range(n_chunks):
        r0 = c * CHUNK
        rows = slice(r0, r0 + CHUNK)
        cs_c = cs[rows, :]
        cs_last = cs_c[CHUNK - 1:CHUNK, :]
        xc_c = xc[rows, :]
        xd = (xc_c * jnp.exp(cs_last - cs_c)).astype(BF16)
        xcb = xc_c.astype(BF16)
        cs_t = cs_h[rows, :].T
        for g in range(groups):
            gl = slice(g * gw, (g + 1) * gw)
            b_g = bm[rows, g * state:(g + 1) * state]
            c_g = cm[rows, g * state:(g + 1) * state]
            cb = _dot_nt(c_g, b_g)
            st = st_ref[g]
            y_off = _dot(c_g, st.astype(BF16)) * ecs[rows, gl]
            for hh in range(hpg):
                h = g * hpg + hh
                hs = slice(h * HEAD, (h + 1) * HEAD)
                seg = cs_c[:, hs] - cs_t[h:h + 1, :]
                lmat = jnp.where(causal, jnp.exp(jnp.where(causal, seg, 0.0)), 0.0)
                y_ref[rows, hs] = _dot((cb * lmat).astype(BF16), xcb[:, hs]) + y_off[:, hh * HEAD:(hh + 1) * HEAD]
            st_ref[g] = st * ecs[r0 + CHUNK - 1:r0 + CHUNK, gl] + _dot_tn(b_g, xd[:, gl])
        for k in tiles_after_chunk[c]:
            project_tile(h_next, k)

    y = (y_ref[...] + dskip_ref[...] * xs) * zs
    for g in range(groups):
        gl = slice(g * gw, (g + 1) * gw)
        yg = y[:, gl]
        ms = jnp.mean(yg * yg, axis=-1, keepdims=True)
        o_ref[:, gl] = (yg * lax.rsqrt(ms + NORM_EPS) * nw_ref[:, gl]).astype(o_ref.dtype)


def _ssd_mix(x, gn, w, conv_w, conv_b, dt_bias, a_log, d_skip, norm_w, batch, seq, tc, width, state, groups):
    n = x.shape[0]
    pw = w.shape[1]
    conv_ch = width + 2 * groups * state
    params = (conv_w, conv_b, dt_bias, a_log, d_skip, norm_w)
    grid, in_specs, out_spec = _mixer_specs(x, gn, w, params, batch, seq, tc, width)
    return pl.pallas_call(
        functools.partial(_ssd_body, tc=tc, width=width, state=state, groups=groups),
        grid=grid,
        in_specs=in_specs,
        out_specs=out_spec,
        out_shape=jax.ShapeDtypeStruct((n, width), BF16),
        scratch_shapes=[pltpu.VMEM((tc, pw), F32), pltpu.VMEM((SUBLANE, conv_ch), F32),
                        pltpu.VMEM((groups, state, width // groups), F32), pltpu.VMEM((tc, width), F32)],
        compiler_params=_cparams(2),
        name="ssd_mix",
    )(x, gn, w, *params)


PAIR = 2 * CHUNK
BAND = (LEFT_CHUNKS + 1) * CHUNK
PAIR_BAND = BAND + CHUNK


def _bias_table_body(rb_ref, o_ref, *, n_bucket, rel_future):
    ext_w = PAIR_BAND + LANE
    m = _iota((n_bucket, ext_w), 1)
    bucket = jnp.clip(PAIR_BAND - 1 - m, -rel_future, REL_PAST_CLIP) + rel_future
    sel = (bucket == _iota((n_bucket, ext_w), 0)).astype(BF16)
    ext = _dot_sel_r(rb_ref[...], sel) * LOG2E
    for r in range(PAIR):
        o_ref[r] = ext[:, PAIR - 1 - r:PAIR - 1 - r + PAIR_BAND]


def _bias_table(rel_bias):
    n_heads, n_bucket = rel_bias.shape
    out = pl.pallas_call(
        functools.partial(_bias_table_body, n_bucket=n_bucket, rel_future=CHUNK - 1),
        out_shape=jax.ShapeDtypeStruct((PAIR, n_heads, PAIR_BAND), F32),
        name="bias_table",
    )(rel_bias)
    return jnp.transpose(out, (1, 0, 2))


def _attn_body(q_ref, kp_ref, kc_ref, vp_ref, vc_ref, b_ref, o_ref, *, tq, dh):
    i = pl.program_id(2)
    n_heads = q_ref.shape[1] // dh
    row, col = _iota((PAIR, PAIR_BAND), 0), _iota((PAIR, PAIR_BAND), 1)
    lo = (row // CHUNK) * CHUNK
    in_band = (col >= lo) & (col < lo + BAND)
    n_pairs = tq // PAIR
    k0 = [tq - LEFT_CHUNKS * CHUNK + jp * PAIR for jp in range(n_pairs)]
    work = [(h, jp) for h in range(n_heads) for jp in range(n_pairs)]

    def attend(first_block):
        hcols = [slice(h * dh, (h + 1) * dh) for h in range(n_heads)]
        kcat = [jnp.concatenate([kp_ref[:, hc], kc_ref[:, hc]], axis=0) for hc in hcols]
        vcat = [jnp.concatenate([vp_ref[:, hc], vc_ref[:, hc]], axis=0) for hc in hcols]
        s = [_dot_nt(q_ref[jp * PAIR:(jp + 1) * PAIR, hcols[h]], kcat[h][k0[jp]:k0[jp] + PAIR_BAND, :]) + b_ref[h]
             for h, jp in work]
        if first_block:
            valid = [in_band & (col + k0[jp] >= tq) for jp in range(n_pairs)]
            s = [jnp.where(valid[jp], s_, -1e30) for (h, jp), s_ in zip(work, s)]
        else:
            edge = PAIR_BAND - LANE
            s = [jnp.concatenate([jnp.where(in_band[:, :LANE], s_[:, :LANE], -1e30), s_[:, LANE:edge],
                                  jnp.where(in_band[:, edge:], s_[:, edge:], -1e30)], axis=1) for s_ in s]
        mx = [jnp.max(s_, axis=-1, keepdims=True) for s_ in s]
        pexp = [jnp.exp2(s_ - m_) for s_, m_ in zip(s, mx)]
        den = [jnp.sum(p_, axis=-1, keepdims=True) for p_ in pexp]
        o = [_dot(p_.astype(BF16), vcat[h][k0[jp]:k0[jp] + PAIR_BAND, :]) for (h, jp), p_ in zip(work, pexp)]
        for (h, jp), o_, d_ in zip(work, o, den):
            o_ref[jp * PAIR:(jp + 1) * PAIR, hcols[h]] = (o_ / d_).astype(o_ref.dtype)

    @pl.when(i == 0)
    def _():
        attend(True)

    @pl.when(i > 0)
    def _():
        attend(False)


def _band_attention(qkv, bias, batch, seq, n_heads, tq, hp):
    n, three_d = qkv.shape
    d = three_d // 3
    dh = d // n_heads
    nt = seq // tq
    ng = n_heads // hp
    cur = lambda off: (lambda b, h, i: (b * nt + i, off * ng + h))
    prv = lambda off: (lambda b, h, i: (b * nt + jnp.maximum(i - 1, 0), off * ng + h))
    blk = lambda f: pl.BlockSpec((tq, hp * dh), f)
    return pl.pallas_call(
        functools.partial(_attn_body, tq=tq, dh=dh),
        grid=(batch, ng, nt),
        in_specs=[blk(cur(0)), blk(prv(1)), blk(cur(1)), blk(prv(2)), blk(cur(2)),
                  pl.BlockSpec((hp, PAIR, PAIR_BAND), lambda b, h, i: (h, 0, 0))],
        out_specs=pl.BlockSpec((tq, hp * dh), lambda b, h, i: (b * nt + i, h)),
        out_shape=jax.ShapeDtypeStruct((n, d), BF16),
        compiler_params=_cparams(3),
        name="band_attention",
    )(qkv, qkv, qkv, qkv, qkv, bias)


def _pad_cols(a, width):
    return jnp.pad(a, ((0, 0), (0, width - a.shape[1])))


def _pad_rows(a, rows):
    return jnp.pad(a, ((0, rows - a.shape[0]), (0, 0)))


def _row(a):
    return a.reshape(1, -1)


class _Tiles(NamedTuple):
    rows: int
    proj_rows: int
    mix_rows: int
    attn_rows: int
    attn_heads: int
    qkv_cols: int
    ffn_cols: int


def _tiles(n, seq):
    rows = min(1024, n)
    return _Tiles(rows=rows, proj_rows=rows, mix_rows=min(4 * CHUNK, seq),
                  attn_rows=min(LEFT_CHUNKS * CHUNK, seq), attn_heads=8, qkv_cols=2048, ffn_cols=512)


def kernel(x, norm_g, w_in_ab, rwkv_mu, rwkv_w0, rwkv_w2, rwkv_a0, rwkv_a2, rwkv_g2, rwkv_k_k, rwkv_k_a, rwkv_r_k, rwkv_ln_w, rwkv_ln_b, ssm_conv_w, ssm_conv_b, ssm_dt_bias, ssm_A_log, ssm_D, ssm_norm_w, w_out_ab, w_qkv, attn_rel_bias, w_out_c, ffn_w_gate, ffn_w_up, ffn_w_down):
    batch, seq, d = x.shape
    n = batch * seq
    depth = norm_g.shape[0]
    xf = x.reshape(n, d)

    rw = rwkv_w0.shape[1]
    lora_w, lora_a, lora_g = rwkv_w2.shape[1], rwkv_a2.shape[1], rwkv_g2.shape[1]
    sw = ssm_norm_w.shape[1]
    n_ssm_heads = ssm_A_log.shape[1]
    conv_ch = ssm_conv_w.shape[2]
    groups = 2
    state = (conv_ch - sw) // (2 * groups)
    n_att_heads = attn_rel_bias.shape[1]
    rwkv_proj = 3 * rw + lora_w + lora_a + lora_g
    t = _tiles(n, seq)

    for l in range(depth):
        i = l // 2
        g = norm_g[l]
        if l % 2 == 0:
            w_in = w_in_ab[i]
            o1, o2, o3 = 3 * rw, 3 * rw + lora_w, 3 * rw + lora_w + lora_a
            seg = lambda a: [a[:, 0:o1], _pad_cols(a[:, o1:o2], LANE), _pad_cols(a[:, o2:o3], LANE), a[:, o3:rwkv_proj]]
            w_r = jnp.concatenate(seg(w_in), axis=1).astype(BF16)
            mu = jnp.concatenate(seg(_row(rwkv_mu[i])), axis=1)
            o4 = rwkv_proj + sw + conv_ch
            w_s = jnp.concatenate([w_in[:, rwkv_proj:o4], _pad_cols(w_in[:, o4:], LANE)], axis=1).astype(BF16)

            y_a = _rwkv_mix(
                xf, g[0:1], w_r, mu, _row(rwkv_w0[i]), _pad_rows(rwkv_w2[i], LANE).astype(BF16), _row(rwkv_a0[i]),
                _pad_rows(rwkv_a2[i], LANE).astype(BF16), rwkv_g2[i].astype(BF16), _row(rwkv_k_k[i]),
                _row(rwkv_k_a[i]), _row(rwkv_r_k[i]), _row(rwkv_ln_w[i]), _row(rwkv_ln_b[i]), batch, seq, t.mix_rows)
            y_b = _ssd_mix(
                xf, g[0:1], w_s, ssm_conv_w[i], _row(ssm_conv_b[i]), _pad_cols(_row(ssm_dt_bias[i]), LANE),
                _pad_cols(_row(ssm_A_log[i]), LANE), _row(jnp.repeat(ssm_D[i], sw // n_ssm_heads)),
                _row(ssm_norm_w[i]), batch, seq, t.mix_rows, sw, state, groups)
            w_o = w_out_ab[i].astype(BF16)
            xf = _proj_res([y_a, y_b], [w_o[:rw], w_o[rw:]], xf, g[1:2], t.proj_rows)
        else:
            qkv = _norm_matmul(xf, g[0:1], w_qkv[i].astype(BF16), BF16, t.rows, t.qkv_cols, n_scaled=d // t.qkv_cols,
                               scale=(d // n_att_heads) ** -0.5 * LOG2E)
            bias = _bias_table(attn_rel_bias[i])
            att = _band_attention(qkv, bias, batch, seq, n_att_heads, t.attn_rows, t.attn_heads)
            xf = _proj_res([att], [w_out_c[i].astype(BF16)], xf, g[1:2], t.proj_rows)
        xf = _ffn(xf, g[2:3], g[3:4], ffn_w_gate[l].astype(BF16), ffn_w_up[l].astype(BF16),
                  ffn_w_down[l].astype(BF16), t.rows, t.ffn_cols)
    return xf.reshape(batch, seq, d)
```

```python
import functools
import math
from typing import NamedTuple

import jax
import jax.numpy as jnp
from jax import lax
from jax.experimental import pallas as pl
from jax.experimental.pallas import tpu as pltpu

F32 = jnp.float32
BF16 = jnp.bfloat16

NORM_EPS = 1e-6
RWKV_LN_EPS = 64e-5
CHUNK = 64
HEAD = 64
LANE = 128
SUBLANE = 8
MXU_COLS = 256
EDGE_ROWS = 256
N_PROJ_TILES = 7
LEFT_CHUNKS = 8
REL_PAST_CLIP = 256
LOG2E = math.log2(math.e)
VMEM_LIMIT = 60 * 1024 * 1024


def _cparams(n_axes):
    return pltpu.CompilerParams(dimension_semantics=("arbitrary",) * n_axes, vmem_limit_bytes=VMEM_LIMIT)


def _dot(a, b):
    return jnp.dot(a, b, preferred_element_type=F32)


def _dot_nt(a, b):
    return lax.dot_general(a, b, (((1,), (1,)), ((), ())), preferred_element_type=F32)


def _dot_tn(a, b):
    return lax.dot_general(a, b, (((0,), (0,)), ((), ())), preferred_element_type=F32)


def _split2(t):
    hi = t.astype(BF16)
    lo = (t - hi.astype(F32)).astype(BF16)
    return hi, lo


def _dot_sel_r(t, sel):
    hi, lo = _split2(t)
    return _dot(hi, sel) + _dot(lo, sel)


def _dot_sel_l(sel, t):
    hi, lo = _split2(t)
    return _dot(sel, hi) + _dot(sel, lo)


def _rms(x, g):
    ms = jnp.mean(x * x, axis=-1, keepdims=True)
    return x * lax.rsqrt(ms + NORM_EPS) * g


def _sigmoid(x):
    return 1.0 / (1.0 + jnp.exp(-x))


def _silu(x):
    return x * _sigmoid(x)


def _softplus(x):
    return jnp.maximum(x, 0.0) + jnp.log1p(jnp.exp(-jnp.abs(x)))


def _iota(shape, axis):
    return lax.broadcasted_iota(jnp.int32, shape, axis)


def _norm_matmul_body(x_ref, g_ref, w_ref, o_ref, h_ref, *, n_scaled, scale):
    j = pl.program_id(1)
    mult = jnp.where(j < n_scaled, jnp.float32(scale), jnp.float32(1.0))

    @pl.when(j == 0)
    def _():
        for r in range(0, x_ref.shape[0], EDGE_ROWS):
            rs = slice(r, r + EDGE_ROWS)
            h = _rms(x_ref[rs, :], g_ref[...]).astype(BF16)
            h_ref[rs, :] = h
            o_ref[rs, :] = (_dot(h, w_ref[...]) * mult).astype(o_ref.dtype)

    @pl.when(j > 0)
    def _():
        o_ref[...] = (_dot(h_ref[...], w_ref[...]) * mult).astype(o_ref.dtype)


def _norm_matmul(x, g, w, out_dtype, tm, tn, n_scaled=0, scale=1.0):
    n, d = x.shape
    m = w.shape[1]
    return pl.pallas_call(
        functools.partial(_norm_matmul_body, n_scaled=n_scaled, scale=scale),
        grid=(n // tm, m // tn),
        in_specs=[
            pl.BlockSpec((tm, d), lambda i, j: (i, 0)),
            pl.BlockSpec((1, d), lambda i, j: (0, 0)),
            pl.BlockSpec((d, tn), lambda i, j: (0, j)),
        ],
        out_specs=pl.BlockSpec((tm, tn), lambda i, j: (i, j)),
        out_shape=jax.ShapeDtypeStruct((n, m), out_dtype),
        scratch_shapes=[pltpu.VMEM((tm, d), BF16)],
        compiler_params=_cparams(2),
        name="norm_matmul",
    )(x, g, w)


def _proj_res_body(*refs, n_in):
    y_refs, w_refs = refs[:n_in], refs[n_in:2 * n_in]
    x_ref, g_ref, o_ref = refs[2 * n_in:]
    m = _dot(y_refs[0][...], w_refs[0][...])
    for y_ref, w_ref in zip(y_refs[1:], w_refs[1:]):
        m = m + _dot(y_ref[...], w_ref[...])
    o_ref[...] = x_ref[...] + _rms(m, g_ref[...])


def _proj_res(ys, ws, x, g, tm):
    n, d = x.shape
    n_in = len(ys)
    in_specs = [pl.BlockSpec((tm, y.shape[1]), lambda i: (i, 0)) for y in ys]
    in_specs += [pl.BlockSpec(w.shape, lambda i: (0, 0), pipeline_mode=pl.Buffered(1)) for w in ws]
    in_specs += [pl.BlockSpec((tm, d), lambda i: (i, 0)), pl.BlockSpec((1, d), lambda i: (0, 0))]
    return pl.pallas_call(
        functools.partial(_proj_res_body, n_in=n_in),
        grid=(n // tm,),
        in_specs=in_specs,
        out_specs=pl.BlockSpec((tm, d), lambda i: (i, 0)),
        out_shape=jax.ShapeDtypeStruct((n, d), F32),
        compiler_params=_cparams(1),
        name="proj_res",
    )(*ys, *ws, x, g)


def _ffn_body(x_ref, gi_ref, go_ref, wg_ref, wu_ref, wd_ref, o_ref, h_ref):
    j = pl.program_id(1)
    last_j = pl.num_programs(1) - 1
    tm = x_ref.shape[0]

    def partial_out(h):
        a = (_silu(_dot(h, wg_ref[...])) * _dot(h, wu_ref[...])).astype(BF16)
        return _dot(a, wd_ref[...])

    row_groups = [slice(r, r + EDGE_ROWS) for r in range(0, tm, EDGE_ROWS)]

    @pl.when(j == 0)
    def _():
        for rs in row_groups:
            h = _rms(x_ref[rs, :], gi_ref[...]).astype(BF16)
            h_ref[rs, :] = h
            o_ref[rs, :] = partial_out(h)

    @pl.when((j > 0) & (j < last_j))
    def _():
        o_ref[...] += partial_out(h_ref[...])

    @pl.when(j == last_j)
    def _():
        for rs in row_groups:
            acc = o_ref[rs, :] + partial_out(h_ref[rs, :])
            o_ref[rs, :] = x_ref[rs, :] + _rms(acc, go_ref[...])


def _ffn(x, g_in, g_out, wg, wu, wd, tm, tf):
    n, d = x.shape
    f = wg.shape[1]
    assert f // tf >= 2 and tm % EDGE_ROWS == 0
    return pl.pallas_call(
        _ffn_body,
        grid=(n // tm, f // tf),
        in_specs=[
            pl.BlockSpec((tm, d), lambda i, j: (i, 0), pipeline_mode=pl.Buffered(1)),
            pl.BlockSpec((1, d), lambda i, j: (0, 0)),
            pl.BlockSpec((1, d), lambda i, j: (0, 0)),
            pl.BlockSpec((d, tf), lambda i, j: (0, j)),
            pl.BlockSpec((d, tf), lambda i, j: (0, j)),
            pl.BlockSpec((tf, d), lambda i, j: (j, 0)),
        ],
        out_specs=pl.BlockSpec((tm, d), lambda i, j: (i, 0)),
        out_shape=jax.ShapeDtypeStruct((n, d), F32),
        scratch_shapes=[pltpu.VMEM((tm, d), BF16)],
        compiler_params=_cparams(2),
        name="ffn",
    )(x, g_in, g_out, wg, wu, wd)


def _seg_matrices(width, n_seg_pad):
    e = (_iota((width, n_seg_pad), 0) // HEAD == _iota((width, n_seg_pad), 1)).astype(BF16)
    et = (_iota((n_seg_pad, width), 1) // HEAD == _iota((n_seg_pad, width), 0)).astype(BF16)
    return e, et


def _blockdiag_tril(tc):
    r, c = _iota((tc, tc), 0), _iota((tc, tc), 1)
    return ((r // CHUNK == c // CHUNK) & (c <= r)).astype(BF16)


def _rwkv_body(x_ref, gn_ref, w_ref, mu_ref, w0_ref, w2_ref, a0_ref, a2_ref, g2_ref, kk_ref, ka_ref, rk_ref,
               lnw_ref, lnb_ref, o_ref, p_ref, prow_ref, s_ref, at_ref, rt_ref, bt_ref, kt_ref, v_ref, gl_ref, y_ref,
               *, tc, width, lora_win):
    step = pl.program_id(1)

    tw = p_ref.shape[1] // N_PROJ_TILES
    assert tw % MXU_COLS == 0

    def h_next_fn():
        return _rms(x_ref[...], gn_ref[...]).astype(BF16)

    def project_tile(h, k):
        p_ref[:, k * tw:(k + 1) * tw] = _dot(h, w_ref[:, k * tw:(k + 1) * tw])

    @pl.when(step == 0)
    def _():
        s_ref[...] = jnp.zeros_like(s_ref)
        prow_ref[...] = jnp.zeros_like(prow_ref)
        h = h_next_fn()
        for k in range(N_PROJ_TILES):
            project_tile(h, k)

    @pl.when(step > 0)
    def _():
        _rwkv_mix_block(h_next_fn, project_tile, mu_ref, w0_ref, w2_ref, a0_ref, a2_ref, g2_ref, kk_ref, ka_ref, rk_ref, lnw_ref,
                        lnb_ref, o_ref, p_ref, prow_ref, s_ref, at_ref, rt_ref, bt_ref, kt_ref, v_ref, gl_ref, y_ref,
                        tc=tc, width=width, lora_win=lora_win)


def _rwkv_mix_block(h_next_fn, project_tile, mu_ref, w0_ref, w2_ref, a0_ref, a2_ref, g2_ref, kk_ref, ka_ref, rk_ref, lnw_ref,
                    lnb_ref, o_ref, p_ref, prow_ref, s_ref, at_ref, rt_ref, bt_ref, kt_ref, v_ref, gl_ref, y_ref,
                    *, tc, width, lora_win):
    n_heads = width // HEAD
    n_chunks = tc // CHUNK

    p = p_ref[...]
    first_row = _iota((tc, 1), 0) == 0

    def lerp(cols):
        pc = p[:, cols]
        psh = jnp.where(first_row, prow_ref[0:1, cols], pltpu.roll(pc, 1, 0))
        return pc + (psh - pc) * mu_ref[:, cols]

    o1 = 3 * width
    xl = lerp(slice(o1, p.shape[1]))
    wc, ac, gc = (xl[:, lo:hi] for lo, hi in lora_win)
    z = w0_ref[...] + _dot(jnp.tanh(wc).astype(BF16), w2_ref[...])
    a = _sigmoid(a0_ref[...] + _dot(ac.astype(BF16), a2_ref[...]))
    g = _dot(_sigmoid(gc).astype(BF16), g2_ref[...])
    lw = -jnp.exp(-_softplus(-z) - 0.5)

    h_next = h_next_fn()
    project_tile(h_next, 0)
    project_tile(h_next, 1)
    project_tile(h_next, 2)

    xr = lerp(slice(0, o1))
    last_row = p[tc - 1:tc, :]
    r = xr[:, 0:width]
    k = xr[:, width:2 * width]
    v = xr[:, 2 * width:3 * width]

    mxu_w = 2 * LANE
    same_head = (_iota((mxu_w, mxu_w), 0) // HEAD == _iota((mxu_w, mxu_w), 1) // HEAD).astype(BF16)

    def seg_sum(t):
        return jnp.concatenate(
            [_dot_sel_r(t[:, j:j + mxu_w], same_head) for j in range(0, width, mxu_w)], axis=1)

    kk = k * kk_ref[...]
    kk = kk * lax.rsqrt(jnp.maximum(seg_sum(kk * kk), 1e-24))
    k2 = k * (1.0 + (a - 1.0) * ka_ref[...])
    project_tile(h_next, 3)
    project_tile(h_next, 4)
    bonus = seg_sum(r * k2 * rk_ref[...]) * v
    cs = _dot_sel_l(_blockdiag_tril(tc), lw)
    project_tile(h_next, 5)
    project_tile(h_next, 6)
    prow_ref[0:1, :] = last_row

    ginv = jnp.exp(-cs)
    at_ref[...] = (-kk * jnp.exp(cs - lw)).astype(BF16)
    rt_ref[...] = (r * jnp.exp(cs)).astype(BF16)
    bt_ref[...] = (kk * a * ginv).astype(BF16)
    kt_ref[...] = (k2 * ginv).astype(BF16)
    v_ref[...] = v.astype(BF16)
    for c in range(n_chunks):
        last = c * CHUNK + CHUNK - 1
        gl_ref[c:c + 1, :] = jnp.exp(cs[last:last + 1, :])

    rr, cc = _iota((CHUNK, CHUNK), 0), _iota((CHUNK, CHUNK), 1)
    rr2, cc2 = _iota((CHUNK, 2 * CHUNK), 0), _iota((CHUNK, 2 * CHUNK), 1) % CHUNK
    strict2 = (cc2 < rr2).astype(F32)
    incl2 = (cc2 <= rr2).astype(F32)
    eye = (cc == rr).astype(F32)
    n_lvl = int(math.log2(CHUNK))
    lvl_masks = []
    for lv in range(n_lvl):
        same = (rr >> (lv + 1)) == (cc >> (lv + 1))
        lvl_masks.append((same & (((rr >> lv) & 1) == 1) & (((cc >> lv) & 1) == 0)).astype(F32))

    def chunk_step(c, carry):
        t0 = pl.multiple_of(c * CHUNK, CHUNK)
        rows = pl.ds(t0, CHUNK)
        gl = gl_ref[pl.ds(c, 1), :]
        heads = range(n_heads)
        hsl = [slice(h * HEAD, (h + 1) * HEAD) for h in heads]
        vh = [v_ref[rows, hs] for hs in hsl]
        ar = [jnp.concatenate([at_ref[rows, hs], rt_ref[rows, hs]], axis=0) for hs in hsl]
        bk = [jnp.concatenate([bt_ref[rows, hs], kt_ref[rows, hs]], axis=0) for hs in hsl]
        pm = [_dot_nt(ar[h], bk[h]) for h in heads]
        s0 = [s_ref[h] for h in heads]
        ars = [_dot_nt(ar[h], s0[h].astype(BF16)) for h in heads]
        pa = [pm[h][:CHUNK, :] * strict2 for h in heads]
        aab = [pa[h][:, :CHUNK] for h in heads]
        rhs = [ars[h][:CHUNK, :] + _dot(pa[h][:, CHUNK:].astype(BF16), vh[h]) for h in heads]
        x = [eye + aab[h] * lvl_masks[0] for h in heads]
        for lv in range(1, n_lvl):
            xb = [x[h].astype(BF16) for h in heads]
            tm_ = [_dot((aab[h] * lvl_masks[lv]).astype(BF16), xb[h]) for h in heads]
            x = [x[h] + _dot(xb[h], tm_[h].astype(BF16)) for h in heads]
        u = [_dot(x[h].astype(BF16), rhs[h].astype(BF16)) for h in heads]
        uv = [jnp.concatenate([u[h].astype(BF16), vh[h]], axis=0) for h in heads]
        for h in heads:
            pr = (pm[h][CHUNK:, :] * incl2).astype(BF16)
            y_ref[rows, hsl[h]] = ars[h][CHUNK:, :] + _dot(pr, uv[h])
        for h in heads:
            s_ref[h] = (s0[h] + _dot_tn(uv[h], bk[h])) * gl[:, hsl[h]]
        return carry

    lax.fori_loop(0, n_chunks, chunk_step, 0)

    y = y_ref[...]
    inv_n = 1.0 / HEAD
    mean = seg_sum(y) * inv_n
    yc = y - mean
    var = seg_sum(yc * yc) * inv_n
    yn = yc * lax.rsqrt(var + RWKV_LN_EPS) * lnw_ref[...] + lnb_ref[...]
    o_ref[...] = ((yn + bonus) * g).astype(o_ref.dtype)


def _mixer_specs(x, gn, w, params, batch, seq, tc, width):
    nt = seq // tc
    d = x.shape[1]
    const = lambda b, s: (0, 0)
    in_specs = [
        pl.BlockSpec((tc, d), lambda b, s: (b * nt + jnp.minimum(s, nt - 1), 0)),
        pl.BlockSpec(gn.shape, const),
        pl.BlockSpec(w.shape, const, pipeline_mode=pl.Buffered(1)),
    ] + [pl.BlockSpec(a.shape, const) for a in params]
    out_spec = pl.BlockSpec((tc, width), lambda b, s: (b * nt + jnp.maximum(s - 1, 0), 0))
    return (batch, nt + 1), in_specs, out_spec


def _rwkv_mix(x, gn, w, mu, w0, w2, a0, a2, g2, k_k, k_a, r_k, ln_w, ln_b, batch, seq, tc, lora_win):
    n = x.shape[0]
    pw = w.shape[1]
    width = w0.shape[1]
    n_heads = width // HEAD
    params = (mu, w0, w2, a0, a2, g2, k_k, k_a, r_k, ln_w, ln_b)
    grid, in_specs, out_spec = _mixer_specs(x, gn, w, params, batch, seq, tc, width)
    return pl.pallas_call(
        functools.partial(_rwkv_body, tc=tc, width=width, lora_win=lora_win),
        grid=grid,
        in_specs=in_specs,
        out_specs=out_spec,
        out_shape=jax.ShapeDtypeStruct((n, width), BF16),
        scratch_shapes=[pltpu.VMEM((tc, pw), F32), pltpu.VMEM((SUBLANE, pw), F32),
                        pltpu.VMEM((n_heads, HEAD, HEAD), F32)]
        + [pltpu.VMEM((tc, width), BF16) for _ in range(5)]
        + [pltpu.VMEM((max(tc // CHUNK, SUBLANE), width), F32), pltpu.VMEM((tc, width), F32)],
        compiler_params=_cparams(2),
        name="rwkv_mix",
    )(x, gn, w, *params)


def _ssd_body(x_ref, gn_ref, w_ref, cw_ref, cb_ref, dtb_ref, alog_ref, dskip_ref, nw_ref, o_ref,
              p_ref, tail_ref, st_ref, y_ref, *, tc, width, state, groups):
    step = pl.program_id(1)
    n_chunks = tc // CHUNK
    pw = p_ref.shape[1]
    tw = 4 * MXU_COLS
    tiles = [slice(c0, min(c0 + tw, pw)) for c0 in range(0, pw, tw)]
    n_tiles = len(tiles)

    def project_tile(h, k):
        p_ref[:, tiles[k]] = _dot(h, w_ref[:, tiles[k]])

    @pl.when(step == 0)
    def _():
        st_ref[...] = jnp.zeros_like(st_ref)
        tail_ref[...] = jnp.zeros_like(tail_ref)
        h = _rms(x_ref[...], gn_ref[...]).astype(BF16)
        for k in range(n_tiles):
            project_tile(h, k)

    @pl.when(step > 0)
    def _():
        _ssd_mix_block(x_ref, gn_ref, project_tile, n_tiles, cw_ref, cb_ref, dtb_ref, alog_ref, dskip_ref, nw_ref,
                       o_ref, p_ref, tail_ref, st_ref, y_ref, tc=tc, width=width, state=state, groups=groups)


def _ssd_mix_block(x_ref, gn_ref, project_tile, n_tiles, cw_ref, cb_ref, dtb_ref, alog_ref, dskip_ref, nw_ref,
                   o_ref, p_ref, tail_ref, st_ref, y_ref, *, tc, width, state, groups):
    n_chunks = tc // CHUNK
    gw = width // groups
    hpg = gw // HEAD
    conv_ch = width + 2 * groups * state
    n_tap = cw_ref.shape[0]

    p = p_ref[...]
    zs = _silu(p[:, 0:width])
    xbc_raw = p[:, width:width + conv_ch]
    dt_raw = p[:, width + conv_ch:width + conv_ch + LANE]
    h_next = _rms(x_ref[...], gn_ref[...]).astype(BF16)
    tiles_after_chunk = {c: [k for k in range(1, n_tiles) if (k - 1) % n_chunks == c] for c in range(n_chunks)}

    prev = tail_ref[...]
    tail_ref[...] = xbc_raw[tc - SUBLANE:tc, :]
    xcat = jnp.concatenate([prev, xbc_raw], axis=0)
    conv = cb_ref[...]
    for j in range(n_tap):
        off = SUBLANE - (n_tap - 1) + j
        conv = conv + cw_ref[j:j + 1, :] * xcat[off:off + tc, :]
    xbc = _silu(conv)
    xs = xbc[:, 0:width]
    bm = xbc[:, width:width + groups * state].astype(BF16)
    cm = xbc[:, width + groups * state:conv_ch].astype(BF16)
    project_tile(h_next, 0)

    _, et = _seg_matrices(width, LANE)
    dt = _softplus(dt_raw + dtb_ref[...])
    da = -jnp.exp(alog_ref[...]) * dt
    cs_h = _dot_sel_l(_blockdiag_tril(tc), da)
    cs = _dot_sel_r(cs_h, et)
    dtx = _dot_sel_r(dt, et)
    xc = xs * dtx
    ecs = jnp.exp(cs)

    rr, cc = _iota((CHUNK, CHUNK), 0), _iota((CHUNK, CHUNK), 1)
    causal = cc <= rr

    for c in range(n_chunks):
        r0 = c * CHUNK
        rows = slice(r0, r0 + CHUNK)
        cs_c = cs[rows, :]
        cs_last = cs_c[CHUNK - 1:CHUNK, :]
        xc_c = xc[rows, :]
        xd = (xc_c * jnp.exp(cs_last - cs_c)).astype(BF16)
        xcb = xc_c.astype(BF16)
        cs_t = cs_h[rows, :].T
        for g in range(groups):
            gl = slice(g * gw, (g + 1) * gw)
            b_g = bm[rows, g * state:(g + 1) * state]
            c_g = cm[rows, g * state:(g + 1) * state]
            cb = _dot_nt(c_g, b_g)
            st = st_ref[g]
            y_off = _dot(c_g, st.astype(BF16)) * ecs[rows, gl]
            for hh in range(hpg):
                h = g * hpg + hh
                hs = slice(h * HEAD, (h + 1) * HEAD)
                seg = cs_c[:, hs] - cs_t[h:h + 1, :]
                lmat = jnp.where(causal, jnp.exp(jnp.where(causal, seg, 0.0)), 0.0)
                y_ref[rows, hs] = _dot((cb * lmat).astype(BF16), xcb[:, hs]) + y_off[:, hh * HEAD:(hh + 1) * HEAD]
            st_ref[g] = st * ecs[r0 + CHUNK - 1:r0 + CHUNK, gl] + _dot_tn(b_g, xd[:, gl])
        for k in tiles_after_chunk[c]:
            project_tile(h_next, k)

    y = (y_ref[...] + dskip_ref[...] * xs) * zs
    for g in range(groups):
        gl = slice(g * gw, (g + 1) * gw)
        yg = y[:, gl]
        ms = jnp.mean(yg * yg, axis=-1, keepdims=True)
        o_ref[:, gl] = (yg * lax.rsqrt(ms + NORM_EPS) * nw_ref[:, gl]).astype(o_ref.dtype)


def _ssd_mix(x, gn, w, conv_w, conv_b, dt_bias, a_log, d_skip, norm_w, batch, seq, tc, width, state, groups):
    n = x.shape[0]
    pw = w.shape[1]
    conv_ch = width + 2 * groups * state
    params = (conv_w, conv_b, dt_bias, a_log, d_skip, norm_w)
    grid, in_specs, out_spec = _mixer_specs(x, gn, w, params, batch, seq, tc, width)
    return pl.pallas_call(
        functools.partial(_ssd_body, tc=tc, width=width, state=state, groups=groups),
        grid=grid,
        in_specs=in_specs,
        out_specs=out_spec,
        out_shape=jax.ShapeDtypeStruct((n, width), BF16),
        scratch_shapes=[pltpu.VMEM((tc, pw), F32), pltpu.VMEM((SUBLANE, conv_ch), F32),
                        pltpu.VMEM((groups, state, width // groups), F32), pltpu.VMEM((tc, width), F32)],
        compiler_params=_cparams(2),
        name="ssd_mix",
    )(x, gn, w, *params)


PAIR = 2 * CHUNK
BAND = (LEFT_CHUNKS + 1) * CHUNK
PAIR_BAND = BAND + CHUNK


def _bias_table_body(rb_ref, o_ref, *, n_bucket, rel_future):
    ext_w = PAIR_BAND + LANE
    m = _iota((n_bucket, ext_w), 1)
    bucket = jnp.clip(PAIR_BAND - 1 - m, -rel_future, REL_PAST_CLIP) + rel_future
    sel = (bucket == _iota((n_bucket, ext_w), 0)).astype(BF16)
    ext = _dot_sel_r(rb_ref[...], sel) * LOG2E
    for r in range(PAIR):
        o_ref[r] = ext[:, PAIR - 1 - r:PAIR - 1 - r + PAIR_BAND]


def _bias_table(rel_bias):
    n_heads, n_bucket = rel_bias.shape
    out = pl.pallas_call(
        functools.partial(_bias_table_body, n_bucket=n_bucket, rel_future=CHUNK - 1),
        out_shape=jax.ShapeDtypeStruct((PAIR, n_heads, PAIR_BAND), F32),
        name="bias_table",
    )(rel_bias)
    return jnp.transpose(out, (1, 0, 2))


def _attn_body(q_ref, kp_ref, kc_ref, vp_ref, vc_ref, b_ref, o_ref, *, tq, dh):
    i = pl.program_id(2)
    n_heads = q_ref.shape[1] // dh
    row, col = _iota((PAIR, PAIR_BAND), 0), _iota((PAIR, PAIR_BAND), 1)
    lo = (row // CHUNK) * CHUNK
    in_band = (col >= lo) & (col < lo + BAND)
    n_pairs = tq // PAIR
    k0 = [tq - LEFT_CHUNKS * CHUNK + jp * PAIR for jp in range(n_pairs)]
    work = [(h, jp) for h in range(n_heads) for jp in range(n_pairs)]

    def attend(first_block):
        hcols = [slice(h * dh, (h + 1) * dh) for h in range(n_heads)]
        kcat = [jnp.concatenate([kp_ref[:, hc], kc_ref[:, hc]], axis=0) for hc in hcols]
        vcat = [jnp.concatenate([vp_ref[:, hc], vc_ref[:, hc]], axis=0) for hc in hcols]
        s = [_dot_nt(q_ref[jp * PAIR:(jp + 1) * PAIR, hcols[h]], kcat[h][k0[jp]:k0[jp] + PAIR_BAND, :]) + b_ref[h]
             for h, jp in work]
        if first_block:
            valid = [in_band & (col + k0[jp] >= tq) for jp in range(n_pairs)]
            s = [jnp.where(valid[jp], s_, -1e30) for (h, jp), s_ in zip(work, s)]
        else:
            edge = PAIR_BAND - LANE
            s = [jnp.concatenate([jnp.where(in_band[:, :LANE], s_[:, :LANE], -1e30), s_[:, LANE:edge],
                                  jnp.where(in_band[:, edge:], s_[:, edge:], -1e30)], axis=1) for s_ in s]
        mx = [jnp.max(s_, axis=-1, keepdims=True) for s_ in s]
        pexp = [jnp.exp2(s_ - m_) for s_, m_ in zip(s, mx)]
        den = [jnp.sum(p_, axis=-1, keepdims=True) for p_ in pexp]
        o = [_dot(p_.astype(BF16), vcat[h][k0[jp]:k0[jp] + PAIR_BAND, :]) for (h, jp), p_ in zip(work, pexp)]
        for (h, jp), o_, d_ in zip(work, o, den):
            o_ref[jp * PAIR:(jp + 1) * PAIR, hcols[h]] = (o_ / d_).astype(o_ref.dtype)

    @pl.when(i == 0)
    def _():
        attend(True)

    @pl.when(i > 0)
    def _():
        attend(False)


def _band_attention(qkv, bias, batch, seq, n_heads, tq, hp):
    n, three_d = qkv.shape
    d = three_d // 3
    dh = d // n_heads
    nt = seq // tq
    ng = n_heads // hp
    cur = lambda off: (lambda b, h, i: (b * nt + i, off * ng + h))
    prv = lambda off: (lambda b, h, i: (b * nt + jnp.maximum(i - 1, 0), off * ng + h))
    blk = lambda f: pl.BlockSpec((tq, hp * dh), f)
    return pl.pallas_call(
        functools.partial(_attn_body, tq=tq, dh=dh),
        grid=(batch, ng, nt),
        in_specs=[blk(cur(0)), blk(prv(1)), blk(cur(1)), blk(prv(2)), blk(cur(2)),
                  pl.BlockSpec((hp, PAIR, PAIR_BAND), lambda b, h, i: (h, 0, 0))],
        out_specs=pl.BlockSpec((tq, hp * dh), lambda b, h, i: (b * nt + i, h)),
        out_shape=jax.ShapeDtypeStruct((n, d), BF16),
        compiler_params=_cparams(3),
        name="band_attention",
    )(qkv, qkv, qkv, qkv, qkv, bias)


def _pad_cols(a, width):
    return jnp.pad(a, ((0, 0), (0, width - a.shape[1])))


def _pad_rows(a, rows):
    return jnp.pad(a, ((0, rows - a.shape[0]), (0, 0)))


def _row(a):
    return a.reshape(1, -1)


def _lora_window(w, off):
    rank = w.shape[0]
    lo, hi = off // LANE * LANE, -(-(off + rank) // LANE) * LANE
    return jnp.pad(w, ((off - lo, hi - off - rank), (0, 0))).astype(BF16), (lo, hi)


class _Tiles(NamedTuple):
    rows: int
    proj_rows: int
    mix_rows: int
    attn_rows: int
    attn_heads: int
    qkv_cols: int
    ffn_cols: int


def _tiles(n, seq):
    rows = min(1024, n)
    return _Tiles(rows=rows, proj_rows=rows, mix_rows=min(4 * CHUNK, seq),
                  attn_rows=min(LEFT_CHUNKS * CHUNK, seq), attn_heads=8, qkv_cols=2048, ffn_cols=512)


def kernel(x, norm_g, w_in_ab, rwkv_mu, rwkv_w0, rwkv_w2, rwkv_a0, rwkv_a2, rwkv_g2, rwkv_k_k, rwkv_k_a, rwkv_r_k, rwkv_ln_w, rwkv_ln_b, ssm_conv_w, ssm_conv_b, ssm_dt_bias, ssm_A_log, ssm_D, ssm_norm_w, w_out_ab, w_qkv, attn_rel_bias, w_out_c, ffn_w_gate, ffn_w_up, ffn_w_down):
    batch, seq, d = x.shape
    n = batch * seq
    depth = norm_g.shape[0]
    xf = x.reshape(n, d)

    rw = rwkv_w0.shape[1]
    lora_w, lora_a, lora_g = rwkv_w2.shape[1], rwkv_a2.shape[1], rwkv_g2.shape[1]
    sw = ssm_norm_w.shape[1]
    n_ssm_heads = ssm_A_log.shape[1]
    conv_ch = ssm_conv_w.shape[2]
    groups = 2
    state = (conv_ch - sw) // (2 * groups)
    n_att_heads = attn_rel_bias.shape[1]
    rwkv_proj = 3 * rw + lora_w + lora_a + lora_g
    t = _tiles(n, seq)

    for l in range(depth):
        i = l // 2
        g = norm_g[l]
        if l % 2 == 0:
            w_in = w_in_ab[i]
            pw_r = -(-rwkv_proj // MXU_COLS) * MXU_COLS
            w_r = _pad_cols(w_in[:, :rwkv_proj], pw_r).astype(BF16)
            mu = _pad_cols(_row(rwkv_mu[i]), pw_r)
            lora_w2, win_w = _lora_window(rwkv_w2[i], 0)
            lora_a2, win_a = _lora_window(rwkv_a2[i], lora_w)
            lora_g2, win_g = _lora_window(rwkv_g2[i], lora_w + lora_a)
            o4 = rwkv_proj + sw + conv_ch
            w_s = jnp.concatenate([w_in[:, rwkv_proj:o4], _pad_cols(w_in[:, o4:], LANE)], axis=1).astype(BF16)

            y_a = _rwkv_mix(
                xf, g[0:1], w_r, mu, _row(rwkv_w0[i]), lora_w2, _row(rwkv_a0[i]), lora_a2, lora_g2, _row(rwkv_k_k[i]),
                _row(rwkv_k_a[i]), _row(rwkv_r_k[i]), _row(rwkv_ln_w[i]), _row(rwkv_ln_b[i]), batch, seq, t.mix_rows,
                (win_w, win_a, win_g))
            y_b = _ssd_mix(
                xf, g[0:1], w_s, ssm_conv_w[i], _row(ssm_conv_b[i]), _pad_cols(_row(ssm_dt_bias[i]), LANE),
                _pad_cols(_row(ssm_A_log[i]), LANE), _row(jnp.repeat(ssm_D[i], sw // n_ssm_heads)),
                _row(ssm_norm_w[i]), batch, seq, t.mix_rows, sw, state, groups)
            w_o = w_out_ab[i].astype(BF16)
            xf = _proj_res([y_a, y_b], [w_o[:rw], w_o[rw:]], xf, g[1:2], t.proj_rows)
        else:
            qkv = _norm_matmul(xf, g[0:1], w_qkv[i].astype(BF16), BF16, t.rows, t.qkv_cols, n_scaled=d // t.qkv_cols,
                               scale=(d // n_att_heads) ** -0.5 * LOG2E)
            bias = _bias_table(attn_rel_bias[i])
            att = _band_attention(qkv, bias, batch, seq, n_att_heads, t.attn_rows, t.attn_heads)
            xf = _proj_res([att], [w_out_c[i].astype(BF16)], xf, g[1:2], t.proj_rows)
        xf = _ffn(xf, g[2:3], g[3:4], ffn_w_gate[l].astype(BF16), ffn_w_up[l].astype(BF16),
                  ffn_w_down[l].astype(BF16), t.rows, t.ffn_cols)
    return xf.reshape(batch, seq, d)
```

```python
import functools
import math
from typing import NamedTuple

import jax
import jax.numpy as jnp
from jax import lax
from jax.experimental import pallas as pl
from jax.experimental.pallas import tpu as pltpu

F32 = jnp.float32
BF16 = jnp.bfloat16

NORM_EPS = 1e-6
RWKV_LN_EPS = 64e-5
CHUNK = 64
HEAD = 64
LANE = 128
SUBLANE = 8
MXU_COLS = 256
EDGE_ROWS = 256
N_PROJ_TILES = 7
LEFT_CHUNKS = 8
REL_PAST_CLIP = 256
LOG2E = math.log2(math.e)
VMEM_LIMIT = 60 * 1024 * 1024


def _cparams(n_axes):
    return pltpu.CompilerParams(dimension_semantics=("arbitrary",) * n_axes, vmem_limit_bytes=VMEM_LIMIT)


def _dot(a, b):
    return jnp.dot(a, b, preferred_element_type=F32)


def _dot_nt(a, b):
    return lax.dot_general(a, b, (((1,), (1,)), ((), ())), preferred_element_type=F32)


def _dot_tn(a, b):
    return lax.dot_general(a, b, (((0,), (0,)), ((), ())), preferred_element_type=F32)


def _split2(t):
    hi = t.astype(BF16)
    lo = (t - hi.astype(F32)).astype(BF16)
    return hi, lo


def _dot_sel_r(t, sel):
    hi, lo = _split2(t)
    return _dot(hi, sel) + _dot(lo, sel)


def _dot_sel_l(sel, t):
    hi, lo = _split2(t)
    return _dot(sel, hi) + _dot(sel, lo)


def _rms(x, g):
    ms = jnp.mean(x * x, axis=-1, keepdims=True)
    return x * lax.rsqrt(ms + NORM_EPS) * g


def _sigmoid(x):
    return 1.0 / (1.0 + jnp.exp(-x))


def _silu(x):
    return x * _sigmoid(x)


def _softplus(x):
    return jnp.maximum(x, 0.0) + jnp.log1p(jnp.exp(-jnp.abs(x)))


def _iota(shape, axis):
    return lax.broadcasted_iota(jnp.int32, shape, axis)


def _norm_matmul_body(x_ref, g_ref, w_ref, o_ref, h_ref, *, n_scaled, scale):
    j = pl.program_id(1)
    mult = jnp.where(j < n_scaled, jnp.float32(scale), jnp.float32(1.0))

    @pl.when(j == 0)
    def _():
        for r in range(0, x_ref.shape[0], EDGE_ROWS):
            rs = slice(r, r + EDGE_ROWS)
            h = _rms(x_ref[rs, :], g_ref[...]).astype(BF16)
            h_ref[rs, :] = h
            o_ref[rs, :] = (_dot(h, w_ref[...]) * mult).astype(o_ref.dtype)

    @pl.when(j > 0)
    def _():
        o_ref[...] = (_dot(h_ref[...], w_ref[...]) * mult).astype(o_ref.dtype)


def _norm_matmul(x, g, w, out_dtype, tm, tn, n_scaled=0, scale=1.0):
    n, d = x.shape
    m = w.shape[1]
    return pl.pallas_call(
        functools.partial(_norm_matmul_body, n_scaled=n_scaled, scale=scale),
        grid=(n // tm, m // tn),
        in_specs=[
            pl.BlockSpec((tm, d), lambda i, j: (i, 0)),
            pl.BlockSpec((1, d), lambda i, j: (0, 0)),
            pl.BlockSpec((d, tn), lambda i, j: (0, j)),
        ],
        out_specs=pl.BlockSpec((tm, tn), lambda i, j: (i, j)),
        out_shape=jax.ShapeDtypeStruct((n, m), out_dtype),
        scratch_shapes=[pltpu.VMEM((tm, d), BF16)],
        compiler_params=_cparams(2),
        name="norm_matmul",
    )(x, g, w)


def _proj_res_body(*refs, n_in):
    y_refs, w_refs = refs[:n_in], refs[n_in:2 * n_in]
    x_ref, g_ref, o_ref = refs[2 * n_in:]
    m = _dot(y_refs[0][...], w_refs[0][...])
    for y_ref, w_ref in zip(y_refs[1:], w_refs[1:]):
        m = m + _dot(y_ref[...], w_ref[...])
    o_ref[...] = x_ref[...] + _rms(m, g_ref[...])


def _proj_res(ys, ws, x, g, tm):
    n, d = x.shape
    n_in = len(ys)
    in_specs = [pl.BlockSpec((tm, y.shape[1]), lambda i: (i, 0)) for y in ys]
    in_specs += [pl.BlockSpec(w.shape, lambda i: (0, 0), pipeline_mode=pl.Buffered(1)) for w in ws]
    in_specs += [pl.BlockSpec((tm, d), lambda i: (i, 0)), pl.BlockSpec((1, d), lambda i: (0, 0))]
    return pl.pallas_call(
        functools.partial(_proj_res_body, n_in=n_in),
        grid=(n // tm,),
        in_specs=in_specs,
        out_specs=pl.BlockSpec((tm, d), lambda i: (i, 0)),
        out_shape=jax.ShapeDtypeStruct((n, d), F32),
        compiler_params=_cparams(1),
        name="proj_res",
    )(*ys, *ws, x, g)


def _ffn_body(x_ref, gi_ref, go_ref, wg_ref, wu_ref, wd_ref, o_ref, h_ref):
    j = pl.program_id(1)
    last_j = pl.num_programs(1) - 1
    tm = x_ref.shape[0]

    def partial_out(h):
        a = (_silu(_dot(h, wg_ref[...])) * _dot(h, wu_ref[...])).astype(BF16)
        return _dot(a, wd_ref[...])

    row_groups = [slice(r, r + EDGE_ROWS) for r in range(0, tm, EDGE_ROWS)]

    @pl.when(j == 0)
    def _():
        for rs in row_groups:
            h = _rms(x_ref[rs, :], gi_ref[...]).astype(BF16)
            h_ref[rs, :] = h
            o_ref[rs, :] = partial_out(h)

    @pl.when((j > 0) & (j < last_j))
    def _():
        o_ref[...] += partial_out(h_ref[...])

    @pl.when(j == last_j)
    def _():
        for rs in row_groups:
            acc = o_ref[rs, :] + partial_out(h_ref[rs, :])
            o_ref[rs, :] = x_ref[rs, :] + _rms(acc, go_ref[...])


def _ffn(x, g_in, g_out, wg, wu, wd, tm, tf):
    n, d = x.shape
    f = wg.shape[1]
    assert f // tf >= 2 and tm % EDGE_ROWS == 0
    return pl.pallas_call(
        _ffn_body,
        grid=(n // tm, f // tf),
        in_specs=[
            pl.BlockSpec((tm, d), lambda i, j: (i, 0), pipeline_mode=pl.Buffered(1)),
            pl.BlockSpec((1, d), lambda i, j: (0, 0)),
            pl.BlockSpec((1, d), lambda i, j: (0, 0)),
            pl.BlockSpec((d, tf), lambda i, j: (0, j)),
            pl.BlockSpec((d, tf), lambda i, j: (0, j)),
            pl.BlockSpec((tf, d), lambda i, j: (j, 0)),
        ],
        out_specs=pl.BlockSpec((tm, d), lambda i, j: (i, 0)),
        out_shape=jax.ShapeDtypeStruct((n, d), F32),
        scratch_shapes=[pltpu.VMEM((tm, d), BF16)],
        compiler_params=_cparams(2),
        name="ffn",
    )(x, g_in, g_out, wg, wu, wd)


def _seg_matrices(width, n_seg_pad):
    e = (_iota((width, n_seg_pad), 0) // HEAD == _iota((width, n_seg_pad), 1)).astype(BF16)
    et = (_iota((n_seg_pad, width), 1) // HEAD == _iota((n_seg_pad, width), 0)).astype(BF16)
    return e, et


def _blockdiag_tril(tc):
    r, c = _iota((tc, tc), 0), _iota((tc, tc), 1)
    return ((r // CHUNK == c // CHUNK) & (c <= r)).astype(BF16)


def _rwkv_body(x_ref, gn_ref, w_ref, mu_ref, w0_ref, w2_ref, a0_ref, a2_ref, g2_ref, kk_ref, ka_ref, rk_ref,
               lnw_ref, lnb_ref, o_ref, p_ref, prow_ref, s_ref, at_ref, rt_ref, bt_ref, kt_ref, v_ref, gl_ref, y_ref,
               *, tc, width, lora_win):
    step = pl.program_id(1)

    tw = p_ref.shape[1] // N_PROJ_TILES
    assert tw % MXU_COLS == 0

    def h_next_fn():
        return _rms(x_ref[...], gn_ref[...]).astype(BF16)

    def project_tile(h, k):
        p_ref[:, k * tw:(k + 1) * tw] = _dot(h, w_ref[:, k * tw:(k + 1) * tw])

    @pl.when(step == 0)
    def _():
        s_ref[...] = jnp.zeros_like(s_ref)
        prow_ref[...] = jnp.zeros_like(prow_ref)
        h = h_next_fn()
        for k in range(N_PROJ_TILES):
            project_tile(h, k)

    @pl.when(step > 0)
    def _():
        _rwkv_mix_block(h_next_fn, project_tile, mu_ref, w0_ref, w2_ref, a0_ref, a2_ref, g2_ref, kk_ref, ka_ref, rk_ref, lnw_ref,
                        lnb_ref, o_ref, p_ref, prow_ref, s_ref, at_ref, rt_ref, bt_ref, kt_ref, v_ref, gl_ref, y_ref,
                        tc=tc, width=width, lora_win=lora_win)


def _rwkv_mix_block(h_next_fn, project_tile, mu_ref, w0_ref, w2_ref, a0_ref, a2_ref, g2_ref, kk_ref, ka_ref, rk_ref, lnw_ref,
                    lnb_ref, o_ref, p_ref, prow_ref, s_ref, at_ref, rt_ref, bt_ref, kt_ref, v_ref, gl_ref, y_ref,
                    *, tc, width, lora_win):
    n_heads = width // HEAD
    n_chunks = tc // CHUNK

    p = p_ref[...]
    first_row = _iota((tc, 1), 0) == 0

    def lerp(cols):
        pc = p[:, cols]
        psh = jnp.where(first_row, prow_ref[0:1, cols], pltpu.roll(pc, 1, 0))
        return pc + (psh - pc) * mu_ref[:, cols]

    o1 = 3 * width
    xl = lerp(slice(o1, p.shape[1]))
    wc, ac, gc = (xl[:, lo:hi] for lo, hi in lora_win)
    z = w0_ref[...] + _dot(jnp.tanh(wc).astype(BF16), w2_ref[...])
    a = _sigmoid(a0_ref[...] + _dot(ac.astype(BF16), a2_ref[...]))
    g = _dot(_sigmoid(gc).astype(BF16), g2_ref[...])
    lw = -jnp.exp(-_softplus(-z) - 0.5)

    h_next = h_next_fn()
    project_tile(h_next, 0)
    project_tile(h_next, 1)
    project_tile(h_next, 2)

    xr = lerp(slice(0, o1))
    last_row = p[tc - 1:tc, :]
    r = xr[:, 0:width]
    k = xr[:, width:2 * width]
    v = xr[:, 2 * width:3 * width]

    mxu_w = 2 * LANE
    same_head = (_iota((mxu_w, mxu_w), 0) // HEAD == _iota((mxu_w, mxu_w), 1) // HEAD).astype(BF16)

    def seg_sum(t):
        return jnp.concatenate(
            [_dot_sel_r(t[:, j:j + mxu_w], same_head) for j in range(0, width, mxu_w)], axis=1)

    kk = k * kk_ref[...]
    kk = kk * lax.rsqrt(jnp.maximum(seg_sum(kk * kk), 1e-24))
    k2 = k * (1.0 + (a - 1.0) * ka_ref[...])
    project_tile(h_next, 3)
    project_tile(h_next, 4)
    bonus = seg_sum(r * k2 * rk_ref[...]) * v
    cs = _dot_sel_l(_blockdiag_tril(tc), lw)
    project_tile(h_next, 5)
    project_tile(h_next, 6)
    prow_ref[0:1, :] = last_row

    ginv = jnp.exp(-cs)
    at_ref[...] = (-kk * jnp.exp(cs - lw)).astype(BF16)
    rt_ref[...] = (r * jnp.exp(cs)).astype(BF16)
    bt_ref[...] = (kk * a * ginv).astype(BF16)
    kt_ref[...] = (k2 * ginv).astype(BF16)
    v_ref[...] = v.astype(BF16)
    for c in range(n_chunks):
        last = c * CHUNK + CHUNK - 1
        gl_ref[c:c + 1, :] = jnp.exp(cs[last:last + 1, :])

    rr, cc = _iota((CHUNK, CHUNK), 0), _iota((CHUNK, CHUNK), 1)
    rr2, cc2 = _iota((CHUNK, 2 * CHUNK), 0), _iota((CHUNK, 2 * CHUNK), 1) % CHUNK
    strict2 = (cc2 < rr2).astype(F32)
    incl2 = (cc2 <= rr2).astype(F32)
    eye = (cc == rr).astype(F32)
    n_lvl = int(math.log2(CHUNK))
    lvl_masks = []
    for lv in range(n_lvl):
        same = (rr >> (lv + 1)) == (cc >> (lv + 1))
        lvl_masks.append((same & (((rr >> lv) & 1) == 1) & (((cc >> lv) & 1) == 0)).astype(F32))

    def chunk_step(c, carry):
        t0 = pl.multiple_of(c * CHUNK, CHUNK)
        rows = pl.ds(t0, CHUNK)
        gl = gl_ref[pl.ds(c, 1), :]
        heads = range(n_heads)
        hsl = [slice(h * HEAD, (h + 1) * HEAD) for h in heads]
        vh = [v_ref[rows, hs] for hs in hsl]
        ar = [jnp.concatenate([at_ref[rows, hs], rt_ref[rows, hs]], axis=0) for hs in hsl]
        bk = [jnp.concatenate([bt_ref[rows, hs], kt_ref[rows, hs]], axis=0) for hs in hsl]
        pm = [_dot_nt(ar[h], bk[h]) for h in heads]
        s0 = [s_ref[h] for h in heads]
        ars = [_dot_nt(ar[h], s0[h].astype(BF16)) for h in heads]
        pa = [pm[h][:CHUNK, :] * strict2 for h in heads]
        aab = [pa[h][:, :CHUNK] for h in heads]
        rhs = [ars[h][:CHUNK, :] + _dot(pa[h][:, CHUNK:].astype(BF16), vh[h]) for h in heads]
        x = [eye + aab[h] * lvl_masks[0] for h in heads]
        for lv in range(1, n_lvl):
            xb = [x[h].astype(BF16) for h in heads]
            tm_ = [_dot((aab[h] * lvl_masks[lv]).astype(BF16), xb[h]) for h in heads]
            x = [x[h] + _dot(xb[h], tm_[h].astype(BF16)) for h in heads]
        u = [_dot(x[h].astype(BF16), rhs[h].astype(BF16)) for h in heads]
        uv = [jnp.concatenate([u[h].astype(BF16), vh[h]], axis=0) for h in heads]
        for h in heads:
            pr = (pm[h][CHUNK:, :] * incl2).astype(BF16)
            y_ref[rows, hsl[h]] = ars[h][CHUNK:, :] + _dot(pr, uv[h])
        for h in heads:
            s_ref[h] = (s0[h] + _dot_tn(uv[h], bk[h])) * gl[:, hsl[h]]
        return carry

    lax.fori_loop(0, n_chunks, chunk_step, 0)

    y = y_ref[...]
    inv_n = 1.0 / HEAD
    mean = seg_sum(y) * inv_n
    yc = y - mean
    var = seg_sum(yc * yc) * inv_n
    yn = yc * lax.rsqrt(var + RWKV_LN_EPS) * lnw_ref[...] + lnb_ref[...]
    o_ref[...] = ((yn + bonus) * g).astype(o_ref.dtype)


def _mixer_specs(x, gn, w, pw, params, batch, seq, tc, width):
    nt = seq // tc
    d = x.shape[1]
    const = lambda b, s: (0, 0)
    in_specs = [
        pl.BlockSpec((tc, d), lambda b, s: (b * nt + jnp.minimum(s, nt - 1), 0)),
        pl.BlockSpec(gn.shape, const),
        pl.BlockSpec((w.shape[0], pw), const, pipeline_mode=pl.Buffered(1)),
    ] + [pl.BlockSpec(a.shape, const) for a in params]
    out_spec = pl.BlockSpec((tc, width), lambda b, s: (b * nt + jnp.maximum(s - 1, 0), 0))
    return (batch, nt + 1), in_specs, out_spec


def _rwkv_mix(x, gn, w, mu, w0, w2, a0, a2, g2, k_k, k_a, r_k, ln_w, ln_b, batch, seq, tc, lora_win):
    n = x.shape[0]
    pw = mu.shape[1]
    width = w0.shape[1]
    n_heads = width // HEAD
    params = (mu, w0, w2, a0, a2, g2, k_k, k_a, r_k, ln_w, ln_b)
    grid, in_specs, out_spec = _mixer_specs(x, gn, w, pw, params, batch, seq, tc, width)
    return pl.pallas_call(
        functools.partial(_rwkv_body, tc=tc, width=width, lora_win=lora_win),
        grid=grid,
        in_specs=in_specs,
        out_specs=out_spec,
        out_shape=jax.ShapeDtypeStruct((n, width), BF16),
        scratch_shapes=[pltpu.VMEM((tc, pw), F32), pltpu.VMEM((SUBLANE, pw), F32),
                        pltpu.VMEM((n_heads, HEAD, HEAD), F32)]
        + [pltpu.VMEM((tc, width), BF16) for _ in range(5)]
        + [pltpu.VMEM((max(tc // CHUNK, SUBLANE), width), F32), pltpu.VMEM((tc, width), F32)],
        compiler_params=_cparams(2),
        name="rwkv_mix",
    )(x, gn, w, *params)


def _ssd_body(x_ref, gn_ref, w_ref, cw_ref, cb_ref, dtb_ref, alog_ref, dskip_ref, nw_ref, o_ref,
              p_ref, tail_ref, st_ref, y_ref, *, tc, width, state, groups):
    step = pl.program_id(1)
    n_chunks = tc // CHUNK
    pw = p_ref.shape[1]
    tw = 4 * MXU_COLS
    tiles = [slice(c0, min(c0 + tw, pw)) for c0 in range(0, pw, tw)]
    n_tiles = len(tiles)

    def project_tile(h, k):
        p_ref[:, tiles[k]] = _dot(h, w_ref[:, tiles[k]])

    @pl.when(step == 0)
    def _():
        st_ref[...] = jnp.zeros_like(st_ref)
        tail_ref[...] = jnp.zeros_like(tail_ref)
        h = _rms(x_ref[...], gn_ref[...]).astype(BF16)
        for k in range(n_tiles):
            project_tile(h, k)

    @pl.when(step > 0)
    def _():
        _ssd_mix_block(x_ref, gn_ref, project_tile, n_tiles, cw_ref, cb_ref, dtb_ref, alog_ref, dskip_ref, nw_ref,
                       o_ref, p_ref, tail_ref, st_ref, y_ref, tc=tc, width=width, state=state, groups=groups)


def _ssd_mix_block(x_ref, gn_ref, project_tile, n_tiles, cw_ref, cb_ref, dtb_ref, alog_ref, dskip_ref, nw_ref,
                   o_ref, p_ref, tail_ref, st_ref, y_ref, *, tc, width, state, groups):
    n_chunks = tc // CHUNK
    gw = width // groups
    hpg = gw // HEAD
    conv_ch = width + 2 * groups * state
    n_tap = cw_ref.shape[0]

    p = p_ref[...]
    zs = _silu(p[:, 0:width])
    xbc_raw = p[:, width:width + conv_ch]
    dt_raw = p[:, width + conv_ch:width + conv_ch + LANE]
    h_next = _rms(x_ref[...], gn_ref[...]).astype(BF16)
    tiles_after_chunk = {c: [k for k in range(1, n_tiles) if (k - 1) % n_chunks == c] for c in range(n_chunks)}

    prev = tail_ref[...]
    tail_ref[...] = xbc_raw[tc - SUBLANE:tc, :]
    xcat = jnp.concatenate([prev, xbc_raw], axis=0)
    conv = cb_ref[...]
    for j in range(n_tap):
        off = SUBLANE - (n_tap - 1) + j
        conv = conv + cw_ref[j:j + 1, :] * xcat[off:off + tc, :]
    xbc = _silu(conv)
    xs = xbc[:, 0:width]
    bm = xbc[:, width:width + groups * state].astype(BF16)
    cm = xbc[:, width + groups * state:conv_ch].astype(BF16)
    project_tile(h_next, 0)

    _, et = _seg_matrices(width, LANE)
    dt = _softplus(dt_raw + dtb_ref[...])
    da = -jnp.exp(alog_ref[...]) * dt
    cs_h = _dot_sel_l(_blockdiag_tril(tc), da)
    cs = _dot_sel_r(cs_h, et)
    dtx = _dot_sel_r(dt, et)
    xc = xs * dtx
    ecs = jnp.exp(cs)

    rr, cc = _iota((CHUNK, CHUNK), 0), _iota((CHUNK, CHUNK), 1)
    causal = cc <= rr

    for c in range(n_chunks):
        r0 = c * CHUNK
        rows = slice(r0, r0 + CHUNK)
        cs_c = cs[rows, :]
        cs_last = cs_c[CHUNK - 1:CHUNK, :]
        xc_c = xc[rows, :]
        xd = (xc_c * jnp.exp(cs_last - cs_c)).astype(BF16)
        xcb = xc_c.astype(BF16)
        cs_t = cs_h[rows, :].T
        for g in range(groups):
            gl = slice(g * gw, (g + 1) * gw)
            b_g = bm[rows, g * state:(g + 1) * state]
            c_g = cm[rows, g * state:(g + 1) * state]
            cb = _dot_nt(c_g, b_g)
            st = st_ref[g]
            y_off = _dot(c_g, st.astype(BF16)) * ecs[rows, gl]
            for hh in range(hpg):
                h = g * hpg + hh
                hs = slice(h * HEAD, (h + 1) * HEAD)
                seg = cs_c[:, hs] - cs_t[h:h + 1, :]
                lmat = jnp.where(causal, jnp.exp(jnp.where(causal, seg, 0.0)), 0.0)
                y_ref[rows, hs] = _dot((cb * lmat).astype(BF16), xcb[:, hs]) + y_off[:, hh * HEAD:(hh + 1) * HEAD]
            st_ref[g] = st * ecs[r0 + CHUNK - 1:r0 + CHUNK, gl] + _dot_tn(b_g, xd[:, gl])
        for k in tiles_after_chunk[c]:
            project_tile(h_next, k)

    y = (y_ref[...] + dskip_ref[...] * xs) * zs
    for g in range(groups):
        gl = slice(g * gw, (g + 1) * gw)
        yg = y[:, gl]
        ms = jnp.mean(yg * yg, axis=-1, keepdims=True)
        o_ref[:, gl] = (yg * lax.rsqrt(ms + NORM_EPS) * nw_ref[:, gl]).astype(o_ref.dtype)


def _ssd_mix(x, gn, w, conv_w, conv_b, dt_bias, a_log, d_skip, norm_w, batch, seq, tc, width, state, groups):
    n = x.shape[0]
    pw = w.shape[1]
    conv_ch = width + 2 * groups * state
    params = (conv_w, conv_b, dt_bias, a_log, d_skip, norm_w)
    grid, in_specs, out_spec = _mixer_specs(x, gn, w, pw, params, batch, seq, tc, width)
    return pl.pallas_call(
        functools.partial(_ssd_body, tc=tc, width=width, state=state, groups=groups),
        grid=grid,
        in_specs=in_specs,
        out_specs=out_spec,
        out_shape=jax.ShapeDtypeStruct((n, width), BF16),
        scratch_shapes=[pltpu.VMEM((tc, pw), F32), pltpu.VMEM((SUBLANE, conv_ch), F32),
                        pltpu.VMEM((groups, state, width // groups), F32), pltpu.VMEM((tc, width), F32)],
        compiler_params=_cparams(2),
        name="ssd_mix",
    )(x, gn, w, *params)


PAIR = 2 * CHUNK
BAND = (LEFT_CHUNKS + 1) * CHUNK
PAIR_BAND = BAND + CHUNK


def _bias_table_body(rb_ref, o_ref, *, n_bucket, rel_future):
    ext_w = PAIR_BAND + LANE
    m = _iota((n_bucket, ext_w), 1)
    bucket = jnp.clip(PAIR_BAND - 1 - m, -rel_future, REL_PAST_CLIP) + rel_future
    sel = (bucket == _iota((n_bucket, ext_w), 0)).astype(BF16)
    ext = _dot_sel_r(rb_ref[...], sel) * LOG2E
    for r in range(PAIR):
        o_ref[r] = ext[:, PAIR - 1 - r:PAIR - 1 - r + PAIR_BAND]


def _bias_table(rel_bias):
    n_heads, n_bucket = rel_bias.shape
    out = pl.pallas_call(
        functools.partial(_bias_table_body, n_bucket=n_bucket, rel_future=CHUNK - 1),
        out_shape=jax.ShapeDtypeStruct((PAIR, n_heads, PAIR_BAND), F32),
        name="bias_table",
    )(rel_bias)
    return jnp.transpose(out, (1, 0, 2))


def _attn_body(q_ref, kp_ref, kc_ref, vp_ref, vc_ref, b_ref, o_ref, *, tq, dh):
    i = pl.program_id(2)
    n_heads = q_ref.shape[1] // dh
    row, col = _iota((PAIR, PAIR_BAND), 0), _iota((PAIR, PAIR_BAND), 1)
    lo = (row // CHUNK) * CHUNK
    in_band = (col >= lo) & (col < lo + BAND)
    n_pairs = tq // PAIR
    k0 = [tq - LEFT_CHUNKS * CHUNK + jp * PAIR for jp in range(n_pairs)]
    work = [(h, jp) for h in range(n_heads) for jp in range(n_pairs)]

    def attend(first_block):
        hcols = [slice(h * dh, (h + 1) * dh) for h in range(n_heads)]
        kcat = [jnp.concatenate([kp_ref[:, hc], kc_ref[:, hc]], axis=0) for hc in hcols]
        vcat = [jnp.concatenate([vp_ref[:, hc], vc_ref[:, hc]], axis=0) for hc in hcols]
        s = [_dot_nt(q_ref[jp * PAIR:(jp + 1) * PAIR, hcols[h]], kcat[h][k0[jp]:k0[jp] + PAIR_BAND, :]) + b_ref[h]
             for h, jp in work]
        if first_block:
            valid = [in_band & (col + k0[jp] >= tq) for jp in range(n_pairs)]
            s = [jnp.where(valid[jp], s_, -1e30) for (h, jp), s_ in zip(work, s)]
        else:
            edge = PAIR_BAND - LANE
            s = [jnp.concatenate([jnp.where(in_band[:, :LANE], s_[:, :LANE], -1e30), s_[:, LANE:edge],
                                  jnp.where(in_band[:, edge:], s_[:, edge:], -1e30)], axis=1) for s_ in s]
        mx = [jnp.max(s_, axis=-1, keepdims=True) for s_ in s]
        pexp = [jnp.exp2(s_ - m_) for s_, m_ in zip(s, mx)]
        den = [jnp.sum(p_, axis=-1, keepdims=True) for p_ in pexp]
        o = [_dot(p_.astype(BF16), vcat[h][k0[jp]:k0[jp] + PAIR_BAND, :]) for (h, jp), p_ in zip(work, pexp)]
        for (h, jp), o_, d_ in zip(work, o, den):
            o_ref[jp * PAIR:(jp + 1) * PAIR, hcols[h]] = (o_ / d_).astype(o_ref.dtype)

    @pl.when(i == 0)
    def _():
        attend(True)

    @pl.when(i > 0)
    def _():
        attend(False)


def _band_attention(qkv, bias, batch, seq, n_heads, tq, hp):
    n, three_d = qkv.shape
    d = three_d // 3
    dh = d // n_heads
    nt = seq // tq
    ng = n_heads // hp
    cur = lambda off: (lambda b, h, i: (b * nt + i, off * ng + h))
    prv = lambda off: (lambda b, h, i: (b * nt + jnp.maximum(i - 1, 0), off * ng + h))
    blk = lambda f: pl.BlockSpec((tq, hp * dh), f)
    return pl.pallas_call(
        functools.partial(_attn_body, tq=tq, dh=dh),
        grid=(batch, ng, nt),
        in_specs=[blk(cur(0)), blk(prv(1)), blk(cur(1)), blk(prv(2)), blk(cur(2)),
                  pl.BlockSpec((hp, PAIR, PAIR_BAND), lambda b, h, i: (h, 0, 0))],
        out_specs=pl.BlockSpec((tq, hp * dh), lambda b, h, i: (b * nt + i, h)),
        out_shape=jax.ShapeDtypeStruct((n, d), BF16),
        compiler_params=_cparams(3),
        name="band_attention",
    )(qkv, qkv, qkv, qkv, qkv, bias)


def _pad_cols(a, width):
    return jnp.pad(a, ((0, 0), (0, width - a.shape[1])))


def _pad_rows(a, rows):
    return jnp.pad(a, ((0, rows - a.shape[0]), (0, 0)))


def _row(a):
    return a.reshape(1, -1)


def _lora_window(w, off):
    rank = w.shape[0]
    lo, hi = off // LANE * LANE, -(-(off + rank) // LANE) * LANE
    return jnp.pad(w, ((off - lo, hi - off - rank), (0, 0))).astype(BF16), (lo, hi)


class _Tiles(NamedTuple):
    rows: int
    proj_rows: int
    mix_rows: int
    attn_rows: int
    attn_heads: int
    qkv_cols: int
    ffn_cols: int


def _tiles(n, seq):
    rows = min(1024, n)
    return _Tiles(rows=rows, proj_rows=rows, mix_rows=min(4 * CHUNK, seq),
                  attn_rows=min(LEFT_CHUNKS * CHUNK, seq), attn_heads=8, qkv_cols=2048, ffn_cols=512)


def kernel(x, norm_g, w_in_ab, rwkv_mu, rwkv_w0, rwkv_w2, rwkv_a0, rwkv_a2, rwkv_g2, rwkv_k_k, rwkv_k_a, rwkv_r_k, rwkv_ln_w, rwkv_ln_b, ssm_conv_w, ssm_conv_b, ssm_dt_bias, ssm_A_log, ssm_D, ssm_norm_w, w_out_ab, w_qkv, attn_rel_bias, w_out_c, ffn_w_gate, ffn_w_up, ffn_w_down):
    batch, seq, d = x.shape
    n = batch * seq
    depth = norm_g.shape[0]
    xf = x.reshape(n, d)

    rw = rwkv_w0.shape[1]
    lora_w, lora_a, lora_g = rwkv_w2.shape[1], rwkv_a2.shape[1], rwkv_g2.shape[1]
    sw = ssm_norm_w.shape[1]
    n_ssm_heads = ssm_A_log.shape[1]
    conv_ch = ssm_conv_w.shape[2]
    groups = 2
    state = (conv_ch - sw) // (2 * groups)
    n_att_heads = attn_rel_bias.shape[1]
    rwkv_proj = 3 * rw + lora_w + lora_a + lora_g
    t = _tiles(n, seq)

    for l in range(depth):
        i = l // 2
        g = norm_g[l]
        if l % 2 == 0:
            w_in = w_in_ab[i].astype(BF16)
            pw_r = -(-rwkv_proj // MXU_COLS) * MXU_COLS
            mu = _pad_cols(_row(rwkv_mu[i]), pw_r)
            lora_w2, win_w = _lora_window(rwkv_w2[i], 0)
            lora_a2, win_a = _lora_window(rwkv_a2[i], lora_w)
            lora_g2, win_g = _lora_window(rwkv_g2[i], lora_w + lora_a)
            o4 = rwkv_proj + sw + conv_ch
            w_s = jnp.concatenate([w_in[:, rwkv_proj:o4], _pad_cols(w_in[:, o4:], LANE)], axis=1)

            y_a = _rwkv_mix(
                xf, g[0:1], w_in, mu, _row(rwkv_w0[i]), lora_w2, _row(rwkv_a0[i]), lora_a2, lora_g2, _row(rwkv_k_k[i]),
                _row(rwkv_k_a[i]), _row(rwkv_r_k[i]), _row(rwkv_ln_w[i]), _row(rwkv_ln_b[i]), batch, seq, t.mix_rows,
                (win_w, win_a, win_g))
            y_b = _ssd_mix(
                xf, g[0:1], w_s, ssm_conv_w[i], _row(ssm_conv_b[i]), _pad_cols(_row(ssm_dt_bias[i]), LANE),
                _pad_cols(_row(ssm_A_log[i]), LANE), _row(jnp.repeat(ssm_D[i], sw // n_ssm_heads)),
                _row(ssm_norm_w[i]), batch, seq, t.mix_rows, sw, state, groups)
            w_o = w_out_ab[i].astype(BF16)
            xf = _proj_res([y_a, y_b], [w_o[:rw], w_o[rw:]], xf, g[1:2], t.proj_rows)
        else:
            qkv = _norm_matmul(xf, g[0:1], w_qkv[i].astype(BF16), BF16, t.rows, t.qkv_cols, n_scaled=d // t.qkv_cols,
                               scale=(d // n_att_heads) ** -0.5 * LOG2E)
            bias = _bias_table(attn_rel_bias[i])
            att = _band_attention(qkv, bias, batch, seq, n_att_heads, t.attn_rows, t.attn_heads)
            xf = _proj_res([att], [w_out_c[i].astype(BF16)], xf, g[1:2], t.proj_rows)
        xf = _ffn(xf, g[2:3], g[3:4], ffn_w_gate[l].astype(BF16), ffn_w_up[l].astype(BF16),
                  ffn_w_down[l].astype(BF16), t.rows, t.ffn_cols)
    return xf.reshape(batch, seq, d)
```

```python
import functools
import math
from typing import NamedTuple

import jax
import jax.numpy as jnp
from jax import lax
from jax.experimental import pallas as pl
from jax.experimental.pallas import tpu as pltpu

F32 = jnp.float32
BF16 = jnp.bfloat16

NORM_EPS = 1e-6
RWKV_LN_EPS = 64e-5
CHUNK = 64
HEAD = 64
LANE = 128
SUBLANE = 8
MXU_COLS = 256
EDGE_ROWS = 256
N_PROJ_TILES = 7
LEFT_CHUNKS = 8
REL_PAST_CLIP = 256
LOG2E = math.log2(math.e)
VMEM_LIMIT = 60 * 1024 * 1024


def _cparams(n_axes):
    return pltpu.CompilerParams(dimension_semantics=("arbitrary",) * n_axes, vmem_limit_bytes=VMEM_LIMIT)


def _dot(a, b):
    return jnp.dot(a, b, preferred_element_type=F32)


def _dot_nt(a, b):
    return lax.dot_general(a, b, (((1,), (1,)), ((), ())), preferred_element_type=F32)


def _dot_tn(a, b):
    return lax.dot_general(a, b, (((0,), (0,)), ((), ())), preferred_element_type=F32)


def _split2(t):
    hi = t.astype(BF16)
    lo = (t - hi.astype(F32)).astype(BF16)
    return hi, lo


def _dot_sel_r(t, sel):
    hi, lo = _split2(t)
    return _dot(hi, sel) + _dot(lo, sel)


def _dot_sel_l(sel, t):
    hi, lo = _split2(t)
    return _dot(sel, hi) + _dot(sel, lo)


def _rms(x, g):
    ms = jnp.mean(x * x, axis=-1, keepdims=True)
    return x * lax.rsqrt(ms + NORM_EPS) * g


def _sigmoid(x):
    return 1.0 / (1.0 + jnp.exp(-x))


def _silu(x):
    return x * _sigmoid(x)


def _softplus(x):
    return jnp.maximum(x, 0.0) + jnp.log1p(jnp.exp(-jnp.abs(x)))


def _iota(shape, axis):
    return lax.broadcasted_iota(jnp.int32, shape, axis)


def _norm_matmul_body(x_ref, g_ref, w_ref, o_ref, h_ref, *, n_scaled, scale):
    j = pl.program_id(1)
    mult = jnp.where(j < n_scaled, jnp.float32(scale), jnp.float32(1.0))

    @pl.when(j == 0)
    def _():
        for r in range(0, x_ref.shape[0], EDGE_ROWS):
            rs = slice(r, r + EDGE_ROWS)
            h = _rms(x_ref[rs, :], g_ref[...]).astype(BF16)
            h_ref[rs, :] = h
            o_ref[rs, :] = (_dot(h, w_ref[...]) * mult).astype(o_ref.dtype)

    @pl.when(j > 0)
    def _():
        o_ref[...] = (_dot(h_ref[...], w_ref[...]) * mult).astype(o_ref.dtype)


def _norm_matmul(x, g, w, out_dtype, tm, tn, n_scaled=0, scale=1.0):
    n, d = x.shape
    m = w.shape[1]
    return pl.pallas_call(
        functools.partial(_norm_matmul_body, n_scaled=n_scaled, scale=scale),
        grid=(n // tm, m // tn),
        in_specs=[
            pl.BlockSpec((tm, d), lambda i, j: (i, 0)),
            pl.BlockSpec((1, d), lambda i, j: (0, 0)),
            pl.BlockSpec((d, tn), lambda i, j: (0, j)),
        ],
        out_specs=pl.BlockSpec((tm, tn), lambda i, j: (i, j)),
        out_shape=jax.ShapeDtypeStruct((n, m), out_dtype),
        scratch_shapes=[pltpu.VMEM((tm, d), BF16)],
        compiler_params=_cparams(2),
        name="norm_matmul",
    )(x, g, w)


def _proj_res_body(*refs, n_in):
    y_refs, w_refs = refs[:n_in], refs[n_in:2 * n_in]
    x_ref, g_ref, o_ref = refs[2 * n_in:]
    m = _dot(y_refs[0][...], w_refs[0][...])
    for y_ref, w_ref in zip(y_refs[1:], w_refs[1:]):
        m = m + _dot(y_ref[...], w_ref[...])
    o_ref[...] = x_ref[...] + _rms(m, g_ref[...])


def _proj_res(ys, w, x, g, tm):
    n, d = x.shape
    n_in = len(ys)
    k = ys[0].shape[1]
    assert all(y.shape[1] == k for y in ys) and w.shape[0] == n_in * k
    in_specs = [pl.BlockSpec((tm, k), lambda i: (i, 0)) for y in ys]
    in_specs += [pl.BlockSpec((k, d), functools.partial(lambda i, r: (r, 0), r=r), pipeline_mode=pl.Buffered(1))
                 for r in range(n_in)]
    in_specs += [pl.BlockSpec((tm, d), lambda i: (i, 0)), pl.BlockSpec((1, d), lambda i: (0, 0))]
    return pl.pallas_call(
        functools.partial(_proj_res_body, n_in=n_in),
        grid=(n // tm,),
        in_specs=in_specs,
        out_specs=pl.BlockSpec((tm, d), lambda i: (i, 0)),
        out_shape=jax.ShapeDtypeStruct((n, d), F32),
        compiler_params=_cparams(1),
        name="proj_res",
    )(*ys, *([w] * n_in), x, g)


def _ffn_body(x_ref, gi_ref, go_ref, wg_ref, wu_ref, wd_ref, o_ref, h_ref):
    j = pl.program_id(1)
    last_j = pl.num_programs(1) - 1
    tm = x_ref.shape[0]

    def partial_out(h):
        a = (_silu(_dot(h, wg_ref[...])) * _dot(h, wu_ref[...])).astype(BF16)
        return _dot(a, wd_ref[...])

    row_groups = [slice(r, r + EDGE_ROWS) for r in range(0, tm, EDGE_ROWS)]

    @pl.when(j == 0)
    def _():
        for rs in row_groups:
            h = _rms(x_ref[rs, :], gi_ref[...]).astype(BF16)
            h_ref[rs, :] = h
            o_ref[rs, :] = partial_out(h)

    @pl.when((j > 0) & (j < last_j))
    def _():
        o_ref[...] += partial_out(h_ref[...])

    @pl.when(j == last_j)
    def _():
        for rs in row_groups:
            acc = o_ref[rs, :] + partial_out(h_ref[rs, :])
            o_ref[rs, :] = x_ref[rs, :] + _rms(acc, go_ref[...])


def _ffn(x, g_in, g_out, wg, wu, wd, layer, tm, tf):
    n, d = x.shape
    f = wg.shape[2]
    assert f // tf >= 2 and tm % EDGE_ROWS == 0
    return pl.pallas_call(
        _ffn_body,
        grid=(n // tm, f // tf),
        in_specs=[
            pl.BlockSpec((tm, d), lambda i, j: (i, 0), pipeline_mode=pl.Buffered(1)),
            pl.BlockSpec((1, d), lambda i, j: (0, 0)),
            pl.BlockSpec((1, d), lambda i, j: (0, 0)),
            pl.BlockSpec((None, d, tf), lambda i, j: (layer, 0, j)),
            pl.BlockSpec((None, d, tf), lambda i, j: (layer, 0, j)),
            pl.BlockSpec((None, tf, d), lambda i, j: (layer, j, 0)),
        ],
        out_specs=pl.BlockSpec((tm, d), lambda i, j: (i, 0)),
        out_shape=jax.ShapeDtypeStruct((n, d), F32),
        scratch_shapes=[pltpu.VMEM((tm, d), BF16)],
        compiler_params=_cparams(2),
        name="ffn",
    )(x, g_in, g_out, wg, wu, wd)


def _seg_matrices(width, n_seg_pad):
    e = (_iota((width, n_seg_pad), 0) // HEAD == _iota((width, n_seg_pad), 1)).astype(BF16)
    et = (_iota((n_seg_pad, width), 1) // HEAD == _iota((n_seg_pad, width), 0)).astype(BF16)
    return e, et


def _blockdiag_tril(tc):
    r, c = _iota((tc, tc), 0), _iota((tc, tc), 1)
    return ((r // CHUNK == c // CHUNK) & (c <= r)).astype(BF16)


def _rwkv_body(x_ref, gn_ref, w_ref, mu_ref, w0_ref, w2_ref, a0_ref, a2_ref, g2_ref, kk_ref, ka_ref, rk_ref,
               lnw_ref, lnb_ref, o_ref, p_ref, prow_ref, s_ref, at_ref, rt_ref, bt_ref, kt_ref, v_ref, gl_ref, y_ref,
               *, tc, width, lora_win):
    step = pl.program_id(1)

    tw = p_ref.shape[1] // N_PROJ_TILES
    assert tw % MXU_COLS == 0

    def h_next_fn():
        return _rms(x_ref[...], gn_ref[...]).astype(BF16)

    def project_tile(h, k):
        p_ref[:, k * tw:(k + 1) * tw] = _dot(h, w_ref[:, k * tw:(k + 1) * tw])

    @pl.when(step == 0)
    def _():
        s_ref[...] = jnp.zeros_like(s_ref)
        prow_ref[...] = jnp.zeros_like(prow_ref)
        h = h_next_fn()
        for k in range(N_PROJ_TILES):
            project_tile(h, k)

    @pl.when(step > 0)
    def _():
        _rwkv_mix_block(h_next_fn, project_tile, mu_ref, w0_ref, w2_ref, a0_ref, a2_ref, g2_ref, kk_ref, ka_ref, rk_ref, lnw_ref,
                        lnb_ref, o_ref, p_ref, prow_ref, s_ref, at_ref, rt_ref, bt_ref, kt_ref, v_ref, gl_ref, y_ref,
                        tc=tc, width=width, lora_win=lora_win)


def _rwkv_mix_block(h_next_fn, project_tile, mu_ref, w0_ref, w2_ref, a0_ref, a2_ref, g2_ref, kk_ref, ka_ref, rk_ref, lnw_ref,
                    lnb_ref, o_ref, p_ref, prow_ref, s_ref, at_ref, rt_ref, bt_ref, kt_ref, v_ref, gl_ref, y_ref,
                    *, tc, width, lora_win):
    n_heads = width // HEAD
    n_chunks = tc // CHUNK

    p = p_ref[...]
    first_row = _iota((tc, 1), 0) == 0

    def lerp(cols):
        pc = p[:, cols]
        psh = jnp.where(first_row, prow_ref[0:1, cols], pltpu.roll(pc, 1, 0))
        return pc + (psh - pc) * mu_ref[:, cols]

    o1 = 3 * width
    xl = lerp(slice(o1, p.shape[1]))
    wc, ac, gc = (xl[:, lo:hi] for lo, hi in lora_win)
    z = w0_ref[...] + _dot(jnp.tanh(wc).astype(BF16), w2_ref[...])
    a = _sigmoid(a0_ref[...] + _dot(ac.astype(BF16), a2_ref[...]))
    g = _dot(_sigmoid(gc).astype(BF16), g2_ref[...])
    lw = -jnp.exp(-_softplus(-z) - 0.5)

    h_next = h_next_fn()
    project_tile(h_next, 0)
    project_tile(h_next, 1)
    project_tile(h_next, 2)

    xr = lerp(slice(0, o1))
    last_row = p[tc - 1:tc, :]
    r = xr[:, 0:width]
    k = xr[:, width:2 * width]
    v = xr[:, 2 * width:3 * width]

    mxu_w = 2 * LANE
    same_head = (_iota((mxu_w, mxu_w), 0) // HEAD == _iota((mxu_w, mxu_w), 1) // HEAD).astype(BF16)

    def seg_sum(t):
        return jnp.concatenate(
            [_dot_sel_r(t[:, j:j + mxu_w], same_head) for j in range(0, width, mxu_w)], axis=1)

    kk = k * kk_ref[...]
    kk = kk * lax.rsqrt(jnp.maximum(seg_sum(kk * kk), 1e-24))
    k2 = k * (1.0 + (a - 1.0) * ka_ref[...])
    project_tile(h_next, 3)
    project_tile(h_next, 4)
    bonus = seg_sum(r * k2 * rk_ref[...]) * v
    cs = _dot_sel_l(_blockdiag_tril(tc), lw)
    project_tile(h_next, 5)
    project_tile(h_next, 6)
    prow_ref[0:1, :] = last_row

    ginv = jnp.exp(-cs)
    at_ref[...] = (-kk * jnp.exp(cs - lw)).astype(BF16)
    rt_ref[...] = (r * jnp.exp(cs)).astype(BF16)
    bt_ref[...] = (kk * a * ginv).astype(BF16)
    kt_ref[...] = (k2 * ginv).astype(BF16)
    v_ref[...] = v.astype(BF16)
    for c in range(n_chunks):
        last = c * CHUNK + CHUNK - 1
        gl_ref[c:c + 1, :] = jnp.exp(cs[last:last + 1, :])

    rr, cc = _iota((CHUNK, CHUNK), 0), _iota((CHUNK, CHUNK), 1)
    rr2, cc2 = _iota((CHUNK, 2 * CHUNK), 0), _iota((CHUNK, 2 * CHUNK), 1) % CHUNK
    strict2 = (cc2 < rr2).astype(F32)
    incl2 = (cc2 <= rr2).astype(F32)
    eye = (cc == rr).astype(F32)
    n_lvl = int(math.log2(CHUNK))
    lvl_masks = []
    for lv in range(n_lvl):
        same = (rr >> (lv + 1)) == (cc >> (lv + 1))
        lvl_masks.append((same & (((rr >> lv) & 1) == 1) & (((cc >> lv) & 1) == 0)).astype(F32))

    def chunk_step(c, carry):
        t0 = pl.multiple_of(c * CHUNK, CHUNK)
        rows = pl.ds(t0, CHUNK)
        gl = gl_ref[pl.ds(c, 1), :]
        heads = range(n_heads)
        hsl = [slice(h * HEAD, (h + 1) * HEAD) for h in heads]
        vh = [v_ref[rows, hs] for hs in hsl]
        ar = [jnp.concatenate([at_ref[rows, hs], rt_ref[rows, hs]], axis=0) for hs in hsl]
        bk = [jnp.concatenate([bt_ref[rows, hs], kt_ref[rows, hs]], axis=0) for hs in hsl]
        pm = [_dot_nt(ar[h], bk[h]) for h in heads]
        s0 = [s_ref[h] for h in heads]
        ars = [_dot_nt(ar[h], s0[h].astype(BF16)) for h in heads]
        pa = [pm[h][:CHUNK, :] * strict2 for h in heads]
        aab = [pa[h][:, :CHUNK] for h in heads]
        rhs = [ars[h][:CHUNK, :] + _dot(pa[h][:, CHUNK:].astype(BF16), vh[h]) for h in heads]
        x = [eye + aab[h] * lvl_masks[0] for h in heads]
        for lv in range(1, n_lvl):
            xb = [x[h].astype(BF16) for h in heads]
            tm_ = [_dot((aab[h] * lvl_masks[lv]).astype(BF16), xb[h]) for h in heads]
            x = [x[h] + _dot(xb[h], tm_[h].astype(BF16)) for h in heads]
        u = [_dot(x[h].astype(BF16), rhs[h].astype(BF16)) for h in heads]
        uv = [jnp.concatenate([u[h].astype(BF16), vh[h]], axis=0) for h in heads]
        for h in heads:
            pr = (pm[h][CHUNK:, :] * incl2).astype(BF16)
            y_ref[rows, hsl[h]] = ars[h][CHUNK:, :] + _dot(pr, uv[h])
        for h in heads:
            s_ref[h] = (s0[h] + _dot_tn(uv[h], bk[h])) * gl[:, hsl[h]]
        return carry

    lax.fori_loop(0, n_chunks, chunk_step, 0)

    y = y_ref[...]
    inv_n = 1.0 / HEAD
    mean = seg_sum(y) * inv_n
    yc = y - mean
    var = seg_sum(yc * yc) * inv_n
    yn = yc * lax.rsqrt(var + RWKV_LN_EPS) * lnw_ref[...] + lnb_ref[...]
    o_ref[...] = ((yn + bonus) * g).astype(o_ref.dtype)


def _mixer_specs(x, gn, w, pw, params, batch, seq, tc, width):
    nt = seq // tc
    d = x.shape[1]
    const = lambda b, s: (0, 0)
    in_specs = [
        pl.BlockSpec((tc, d), lambda b, s: (b * nt + jnp.minimum(s, nt - 1), 0)),
        pl.BlockSpec(gn.shape, const),
        pl.BlockSpec((w.shape[0], pw), const, pipeline_mode=pl.Buffered(1)),
    ] + [pl.BlockSpec(a.shape, const) for a in params]
    out_spec = pl.BlockSpec((tc, width), lambda b, s: (b * nt + jnp.maximum(s - 1, 0), 0))
    return (batch, nt + 1), in_specs, out_spec


def _rwkv_mix(x, gn, w, mu, w0, w2, a0, a2, g2, k_k, k_a, r_k, ln_w, ln_b, batch, seq, tc, lora_win):
    n = x.shape[0]
    pw = mu.shape[1]
    width = w0.shape[1]
    n_heads = width // HEAD
    params = (mu, w0, w2, a0, a2, g2, k_k, k_a, r_k, ln_w, ln_b)
    grid, in_specs, out_spec = _mixer_specs(x, gn, w, pw, params, batch, seq, tc, width)
    return pl.pallas_call(
        functools.partial(_rwkv_body, tc=tc, width=width, lora_win=lora_win),
        grid=grid,
        in_specs=in_specs,
        out_specs=out_spec,
        out_shape=jax.ShapeDtypeStruct((n, width), BF16),
        scratch_shapes=[pltpu.VMEM((tc, pw), F32), pltpu.VMEM((SUBLANE, pw), F32),
                        pltpu.VMEM((n_heads, HEAD, HEAD), F32)]
        + [pltpu.VMEM((tc, width), BF16) for _ in range(5)]
        + [pltpu.VMEM((max(tc // CHUNK, SUBLANE), width), F32), pltpu.VMEM((tc, width), F32)],
        compiler_params=_cparams(2),
        name="rwkv_mix",
    )(x, gn, w, *params)


def _ssd_body(x_ref, gn_ref, w_ref, cw_ref, cb_ref, dtb_ref, alog_ref, dskip_ref, nw_ref, o_ref,
              p_ref, tail_ref, st_ref, y_ref, *, tc, width, state, groups):
    step = pl.program_id(1)
    n_chunks = tc // CHUNK
    pw = p_ref.shape[1]
    tw = 4 * MXU_COLS
    tiles = [slice(c0, min(c0 + tw, pw)) for c0 in range(0, pw, tw)]
    n_tiles = len(tiles)

    def project_tile(h, k):
        p_ref[:, tiles[k]] = _dot(h, w_ref[:, tiles[k]])

    @pl.when(step == 0)
    def _():
        st_ref[...] = jnp.zeros_like(st_ref)
        tail_ref[...] = jnp.zeros_like(tail_ref)
        h = _rms(x_ref[...], gn_ref[...]).astype(BF16)
        for k in range(n_tiles):
            project_tile(h, k)

    @pl.when(step > 0)
    def _():
        _ssd_mix_block(x_ref, gn_ref, project_tile, n_tiles, cw_ref, cb_ref, dtb_ref, alog_ref, dskip_ref, nw_ref,
                       o_ref, p_ref, tail_ref, st_ref, y_ref, tc=tc, width=width, state=state, groups=groups)


def _ssd_mix_block(x_ref, gn_ref, project_tile, n_tiles, cw_ref, cb_ref, dtb_ref, alog_ref, dskip_ref, nw_ref,
                   o_ref, p_ref, tail_ref, st_ref, y_ref, *, tc, width, state, groups):
    n_chunks = tc // CHUNK
    gw = width // groups
    hpg = gw // HEAD
    conv_ch = width + 2 * groups * state
    n_tap = cw_ref.shape[0]

    p = p_ref[...]
    zs = _silu(p[:, 0:width])
    xbc_raw = p[:, width:width + conv_ch]
    dt_raw = p[:, width + conv_ch:width + conv_ch + LANE]
    h_next = _rms(x_ref[...], gn_ref[...]).astype(BF16)
    tiles_after_chunk = {c: [k for k in range(1, n_tiles) if (k - 1) % n_chunks == c] for c in range(n_chunks)}

    prev = tail_ref[...]
    tail_ref[...] = xbc_raw[tc - SUBLANE:tc, :]
    xcat = jnp.concatenate([prev, xbc_raw], axis=0)
    conv = cb_ref[...]
    for j in range(n_tap):
        off = SUBLANE - (n_tap - 1) + j
        conv = conv + cw_ref[j:j + 1, :] * xcat[off:off + tc, :]
    xbc = _silu(conv)
    xs = xbc[:, 0:width]
    bm = xbc[:, width:width + groups * state].astype(BF16)
    cm = xbc[:, width + groups * state:conv_ch].astype(BF16)
    project_tile(h_next, 0)

    _, et = _seg_matrices(width, LANE)
    dt = _softplus(dt_raw + dtb_ref[...])
    da = -jnp.exp(alog_ref[...]) * dt
    cs_h = _dot_sel_l(_blockdiag_tril(tc), da)
    cs = _dot_sel_r(cs_h, et)
    dtx = _dot_sel_r(dt, et)
    xc = xs * dtx
    ecs = jnp.exp(cs)

    rr, cc = _iota((CHUNK, CHUNK), 0), _iota((CHUNK, CHUNK), 1)
    causal = cc <= rr

    for c in range(n_chunks):
        r0 = c * CHUNK
        rows = slice(r0, r0 + CHUNK)
        cs_c = cs[rows, :]
        cs_last = cs_c[CHUNK - 1:CHUNK, :]
        xc_c = xc[rows, :]
        xd = (xc_c * jnp.exp(cs_last - cs_c)).astype(BF16)
        xcb = xc_c.astype(BF16)
        cs_t = cs_h[rows, :].T
        for g in range(groups):
            gl = slice(g * gw, (g + 1) * gw)
            b_g = bm[rows, g * state:(g + 1) * state]
            c_g = cm[rows, g * state:(g + 1) * state]
            cb = _dot_nt(c_g, b_g)
            st = st_ref[g]
            y_off = _dot(c_g, st.astype(BF16)) * ecs[rows, gl]
            for hh in range(hpg):
                h = g * hpg + hh
                hs = slice(h * HEAD, (h + 1) * HEAD)
                seg = cs_c[:, hs] - cs_t[h:h + 1, :]
                lmat = jnp.where(causal, jnp.exp(jnp.where(causal, seg, 0.0)), 0.0)
                y_ref[rows, hs] = _dot((cb * lmat).astype(BF16), xcb[:, hs]) + y_off[:, hh * HEAD:(hh + 1) * HEAD]
            st_ref[g] = st * ecs[r0 + CHUNK - 1:r0 + CHUNK, gl] + _dot_tn(b_g, xd[:, gl])
        for k in tiles_after_chunk[c]:
            project_tile(h_next, k)

    y = (y_ref[...] + dskip_ref[...] * xs) * zs
    for g in range(groups):
        gl = slice(g * gw, (g + 1) * gw)
        yg = y[:, gl]
        ms = jnp.mean(yg * yg, axis=-1, keepdims=True)
        o_ref[:, gl] = (yg * lax.rsqrt(ms + NORM_EPS) * nw_ref[:, gl]).astype(o_ref.dtype)


def _ssd_mix(x, gn, w, conv_w, conv_b, dt_bias, a_log, d_skip, norm_w, batch, seq, tc, width, state, groups):
    n = x.shape[0]
    pw = w.shape[1]
    conv_ch = width + 2 * groups * state
    params = (conv_w, conv_b, dt_bias, a_log, d_skip, norm_w)
    grid, in_specs, out_spec = _mixer_specs(x, gn, w, pw, params, batch, seq, tc, width)
    return pl.pallas_call(
        functools.partial(_ssd_body, tc=tc, width=width, state=state, groups=groups),
        grid=grid,
        in_specs=in_specs,
        out_specs=out_spec,
        out_shape=jax.ShapeDtypeStruct((n, width), BF16),
        scratch_shapes=[pltpu.VMEM((tc, pw), F32), pltpu.VMEM((SUBLANE, conv_ch), F32),
                        pltpu.VMEM((groups, state, width // groups), F32), pltpu.VMEM((tc, width), F32)],
        compiler_params=_cparams(2),
        name="ssd_mix",
    )(x, gn, w, *params)


PAIR = 2 * CHUNK
BAND = (LEFT_CHUNKS + 1) * CHUNK
PAIR_BAND = BAND + CHUNK


def _bias_table_body(rb_ref, o_ref, *, n_bucket, rel_future):
    ext_w = PAIR_BAND + LANE
    m = _iota((n_bucket, ext_w), 1)
    bucket = jnp.clip(PAIR_BAND - 1 - m, -rel_future, REL_PAST_CLIP) + rel_future
    sel = (bucket == _iota((n_bucket, ext_w), 0)).astype(BF16)
    ext = _dot_sel_r(rb_ref[...], sel) * LOG2E
    for r in range(PAIR):
        o_ref[r] = ext[:, PAIR - 1 - r:PAIR - 1 - r + PAIR_BAND]


def _bias_table(rel_bias):
    n_heads, n_bucket = rel_bias.shape
    out = pl.pallas_call(
        functools.partial(_bias_table_body, n_bucket=n_bucket, rel_future=CHUNK - 1),
        out_shape=jax.ShapeDtypeStruct((PAIR, n_heads, PAIR_BAND), F32),
        name="bias_table",
    )(rel_bias)
    return jnp.transpose(out, (1, 0, 2))


def _attn_body(q_ref, kp_ref, kc_ref, vp_ref, vc_ref, b_ref, o_ref, *, tq, dh):
    i = pl.program_id(2)
    n_heads = q_ref.shape[1] // dh
    row, col = _iota((PAIR, PAIR_BAND), 0), _iota((PAIR, PAIR_BAND), 1)
    lo = (row // CHUNK) * CHUNK
    in_band = (col >= lo) & (col < lo + BAND)
    n_pairs = tq // PAIR
    k0 = [tq - LEFT_CHUNKS * CHUNK + jp * PAIR for jp in range(n_pairs)]
    work = [(h, jp) for h in range(n_heads) for jp in range(n_pairs)]

    def attend(first_block):
        hcols = [slice(h * dh, (h + 1) * dh) for h in range(n_heads)]
        kcat = [jnp.concatenate([kp_ref[:, hc], kc_ref[:, hc]], axis=0) for hc in hcols]
        vcat = [jnp.concatenate([vp_ref[:, hc], vc_ref[:, hc]], axis=0) for hc in hcols]
        s = [_dot_nt(q_ref[jp * PAIR:(jp + 1) * PAIR, hcols[h]], kcat[h][k0[jp]:k0[jp] + PAIR_BAND, :]) + b_ref[h]
             for h, jp in work]
        if first_block:
            valid = [in_band & (col + k0[jp] >= tq) for jp in range(n_pairs)]
            s = [jnp.where(valid[jp], s_, -1e30) for (h, jp), s_ in zip(work, s)]
        else:
            edge = PAIR_BAND - LANE
            s = [jnp.concatenate([jnp.where(in_band[:, :LANE], s_[:, :LANE], -1e30), s_[:, LANE:edge],
                                  jnp.where(in_band[:, edge:], s_[:, edge:], -1e30)], axis=1) for s_ in s]
        mx = [jnp.max(s_, axis=-1, keepdims=True) for s_ in s]
        pexp = [jnp.exp2(s_ - m_) for s_, m_ in zip(s, mx)]
        den = [jnp.sum(p_, axis=-1, keepdims=True) for p_ in pexp]
        o = [_dot(p_.astype(BF16), vcat[h][k0[jp]:k0[jp] + PAIR_BAND, :]) for (h, jp), p_ in zip(work, pexp)]
        for (h, jp), o_, d_ in zip(work, o, den):
            o_ref[jp * PAIR:(jp + 1) * PAIR, hcols[h]] = (o_ / d_).astype(o_ref.dtype)

    @pl.when(i == 0)
    def _():
        attend(True)

    @pl.when(i > 0)
    def _():
        attend(False)


def _band_attention(qkv, bias, batch, seq, n_heads, tq, hp):
    n, three_d = qkv.shape
    d = three_d // 3
    dh = d // n_heads
    nt = seq // tq
    ng = n_heads // hp
    cur = lambda off: (lambda b, h, i: (b * nt + i, off * ng + h))
    prv = lambda off: (lambda b, h, i: (b * nt + jnp.maximum(i - 1, 0), off * ng + h))
    blk = lambda f: pl.BlockSpec((tq, hp * dh), f)
    return pl.pallas_call(
        functools.partial(_attn_body, tq=tq, dh=dh),
        grid=(batch, ng, nt),
        in_specs=[blk(cur(0)), blk(prv(1)), blk(cur(1)), blk(prv(2)), blk(cur(2)),
                  pl.BlockSpec((hp, PAIR, PAIR_BAND), lambda b, h, i: (h, 0, 0))],
        out_specs=pl.BlockSpec((tq, hp * dh), lambda b, h, i: (b * nt + i, h)),
        out_shape=jax.ShapeDtypeStruct((n, d), BF16),
        compiler_params=_cparams(3),
        name="band_attention",
    )(qkv, qkv, qkv, qkv, qkv, bias)


def _pad_cols(a, width):
    return jnp.pad(a, ((0, 0), (0, width - a.shape[1])))


def _pad_rows(a, rows):
    return jnp.pad(a, ((0, rows - a.shape[0]), (0, 0)))


def _row(a):
    return a.reshape(1, -1)


def _lora_window(w, off):
    rank = w.shape[0]
    lo, hi = off // LANE * LANE, -(-(off + rank) // LANE) * LANE
    return jnp.pad(w, ((off - lo, hi - off - rank), (0, 0))).astype(BF16), (lo, hi)


class _Tiles(NamedTuple):
    rows: int
    proj_rows: int
    mix_rows: int
    attn_rows: int
    attn_heads: int
    qkv_cols: int
    ffn_cols: int


def _tiles(n, seq):
    rows = min(1024, n)
    return _Tiles(rows=rows, proj_rows=rows, mix_rows=min(4 * CHUNK, seq),
                  attn_rows=min(LEFT_CHUNKS * CHUNK, seq), attn_heads=8, qkv_cols=2048, ffn_cols=512)


def kernel(x, norm_g, w_in_ab, rwkv_mu, rwkv_w0, rwkv_w2, rwkv_a0, rwkv_a2, rwkv_g2, rwkv_k_k, rwkv_k_a, rwkv_r_k, rwkv_ln_w, rwkv_ln_b, ssm_conv_w, ssm_conv_b, ssm_dt_bias, ssm_A_log, ssm_D, ssm_norm_w, w_out_ab, w_qkv, attn_rel_bias, w_out_c, ffn_w_gate, ffn_w_up, ffn_w_down):
    batch, seq, d = x.shape
    n = batch * seq
    depth = norm_g.shape[0]
    xf = x.reshape(n, d)

    rw = rwkv_w0.shape[1]
    lora_w, lora_a, lora_g = rwkv_w2.shape[1], rwkv_a2.shape[1], rwkv_g2.shape[1]
    sw = ssm_norm_w.shape[1]
    n_ssm_heads = ssm_A_log.shape[1]
    conv_ch = ssm_conv_w.shape[2]
    groups = 2
    state = (conv_ch - sw) // (2 * groups)
    n_att_heads = attn_rel_bias.shape[1]
    rwkv_proj = 3 * rw + lora_w + lora_a + lora_g
    t = _tiles(n, seq)

    wg_all, wu_all, wd_all = ffn_w_gate.astype(BF16), ffn_w_up.astype(BF16), ffn_w_down.astype(BF16)
    for l in range(depth):
        i = l // 2
        g = norm_g[l]
        if l % 2 == 0:
            w_in = w_in_ab[i].astype(BF16)
            pw_r = -(-rwkv_proj // MXU_COLS) * MXU_COLS
            mu = _pad_cols(_row(rwkv_mu[i]), pw_r)
            lora_w2, win_w = _lora_window(rwkv_w2[i], 0)
            lora_a2, win_a = _lora_window(rwkv_a2[i], lora_w)
            lora_g2, win_g = _lora_window(rwkv_g2[i], lora_w + lora_a)
            o4 = rwkv_proj + sw + conv_ch
            w_s = jnp.concatenate([w_in[:, rwkv_proj:o4], _pad_cols(w_in[:, o4:], LANE)], axis=1)

            y_a = _rwkv_mix(
                xf, g[0:1], w_in, mu, _row(rwkv_w0[i]), lora_w2, _row(rwkv_a0[i]), lora_a2, lora_g2, _row(rwkv_k_k[i]),
                _row(rwkv_k_a[i]), _row(rwkv_r_k[i]), _row(rwkv_ln_w[i]), _row(rwkv_ln_b[i]), batch, seq, t.mix_rows,
                (win_w, win_a, win_g))
            y_b = _ssd_mix(
                xf, g[0:1], w_s, ssm_conv_w[i], _row(ssm_conv_b[i]), _pad_cols(_row(ssm_dt_bias[i]), LANE),
                _pad_cols(_row(ssm_A_log[i]), LANE), _row(jnp.repeat(ssm_D[i], sw // n_ssm_heads)),
                _row(ssm_norm_w[i]), batch, seq, t.mix_rows, sw, state, groups)
            w_o = w_out_ab[i].astype(BF16)
            xf = _proj_res([y_a, y_b], w_o, xf, g[1:2], t.proj_rows)
        else:
            qkv = _norm_matmul(xf, g[0:1], w_qkv[i].astype(BF16), BF16, t.rows, t.qkv_cols, n_scaled=d // t.qkv_cols,
                               scale=(d // n_att_heads) ** -0.5 * LOG2E)
            bias = _bias_table(attn_rel_bias[i])
            att = _band_attention(qkv, bias, batch, seq, n_att_heads, t.attn_rows, t.attn_heads)
            xf = _proj_res([att], w_out_c[i].astype(BF16), xf, g[1:2], t.proj_rows)
        xf = _ffn(xf, g[2:3], g[3:4], wg_all, wu_all, wd_all, l, t.rows, t.ffn_cols)
    return xf.reshape(batch, seq, d)
```

```python
import functools
import math
from typing import NamedTuple

import jax
import jax.numpy as jnp
from jax import lax
from jax.experimental import pallas as pl
from jax.experimental.pallas import tpu as pltpu

F32 = jnp.float32
BF16 = jnp.bfloat16

NORM_EPS = 1e-6
RWKV_LN_EPS = 64e-5
CHUNK = 64
HEAD = 64
LANE = 128
SUBLANE = 8
MXU_COLS = 256
EDGE_ROWS = 256
N_PROJ_TILES = 7
LEFT_CHUNKS = 8
REL_PAST_CLIP = 256
LOG2E = math.log2(math.e)
VMEM_LIMIT = 60 * 1024 * 1024


def _cparams(n_axes):
    return pltpu.CompilerParams(dimension_semantics=("arbitrary",) * n_axes, vmem_limit_bytes=VMEM_LIMIT)


def _dot(a, b):
    return jnp.dot(a, b, preferred_element_type=F32)


def _dot_nt(a, b):
    return lax.dot_general(a, b, (((1,), (1,)), ((), ())), preferred_element_type=F32)


def _dot_tn(a, b):
    return lax.dot_general(a, b, (((0,), (0,)), ((), ())), preferred_element_type=F32)


def _split2(t):
    hi = t.astype(BF16)
    lo = (t - hi.astype(F32)).astype(BF16)
    return hi, lo


def _dot_sel_r(t, sel):
    hi, lo = _split2(t)
    return _dot(hi, sel) + _dot(lo, sel)


def _dot_sel_l(sel, t):
    hi, lo = _split2(t)
    return _dot(sel, hi) + _dot(sel, lo)


def _rms(x, g):
    ms = jnp.mean(x * x, axis=-1, keepdims=True)
    return x * lax.rsqrt(ms + NORM_EPS) * g


def _sigmoid(x):
    return 1.0 / (1.0 + jnp.exp(-x))


def _silu(x):
    return x * _sigmoid(x)


def _softplus(x):
    return jnp.maximum(x, 0.0) + jnp.log1p(jnp.exp(-jnp.abs(x)))


def _iota(shape, axis):
    return lax.broadcasted_iota(jnp.int32, shape, axis)


def _norm_matmul_body(x_ref, g_ref, w_ref, o_ref, h_ref, *, n_scaled, scale):
    j = pl.program_id(1)
    mult = jnp.where(j < n_scaled, jnp.float32(scale), jnp.float32(1.0))

    @pl.when(j == 0)
    def _():
        for r in range(0, x_ref.shape[0], EDGE_ROWS):
            rs = slice(r, r + EDGE_ROWS)
            h = _rms(x_ref[rs, :], g_ref[...]).astype(BF16)
            h_ref[rs, :] = h
            o_ref[rs, :] = (_dot(h, w_ref[...]) * mult).astype(o_ref.dtype)

    @pl.when(j > 0)
    def _():
        o_ref[...] = (_dot(h_ref[...], w_ref[...]) * mult).astype(o_ref.dtype)


def _norm_matmul(x, g, w, out_dtype, tm, tn, n_scaled=0, scale=1.0):
    n, d = x.shape
    m = w.shape[1]
    return pl.pallas_call(
        functools.partial(_norm_matmul_body, n_scaled=n_scaled, scale=scale),
        grid=(n // tm, m // tn),
        in_specs=[
            pl.BlockSpec((tm, d), lambda i, j: (i, 0)),
            pl.BlockSpec((1, d), lambda i, j: (0, 0)),
            pl.BlockSpec((d, tn), lambda i, j: (0, j)),
        ],
        out_specs=pl.BlockSpec((tm, tn), lambda i, j: (i, j)),
        out_shape=jax.ShapeDtypeStruct((n, m), out_dtype),
        scratch_shapes=[pltpu.VMEM((tm, d), BF16)],
        compiler_params=_cparams(2),
        name="norm_matmul",
    )(x, g, w)


def _proj_res_body(*refs, n_in):
    y_refs, w_refs = refs[:n_in], refs[n_in:2 * n_in]
    x_ref, g_ref, o_ref = refs[2 * n_in:]
    m = _dot(y_refs[0][...], w_refs[0][...])
    for y_ref, w_ref in zip(y_refs[1:], w_refs[1:]):
        m = m + _dot(y_ref[...], w_ref[...])
    o_ref[...] = x_ref[...] + _rms(m, g_ref[...])


def _proj_res(ys, w, x, g, tm):
    n, d = x.shape
    n_in = len(ys)
    k = ys[0].shape[1]
    assert all(y.shape[1] == k for y in ys) and w.shape[0] == n_in * k
    in_specs = [pl.BlockSpec((tm, k), lambda i: (i, 0)) for y in ys]
    in_specs += [pl.BlockSpec((k, d), functools.partial(lambda i, r: (r, 0), r=r), pipeline_mode=pl.Buffered(1))
                 for r in range(n_in)]
    in_specs += [pl.BlockSpec((tm, d), lambda i: (i, 0)), pl.BlockSpec((1, d), lambda i: (0, 0))]
    return pl.pallas_call(
        functools.partial(_proj_res_body, n_in=n_in),
        grid=(n // tm,),
        in_specs=in_specs,
        out_specs=pl.BlockSpec((tm, d), lambda i: (i, 0)),
        out_shape=jax.ShapeDtypeStruct((n, d), F32),
        compiler_params=_cparams(1),
        name="proj_res",
    )(*ys, *([w] * n_in), x, g)


def _ffn_body(x_ref, gi_ref, go_ref, wg_ref, wu_ref, wd_ref, o_ref, h_ref):
    j = pl.program_id(1)
    last_j = pl.num_programs(1) - 1
    tm = x_ref.shape[0]

    def partial_out(h):
        a = (_silu(_dot(h, wg_ref[...])) * _dot(h, wu_ref[...])).astype(BF16)
        return _dot(a, wd_ref[...])

    row_groups = [slice(r, r + EDGE_ROWS) for r in range(0, tm, EDGE_ROWS)]

    @pl.when(j == 0)
    def _():
        for rs in row_groups:
            h = _rms(x_ref[rs, :], gi_ref[...]).astype(BF16)
            h_ref[rs, :] = h
            o_ref[rs, :] = partial_out(h)

    @pl.when((j > 0) & (j < last_j))
    def _():
        o_ref[...] += partial_out(h_ref[...])

    @pl.when(j == last_j)
    def _():
        for rs in row_groups:
            acc = o_ref[rs, :] + partial_out(h_ref[rs, :])
            o_ref[rs, :] = x_ref[rs, :] + _rms(acc, go_ref[...])


def _ffn(x, g_in, g_out, wg, wu, wd, layer, tm, tf):
    n, d = x.shape
    f = wg.shape[2]
    assert f // tf >= 2 and tm % EDGE_ROWS == 0
    return pl.pallas_call(
        _ffn_body,
        grid=(n // tm, f // tf),
        in_specs=[
            pl.BlockSpec((tm, d), lambda i, j: (i, 0), pipeline_mode=pl.Buffered(1)),
            pl.BlockSpec((1, d), lambda i, j: (0, 0)),
            pl.BlockSpec((1, d), lambda i, j: (0, 0)),
            pl.BlockSpec((None, d, tf), lambda i, j: (layer, 0, j)),
            pl.BlockSpec((None, d, tf), lambda i, j: (layer, 0, j)),
            pl.BlockSpec((None, tf, d), lambda i, j: (layer, j, 0)),
        ],
        out_specs=pl.BlockSpec((tm, d), lambda i, j: (i, 0)),
        out_shape=jax.ShapeDtypeStruct((n, d), F32),
        scratch_shapes=[pltpu.VMEM((tm, d), BF16)],
        compiler_params=_cparams(2),
        name="ffn",
    )(x, g_in, g_out, wg, wu, wd)


def _seg_matrices(width, n_seg_pad):
    e = (_iota((width, n_seg_pad), 0) // HEAD == _iota((width, n_seg_pad), 1)).astype(BF16)
    et = (_iota((n_seg_pad, width), 1) // HEAD == _iota((n_seg_pad, width), 0)).astype(BF16)
    return e, et


def _blockdiag_tril(tc):
    r, c = _iota((tc, tc), 0), _iota((tc, tc), 1)
    return ((r // CHUNK == c // CHUNK) & (c <= r)).astype(BF16)


def _rwkv_body(x_ref, gn_ref, w_ref, mu_ref, w0_ref, w2_ref, a0_ref, a2_ref, g2_ref, kk_ref, ka_ref, rk_ref,
               lnw_ref, lnb_ref, o_ref, p_ref, prow_ref, s_ref, at_ref, rt_ref, bt_ref, kt_ref, v_ref, gl_ref, y_ref,
               *, tc, width, lora_win):
    step = pl.program_id(1)

    tw = p_ref.shape[1] // N_PROJ_TILES
    assert tw % MXU_COLS == 0

    def h_next_fn():
        return _rms(x_ref[...], gn_ref[...]).astype(BF16)

    def project_tile(h, k):
        p_ref[:, k * tw:(k + 1) * tw] = _dot(h, w_ref[:, k * tw:(k + 1) * tw])

    @pl.when(step == 0)
    def _():
        s_ref[...] = jnp.zeros_like(s_ref)
        prow_ref[...] = jnp.zeros_like(prow_ref)
        h = h_next_fn()
        for k in range(N_PROJ_TILES):
            project_tile(h, k)

    @pl.when(step > 0)
    def _():
        _rwkv_mix_block(h_next_fn, project_tile, mu_ref, w0_ref, w2_ref, a0_ref, a2_ref, g2_ref, kk_ref, ka_ref, rk_ref, lnw_ref,
                        lnb_ref, o_ref, p_ref, prow_ref, s_ref, at_ref, rt_ref, bt_ref, kt_ref, v_ref, gl_ref, y_ref,
                        tc=tc, width=width, lora_win=lora_win)


def _rwkv_mix_block(h_next_fn, project_tile, mu_ref, w0_ref, w2_ref, a0_ref, a2_ref, g2_ref, kk_ref, ka_ref, rk_ref, lnw_ref,
                    lnb_ref, o_ref, p_ref, prow_ref, s_ref, at_ref, rt_ref, bt_ref, kt_ref, v_ref, gl_ref, y_ref,
                    *, tc, width, lora_win):
    n_heads = width // HEAD
    n_chunks = tc // CHUNK

    p = p_ref[...]
    first_row = _iota((tc, 1), 0) == 0

    def lerp(cols):
        pc = p[:, cols]
        psh = jnp.where(first_row, prow_ref[0:1, cols], pltpu.roll(pc, 1, 0))
        return pc + (psh - pc) * mu_ref[:, cols]

    o1 = 3 * width
    xl = lerp(slice(o1, p.shape[1]))
    wc, ac, gc = (xl[:, lo:hi] for lo, hi in lora_win)
    z = w0_ref[...] + _dot(jnp.tanh(wc).astype(BF16), w2_ref[...])
    a = _sigmoid(a0_ref[...] + _dot(ac.astype(BF16), a2_ref[...]))
    g = _dot(_sigmoid(gc).astype(BF16), g2_ref[...])
    lw = -jnp.exp(-_softplus(-z) - 0.5)

    h_next = h_next_fn()
    project_tile(h_next, 0)
    project_tile(h_next, 1)
    project_tile(h_next, 2)

    xr = lerp(slice(0, o1))
    last_row = p[tc - 1:tc, :]
    r = xr[:, 0:width]
    k = xr[:, width:2 * width]
    v = xr[:, 2 * width:3 * width]

    mxu_w = 2 * LANE
    same_head = (_iota((mxu_w, mxu_w), 0) // HEAD == _iota((mxu_w, mxu_w), 1) // HEAD).astype(BF16)

    def seg_sum(t):
        return jnp.concatenate(
            [_dot_sel_r(t[:, j:j + mxu_w], same_head) for j in range(0, width, mxu_w)], axis=1)

    kk = k * kk_ref[...]
    kk = kk * lax.rsqrt(jnp.maximum(seg_sum(kk * kk), 1e-24))
    k2 = k * (1.0 + (a - 1.0) * ka_ref[...])
    project_tile(h_next, 3)
    project_tile(h_next, 4)
    bonus = seg_sum(r * k2 * rk_ref[...]) * v
    cs = _dot_sel_l(_blockdiag_tril(tc), lw)
    project_tile(h_next, 5)
    project_tile(h_next, 6)
    prow_ref[0:1, :] = last_row

    ginv = jnp.exp(-cs)
    at_ref[...] = (-kk * jnp.exp(cs - lw)).astype(BF16)
    rt_ref[...] = (r * jnp.exp(cs)).astype(BF16)
    bt_ref[...] = (kk * a * ginv).astype(BF16)
    kt_ref[...] = (k2 * ginv).astype(BF16)
    v_ref[...] = v.astype(BF16)
    for c in range(n_chunks):
        last = c * CHUNK + CHUNK - 1
        gl_ref[c:c + 1, :] = jnp.exp(cs[last:last + 1, :])

    rr, cc = _iota((CHUNK, CHUNK), 0), _iota((CHUNK, CHUNK), 1)
    rr2, cc2 = _iota((CHUNK, 2 * CHUNK), 0), _iota((CHUNK, 2 * CHUNK), 1) % CHUNK
    strict2 = (cc2 < rr2).astype(F32)
    incl2 = (cc2 <= rr2).astype(F32)
    eye = (cc == rr).astype(F32)
    n_lvl = int(math.log2(CHUNK))
    lvl_masks = []
    for lv in range(n_lvl):
        same = (rr >> (lv + 1)) == (cc >> (lv + 1))
        lvl_masks.append((same & (((rr >> lv) & 1) == 1) & (((cc >> lv) & 1) == 0)).astype(F32))

    def chunk_step(c, carry):
        t0 = pl.multiple_of(c * CHUNK, CHUNK)
        rows = pl.ds(t0, CHUNK)
        gl = gl_ref[pl.ds(c, 1), :]
        heads = range(n_heads)
        hsl = [slice(h * HEAD, (h + 1) * HEAD) for h in heads]
        vh = [v_ref[rows, hs] for hs in hsl]
        ar = [jnp.concatenate([at_ref[rows, hs], rt_ref[rows, hs]], axis=0) for hs in hsl]
        bk = [jnp.concatenate([bt_ref[rows, hs], kt_ref[rows, hs]], axis=0) for hs in hsl]
        pm = [_dot_nt(ar[h], bk[h]) for h in heads]
        s0 = [s_ref[h] for h in heads]
        ars = [_dot_nt(ar[h], s0[h].astype(BF16)) for h in heads]
        pa = [pm[h][:CHUNK, :] * strict2 for h in heads]
        aab = [pa[h][:, :CHUNK] for h in heads]
        rhs = [ars[h][:CHUNK, :] + _dot(pa[h][:, CHUNK:].astype(BF16), vh[h]) for h in heads]
        x = [eye + aab[h] * lvl_masks[0] for h in heads]
        for lv in range(1, n_lvl):
            xb = [x[h].astype(BF16) for h in heads]
            tm_ = [_dot((aab[h] * lvl_masks[lv]).astype(BF16), xb[h]) for h in heads]
            x = [x[h] + _dot(xb[h], tm_[h].astype(BF16)) for h in heads]
        u = [_dot(x[h].astype(BF16), rhs[h].astype(BF16)) for h in heads]
        uv = [jnp.concatenate([u[h].astype(BF16), vh[h]], axis=0) for h in heads]
        for h in heads:
            pr = (pm[h][CHUNK:, :] * incl2).astype(BF16)
            y_ref[rows, hsl[h]] = ars[h][CHUNK:, :] + _dot(pr, uv[h])
        for h in heads:
            s_ref[h] = (s0[h] + _dot_tn(uv[h], bk[h])) * gl[:, hsl[h]]
        return carry

    lax.fori_loop(0, n_chunks, chunk_step, 0)

    y = y_ref[...]
    inv_n = 1.0 / HEAD
    mean = seg_sum(y) * inv_n
    yc = y - mean
    var = seg_sum(yc * yc) * inv_n
    yn = yc * lax.rsqrt(var + RWKV_LN_EPS) * lnw_ref[...] + lnb_ref[...]
    o_ref[...] = ((yn + bonus) * g).astype(o_ref.dtype)


def _mixer_specs(x, gn, w, pw, params, batch, seq, tc, width):
    nt = seq // tc
    d = x.shape[1]
    const = lambda b, s: (0, 0)
    in_specs = [
        pl.BlockSpec((tc, d), lambda b, s: (b * nt + jnp.minimum(s, nt - 1), 0)),
        pl.BlockSpec(gn.shape, const),
        pl.BlockSpec((w.shape[0], pw), const, pipeline_mode=pl.Buffered(1)),
    ] + [pl.BlockSpec(a.shape, const) for a in params]
    out_spec = pl.BlockSpec((tc, width), lambda b, s: (b * nt + jnp.maximum(s - 1, 0), 0))
    return (batch, nt + 1), in_specs, out_spec


def _rwkv_mix(x, gn, w, mu, w0, w2, a0, a2, g2, k_k, k_a, r_k, ln_w, ln_b, batch, seq, tc, lora_win):
    n = x.shape[0]
    pw = mu.shape[1]
    width = w0.shape[1]
    n_heads = width // HEAD
    params = (mu, w0, w2, a0, a2, g2, k_k, k_a, r_k, ln_w, ln_b)
    grid, in_specs, out_spec = _mixer_specs(x, gn, w, pw, params, batch, seq, tc, width)
    return pl.pallas_call(
        functools.partial(_rwkv_body, tc=tc, width=width, lora_win=lora_win),
        grid=grid,
        in_specs=in_specs,
        out_specs=out_spec,
        out_shape=jax.ShapeDtypeStruct((n, width), BF16),
        scratch_shapes=[pltpu.VMEM((tc, pw), F32), pltpu.VMEM((SUBLANE, pw), F32),
                        pltpu.VMEM((n_heads, HEAD, HEAD), F32)]
        + [pltpu.VMEM((tc, width), BF16) for _ in range(5)]
        + [pltpu.VMEM((max(tc // CHUNK, SUBLANE), width), F32), pltpu.VMEM((tc, width), F32)],
        compiler_params=_cparams(2),
        name="rwkv_mix",
    )(x, gn, w, *params)


def _ssd_body(x_ref, gn_ref, w_ref, cw_ref, cb_ref, dtb_ref, alog_ref, dskip_ref, nw_ref, o_ref,
              p_ref, tail_ref, st_ref, y_ref, *, tc, width, state, groups):
    step = pl.program_id(1)
    n_chunks = tc // CHUNK
    pw = p_ref.shape[1]
    tw = 4 * MXU_COLS
    tiles = [slice(c0, min(c0 + tw, pw)) for c0 in range(0, pw, tw)]
    n_tiles = len(tiles)

    def project_tile(h, k):
        p_ref[:, tiles[k]] = _dot(h, w_ref[:, tiles[k]])

    @pl.when(step == 0)
    def _():
        st_ref[...] = jnp.zeros_like(st_ref)
        tail_ref[...] = jnp.zeros_like(tail_ref)
        h = _rms(x_ref[...], gn_ref[...]).astype(BF16)
        for k in range(n_tiles):
            project_tile(h, k)

    @pl.when(step > 0)
    def _():
        _ssd_mix_block(x_ref, gn_ref, project_tile, n_tiles, cw_ref, cb_ref, dtb_ref, alog_ref, dskip_ref, nw_ref,
                       o_ref, p_ref, tail_ref, st_ref, y_ref, tc=tc, width=width, state=state, groups=groups)


def _ssd_mix_block(x_ref, gn_ref, project_tile, n_tiles, cw_ref, cb_ref, dtb_ref, alog_ref, dskip_ref, nw_ref,
                   o_ref, p_ref, tail_ref, st_ref, y_ref, *, tc, width, state, groups):
    n_chunks = tc // CHUNK
    gw = width // groups
    hpg = gw // HEAD
    conv_ch = width + 2 * groups * state
    n_tap = cw_ref.shape[0]

    p = p_ref[...]
    zs = _silu(p[:, 0:width])
    xbc_raw = p[:, width:width + conv_ch]
    dt_raw = p[:, width + conv_ch:width + conv_ch + LANE]
    h_next = _rms(x_ref[...], gn_ref[...]).astype(BF16)
    tiles_after_chunk = {c: [k for k in range(1, n_tiles) if (k - 1) % n_chunks == c] for c in range(n_chunks)}

    prev = tail_ref[...]
    tail_ref[...] = xbc_raw[tc - SUBLANE:tc, :]
    xcat = jnp.concatenate([prev, xbc_raw], axis=0)
    conv = cb_ref[...]
    for j in range(n_tap):
        off = SUBLANE - (n_tap - 1) + j
        conv = conv + cw_ref[j:j + 1, :] * xcat[off:off + tc, :]
    xbc = _silu(conv)
    xs = xbc[:, 0:width]
    bm = xbc[:, width:width + groups * state].astype(BF16)
    cm = xbc[:, width + groups * state:conv_ch].astype(BF16)
    project_tile(h_next, 0)

    _, et = _seg_matrices(width, LANE)
    dt = _softplus(dt_raw + dtb_ref[...])
    da = -jnp.exp(alog_ref[...]) * dt
    cs_h = _dot_sel_l(_blockdiag_tril(tc), da)
    cs = _dot_sel_r(cs_h, et)
    dtx = _dot_sel_r(dt, et)
    xc = xs * dtx
    ecs = jnp.exp(cs)

    rr, cc = _iota((CHUNK, CHUNK), 0), _iota((CHUNK, CHUNK), 1)
    causal = cc <= rr

    for c in range(n_chunks):
        r0 = c * CHUNK
        rows = slice(r0, r0 + CHUNK)
        cs_c = cs[rows, :]
        cs_last = cs_c[CHUNK - 1:CHUNK, :]
        xc_c = xc[rows, :]
        xd = (xc_c * jnp.exp(cs_last - cs_c)).astype(BF16)
        xcb = xc_c.astype(BF16)
        cs_t = cs_h[rows, :].T
        for g in range(groups):
            gl = slice(g * gw, (g + 1) * gw)
            b_g = bm[rows, g * state:(g + 1) * state]
            c_g = cm[rows, g * state:(g + 1) * state]
            cb = _dot_nt(c_g, b_g)
            st = st_ref[g]
            y_off = _dot(c_g, st.astype(BF16)) * ecs[rows, gl]
            for hh in range(hpg):
                h = g * hpg + hh
                hs = slice(h * HEAD, (h + 1) * HEAD)
                seg = cs_c[:, hs] - cs_t[h:h + 1, :]
                lmat = jnp.where(causal, jnp.exp(jnp.where(causal, seg, 0.0)), 0.0)
                y_ref[rows, hs] = _dot((cb * lmat).astype(BF16), xcb[:, hs]) + y_off[:, hh * HEAD:(hh + 1) * HEAD]
            st_ref[g] = st * ecs[r0 + CHUNK - 1:r0 + CHUNK, gl] + _dot_tn(b_g, xd[:, gl])
        for k in tiles_after_chunk[c]:
            project_tile(h_next, k)

    y = (y_ref[...] + dskip_ref[...] * xs) * zs
    for g in range(groups):
        gl = slice(g * gw, (g + 1) * gw)
        yg = y[:, gl]
        ms = jnp.mean(yg * yg, axis=-1, keepdims=True)
        o_ref[:, gl] = (yg * lax.rsqrt(ms + NORM_EPS) * nw_ref[:, gl]).astype(o_ref.dtype)


def _ssd_mix(x, gn, w, conv_w, conv_b, dt_bias, a_log, d_skip, norm_w, batch, seq, tc, width, state, groups):
    n = x.shape[0]
    pw = w.shape[1]
    conv_ch = width + 2 * groups * state
    params = (conv_w, conv_b, dt_bias, a_log, d_skip, norm_w)
    grid, in_specs, out_spec = _mixer_specs(x, gn, w, pw, params, batch, seq, tc, width)
    return pl.pallas_call(
        functools.partial(_ssd_body, tc=tc, width=width, state=state, groups=groups),
        grid=grid,
        in_specs=in_specs,
        out_specs=out_spec,
        out_shape=jax.ShapeDtypeStruct((n, width), BF16),
        scratch_shapes=[pltpu.VMEM((tc, pw), F32), pltpu.VMEM((SUBLANE, conv_ch), F32),
                        pltpu.VMEM((groups, state, width // groups), F32), pltpu.VMEM((tc, width), F32)],
        compiler_params=_cparams(2),
        name="ssd_mix",
    )(x, gn, w, *params)


def _ssm_weight_body(w_ref, o_ref, *, start, main, tail):
    w = w_ref[...]
    o_ref[:, 0:main] = w[:, start:start + main]
    last = w[:, w.shape[1] - LANE:]
    lane = _iota(last.shape, 1)
    o_ref[:, main:main + LANE] = jnp.where(lane < tail, pltpu.roll(last, tail, 1), jnp.zeros_like(last))


def _ssm_weight(w, start, main, tail):
    k, m = w.shape
    assert main % LANE == 0 and start + main + tail == m and tail <= LANE
    rows = 256
    return pl.pallas_call(
        functools.partial(_ssm_weight_body, start=start, main=main, tail=tail),
        grid=(k // rows,),
        in_specs=[pl.BlockSpec((rows, m), lambda i: (i, 0))],
        out_specs=pl.BlockSpec((rows, main + LANE), lambda i: (i, 0)),
        out_shape=jax.ShapeDtypeStruct((k, main + LANE), w.dtype),
        compiler_params=_cparams(1),
        name="ssm_weight",
    )(w)


PAIR = 2 * CHUNK
BAND = (LEFT_CHUNKS + 1) * CHUNK
PAIR_BAND = BAND + CHUNK


def _bias_table_body(rb_ref, o_ref, *, n_bucket, rel_future):
    ext_w = PAIR_BAND + LANE
    m = _iota((n_bucket, ext_w), 1)
    bucket = jnp.clip(PAIR_BAND - 1 - m, -rel_future, REL_PAST_CLIP) + rel_future
    sel = (bucket == _iota((n_bucket, ext_w), 0)).astype(BF16)
    ext = _dot_sel_r(rb_ref[...], sel) * LOG2E
    for r in range(PAIR):
        o_ref[r] = ext[:, PAIR - 1 - r:PAIR - 1 - r + PAIR_BAND]


def _bias_table(rel_bias):
    n_heads, n_bucket = rel_bias.shape
    out = pl.pallas_call(
        functools.partial(_bias_table_body, n_bucket=n_bucket, rel_future=CHUNK - 1),
        out_shape=jax.ShapeDtypeStruct((PAIR, n_heads, PAIR_BAND), F32),
        name="bias_table",
    )(rel_bias)
    return jnp.transpose(out, (1, 0, 2))


def _attn_body(q_ref, kp_ref, kc_ref, vp_ref, vc_ref, b_ref, o_ref, *, tq, dh):
    i = pl.program_id(2)
    n_heads = q_ref.shape[1] // dh
    row, col = _iota((PAIR, PAIR_BAND), 0), _iota((PAIR, PAIR_BAND), 1)
    lo = (row // CHUNK) * CHUNK
    in_band = (col >= lo) & (col < lo + BAND)
    n_pairs = tq // PAIR
    k0 = [tq - LEFT_CHUNKS * CHUNK + jp * PAIR for jp in range(n_pairs)]
    work = [(h, jp) for h in range(n_heads) for jp in range(n_pairs)]

    def attend(first_block):
        hcols = [slice(h * dh, (h + 1) * dh) for h in range(n_heads)]
        kcat = [jnp.concatenate([kp_ref[:, hc], kc_ref[:, hc]], axis=0) for hc in hcols]
        vcat = [jnp.concatenate([vp_ref[:, hc], vc_ref[:, hc]], axis=0) for hc in hcols]
        s = [_dot_nt(q_ref[jp * PAIR:(jp + 1) * PAIR, hcols[h]], kcat[h][k0[jp]:k0[jp] + PAIR_BAND, :]) + b_ref[h]
             for h, jp in work]
        if first_block:
            valid = [in_band & (col + k0[jp] >= tq) for jp in range(n_pairs)]
            s = [jnp.where(valid[jp], s_, -1e30) for (h, jp), s_ in zip(work, s)]
        else:
            edge = PAIR_BAND - LANE
            s = [jnp.concatenate([jnp.where(in_band[:, :LANE], s_[:, :LANE], -1e30), s_[:, LANE:edge],
                                  jnp.where(in_band[:, edge:], s_[:, edge:], -1e30)], axis=1) for s_ in s]
        mx = [jnp.max(s_, axis=-1, keepdims=True) for s_ in s]
        pexp = [jnp.exp2(s_ - m_) for s_, m_ in zip(s, mx)]
        den = [jnp.sum(p_, axis=-1, keepdims=True) for p_ in pexp]
        o = [_dot(p_.astype(BF16), vcat[h][k0[jp]:k0[jp] + PAIR_BAND, :]) for (h, jp), p_ in zip(work, pexp)]
        for (h, jp), o_, d_ in zip(work, o, den):
            o_ref[jp * PAIR:(jp + 1) * PAIR, hcols[h]] = (o_ / d_).astype(o_ref.dtype)

    @pl.when(i == 0)
    def _():
        attend(True)

    @pl.when(i > 0)
    def _():
        attend(False)


def _band_attention(qkv, bias, batch, seq, n_heads, tq, hp):
    n, three_d = qkv.shape
    d = three_d // 3
    dh = d // n_heads
    nt = seq // tq
    ng = n_heads // hp
    cur = lambda off: (lambda b, h, i: (b * nt + i, off * ng + h))
    prv = lambda off: (lambda b, h, i: (b * nt + jnp.maximum(i - 1, 0), off * ng + h))
    blk = lambda f: pl.BlockSpec((tq, hp * dh), f)
    return pl.pallas_call(
        functools.partial(_attn_body, tq=tq, dh=dh),
        grid=(batch, ng, nt),
        in_specs=[blk(cur(0)), blk(prv(1)), blk(cur(1)), blk(prv(2)), blk(cur(2)),
                  pl.BlockSpec((hp, PAIR, PAIR_BAND), lambda b, h, i: (h, 0, 0))],
        out_specs=pl.BlockSpec((tq, hp * dh), lambda b, h, i: (b * nt + i, h)),
        out_shape=jax.ShapeDtypeStruct((n, d), BF16),
        compiler_params=_cparams(3),
        name="band_attention",
    )(qkv, qkv, qkv, qkv, qkv, bias)


def _pad_cols(a, width):
    return jnp.pad(a, ((0, 0), (0, width - a.shape[1])))


def _pad_rows(a, rows):
    return jnp.pad(a, ((0, rows - a.shape[0]), (0, 0)))


def _row(a):
    return a.reshape(1, -1)


def _lora_window(w, off):
    rank = w.shape[0]
    lo, hi = off // LANE * LANE, -(-(off + rank) // LANE) * LANE
    return jnp.pad(w, ((off - lo, hi - off - rank), (0, 0))).astype(BF16), (lo, hi)


class _Tiles(NamedTuple):
    rows: int
    proj_rows: int
    mix_rows: int
    attn_rows: int
    attn_heads: int
    qkv_cols: int
    ffn_cols: int


def _tiles(n, seq):
    rows = min(1024, n)
    return _Tiles(rows=rows, proj_rows=rows, mix_rows=min(4 * CHUNK, seq),
                  attn_rows=min(LEFT_CHUNKS * CHUNK, seq), attn_heads=8, qkv_cols=2048, ffn_cols=512)


def kernel(x, norm_g, w_in_ab, rwkv_mu, rwkv_w0, rwkv_w2, rwkv_a0, rwkv_a2, rwkv_g2, rwkv_k_k, rwkv_k_a, rwkv_r_k, rwkv_ln_w, rwkv_ln_b, ssm_conv_w, ssm_conv_b, ssm_dt_bias, ssm_A_log, ssm_D, ssm_norm_w, w_out_ab, w_qkv, attn_rel_bias, w_out_c, ffn_w_gate, ffn_w_up, ffn_w_down):
    batch, seq, d = x.shape
    n = batch * seq
    depth = norm_g.shape[0]
    xf = x.reshape(n, d)

    rw = rwkv_w0.shape[1]
    lora_w, lora_a, lora_g = rwkv_w2.shape[1], rwkv_a2.shape[1], rwkv_g2.shape[1]
    sw = ssm_norm_w.shape[1]
    n_ssm_heads = ssm_A_log.shape[1]
    conv_ch = ssm_conv_w.shape[2]
    groups = 2
    state = (conv_ch - sw) // (2 * groups)
    n_att_heads = attn_rel_bias.shape[1]
    rwkv_proj = 3 * rw + lora_w + lora_a + lora_g
    t = _tiles(n, seq)

    wg_all, wu_all, wd_all = ffn_w_gate.astype(BF16), ffn_w_up.astype(BF16), ffn_w_down.astype(BF16)
    for l in range(depth):
        i = l // 2
        g = norm_g[l]
        if l % 2 == 0:
            w_in = w_in_ab[i].astype(BF16)
            pw_r = -(-rwkv_proj // MXU_COLS) * MXU_COLS
            mu = _pad_cols(_row(rwkv_mu[i]), pw_r)
            lora_w2, win_w = _lora_window(rwkv_w2[i], 0)
            lora_a2, win_a = _lora_window(rwkv_a2[i], lora_w)
            lora_g2, win_g = _lora_window(rwkv_g2[i], lora_w + lora_a)
            o4 = rwkv_proj + sw + conv_ch
            w_s = _ssm_weight(w_in, rwkv_proj, o4 - rwkv_proj, n_ssm_heads)

            y_a = _rwkv_mix(
                xf, g[0:1], w_in, mu, _row(rwkv_w0[i]), lora_w2, _row(rwkv_a0[i]), lora_a2, lora_g2, _row(rwkv_k_k[i]),
                _row(rwkv_k_a[i]), _row(rwkv_r_k[i]), _row(rwkv_ln_w[i]), _row(rwkv_ln_b[i]), batch, seq, t.mix_rows,
                (win_w, win_a, win_g))
            y_b = _ssd_mix(
                xf, g[0:1], w_s, ssm_conv_w[i], _row(ssm_conv_b[i]), _pad_cols(_row(ssm_dt_bias[i]), LANE),
                _pad_cols(_row(ssm_A_log[i]), LANE), _row(jnp.repeat(ssm_D[i], sw // n_ssm_heads)),
                _row(ssm_norm_w[i]), batch, seq, t.mix_rows, sw, state, groups)
            w_o = w_out_ab[i].astype(BF16)
            xf = _proj_res([y_a, y_b], w_o, xf, g[1:2], t.proj_rows)
        else:
            qkv = _norm_matmul(xf, g[0:1], w_qkv[i].astype(BF16), BF16, t.rows, t.qkv_cols, n_scaled=d // t.qkv_cols,
                               scale=(d // n_att_heads) ** -0.5 * LOG2E)
            bias = _bias_table(attn_rel_bias[i])
            att = _band_attention(qkv, bias, batch, seq, n_att_heads, t.attn_rows, t.attn_heads)
            xf = _proj_res([att], w_out_c[i].astype(BF16), xf, g[1:2], t.proj_rows)
        xf = _ffn(xf, g[2:3], g[3:4], wg_all, wu_all, wd_all, l, t.rows, t.ffn_cols)
    return xf.reshape(batch, seq, d)
```

```python
import functools
import math
from typing import NamedTuple

import jax
import jax.numpy as jnp
from jax import lax
from jax.experimental import pallas as pl
from jax.experimental.pallas import tpu as pltpu

F32 = jnp.float32
BF16 = jnp.bfloat16

NORM_EPS = 1e-6
RWKV_LN_EPS = 64e-5
CHUNK = 64
HEAD = 64
LANE = 128
SUBLANE = 8
MXU_COLS = 256
EDGE_ROWS = 256
N_PROJ_TILES = 7
LEFT_CHUNKS = 8
REL_PAST_CLIP = 256
LOG2E = math.log2(math.e)
VMEM_LIMIT = 60 * 1024 * 1024


def _cparams(n_axes):
    return pltpu.CompilerParams(dimension_semantics=("arbitrary",) * n_axes, vmem_limit_bytes=VMEM_LIMIT)


def _dot(a, b):
    return jnp.dot(a, b, preferred_element_type=F32)


def _dot_nt(a, b):
    return lax.dot_general(a, b, (((1,), (1,)), ((), ())), preferred_element_type=F32)


def _dot_tn(a, b):
    return lax.dot_general(a, b, (((0,), (0,)), ((), ())), preferred_element_type=F32)


def _split2(t):
    hi = t.astype(BF16)
    lo = (t - hi.astype(F32)).astype(BF16)
    return hi, lo


def _dot_sel_r(t, sel):
    hi, lo = _split2(t)
    return _dot(hi, sel) + _dot(lo, sel)


def _dot_sel_l(sel, t):
    hi, lo = _split2(t)
    return _dot(sel, hi) + _dot(sel, lo)


def _rms(x, g):
    ms = jnp.mean(x * x, axis=-1, keepdims=True)
    return x * lax.rsqrt(ms + NORM_EPS) * g


def _sigmoid(x):
    return 1.0 / (1.0 + jnp.exp(-x))


def _silu(x):
    return x * _sigmoid(x)


def _softplus(x):
    return jnp.maximum(x, 0.0) + jnp.log1p(jnp.exp(-jnp.abs(x)))


def _iota(shape, axis):
    return lax.broadcasted_iota(jnp.int32, shape, axis)


def _norm_matmul_body(x_ref, g_ref, w_ref, o_ref, h_ref, *, n_scaled, scale):
    j = pl.program_id(1)
    mult = jnp.where(j < n_scaled, jnp.float32(scale), jnp.float32(1.0))

    @pl.when(j == 0)
    def _():
        for r in range(0, x_ref.shape[0], EDGE_ROWS):
            rs = slice(r, r + EDGE_ROWS)
            h = _rms(x_ref[rs, :], g_ref[...]).astype(BF16)
            h_ref[rs, :] = h
            o_ref[rs, :] = (_dot(h, w_ref[...]) * mult).astype(o_ref.dtype)

    @pl.when(j > 0)
    def _():
        o_ref[...] = (_dot(h_ref[...], w_ref[...]) * mult).astype(o_ref.dtype)


def _norm_matmul(x, g, w, out_dtype, tm, tn, n_scaled=0, scale=1.0):
    n, d = x.shape
    m = w.shape[1]
    return pl.pallas_call(
        functools.partial(_norm_matmul_body, n_scaled=n_scaled, scale=scale),
        grid=(n // tm, m // tn),
        in_specs=[
            pl.BlockSpec((tm, d), lambda i, j: (i, 0)),
            pl.BlockSpec((1, d), lambda i, j: (0, 0)),
            pl.BlockSpec((d, tn), lambda i, j: (0, j)),
        ],
        out_specs=pl.BlockSpec((tm, tn), lambda i, j: (i, j)),
        out_shape=jax.ShapeDtypeStruct((n, m), out_dtype),
        scratch_shapes=[pltpu.VMEM((tm, d), BF16)],
        compiler_params=_cparams(2),
        name="norm_matmul",
    )(x, g, w)


def _proj_res_body(*refs, n_in):
    y_refs, w_refs = refs[:n_in], refs[n_in:2 * n_in]
    x_ref, g_ref, o_ref = refs[2 * n_in:]
    m = _dot(y_refs[0][...], w_refs[0][...])
    for y_ref, w_ref in zip(y_refs[1:], w_refs[1:]):
        m = m + _dot(y_ref[...], w_ref[...])
    o_ref[...] = x_ref[...] + _rms(m, g_ref[...])


def _proj_res(ys, w, x, g, tm):
    n, d = x.shape
    n_in = len(ys)
    k = ys[0].shape[1]
    assert all(y.shape[1] == k for y in ys) and w.shape[0] == n_in * k
    in_specs = [pl.BlockSpec((tm, k), lambda i: (i, 0)) for y in ys]
    in_specs += [pl.BlockSpec((k, d), functools.partial(lambda i, r: (r, 0), r=r), pipeline_mode=pl.Buffered(1))
                 for r in range(n_in)]
    in_specs += [pl.BlockSpec((tm, d), lambda i: (i, 0)), pl.BlockSpec((1, d), lambda i: (0, 0))]
    return pl.pallas_call(
        functools.partial(_proj_res_body, n_in=n_in),
        grid=(n // tm,),
        in_specs=in_specs,
        out_specs=pl.BlockSpec((tm, d), lambda i: (i, 0)),
        out_shape=jax.ShapeDtypeStruct((n, d), F32),
        compiler_params=_cparams(1),
        name="proj_res",
    )(*ys, *([w] * n_in), x, g)


def _ffn_body(x_ref, gi_ref, go_ref, wg_ref, wu_ref, wd_ref, o_ref, h_ref):
    j = pl.program_id(1)
    last_j = pl.num_programs(1) - 1
    tm = x_ref.shape[0]

    def partial_out(h):
        a = (_silu(_dot(h, wg_ref[...])) * _dot(h, wu_ref[...])).astype(BF16)
        return _dot(a, wd_ref[...])

    row_groups = [slice(r, r + EDGE_ROWS) for r in range(0, tm, EDGE_ROWS)]

    @pl.when(j == 0)
    def _():
        for rs in row_groups:
            h = _rms(x_ref[rs, :], gi_ref[...]).astype(BF16)
            h_ref[rs, :] = h
            o_ref[rs, :] = partial_out(h)

    @pl.when((j > 0) & (j < last_j))
    def _():
        o_ref[...] += partial_out(h_ref[...])

    @pl.when(j == last_j)
    def _():
        for rs in row_groups:
            acc = o_ref[rs, :] + partial_out(h_ref[rs, :])
            o_ref[rs, :] = x_ref[rs, :] + _rms(acc, go_ref[...])


def _ffn(x, g_in, g_out, wg, wu, wd, layer, tm, tf):
    n, d = x.shape
    f = wg.shape[2]
    assert f // tf >= 2 and tm % EDGE_ROWS == 0
    return pl.pallas_call(
        _ffn_body,
        grid=(n // tm, f // tf),
        in_specs=[
            pl.BlockSpec((tm, d), lambda i, j: (i, 0), pipeline_mode=pl.Buffered(1)),
            pl.BlockSpec((1, d), lambda i, j: (0, 0)),
            pl.BlockSpec((1, d), lambda i, j: (0, 0)),
            pl.BlockSpec((None, d, tf), lambda i, j: (layer, 0, j)),
            pl.BlockSpec((None, d, tf), lambda i, j: (layer, 0, j)),
            pl.BlockSpec((None, tf, d), lambda i, j: (layer, j, 0)),
        ],
        out_specs=pl.BlockSpec((tm, d), lambda i, j: (i, 0)),
        out_shape=jax.ShapeDtypeStruct((n, d), F32),
        scratch_shapes=[pltpu.VMEM((tm, d), BF16)],
        compiler_params=_cparams(2),
        name="ffn",
    )(x, g_in, g_out, wg, wu, wd)


def _seg_matrices(width, n_seg_pad):
    e = (_iota((width, n_seg_pad), 0) // HEAD == _iota((width, n_seg_pad), 1)).astype(BF16)
    et = (_iota((n_seg_pad, width), 1) // HEAD == _iota((n_seg_pad, width), 0)).astype(BF16)
    return e, et


def _blockdiag_tril(tc):
    r, c = _iota((tc, tc), 0), _iota((tc, tc), 1)
    return ((r // CHUNK == c // CHUNK) & (c <= r)).astype(BF16)


def _rwkv_body(x_ref, gn_ref, w_ref, mu_ref, w0_ref, w2_ref, a0_ref, a2_ref, g2_ref, kk_ref, ka_ref, rk_ref,
               lnw_ref, lnb_ref, o_ref, p_ref, prow_ref, s_ref, at_ref, rt_ref, bt_ref, kt_ref, v_ref, gl_ref, y_ref,
               *, tc, width, lora_win):
    step = pl.program_id(1)

    tw = p_ref.shape[1] // N_PROJ_TILES
    assert tw % MXU_COLS == 0

    def h_next_fn():
        return _rms(x_ref[...], gn_ref[...]).astype(BF16)

    def project_tile(h, k):
        p_ref[:, k * tw:(k + 1) * tw] = _dot(h, w_ref[:, k * tw:(k + 1) * tw])

    @pl.when(step == 0)
    def _():
        s_ref[...] = jnp.zeros_like(s_ref)
        prow_ref[...] = jnp.zeros_like(prow_ref)
        h = h_next_fn()
        for k in range(N_PROJ_TILES):
            project_tile(h, k)

    @pl.when(step > 0)
    def _():
        _rwkv_mix_block(h_next_fn, project_tile, mu_ref, w0_ref, w2_ref, a0_ref, a2_ref, g2_ref, kk_ref, ka_ref, rk_ref, lnw_ref,
                        lnb_ref, o_ref, p_ref, prow_ref, s_ref, at_ref, rt_ref, bt_ref, kt_ref, v_ref, gl_ref, y_ref,
                        tc=tc, width=width, lora_win=lora_win)


def _rwkv_mix_block(h_next_fn, project_tile, mu_ref, w0_ref, w2_ref, a0_ref, a2_ref, g2_ref, kk_ref, ka_ref, rk_ref, lnw_ref,
                    lnb_ref, o_ref, p_ref, prow_ref, s_ref, at_ref, rt_ref, bt_ref, kt_ref, v_ref, gl_ref, y_ref,
                    *, tc, width, lora_win):
    n_heads = width // HEAD
    n_chunks = tc // CHUNK

    p = p_ref[...]
    first_row = _iota((tc, 1), 0) == 0

    def lerp(cols):
        pc = p[:, cols]
        psh = jnp.where(first_row, prow_ref[0:1, cols], pltpu.roll(pc, 1, 0))
        return pc + (psh - pc) * mu_ref[:, cols]

    o1 = 3 * width
    xl = lerp(slice(o1, p.shape[1]))
    wc, ac, gc = (xl[:, lo:hi] for lo, hi in lora_win)
    z = w0_ref[...] + _dot(jnp.tanh(wc).astype(BF16), w2_ref[...])
    a = _sigmoid(a0_ref[...] + _dot(ac.astype(BF16), a2_ref[...]))
    g = _dot(_sigmoid(gc).astype(BF16), g2_ref[...])
    lw = -jnp.exp(-_softplus(-z) - 0.5)

    h_next = h_next_fn()
    project_tile(h_next, 0)
    project_tile(h_next, 1)
    project_tile(h_next, 2)

    xr = lerp(slice(0, o1))
    last_row = p[tc - 1:tc, :]
    r = xr[:, 0:width]
    k = xr[:, width:2 * width]
    v = xr[:, 2 * width:3 * width]

    mxu_w = 2 * LANE
    same_head = (_iota((mxu_w, mxu_w), 0) // HEAD == _iota((mxu_w, mxu_w), 1) // HEAD).astype(BF16)

    def seg_sum(t):
        return jnp.concatenate(
            [_dot_sel_r(t[:, j:j + mxu_w], same_head) for j in range(0, width, mxu_w)], axis=1)

    kk = k * kk_ref[...]
    kk = kk * lax.rsqrt(jnp.maximum(seg_sum(kk * kk), 1e-24))
    k2 = k * (1.0 + (a - 1.0) * ka_ref[...])
    project_tile(h_next, 3)
    project_tile(h_next, 4)
    bonus = seg_sum(r * k2 * rk_ref[...]) * v
    cs = _dot_sel_l(_blockdiag_tril(tc), lw)
    project_tile(h_next, 5)
    project_tile(h_next, 6)
    prow_ref[0:1, :] = last_row

    ginv = jnp.exp(-cs)
    at_ref[...] = (-kk * jnp.exp(cs - lw)).astype(BF16)
    rt_ref[...] = (r * jnp.exp(cs)).astype(BF16)
    bt_ref[...] = (kk * a * ginv).astype(BF16)
    kt_ref[...] = (k2 * ginv).astype(BF16)
    v_ref[...] = v.astype(BF16)
    for c in range(n_chunks):
        last = c * CHUNK + CHUNK - 1
        gl_ref[c:c + 1, :] = jnp.exp(cs[last:last + 1, :])

    rr, cc = _iota((CHUNK, CHUNK), 0), _iota((CHUNK, CHUNK), 1)
    rr2, cc2 = _iota((CHUNK, 2 * CHUNK), 0), _iota((CHUNK, 2 * CHUNK), 1) % CHUNK
    strict2 = (cc2 < rr2).astype(F32)
    incl2 = (cc2 <= rr2).astype(F32)
    eye = (cc == rr).astype(F32)
    n_lvl = int(math.log2(CHUNK))
    lvl_masks = []
    for lv in range(n_lvl):
        same = (rr >> (lv + 1)) == (cc >> (lv + 1))
        lvl_masks.append((same & (((rr >> lv) & 1) == 1) & (((cc >> lv) & 1) == 0)).astype(F32))

    def chunk_step(c, carry):
        t0 = pl.multiple_of(c * CHUNK, CHUNK)
        rows = pl.ds(t0, CHUNK)
        gl = gl_ref[pl.ds(c, 1), :]
        heads = range(n_heads)
        hsl = [slice(h * HEAD, (h + 1) * HEAD) for h in heads]
        vh = [v_ref[rows, hs] for hs in hsl]
        ar = [jnp.concatenate([at_ref[rows, hs], rt_ref[rows, hs]], axis=0) for hs in hsl]
        bk = [jnp.concatenate([bt_ref[rows, hs], kt_ref[rows, hs]], axis=0) for hs in hsl]
        pm = [_dot_nt(ar[h], bk[h]) for h in heads]
        s0 = [s_ref[h] for h in heads]
        ars = [_dot_nt(ar[h], s0[h].astype(BF16)) for h in heads]
        pa = [pm[h][:CHUNK, :] * strict2 for h in heads]
        aab = [pa[h][:, :CHUNK] for h in heads]
        rhs = [ars[h][:CHUNK, :] + _dot(pa[h][:, CHUNK:].astype(BF16), vh[h]) for h in heads]
        x = [eye + aab[h] * lvl_masks[0] for h in heads]
        for lv in range(1, n_lvl):
            xb = [x[h].astype(BF16) for h in heads]
            tm_ = [_dot((aab[h] * lvl_masks[lv]).astype(BF16), xb[h]) for h in heads]
            x = [x[h] + _dot(xb[h], tm_[h].astype(BF16)) for h in heads]
        u = [_dot(x[h].astype(BF16), rhs[h].astype(BF16)) for h in heads]
        uv = [jnp.concatenate([u[h].astype(BF16), vh[h]], axis=0) for h in heads]
        for h in heads:
            pr = (pm[h][CHUNK:, :] * incl2).astype(BF16)
            y_ref[rows, hsl[h]] = ars[h][CHUNK:, :] + _dot(pr, uv[h])
        for h in heads:
            s_ref[h] = (s0[h] + _dot_tn(uv[h], bk[h])) * gl[:, hsl[h]]
        return carry

    lax.fori_loop(0, n_chunks, chunk_step, 0)

    y = y_ref[...]
    inv_n = 1.0 / HEAD
    mean = seg_sum(y) * inv_n
    yc = y - mean
    var = seg_sum(yc * yc) * inv_n
    yn = yc * lax.rsqrt(var + RWKV_LN_EPS) * lnw_ref[...] + lnb_ref[...]
    o_ref[...] = ((yn + bonus) * g).astype(o_ref.dtype)


def _mixer_specs(x, gn, w, pw, params, batch, seq, tc, width):
    nt = seq // tc
    d = x.shape[1]
    const = lambda b, s: (0, 0)
    in_specs = [
        pl.BlockSpec((tc, d), lambda b, s: (b * nt + jnp.minimum(s, nt - 1), 0)),
        pl.BlockSpec(gn.shape, const),
        pl.BlockSpec((w.shape[0], pw), const, pipeline_mode=pl.Buffered(1)),
    ] + [pl.BlockSpec(a.shape, const) for a in params]
    out_spec = pl.BlockSpec((tc, width), lambda b, s: (b * nt + jnp.maximum(s - 1, 0), 0))
    return (batch, nt + 1), in_specs, out_spec


def _rwkv_mix(x, gn, w, mu, w0, w2, a0, a2, g2, k_k, k_a, r_k, ln_w, ln_b, batch, seq, tc, lora_win):
    n = x.shape[0]
    pw = mu.shape[1]
    width = w0.shape[1]
    n_heads = width // HEAD
    params = (mu, w0, w2, a0, a2, g2, k_k, k_a, r_k, ln_w, ln_b)
    grid, in_specs, out_spec = _mixer_specs(x, gn, w, pw, params, batch, seq, tc, width)
    return pl.pallas_call(
        functools.partial(_rwkv_body, tc=tc, width=width, lora_win=lora_win),
        grid=grid,
        in_specs=in_specs,
        out_specs=out_spec,
        out_shape=jax.ShapeDtypeStruct((n, width), BF16),
        scratch_shapes=[pltpu.VMEM((tc, pw), F32), pltpu.VMEM((SUBLANE, pw), F32),
                        pltpu.VMEM((n_heads, HEAD, HEAD), F32)]
        + [pltpu.VMEM((tc, width), BF16) for _ in range(5)]
        + [pltpu.VMEM((max(tc // CHUNK, SUBLANE), width), F32), pltpu.VMEM((tc, width), F32)],
        compiler_params=_cparams(2),
        name="rwkv_mix",
    )(x, gn, w, *params)


def _ssd_body(x_ref, gn_ref, w_ref, cw_ref, cb_ref, dtb_ref, alog_ref, dskip_ref, nw_ref, o_ref,
              p_ref, tail_ref, st_ref, y_ref, *, tc, width, state, groups):
    step = pl.program_id(1)
    n_chunks = tc // CHUNK
    pw = p_ref.shape[1]
    tw = 4 * MXU_COLS
    tiles = [slice(c0, min(c0 + tw, pw)) for c0 in range(0, pw, tw)]
    n_tiles = len(tiles)

    def project_tile(h, k):
        p_ref[:, tiles[k]] = _dot(h, w_ref[:, tiles[k]])

    @pl.when(step == 0)
    def _():
        st_ref[...] = jnp.zeros_like(st_ref)
        tail_ref[...] = jnp.zeros_like(tail_ref)
        h = _rms(x_ref[...], gn_ref[...]).astype(BF16)
        for k in range(n_tiles):
            project_tile(h, k)

    @pl.when(step > 0)
    def _():
        _ssd_mix_block(x_ref, gn_ref, project_tile, n_tiles, cw_ref, cb_ref, dtb_ref, alog_ref, dskip_ref, nw_ref,
                       o_ref, p_ref, tail_ref, st_ref, y_ref, tc=tc, width=width, state=state, groups=groups)


def _ssd_mix_block(x_ref, gn_ref, project_tile, n_tiles, cw_ref, cb_ref, dtb_ref, alog_ref, dskip_ref, nw_ref,
                   o_ref, p_ref, tail_ref, st_ref, y_ref, *, tc, width, state, groups):
    n_chunks = tc // CHUNK
    gw = width // groups
    hpg = gw // HEAD
    conv_ch = width + 2 * groups * state
    n_tap = cw_ref.shape[0]

    p = p_ref[...]
    zs = _silu(p[:, 0:width])
    xbc_raw = p[:, width:width + conv_ch]
    dt_raw = p[:, width + conv_ch:width + conv_ch + LANE]
    h_next = _rms(x_ref[...], gn_ref[...]).astype(BF16)
    tiles_after_chunk = {c: [k for k in range(1, n_tiles) if (k - 1) % n_chunks == c] for c in range(n_chunks)}

    prev = tail_ref[...]
    tail_ref[...] = xbc_raw[tc - SUBLANE:tc, :]
    xcat = jnp.concatenate([prev, xbc_raw], axis=0)
    conv = cb_ref[...]
    for j in range(n_tap):
        off = SUBLANE - (n_tap - 1) + j
        conv = conv + cw_ref[j:j + 1, :] * xcat[off:off + tc, :]
    xbc = _silu(conv)
    xs = xbc[:, 0:width]
    bm = xbc[:, width:width + groups * state].astype(BF16)
    cm = xbc[:, width + groups * state:conv_ch].astype(BF16)
    project_tile(h_next, 0)

    _, et = _seg_matrices(width, LANE)
    dt = _softplus(dt_raw + dtb_ref[...])
    da = -jnp.exp(alog_ref[...]) * dt
    cs_h = _dot_sel_l(_blockdiag_tril(tc), da)
    cs = _dot_sel_r(cs_h, et)
    dtx = _dot_sel_r(dt, et)
    xc = xs * dtx
    ecs = jnp.exp(cs)

    rr, cc = _iota((CHUNK, CHUNK), 0), _iota((CHUNK, CHUNK), 1)
    causal = cc <= rr

    for c in range(n_chunks):
        r0 = c * CHUNK
        rows = slice(r0, r0 + CHUNK)
        cs_c = cs[rows, :]
        cs_last = cs_c[CHUNK - 1:CHUNK, :]
        xc_c = xc[rows, :]
        xd = (xc_c * jnp.exp(cs_last - cs_c)).astype(BF16)
        xcb = xc_c.astype(BF16)
        cs_t = cs_h[rows, :].T
        for g in range(groups):
            gl = slice(g * gw, (g + 1) * gw)
            b_g = bm[rows, g * state:(g + 1) * state]
            c_g = cm[rows, g * state:(g + 1) * state]
            cb = _dot_nt(c_g, b_g)
            st = st_ref[g]
            y_off = _dot(c_g, st.astype(BF16)) * ecs[rows, gl]
            for hh in range(hpg):
                h = g * hpg + hh
                hs = slice(h * HEAD, (h + 1) * HEAD)
                seg = cs_c[:, hs] - cs_t[h:h + 1, :]
                lmat = jnp.where(causal, jnp.exp(jnp.where(causal, seg, 0.0)), 0.0)
                y_ref[rows, hs] = _dot((cb * lmat).astype(BF16), xcb[:, hs]) + y_off[:, hh * HEAD:(hh + 1) * HEAD]
            st_ref[g] = st * ecs[r0 + CHUNK - 1:r0 + CHUNK, gl] + _dot_tn(b_g, xd[:, gl])
        for k in tiles_after_chunk[c]:
            project_tile(h_next, k)

    y = (y_ref[...] + dskip_ref[...] * xs) * zs
    for g in range(groups):
        gl = slice(g * gw, (g + 1) * gw)
        yg = y[:, gl]
        ms = jnp.mean(yg * yg, axis=-1, keepdims=True)
        o_ref[:, gl] = (yg * lax.rsqrt(ms + NORM_EPS) * nw_ref[:, gl]).astype(o_ref.dtype)


def _ssd_mix(x, gn, w, conv_w, conv_b, dt_bias, a_log, d_skip, norm_w, batch, seq, tc, width, state, groups):
    n = x.shape[0]
    pw = w.shape[1]
    conv_ch = width + 2 * groups * state
    params = (conv_w, conv_b, dt_bias, a_log, d_skip, norm_w)
    grid, in_specs, out_spec = _mixer_specs(x, gn, w, pw, params, batch, seq, tc, width)
    return pl.pallas_call(
        functools.partial(_ssd_body, tc=tc, width=width, state=state, groups=groups),
        grid=grid,
        in_specs=in_specs,
        out_specs=out_spec,
        out_shape=jax.ShapeDtypeStruct((n, width), BF16),
        scratch_shapes=[pltpu.VMEM((tc, pw), F32), pltpu.VMEM((SUBLANE, conv_ch), F32),
                        pltpu.VMEM((groups, state, width // groups), F32), pltpu.VMEM((tc, width), F32)],
        compiler_params=_cparams(2),
        name="ssd_mix",
    )(x, gn, w, *params)


def _split_w_in_body(w_ref, r_ref, s_ref, *, start, main, tail):
    w = w_ref[...].astype(BF16)
    r_ref[...] = w[:, 0:r_ref.shape[1]]
    s_ref[:, 0:main] = w[:, start:start + main]
    last = w[:, w.shape[1] - LANE:]
    lane = _iota(last.shape, 1)
    s_ref[:, main:main + LANE] = jnp.where(lane < tail, pltpu.roll(last, tail, 1), jnp.zeros_like(last))


def _split_w_in(w, pw_r, start, main, tail):
    k, m = w.shape
    assert main % LANE == 0 and start + main + tail == m and tail <= LANE and pw_r % LANE == 0 and pw_r <= m
    rows = 256
    return pl.pallas_call(
        functools.partial(_split_w_in_body, start=start, main=main, tail=tail),
        grid=(k // rows,),
        in_specs=[pl.BlockSpec((rows, m), lambda i: (i, 0))],
        out_specs=[pl.BlockSpec((rows, pw_r), lambda i: (i, 0)), pl.BlockSpec((rows, main + LANE), lambda i: (i, 0))],
        out_shape=[jax.ShapeDtypeStruct((k, pw_r), BF16), jax.ShapeDtypeStruct((k, main + LANE), BF16)],
        compiler_params=_cparams(1),
        name="split_w_in",
    )(w)


PAIR = 2 * CHUNK
BAND = (LEFT_CHUNKS + 1) * CHUNK
PAIR_BAND = BAND + CHUNK


def _bias_table_body(rb_ref, o_ref, *, n_bucket, rel_future):
    ext_w = PAIR_BAND + LANE
    m = _iota((n_bucket, ext_w), 1)
    bucket = jnp.clip(PAIR_BAND - 1 - m, -rel_future, REL_PAST_CLIP) + rel_future
    sel = (bucket == _iota((n_bucket, ext_w), 0)).astype(BF16)
    ext = _dot_sel_r(rb_ref[...], sel) * LOG2E
    for r in range(PAIR):
        o_ref[r] = ext[:, PAIR - 1 - r:PAIR - 1 - r + PAIR_BAND]


def _bias_table(rel_bias):
    n_heads, n_bucket = rel_bias.shape
    out = pl.pallas_call(
        functools.partial(_bias_table_body, n_bucket=n_bucket, rel_future=CHUNK - 1),
        out_shape=jax.ShapeDtypeStruct((PAIR, n_heads, PAIR_BAND), F32),
        name="bias_table",
    )(rel_bias)
    return jnp.transpose(out, (1, 0, 2))


def _attn_body(q_ref, kp_ref, kc_ref, vp_ref, vc_ref, b_ref, o_ref, *, tq, dh):
    i = pl.program_id(2)
    n_heads = q_ref.shape[1] // dh
    row, col = _iota((PAIR, PAIR_BAND), 0), _iota((PAIR, PAIR_BAND), 1)
    lo = (row // CHUNK) * CHUNK
    in_band = (col >= lo) & (col < lo + BAND)
    n_pairs = tq // PAIR
    k0 = [tq - LEFT_CHUNKS * CHUNK + jp * PAIR for jp in range(n_pairs)]
    work = [(h, jp) for h in range(n_heads) for jp in range(n_pairs)]

    def attend(first_block):
        hcols = [slice(h * dh, (h + 1) * dh) for h in range(n_heads)]
        kcat = [jnp.concatenate([kp_ref[:, hc], kc_ref[:, hc]], axis=0) for hc in hcols]
        vcat = [jnp.concatenate([vp_ref[:, hc], vc_ref[:, hc]], axis=0) for hc in hcols]
        s = [_dot_nt(q_ref[jp * PAIR:(jp + 1) * PAIR, hcols[h]], kcat[h][k0[jp]:k0[jp] + PAIR_BAND, :]) + b_ref[h]
             for h, jp in work]
        if first_block:
            valid = [in_band & (col + k0[jp] >= tq) for jp in range(n_pairs)]
            s = [jnp.where(valid[jp], s_, -1e30) for (h, jp), s_ in zip(work, s)]
        else:
            edge = PAIR_BAND - LANE
            s = [jnp.concatenate([jnp.where(in_band[:, :LANE], s_[:, :LANE], -1e30), s_[:, LANE:edge],
                                  jnp.where(in_band[:, edge:], s_[:, edge:], -1e30)], axis=1) for s_ in s]
        mx = [jnp.max(s_, axis=-1, keepdims=True) for s_ in s]
        pexp = [jnp.exp2(s_ - m_) for s_, m_ in zip(s, mx)]
        den = [jnp.sum(p_, axis=-1, keepdims=True) for p_ in pexp]
        o = [_dot(p_.astype(BF16), vcat[h][k0[jp]:k0[jp] + PAIR_BAND, :]) for (h, jp), p_ in zip(work, pexp)]
        for (h, jp), o_, d_ in zip(work, o, den):
            o_ref[jp * PAIR:(jp + 1) * PAIR, hcols[h]] = (o_ / d_).astype(o_ref.dtype)

    @pl.when(i == 0)
    def _():
        attend(True)

    @pl.when(i > 0)
    def _():
        attend(False)


def _band_attention(qkv, bias, batch, seq, n_heads, tq, hp):
    n, three_d = qkv.shape
    d = three_d // 3
    dh = d // n_heads
    nt = seq // tq
    ng = n_heads // hp
    cur = lambda off: (lambda b, h, i: (b * nt + i, off * ng + h))
    prv = lambda off: (lambda b, h, i: (b * nt + jnp.maximum(i - 1, 0), off * ng + h))
    blk = lambda f: pl.BlockSpec((tq, hp * dh), f)
    return pl.pallas_call(
        functools.partial(_attn_body, tq=tq, dh=dh),
        grid=(batch, ng, nt),
        in_specs=[blk(cur(0)), blk(prv(1)), blk(cur(1)), blk(prv(2)), blk(cur(2)),
                  pl.BlockSpec((hp, PAIR, PAIR_BAND), lambda b, h, i: (h, 0, 0))],
        out_specs=pl.BlockSpec((tq, hp * dh), lambda b, h, i: (b * nt + i, h)),
        out_shape=jax.ShapeDtypeStruct((n, d), BF16),
        compiler_params=_cparams(3),
        name="band_attention",
    )(qkv, qkv, qkv, qkv, qkv, bias)


def _pad_cols(a, width):
    return jnp.pad(a, ((0, 0), (0, width - a.shape[1])))


def _pad_rows(a, rows):
    return jnp.pad(a, ((0, rows - a.shape[0]), (0, 0)))


def _row(a):
    return a.reshape(1, -1)


def _lora_window(w, off):
    rank = w.shape[0]
    lo, hi = off // LANE * LANE, -(-(off + rank) // LANE) * LANE
    return jnp.pad(w, ((off - lo, hi - off - rank), (0, 0))).astype(BF16), (lo, hi)


class _Tiles(NamedTuple):
    rows: int
    proj_rows: int
    mix_rows: int
    attn_rows: int
    attn_heads: int
    qkv_cols: int
    ffn_cols: int


def _tiles(n, seq):
    rows = min(1024, n)
    return _Tiles(rows=rows, proj_rows=rows, mix_rows=min(4 * CHUNK, seq),
                  attn_rows=min(LEFT_CHUNKS * CHUNK, seq), attn_heads=8, qkv_cols=2048, ffn_cols=512)


def kernel(x, norm_g, w_in_ab, rwkv_mu, rwkv_w0, rwkv_w2, rwkv_a0, rwkv_a2, rwkv_g2, rwkv_k_k, rwkv_k_a, rwkv_r_k, rwkv_ln_w, rwkv_ln_b, ssm_conv_w, ssm_conv_b, ssm_dt_bias, ssm_A_log, ssm_D, ssm_norm_w, w_out_ab, w_qkv, attn_rel_bias, w_out_c, ffn_w_gate, ffn_w_up, ffn_w_down):
    batch, seq, d = x.shape
    n = batch * seq
    depth = norm_g.shape[0]
    xf = x.reshape(n, d)

    rw = rwkv_w0.shape[1]
    lora_w, lora_a, lora_g = rwkv_w2.shape[1], rwkv_a2.shape[1], rwkv_g2.shape[1]
    sw = ssm_norm_w.shape[1]
    n_ssm_heads = ssm_A_log.shape[1]
    conv_ch = ssm_conv_w.shape[2]
    groups = 2
    state = (conv_ch - sw) // (2 * groups)
    n_att_heads = attn_rel_bias.shape[1]
    rwkv_proj = 3 * rw + lora_w + lora_a + lora_g
    t = _tiles(n, seq)

    wg_all, wu_all, wd_all = ffn_w_gate.astype(BF16), ffn_w_up.astype(BF16), ffn_w_down.astype(BF16)
    for l in range(depth):
        i = l // 2
        g = norm_g[l]
        if l % 2 == 0:
            pw_r = -(-rwkv_proj // MXU_COLS) * MXU_COLS
            mu = _pad_cols(_row(rwkv_mu[i]), pw_r)
            lora_w2, win_w = _lora_window(rwkv_w2[i], 0)
            lora_a2, win_a = _lora_window(rwkv_a2[i], lora_w)
            lora_g2, win_g = _lora_window(rwkv_g2[i], lora_w + lora_a)
            o4 = rwkv_proj + sw + conv_ch
            w_r, w_s = _split_w_in(w_in_ab[i], pw_r, rwkv_proj, o4 - rwkv_proj, n_ssm_heads)

            y_a = _rwkv_mix(
                xf, g[0:1], w_r, mu, _row(rwkv_w0[i]), lora_w2, _row(rwkv_a0[i]), lora_a2, lora_g2, _row(rwkv_k_k[i]),
                _row(rwkv_k_a[i]), _row(rwkv_r_k[i]), _row(rwkv_ln_w[i]), _row(rwkv_ln_b[i]), batch, seq, t.mix_rows,
                (win_w, win_a, win_g))
            y_b = _ssd_mix(
                xf, g[0:1], w_s, ssm_conv_w[i], _row(ssm_conv_b[i]), _pad_cols(_row(ssm_dt_bias[i]), LANE),
                _pad_cols(_row(ssm_A_log[i]), LANE), _row(jnp.repeat(ssm_D[i], sw // n_ssm_heads)),
                _row(ssm_norm_w[i]), batch, seq, t.mix_rows, sw, state, groups)
            w_o = w_out_ab[i].astype(BF16)
            xf = _proj_res([y_a, y_b], w_o, xf, g[1:2], t.proj_rows)
        else:
            qkv = _norm_matmul(xf, g[0:1], w_qkv[i].astype(BF16), BF16, t.rows, t.qkv_cols, n_scaled=d // t.qkv_cols,
                               scale=(d // n_att_heads) ** -0.5 * LOG2E)
            bias = _bias_table(attn_rel_bias[i])
            att = _band_attention(qkv, bias, batch, seq, n_att_heads, t.attn_rows, t.attn_heads)
            xf = _proj_res([att], w_out_c[i].astype(BF16), xf, g[1:2], t.proj_rows)
        xf = _ffn(xf, g[2:3], g[3:4], wg_all, wu_all, wd_all, l, t.rows, t.ffn_cols)
    return xf.reshape(batch, seq, d)
```

```python
import functools
import math
from typing import NamedTuple

import jax
import jax.numpy as jnp
from jax import lax
from jax.experimental import pallas as pl
from jax.experimental.pallas import tpu as pltpu

F32 = jnp.float32
BF16 = jnp.bfloat16

NORM_EPS = 1e-6
RWKV_LN_EPS = 64e-5
CHUNK = 64
HEAD = 64
LANE = 128
SUBLANE = 8
MXU_COLS = 256
EDGE_ROWS = 256
N_PROJ_TILES = 7
LEFT_CHUNKS = 8
REL_PAST_CLIP = 256
LOG2E = math.log2(math.e)
VMEM_LIMIT = 60 * 1024 * 1024


def _cparams(n_axes, fuse_inputs=None):
    return pltpu.CompilerParams(dimension_semantics=("arbitrary",) * n_axes, vmem_limit_bytes=VMEM_LIMIT,
                                allow_input_fusion=fuse_inputs)


def _dot(a, b):
    return jnp.dot(a, b, preferred_element_type=F32)


def _dot_nt(a, b):
    return lax.dot_general(a, b, (((1,), (1,)), ((), ())), preferred_element_type=F32)


def _dot_tn(a, b):
    return lax.dot_general(a, b, (((0,), (0,)), ((), ())), preferred_element_type=F32)


def _split2(t):
    hi = t.astype(BF16)
    lo = (t - hi.astype(F32)).astype(BF16)
    return hi, lo


def _dot_sel_r(t, sel):
    hi, lo = _split2(t)
    return _dot(hi, sel) + _dot(lo, sel)


def _dot_sel_l(sel, t):
    hi, lo = _split2(t)
    return _dot(sel, hi) + _dot(sel, lo)


def _rms(x, g):
    ms = jnp.mean(x * x, axis=-1, keepdims=True)
    return x * lax.rsqrt(ms + NORM_EPS) * g


def _sigmoid(x):
    return 1.0 / (1.0 + jnp.exp(-x))


def _silu(x):
    return x * _sigmoid(x)


def _softplus(x):
    return jnp.maximum(x, 0.0) + jnp.log1p(jnp.exp(-jnp.abs(x)))


def _iota(shape, axis):
    return lax.broadcasted_iota(jnp.int32, shape, axis)


def _norm_matmul_body(x_ref, g_ref, w_ref, o_ref, h_ref, *, n_scaled, scale):
    j = pl.program_id(1)
    mult = jnp.where(j < n_scaled, jnp.float32(scale), jnp.float32(1.0))

    @pl.when(j == 0)
    def _():
        for r in range(0, x_ref.shape[0], EDGE_ROWS):
            rs = slice(r, r + EDGE_ROWS)
            h = _rms(x_ref[rs, :], g_ref[...]).astype(BF16)
            h_ref[rs, :] = h
            o_ref[rs, :] = (_dot(h, w_ref[...]) * mult).astype(o_ref.dtype)

    @pl.when(j > 0)
    def _():
        o_ref[...] = (_dot(h_ref[...], w_ref[...]) * mult).astype(o_ref.dtype)


def _norm_matmul(x, g, w, out_dtype, tm, tn, n_scaled=0, scale=1.0):
    n, d = x.shape
    m = w.shape[1]
    return pl.pallas_call(
        functools.partial(_norm_matmul_body, n_scaled=n_scaled, scale=scale),
        grid=(n // tm, m // tn),
        in_specs=[
            pl.BlockSpec((tm, d), lambda i, j: (i, 0)),
            pl.BlockSpec((1, d), lambda i, j: (0, 0)),
            pl.BlockSpec((d, tn), lambda i, j: (0, j)),
        ],
        out_specs=pl.BlockSpec((tm, tn), lambda i, j: (i, j)),
        out_shape=jax.ShapeDtypeStruct((n, m), out_dtype),
        scratch_shapes=[pltpu.VMEM((tm, d), BF16)],
        compiler_params=_cparams(2),
        name="norm_matmul",
    )(x, g, w)


def _proj_res_body(*refs, n_in):
    y_refs, w_refs = refs[:n_in], refs[n_in:2 * n_in]
    x_ref, g_ref, o_ref = refs[2 * n_in:]
    m = _dot(y_refs[0][...], w_refs[0][...])
    for y_ref, w_ref in zip(y_refs[1:], w_refs[1:]):
        m = m + _dot(y_ref[...], w_ref[...])
    o_ref[...] = x_ref[...] + _rms(m, g_ref[...])


def _proj_res(ys, w, x, g, tm):
    n, d = x.shape
    n_in = len(ys)
    k = ys[0].shape[1]
    assert all(y.shape[1] == k for y in ys) and w.shape[0] == n_in * k
    in_specs = [pl.BlockSpec((tm, k), lambda i: (i, 0)) for y in ys]
    in_specs += [pl.BlockSpec((k, d), functools.partial(lambda i, r: (r, 0), r=r), pipeline_mode=pl.Buffered(1))
                 for r in range(n_in)]
    in_specs += [pl.BlockSpec((tm, d), lambda i: (i, 0)), pl.BlockSpec((1, d), lambda i: (0, 0))]
    return pl.pallas_call(
        functools.partial(_proj_res_body, n_in=n_in),
        grid=(n // tm,),
        in_specs=in_specs,
        out_specs=pl.BlockSpec((tm, d), lambda i: (i, 0)),
        out_shape=jax.ShapeDtypeStruct((n, d), F32),
        compiler_params=_cparams(1),
        name="proj_res",
    )(*ys, *([w] * n_in), x, g)


def _ffn_body(x_ref, gi_ref, go_ref, wg_ref, wu_ref, wd_ref, o_ref, h_ref):
    j = pl.program_id(1)
    last_j = pl.num_programs(1) - 1
    tm = x_ref.shape[0]

    def partial_out(h):
        a = (_silu(_dot(h, wg_ref[...])) * _dot(h, wu_ref[...])).astype(BF16)
        return _dot(a, wd_ref[...])

    row_groups = [slice(r, r + EDGE_ROWS) for r in range(0, tm, EDGE_ROWS)]

    @pl.when(j == 0)
    def _():
        for rs in row_groups:
            h = _rms(x_ref[rs, :], gi_ref[...]).astype(BF16)
            h_ref[rs, :] = h
            o_ref[rs, :] = partial_out(h)

    @pl.when((j > 0) & (j < last_j))
    def _():
        o_ref[...] += partial_out(h_ref[...])

    @pl.when(j == last_j)
    def _():
        for rs in row_groups:
            acc = o_ref[rs, :] + partial_out(h_ref[rs, :])
            o_ref[rs, :] = x_ref[rs, :] + _rms(acc, go_ref[...])


def _ffn(x, g_in, g_out, wg, wu, wd, layer, tm, tf):
    n, d = x.shape
    f = wg.shape[2]
    assert f // tf >= 2 and tm % EDGE_ROWS == 0
    return pl.pallas_call(
        _ffn_body,
        grid=(n // tm, f // tf),
        in_specs=[
            pl.BlockSpec((tm, d), lambda i, j: (i, 0), pipeline_mode=pl.Buffered(1)),
            pl.BlockSpec((1, d), lambda i, j: (0, 0)),
            pl.BlockSpec((1, d), lambda i, j: (0, 0)),
            pl.BlockSpec((None, d, tf), lambda i, j: (layer, 0, j)),
            pl.BlockSpec((None, d, tf), lambda i, j: (layer, 0, j)),
            pl.BlockSpec((None, tf, d), lambda i, j: (layer, j, 0)),
        ],
        out_specs=pl.BlockSpec((tm, d), lambda i, j: (i, 0)),
        out_shape=jax.ShapeDtypeStruct((n, d), F32),
        scratch_shapes=[pltpu.VMEM((tm, d), BF16)],
        compiler_params=_cparams(2, [False, False, False, True, True, True]),
        name="ffn",
    )(x, g_in, g_out, wg, wu, wd)


def _seg_matrices(width, n_seg_pad):
    e = (_iota((width, n_seg_pad), 0) // HEAD == _iota((width, n_seg_pad), 1)).astype(BF16)
    et = (_iota((n_seg_pad, width), 1) // HEAD == _iota((n_seg_pad, width), 0)).astype(BF16)
    return e, et


def _blockdiag_tril(tc):
    r, c = _iota((tc, tc), 0), _iota((tc, tc), 1)
    return ((r // CHUNK == c // CHUNK) & (c <= r)).astype(BF16)


def _rwkv_body(x_ref, gn_ref, w_ref, mu_ref, w0_ref, w2_ref, a0_ref, a2_ref, g2_ref, kk_ref, ka_ref, rk_ref,
               lnw_ref, lnb_ref, o_ref, p_ref, prow_ref, s_ref, at_ref, rt_ref, bt_ref, kt_ref, v_ref, gl_ref, y_ref,
               *, tc, width, lora_win):
    step = pl.program_id(1)

    tw = p_ref.shape[1] // N_PROJ_TILES
    assert tw % MXU_COLS == 0

    def h_next_fn():
        return _rms(x_ref[...], gn_ref[...]).astype(BF16)

    def project_tile(h, k):
        p_ref[:, k * tw:(k + 1) * tw] = _dot(h, w_ref[:, k * tw:(k + 1) * tw])

    @pl.when(step == 0)
    def _():
        s_ref[...] = jnp.zeros_like(s_ref)
        prow_ref[...] = jnp.zeros_like(prow_ref)
        h = h_next_fn()
        for k in range(N_PROJ_TILES):
            project_tile(h, k)

    @pl.when(step > 0)
    def _():
        _rwkv_mix_block(h_next_fn, project_tile, mu_ref, w0_ref, w2_ref, a0_ref, a2_ref, g2_ref, kk_ref, ka_ref, rk_ref, lnw_ref,
                        lnb_ref, o_ref, p_ref, prow_ref, s_ref, at_ref, rt_ref, bt_ref, kt_ref, v_ref, gl_ref, y_ref,
                        tc=tc, width=width, lora_win=lora_win)


def _rwkv_mix_block(h_next_fn, project_tile, mu_ref, w0_ref, w2_ref, a0_ref, a2_ref, g2_ref, kk_ref, ka_ref, rk_ref, lnw_ref,
                    lnb_ref, o_ref, p_ref, prow_ref, s_ref, at_ref, rt_ref, bt_ref, kt_ref, v_ref, gl_ref, y_ref,
                    *, tc, width, lora_win):
    n_heads = width // HEAD
    n_chunks = tc // CHUNK

    p = p_ref[...]
    first_row = _iota((tc, 1), 0) == 0

    def lerp(cols):
        pc = p[:, cols]
        psh = jnp.where(first_row, prow_ref[0:1, cols], pltpu.roll(pc, 1, 0))
        return pc + (psh - pc) * mu_ref[:, cols]

    o1 = 3 * width
    xl = lerp(slice(o1, p.shape[1]))
    wc, ac, gc = (xl[:, lo:hi] for lo, hi in lora_win)
    z = w0_ref[...] + _dot(jnp.tanh(wc).astype(BF16), w2_ref[...])
    a = _sigmoid(a0_ref[...] + _dot(ac.astype(BF16), a2_ref[...]))
    g = _dot(_sigmoid(gc).astype(BF16), g2_ref[...])
    lw = -jnp.exp(-_softplus(-z) - 0.5)

    h_next = h_next_fn()
    project_tile(h_next, 0)
    project_tile(h_next, 1)
    project_tile(h_next, 2)

    xr = lerp(slice(0, o1))
    last_row = p[tc - 1:tc, :]
    r = xr[:, 0:width]
    k = xr[:, width:2 * width]
    v = xr[:, 2 * width:3 * width]

    mxu_w = 2 * LANE
    same_head = (_iota((mxu_w, mxu_w), 0) // HEAD == _iota((mxu_w, mxu_w), 1) // HEAD).astype(BF16)

    def seg_sum(t):
        return jnp.concatenate(
            [_dot_sel_r(t[:, j:j + mxu_w], same_head) for j in range(0, width, mxu_w)], axis=1)

    kk = k * kk_ref[...]
    kk = kk * lax.rsqrt(jnp.maximum(seg_sum(kk * kk), 1e-24))
    k2 = k * (1.0 + (a - 1.0) * ka_ref[...])
    project_tile(h_next, 3)
    project_tile(h_next, 4)
    bonus = seg_sum(r * k2 * rk_ref[...]) * v
    cs = _dot_sel_l(_blockdiag_tril(tc), lw)
    project_tile(h_next, 5)
    project_tile(h_next, 6)
    prow_ref[0:1, :] = last_row

    ginv = jnp.exp(-cs)
    at_ref[...] = (-kk * jnp.exp(cs - lw)).astype(BF16)
    rt_ref[...] = (r * jnp.exp(cs)).astype(BF16)
    bt_ref[...] = (kk * a * ginv).astype(BF16)
    kt_ref[...] = (k2 * ginv).astype(BF16)
    v_ref[...] = v.astype(BF16)
    for c in range(n_chunks):
        last = c * CHUNK + CHUNK - 1
        gl_ref[c:c + 1, :] = jnp.exp(cs[last:last + 1, :])

    rr, cc = _iota((CHUNK, CHUNK), 0), _iota((CHUNK, CHUNK), 1)
    rr2, cc2 = _iota((CHUNK, 2 * CHUNK), 0), _iota((CHUNK, 2 * CHUNK), 1) % CHUNK
    strict2 = (cc2 < rr2).astype(F32)
    incl2 = (cc2 <= rr2).astype(F32)
    eye = (cc == rr).astype(F32)
    n_lvl = int(math.log2(CHUNK))
    lvl_masks = []
    for lv in range(n_lvl):
        same = (rr >> (lv + 1)) == (cc >> (lv + 1))
        lvl_masks.append((same & (((rr >> lv) & 1) == 1) & (((cc >> lv) & 1) == 0)).astype(F32))

    def chunk_step(c, carry):
        t0 = pl.multiple_of(c * CHUNK, CHUNK)
        rows = pl.ds(t0, CHUNK)
        gl = gl_ref[pl.ds(c, 1), :]
        heads = range(n_heads)
        hsl = [slice(h * HEAD, (h + 1) * HEAD) for h in heads]
        vh = [v_ref[rows, hs] for hs in hsl]
        ar = [jnp.concatenate([at_ref[rows, hs], rt_ref[rows, hs]], axis=0) for hs in hsl]
        bk = [jnp.concatenate([bt_ref[rows, hs], kt_ref[rows, hs]], axis=0) for hs in hsl]
        pm = [_dot_nt(ar[h], bk[h]) for h in heads]
        s0 = [s_ref[h] for h in heads]
        ars = [_dot_nt(ar[h], s0[h].astype(BF16)) for h in heads]
        pa = [pm[h][:CHUNK, :] * strict2 for h in heads]
        aab = [pa[h][:, :CHUNK] for h in heads]
        rhs = [ars[h][:CHUNK, :] + _dot(pa[h][:, CHUNK:].astype(BF16), vh[h]) for h in heads]
        x = [eye + aab[h] * lvl_masks[0] for h in heads]
        for lv in range(1, n_lvl):
            xb = [x[h].astype(BF16) for h in heads]
            tm_ = [_dot((aab[h] * lvl_masks[lv]).astype(BF16), xb[h]) for h in heads]
            x = [x[h] + _dot(xb[h], tm_[h].astype(BF16)) for h in heads]
        u = [_dot(x[h].astype(BF16), rhs[h].astype(BF16)) for h in heads]
        uv = [jnp.concatenate([u[h].astype(BF16), vh[h]], axis=0) for h in heads]
        for h in heads:
            pr = (pm[h][CHUNK:, :] * incl2).astype(BF16)
            y_ref[rows, hsl[h]] = ars[h][CHUNK:, :] + _dot(pr, uv[h])
        for h in heads:
            s_ref[h] = (s0[h] + _dot_tn(uv[h], bk[h])) * gl[:, hsl[h]]
        return carry

    lax.fori_loop(0, n_chunks, chunk_step, 0)

    y = y_ref[...]
    inv_n = 1.0 / HEAD
    mean = seg_sum(y) * inv_n
    yc = y - mean
    var = seg_sum(yc * yc) * inv_n
    yn = yc * lax.rsqrt(var + RWKV_LN_EPS) * lnw_ref[...] + lnb_ref[...]
    o_ref[...] = ((yn + bonus) * g).astype(o_ref.dtype)


def _mixer_specs(x, gn, w, pw, params, batch, seq, tc, width):
    nt = seq // tc
    d = x.shape[1]
    const = lambda b, s: (0, 0)
    in_specs = [
        pl.BlockSpec((tc, d), lambda b, s: (b * nt + jnp.minimum(s, nt - 1), 0)),
        pl.BlockSpec(gn.shape, const),
        pl.BlockSpec((w.shape[0], pw), const, pipeline_mode=pl.Buffered(1)),
    ] + [pl.BlockSpec(a.shape, const) for a in params]
    out_spec = pl.BlockSpec((tc, width), lambda b, s: (b * nt + jnp.maximum(s - 1, 0), 0))
    return (batch, nt + 1), in_specs, out_spec


def _rwkv_mix(x, gn, w, mu, w0, w2, a0, a2, g2, k_k, k_a, r_k, ln_w, ln_b, batch, seq, tc, lora_win):
    n = x.shape[0]
    pw = mu.shape[1]
    width = w0.shape[1]
    n_heads = width // HEAD
    params = (mu, w0, w2, a0, a2, g2, k_k, k_a, r_k, ln_w, ln_b)
    grid, in_specs, out_spec = _mixer_specs(x, gn, w, pw, params, batch, seq, tc, width)
    return pl.pallas_call(
        functools.partial(_rwkv_body, tc=tc, width=width, lora_win=lora_win),
        grid=grid,
        in_specs=in_specs,
        out_specs=out_spec,
        out_shape=jax.ShapeDtypeStruct((n, width), BF16),
        scratch_shapes=[pltpu.VMEM((tc, pw), F32), pltpu.VMEM((SUBLANE, pw), F32),
                        pltpu.VMEM((n_heads, HEAD, HEAD), F32)]
        + [pltpu.VMEM((tc, width), BF16) for _ in range(5)]
        + [pltpu.VMEM((max(tc // CHUNK, SUBLANE), width), F32), pltpu.VMEM((tc, width), F32)],
        compiler_params=_cparams(2),
        name="rwkv_mix",
    )(x, gn, w, *params)


def _ssd_body(x_ref, gn_ref, w_ref, cw_ref, cb_ref, dtb_ref, alog_ref, dskip_ref, nw_ref, o_ref,
              p_ref, tail_ref, st_ref, y_ref, *, tc, width, state, groups):
    step = pl.program_id(1)
    n_chunks = tc // CHUNK
    pw = p_ref.shape[1]
    tw = 4 * MXU_COLS
    tiles = [slice(c0, min(c0 + tw, pw)) for c0 in range(0, pw, tw)]
    n_tiles = len(tiles)

    def project_tile(h, k):
        p_ref[:, tiles[k]] = _dot(h, w_ref[:, tiles[k]])

    @pl.when(step == 0)
    def _():
        st_ref[...] = jnp.zeros_like(st_ref)
        tail_ref[...] = jnp.zeros_like(tail_ref)
        h = _rms(x_ref[...], gn_ref[...]).astype(BF16)
        for k in range(n_tiles):
            project_tile(h, k)

    @pl.when(step > 0)
    def _():
        _ssd_mix_block(x_ref, gn_ref, project_tile, n_tiles, cw_ref, cb_ref, dtb_ref, alog_ref, dskip_ref, nw_ref,
                       o_ref, p_ref, tail_ref, st_ref, y_ref, tc=tc, width=width, state=state, groups=groups)


def _ssd_mix_block(x_ref, gn_ref, project_tile, n_tiles, cw_ref, cb_ref, dtb_ref, alog_ref, dskip_ref, nw_ref,
                   o_ref, p_ref, tail_ref, st_ref, y_ref, *, tc, width, state, groups):
    n_chunks = tc // CHUNK
    gw = width // groups
    hpg = gw // HEAD
    conv_ch = width + 2 * groups * state
    n_tap = cw_ref.shape[0]

    p = p_ref[...]
    zs = _silu(p[:, 0:width])
    xbc_raw = p[:, width:width + conv_ch]
    dt_raw = p[:, width + conv_ch:width + conv_ch + LANE]
    h_next = _rms(x_ref[...], gn_ref[...]).astype(BF16)
    tiles_after_chunk = {c: [k for k in range(1, n_tiles) if (k - 1) % n_chunks == c] for c in range(n_chunks)}

    prev = tail_ref[...]
    tail_ref[...] = xbc_raw[tc - SUBLANE:tc, :]
    xcat = jnp.concatenate([prev, xbc_raw], axis=0)
    conv = cb_ref[...]
    for j in range(n_tap):
        off = SUBLANE - (n_tap - 1) + j
        conv = conv + cw_ref[j:j + 1, :] * xcat[off:off + tc, :]
    xbc = _silu(conv)
    xs = xbc[:, 0:width]
    bm = xbc[:, width:width + groups * state].astype(BF16)
    cm = xbc[:, width + groups * state:conv_ch].astype(BF16)
    project_tile(h_next, 0)

    _, et = _seg_matrices(width, LANE)
    dt = _softplus(dt_raw + dtb_ref[...])
    da = -jnp.exp(alog_ref[...]) * dt
    cs_h = _dot_sel_l(_blockdiag_tril(tc), da)
    cs = _dot_sel_r(cs_h, et)
    dtx = _dot_sel_r(dt, et)
    xc = xs * dtx
    ecs = jnp.exp(cs)

    rr, cc = _iota((CHUNK, CHUNK), 0), _iota((CHUNK, CHUNK), 1)
    causal = cc <= rr

    for c in range(n_chunks):
        r0 = c * CHUNK
        rows = slice(r0, r0 + CHUNK)
        cs_c = cs[rows, :]
        cs_last = cs_c[CHUNK - 1:CHUNK, :]
        xc_c = xc[rows, :]
        xd = (xc_c * jnp.exp(cs_last - cs_c)).astype(BF16)
        xcb = xc_c.astype(BF16)
        cs_t = cs_h[rows, :].T
        for g in range(groups):
            gl = slice(g * gw, (g + 1) * gw)
            b_g = bm[rows, g * state:(g + 1) * state]
            c_g = cm[rows, g * state:(g + 1) * state]
            cb = _dot_nt(c_g, b_g)
            st = st_ref[g]
            y_off = _dot(c_g, st.astype(BF16)) * ecs[rows, gl]
            for hh in range(hpg):
                h = g * hpg + hh
                hs = slice(h * HEAD, (h + 1) * HEAD)
                seg = cs_c[:, hs] - cs_t[h:h + 1, :]
                lmat = jnp.where(causal, jnp.exp(jnp.where(causal, seg, 0.0)), 0.0)
                y_ref[rows, hs] = _dot((cb * lmat).astype(BF16), xcb[:, hs]) + y_off[:, hh * HEAD:(hh + 1) * HEAD]
            st_ref[g] = st * ecs[r0 + CHUNK - 1:r0 + CHUNK, gl] + _dot_tn(b_g, xd[:, gl])
        for k in tiles_after_chunk[c]:
            project_tile(h_next, k)

    y = (y_ref[...] + dskip_ref[...] * xs) * zs
    for g in range(groups):
        gl = slice(g * gw, (g + 1) * gw)
        yg = y[:, gl]
        ms = jnp.mean(yg * yg, axis=-1, keepdims=True)
        o_ref[:, gl] = (yg * lax.rsqrt(ms + NORM_EPS) * nw_ref[:, gl]).astype(o_ref.dtype)


def _ssd_mix(x, gn, w, conv_w, conv_b, dt_bias, a_log, d_skip, norm_w, batch, seq, tc, width, state, groups):
    n = x.shape[0]
    pw = w.shape[1]
    conv_ch = width + 2 * groups * state
    params = (conv_w, conv_b, dt_bias, a_log, d_skip, norm_w)
    grid, in_specs, out_spec = _mixer_specs(x, gn, w, pw, params, batch, seq, tc, width)
    return pl.pallas_call(
        functools.partial(_ssd_body, tc=tc, width=width, state=state, groups=groups),
        grid=grid,
        in_specs=in_specs,
        out_specs=out_spec,
        out_shape=jax.ShapeDtypeStruct((n, width), BF16),
        scratch_shapes=[pltpu.VMEM((tc, pw), F32), pltpu.VMEM((SUBLANE, conv_ch), F32),
                        pltpu.VMEM((groups, state, width // groups), F32), pltpu.VMEM((tc, width), F32)],
        compiler_params=_cparams(2),
        name="ssd_mix",
    )(x, gn, w, *params)


def _ssm_weight_body(w_ref, o_ref, *, start, main, tail):
    w = w_ref[...]
    o_ref[:, 0:main] = w[:, start:start + main]
    last = w[:, w.shape[1] - LANE:]
    lane = _iota(last.shape, 1)
    o_ref[:, main:main + LANE] = jnp.where(lane < tail, pltpu.roll(last, tail, 1), jnp.zeros_like(last))


def _ssm_weight(w, start, main, tail):
    k, m = w.shape
    assert main % LANE == 0 and start + main + tail == m and tail <= LANE
    rows = 256
    return pl.pallas_call(
        functools.partial(_ssm_weight_body, start=start, main=main, tail=tail),
        grid=(k // rows,),
        in_specs=[pl.BlockSpec((rows, m), lambda i: (i, 0))],
        out_specs=pl.BlockSpec((rows, main + LANE), lambda i: (i, 0)),
        out_shape=jax.ShapeDtypeStruct((k, main + LANE), w.dtype),
        compiler_params=_cparams(1),
        name="ssm_weight",
    )(w)


PAIR = 2 * CHUNK
BAND = (LEFT_CHUNKS + 1) * CHUNK
PAIR_BAND = BAND + CHUNK


def _bias_table_body(rb_ref, o_ref, *, n_bucket, rel_future):
    ext_w = PAIR_BAND + LANE
    m = _iota((n_bucket, ext_w), 1)
    bucket = jnp.clip(PAIR_BAND - 1 - m, -rel_future, REL_PAST_CLIP) + rel_future
    sel = (bucket == _iota((n_bucket, ext_w), 0)).astype(BF16)
    ext = _dot_sel_r(rb_ref[...], sel) * LOG2E
    for r in range(PAIR):
        o_ref[r] = ext[:, PAIR - 1 - r:PAIR - 1 - r + PAIR_BAND]


def _bias_table(rel_bias):
    n_heads, n_bucket = rel_bias.shape
    out = pl.pallas_call(
        functools.partial(_bias_table_body, n_bucket=n_bucket, rel_future=CHUNK - 1),
        out_shape=jax.ShapeDtypeStruct((PAIR, n_heads, PAIR_BAND), F32),
        name="bias_table",
    )(rel_bias)
    return jnp.transpose(out, (1, 0, 2))


def _attn_body(q_ref, kp_ref, kc_ref, vp_ref, vc_ref, b_ref, o_ref, *, tq, dh):
    i = pl.program_id(2)
    n_heads = q_ref.shape[1] // dh
    row, col = _iota((PAIR, PAIR_BAND), 0), _iota((PAIR, PAIR_BAND), 1)
    lo = (row // CHUNK) * CHUNK
    in_band = (col >= lo) & (col < lo + BAND)
    n_pairs = tq // PAIR
    k0 = [tq - LEFT_CHUNKS * CHUNK + jp * PAIR for jp in range(n_pairs)]
    work = [(h, jp) for h in range(n_heads) for jp in range(n_pairs)]

    def attend(first_block):
        hcols = [slice(h * dh, (h + 1) * dh) for h in range(n_heads)]
        kcat = [jnp.concatenate([kp_ref[:, hc], kc_ref[:, hc]], axis=0) for hc in hcols]
        vcat = [jnp.concatenate([vp_ref[:, hc], vc_ref[:, hc]], axis=0) for hc in hcols]
        s = [_dot_nt(q_ref[jp * PAIR:(jp + 1) * PAIR, hcols[h]], kcat[h][k0[jp]:k0[jp] + PAIR_BAND, :]) + b_ref[h]
             for h, jp in work]
        if first_block:
            valid = [in_band & (col + k0[jp] >= tq) for jp in range(n_pairs)]
            s = [jnp.where(valid[jp], s_, -1e30) for (h, jp), s_ in zip(work, s)]
        else:
            edge = PAIR_BAND - LANE
            s = [jnp.concatenate([jnp.where(in_band[:, :LANE], s_[:, :LANE], -1e30), s_[:, LANE:edge],
                                  jnp.where(in_band[:, edge:], s_[:, edge:], -1e30)], axis=1) for s_ in s]
        mx = [jnp.max(s_, axis=-1, keepdims=True) for s_ in s]
        pexp = [jnp.exp2(s_ - m_) for s_, m_ in zip(s, mx)]
        den = [jnp.sum(p_, axis=-1, keepdims=True) for p_ in pexp]
        o = [_dot(p_.astype(BF16), vcat[h][k0[jp]:k0[jp] + PAIR_BAND, :]) for (h, jp), p_ in zip(work, pexp)]
        for (h, jp), o_, d_ in zip(work, o, den):
            o_ref[jp * PAIR:(jp + 1) * PAIR, hcols[h]] = (o_ / d_).astype(o_ref.dtype)

    @pl.when(i == 0)
    def _():
        attend(True)

    @pl.when(i > 0)
    def _():
        attend(False)


def _band_attention(qkv, bias, batch, seq, n_heads, tq, hp):
    n, three_d = qkv.shape
    d = three_d // 3
    dh = d // n_heads
    nt = seq // tq
    ng = n_heads // hp
    cur = lambda off: (lambda b, h, i: (b * nt + i, off * ng + h))
    prv = lambda off: (lambda b, h, i: (b * nt + jnp.maximum(i - 1, 0), off * ng + h))
    blk = lambda f: pl.BlockSpec((tq, hp * dh), f)
    return pl.pallas_call(
        functools.partial(_attn_body, tq=tq, dh=dh),
        grid=(batch, ng, nt),
        in_specs=[blk(cur(0)), blk(prv(1)), blk(cur(1)), blk(prv(2)), blk(cur(2)),
                  pl.BlockSpec((hp, PAIR, PAIR_BAND), lambda b, h, i: (h, 0, 0))],
        out_specs=pl.BlockSpec((tq, hp * dh), lambda b, h, i: (b * nt + i, h)),
        out_shape=jax.ShapeDtypeStruct((n, d), BF16),
        compiler_params=_cparams(3),
        name="band_attention",
    )(qkv, qkv, qkv, qkv, qkv, bias)


def _pad_cols(a, width):
    return jnp.pad(a, ((0, 0), (0, width - a.shape[1])))


def _pad_rows(a, rows):
    return jnp.pad(a, ((0, rows - a.shape[0]), (0, 0)))


def _row(a):
    return a.reshape(1, -1)


def _lora_window(w, off):
    rank = w.shape[0]
    lo, hi = off // LANE * LANE, -(-(off + rank) // LANE) * LANE
    return jnp.pad(w, ((off - lo, hi - off - rank), (0, 0))).astype(BF16), (lo, hi)


class _Tiles(NamedTuple):
    rows: int
    proj_rows: int
    mix_rows: int
    attn_rows: int
    attn_heads: int
    qkv_cols: int
    ffn_cols: int


def _tiles(n, seq):
    rows = min(1024, n)
    return _Tiles(rows=rows, proj_rows=rows, mix_rows=min(4 * CHUNK, seq),
                  attn_rows=min(LEFT_CHUNKS * CHUNK, seq), attn_heads=8, qkv_cols=2048, ffn_cols=512)


def kernel(x, norm_g, w_in_ab, rwkv_mu, rwkv_w0, rwkv_w2, rwkv_a0, rwkv_a2, rwkv_g2, rwkv_k_k, rwkv_k_a, rwkv_r_k, rwkv_ln_w, rwkv_ln_b, ssm_conv_w, ssm_conv_b, ssm_dt_bias, ssm_A_log, ssm_D, ssm_norm_w, w_out_ab, w_qkv, attn_rel_bias, w_out_c, ffn_w_gate, ffn_w_up, ffn_w_down):
    batch, seq, d = x.shape
    n = batch * seq
    depth = norm_g.shape[0]
    xf = x.reshape(n, d)

    rw = rwkv_w0.shape[1]
    lora_w, lora_a, lora_g = rwkv_w2.shape[1], rwkv_a2.shape[1], rwkv_g2.shape[1]
    sw = ssm_norm_w.shape[1]
    n_ssm_heads = ssm_A_log.shape[1]
    conv_ch = ssm_conv_w.shape[2]
    groups = 2
    state = (conv_ch - sw) // (2 * groups)
    n_att_heads = attn_rel_bias.shape[1]
    rwkv_proj = 3 * rw + lora_w + lora_a + lora_g
    t = _tiles(n, seq)

    wg_all, wu_all, wd_all = ffn_w_gate.astype(BF16), ffn_w_up.astype(BF16), ffn_w_down.astype(BF16)
    for l in range(depth):
        i = l // 2
        g = norm_g[l]
        if l % 2 == 0:
            w_in = w_in_ab[i].astype(BF16)
            pw_r = -(-rwkv_proj // MXU_COLS) * MXU_COLS
            mu = _pad_cols(_row(rwkv_mu[i]), pw_r)
            lora_w2, win_w = _lora_window(rwkv_w2[i], 0)
            lora_a2, win_a = _lora_window(rwkv_a2[i], lora_w)
            lora_g2, win_g = _lora_window(rwkv_g2[i], lora_w + lora_a)
            o4 = rwkv_proj + sw + conv_ch
            w_s = _ssm_weight(w_in, rwkv_proj, o4 - rwkv_proj, n_ssm_heads)

            y_a = _rwkv_mix(
                xf, g[0:1], w_in, mu, _row(rwkv_w0[i]), lora_w2, _row(rwkv_a0[i]), lora_a2, lora_g2, _row(rwkv_k_k[i]),
                _row(rwkv_k_a[i]), _row(rwkv_r_k[i]), _row(rwkv_ln_w[i]), _row(rwkv_ln_b[i]), batch, seq, t.mix_rows,
                (win_w, win_a, win_g))
            y_b = _ssd_mix(
                xf, g[0:1], w_s, ssm_conv_w[i], _row(ssm_conv_b[i]), _pad_cols(_row(ssm_dt_bias[i]), LANE),
                _pad_cols(_row(ssm_A_log[i]), LANE), _row(jnp.repeat(ssm_D[i], sw // n_ssm_heads)),
                _row(ssm_norm_w[i]), batch, seq, t.mix_rows, sw, state, groups)
            w_o = w_out_ab[i].astype(BF16)
            xf = _proj_res([y_a, y_b], w_o, xf, g[1:2], t.proj_rows)
        else:
            qkv = _norm_matmul(xf, g[0:1], w_qkv[i].astype(BF16), BF16, t.rows, t.qkv_cols, n_scaled=d // t.qkv_cols,
                               scale=(d // n_att_heads) ** -0.5 * LOG2E)
            bias = _bias_table(attn_rel_bias[i])
            att = _band_attention(qkv, bias, batch, seq, n_att_heads, t.attn_rows, t.attn_heads)
            xf = _proj_res([att], w_out_c[i].astype(BF16), xf, g[1:2], t.proj_rows)
        xf = _ffn(xf, g[2:3], g[3:4], wg_all, wu_all, wd_all, l, t.rows, t.ffn_cols)
    return xf.reshape(batch, seq, d)
```
